```python
import jax, jax.numpy as jnp
from jax import lax
import numpy as np

D_MODEL = 1024
BATCH = 8
SEQ = 2048
DEPTH = 2

N_MIXERS = 2
N_A_LAYERS = (DEPTH + 1) // 2
N_B_LAYERS = DEPTH // 2
EPS = 1e-6
NEG_INF = -1e30

GLA_HEADS = 4
GLA_DK = D_MODEL // 2
GLA_DV = D_MODEL
GLA_HK = GLA_DK // GLA_HEADS
GLA_HV = GLA_DV // GLA_HEADS
GLA_GATE_RANK = 16
GLA_GATE_NORMALIZER = 16.0
GLA_CHUNK = 64
GLA_SPLITS = (GLA_DK, 2 * GLA_DK, 2 * GLA_DK + GLA_DV, 2 * GLA_DK + 2 * GLA_DV)
GLA_IN = 2 * GLA_DK + 2 * GLA_DV + GLA_GATE_RANK

MOBA_HEADS = 8
MOBA_HD = D_MODEL // MOBA_HEADS
MOBA_BLOCK = 256
MOBA_TOPK = 3
MOBA_QCHUNK = 32

D_FF = D_MODEL * 7 // 2
N_EXPERTS = 8
TOP_K = 2

kernel_name = "hybrid_gla_moba_moe_trunk"


def rms_norm(x, gain):
    xf = x.astype(jnp.float32)
    y = xf * lax.rsqrt(jnp.mean(xf * xf, axis=-1, keepdims=True) + EPS)
    return (y * gain.astype(jnp.float32)).astype(x.dtype)


def gla_mixer(h, w_in, w_gate2, b_gate, out_gain, w_out):
    B, S, _ = h.shape
    n_chunks = S // GLA_CHUNK
    q, k, v, g, a_lr = jnp.split(h @ w_in, GLA_SPLITS, axis=-1)
    log_a = jax.nn.log_sigmoid((a_lr @ w_gate2 + b_gate).astype(jnp.float32)) / GLA_GATE_NORMALIZER

    def to_chunks(t, hd):
        return t.astype(jnp.float32).reshape(B, n_chunks, GLA_CHUNK, GLA_HEADS, hd).transpose(0, 3, 1, 2, 4)

    q = to_chunks(q, GLA_HK) * (GLA_HK ** -0.5)
    k = to_chunks(k, GLA_HK)
    v = to_chunks(v, GLA_HV)
    cum = jnp.cumsum(to_chunks(log_a, GLA_HK), axis=3)
    q_dec = q * jnp.exp(cum)
    k_inv = k * jnp.exp(-cum)
    causal = jnp.tril(jnp.ones((GLA_CHUNK, GLA_CHUNK), dtype=bool))
    scores = jnp.where(causal, jnp.einsum('bhnid,bhnjd->bhnij', q_dec, k_inv), 0.0)
    o_intra = jnp.einsum('bhnij,bhnjv->bhniv', scores, v)
    cum_last = cum[:, :, :, -1, :]
    k_to_end = k * jnp.exp(cum_last[:, :, :, None, :] - cum)
    chunk_kv = jnp.einsum('bhncd,bhncv->bhndv', k_to_end, v)

    def step(state, inp):
        decay, kv = inp
        return decay[..., None] * state + kv, state

    init = jnp.zeros((B, GLA_HEADS, GLA_HK, GLA_HV), jnp.float32)
    _, prev_states = lax.scan(step, init, (jnp.exp(cum_last).transpose(2, 0, 1, 3),
                                           chunk_kv.transpose(2, 0, 1, 3, 4)))
    o_inter = jnp.einsum('bhncd,nbhdv->bhncv', q_dec, prev_states)
    o = rms_norm(o_intra + o_inter, out_gain)
    o = o.transpose(0, 2, 3, 1, 4).reshape(B, S, GLA_DV)
    o = o * jax.nn.silu(g.astype(jnp.float32))
    return o.astype(h.dtype) @ w_out


def moba_mixer(h, w_qkv, q_gain, k_gain, w_out):
    B, S, _ = h.shape
    n_blocks = -(-S // MOBA_BLOCK)
    s_pad = n_blocks * MOBA_BLOCK
    n_qchunks = s_pad // MOBA_QCHUNK
    top_k = min(MOBA_TOPK, n_blocks)
    q, k, v = jnp.split(h @ w_qkv, 3, axis=-1)

    def to_heads(t, gain):
        t = t.astype(jnp.float32).reshape(B, S, MOBA_HEADS, MOBA_HD)
        if gain is not None:
            t = rms_norm(t, gain)
        t = t.transpose(0, 2, 1, 3)
        return jnp.pad(t, ((0, 0), (0, 0), (0, s_pad - S), (0, 0)))

    q = to_heads(q, q_gain)
    k = to_heads(k, k_gain)
    v = to_heads(v, None)
    kb = k.reshape(B, MOBA_HEADS, n_blocks, MOBA_BLOCK, MOBA_HD)
    vb = v.reshape(B, MOBA_HEADS, n_blocks, MOBA_BLOCK, MOBA_HD)
    k_mean = kb.mean(axis=3)

    q_block = jnp.arange(s_pad) // MOBA_BLOCK
    fully_past = jnp.arange(n_blocks)[None, :] < q_block[:, None]
    gate = jnp.where(fully_past, jnp.einsum('bhsd,bhnd->bhsn', q, k_mean), NEG_INF)
    _, sel = lax.top_k(gate, top_k)
    slot_valid = jnp.arange(top_k)[None, :] < q_block[:, None]

    q_c = q.reshape(B, MOBA_HEADS, n_qchunks, MOBA_QCHUNK, MOBA_HD).transpose(2, 0, 1, 3, 4)
    sel_c = sel.reshape(B, MOBA_HEADS, n_qchunks, MOBA_QCHUNK, top_k).transpose(2, 0, 1, 3, 4)
    valid_c = slot_valid.reshape(n_qchunks, MOBA_QCHUNK, top_k)
    b_idx = jnp.arange(B)[:, None, None, None]
    h_idx = jnp.arange(MOBA_HEADS)[None, :, None, None]
    scale = MOBA_HD ** -0.5
    n_sel = top_k * MOBA_BLOCK

    def attend(args):
        qc, selc, validc, ci = args
        k_sel = kb[b_idx, h_idx, selc].reshape(B, MOBA_HEADS, MOBA_QCHUNK, n_sel, MOBA_HD)
        v_sel = vb[b_idx, h_idx, selc].reshape(B, MOBA_HEADS, MOBA_QCHUNK, n_sel, MOBA_HD)
        s_sel = jnp.einsum('bhqd,bhqjd->bhqj', qc, k_sel) * scale
        s_sel = jnp.where(jnp.repeat(validc, MOBA_BLOCK, axis=-1), s_sel, NEG_INF)
        own = (ci * MOBA_QCHUNK) // MOBA_BLOCK
        k_own = lax.dynamic_index_in_dim(kb, own, axis=2, keepdims=False)
        v_own = lax.dynamic_index_in_dim(vb, own, axis=2, keepdims=False)
        s_own = jnp.einsum('bhqd,bhjd->bhqj', qc, k_own) * scale
        q_pos = ci * MOBA_QCHUNK + jnp.arange(MOBA_QCHUNK)
        k_pos = own * MOBA_BLOCK + jnp.arange(MOBA_BLOCK)
        s_own = jnp.where(k_pos[None, :] <= q_pos[:, None], s_own, NEG_INF)
        p = jax.nn.softmax(jnp.concatenate([s_sel, s_own], axis=-1), axis=-1)
        return (jnp.einsum('bhqj,bhqjd->bhqd', p[..., :n_sel], v_sel)
                + jnp.einsum('bhqj,bhjd->bhqd', p[..., n_sel:], v_own))

    o = lax.map(attend, (q_c, sel_c, valid_c, jnp.arange(n_qchunks)))
    o = o.transpose(1, 0, 3, 2, 4).reshape(B, s_pad, MOBA_HEADS * MOBA_HD)[:, :S]
    return o.astype(h.dtype) @ w_out


def swiglu(h, w_gate, w_up, w_down):
    return (jax.nn.silu(h @ w_gate) * (h @ w_up)) @ w_down


def moe_swiglu(h, w_router, w_gate, w_up, w_down):
    logits = (h @ w_router).astype(jnp.float32)
    top_val, top_idx = lax.top_k(logits, TOP_K)
    weights = jax.nn.softmax(top_val, axis=-1)
    gates = jnp.sum(jax.nn.one_hot(top_idx, N_EXPERTS, dtype=jnp.float32) * weights[..., None], axis=-2)
    gates = gates.astype(h.dtype)
    out = jnp.zeros_like(h)
    for e in range(N_EXPERTS):
        out = out + gates[..., e:e + 1] * swiglu(h, w_gate[e], w_up[e], w_down[e])
    return out


def setup_inputs(seed: int = 0) -> dict:
    key = jax.random.key(seed)
    ks = iter(jax.random.split(key, 24))

    def nrm(shape, fan_in, scale=1.0):
        return jax.random.normal(next(ks), shape, jnp.float32) * (scale * fan_in ** -0.5)

    def gain(shape):
        return 1.0 + 0.02 * jax.random.normal(next(ks), shape, jnp.float32)

    res_scale = (2 * DEPTH) ** -0.5
    return {
        "x": jax.random.normal(next(ks), (BATCH, SEQ, D_MODEL), jnp.float32),
        "norm_mix": gain((DEPTH, D_MODEL)),
        "norm_ffn": gain((DEPTH, D_MODEL)),
        "gla_w_in": nrm((N_A_LAYERS, D_MODEL, GLA_IN), D_MODEL),
        "gla_w_gate2": nrm((N_A_LAYERS, GLA_GATE_RANK, GLA_DK), GLA_GATE_RANK),
        "gla_b_gate": 0.1 * jax.random.normal(next(ks), (N_A_LAYERS, GLA_DK), jnp.float32),
        "gla_out_gain": gain((N_A_LAYERS, GLA_HV)),
        "gla_w_out": nrm((N_A_LAYERS, GLA_DV, D_MODEL), GLA_DV, res_scale),
        "moba_w_qkv": nrm((N_B_LAYERS, D_MODEL, 3 * MOBA_HEADS * MOBA_HD), D_MODEL),
        "moba_q_gain": gain((N_B_LAYERS, MOBA_HD)),
        "moba_k_gain": gain((N_B_LAYERS, MOBA_HD)),
        "moba_w_out": nrm((N_B_LAYERS, MOBA_HEADS * MOBA_HD, D_MODEL), MOBA_HEADS * MOBA_HD, res_scale),
        "ffn_w_gate": nrm((N_A_LAYERS, D_MODEL, D_FF), D_MODEL),
        "ffn_w_up": nrm((N_A_LAYERS, D_MODEL, D_FF), D_MODEL),
        "ffn_w_down": nrm((N_A_LAYERS, D_FF, D_MODEL), D_FF, res_scale),
        "moe_w_router": nrm((N_B_LAYERS, D_MODEL, N_EXPERTS), D_MODEL),
        "moe_w_gate": nrm((N_B_LAYERS, N_EXPERTS, D_MODEL, D_FF), D_MODEL),
        "moe_w_up": nrm((N_B_LAYERS, N_EXPERTS, D_MODEL, D_FF), D_MODEL),
        "moe_w_down": nrm((N_B_LAYERS, N_EXPERTS, D_FF, D_MODEL), D_FF, res_scale),
    }


def reference(x, norm_mix, norm_ffn, gla_w_in, gla_w_gate2, gla_b_gate, gla_out_gain, gla_w_out,
              moba_w_qkv, moba_q_gain, moba_k_gain, moba_w_out, ffn_w_gate, ffn_w_up, ffn_w_down,
              moe_w_router, moe_w_gate, moe_w_up, moe_w_down):
    for i in range(DEPTH):
        j = i // N_MIXERS
        h = rms_norm(x, norm_mix[i])
        if i % N_MIXERS == 0:
            x = x + gla_mixer(h, gla_w_in[j], gla_w_gate2[j], gla_b_gate[j], gla_out_gain[j], gla_w_out[j])
        else:
            x = x + moba_mixer(h, moba_w_qkv[j], moba_q_gain[j], moba_k_gain[j], moba_w_out[j])
        h = rms_norm(x, norm_ffn[i])
        if i % 2 == 0:
            x = x + swiglu(h, ffn_w_gate[j], ffn_w_up[j], ffn_w_down[j])
        else:
            x = x + moe_swiglu(h, moe_w_router[j], moe_w_gate[j], moe_w_up[j], moe_w_down[j])
    return x
```

```python
import functools

import jax
import jax.numpy as jnp
from jax import lax
from jax.experimental import pallas as pl
from jax.experimental.pallas import tpu as pltpu

F32 = jnp.float32
BF16 = jnp.bfloat16

EPS = 1e-6
NEG_INF = -1e30

D_MODEL = 1024
GLA_HEADS = 4
GLA_DK = 512
GLA_DV = 1024
GLA_HK = 128
GLA_HV = 256
GLA_RANK = 16
GLA_NORMALIZER = 16.0
GLA_CHUNK = 64

MOBA_HEADS = 8
MOBA_HD = 128
MOBA_BLOCK = 256
MOBA_TOPK = 3

N_EXPERTS = 8

V7X_VMEM_BYTES = 64 * 1024 * 1024


def _params(semantics, vmem_bytes=None):
    return pltpu.CompilerParams(dimension_semantics=semantics, vmem_limit_bytes=vmem_bytes)


def _rms(x, gain):
    y = x * lax.rsqrt(jnp.mean(x * x, axis=-1, keepdims=True) + EPS)
    return y * gain


def _dot(a, b):
    return jnp.dot(a, b, preferred_element_type=F32)


def _dot_nt(a, b):
    return lax.dot_general(a, b, (((1,), (1,)), ((), ())), preferred_element_type=F32)


def _dot_tn(a, b):
    return lax.dot_general(a, b, (((0,), (0,)), ((), ())), preferred_element_type=F32)


def _split(a):
    hi = a.astype(BF16)
    lo = (a - hi.astype(F32)).astype(BF16)
    return hi, lo


def _log_sigmoid(z):
    return jnp.minimum(z, 0.0) - jnp.log1p(jnp.exp(-jnp.abs(z)))


def _silu(z):
    return z / (1.0 + jnp.exp(-z))


def _gla_inproj_kernel(x_ref, gain_ref, w_ref, wlr_ref, wg2_ref, b_ref,
                       q_ref, k_ref, v_ref, g_ref, la_ref):
    h = _rms(x_ref[...], gain_ref[...]).astype(BF16)
    q_ref[...] = _dot(h, w_ref[:, 0:GLA_DK]).astype(BF16)
    k_ref[...] = _dot(h, w_ref[:, GLA_DK:2 * GLA_DK]).astype(BF16)
    v_ref[...] = _dot(h, w_ref[:, 2 * GLA_DK:2 * GLA_DK + GLA_DV]).astype(BF16)
    g_ref[...] = _dot(h, w_ref[:, 2 * GLA_DK + GLA_DV:]).astype(BF16)
    a_lr = _dot(h, wlr_ref[...])
    a_hi, a_lo = _split(a_lr)
    w_hi, w_lo = _split(wg2_ref[...])
    z = _dot(a_hi, w_hi) + _dot(a_lo, w_hi) + _dot(a_hi, w_lo) + b_ref[...]
    la_ref[...] = _log_sigmoid(z) / GLA_NORMALIZER


def _gla_inproj(x, gain, w_main, w_lr, w_gate2, b_gate, tm):
    t = x.shape[0]
    n_main = w_main.shape[1]
    row = lambda i: (i, 0)
    fixed = lambda i: (0, 0)
    return pl.pallas_call(
        _gla_inproj_kernel,
        grid=(t // tm,),
        in_specs=[
            pl.BlockSpec((tm, D_MODEL), row),
            pl.BlockSpec((1, D_MODEL), fixed),
            pl.BlockSpec((D_MODEL, n_main), fixed),
            pl.BlockSpec((D_MODEL, GLA_RANK), fixed),
            pl.BlockSpec((GLA_RANK, GLA_DK), fixed),
            pl.BlockSpec((1, GLA_DK), fixed),
        ],
        out_specs=[
            pl.BlockSpec((tm, GLA_DK), row),
            pl.BlockSpec((tm, GLA_DK), row),
            pl.BlockSpec((tm, GLA_DV), row),
            pl.BlockSpec((tm, GLA_DV), row),
            pl.BlockSpec((tm, GLA_DK), row),
        ],
        out_shape=[
            jax.ShapeDtypeStruct((t, GLA_DK), BF16),
            jax.ShapeDtypeStruct((t, GLA_DK), BF16),
            jax.ShapeDtypeStruct((t, GLA_DV), BF16),
            jax.ShapeDtypeStruct((t, GLA_DV), BF16),
            jax.ShapeDtypeStruct((t, GLA_DK), F32),
        ],
        compiler_params=_params(("parallel",), 48 * 1024 * 1024),
        name="gla_inproj",
    )(x, gain, w_main, w_lr, w_gate2, b_gate)


def _gla_kernel(q_ref, k_ref, v_ref, g_ref, la_ref, og_ref, o_ref, st_ref, *, n_chunks):
    @pl.when(pl.program_id(1) == 0)
    def _():
        st_ref[...] = jnp.zeros_like(st_ref)

    r = lax.broadcasted_iota(jnp.int32, (GLA_CHUNK, GLA_CHUNK), 0)
    c = lax.broadcasted_iota(jnp.int32, (GLA_CHUNK, GLA_CHUNK), 1)
    causal = r >= c
    tril = causal.astype(BF16)
    og = og_ref[...]
    for h in range(GLA_HEADS):
        hk = slice(h * GLA_HK, (h + 1) * GLA_HK)
        hv = slice(h * GLA_HV, (h + 1) * GLA_HV)
        st = st_ref[h]
        for ci in range(n_chunks):
            rows = slice(ci * GLA_CHUNK, (ci + 1) * GLA_CHUNK)
            la = la_ref[rows, hk]
            la_hi, la_rest = _split(la)
            la_mid = la_rest
            la_lo = (la - la_hi.astype(F32) - la_mid.astype(F32)).astype(BF16)
            cum = _dot(tril, la_hi) + _dot(tril, la_mid) + _dot(tril, la_lo)
            cum_last = cum[GLA_CHUNK - 1:GLA_CHUNK, :]
            q = q_ref[rows, hk].astype(F32) * (GLA_HK ** -0.5)
            k = k_ref[rows, hk].astype(F32)
            v = v_ref[rows, hv]
            q_dec = (q * jnp.exp(cum)).astype(BF16)
            k_inv = (k * jnp.exp(-cum)).astype(BF16)
            k_end = (k * jnp.exp(cum_last - cum)).astype(BF16)
            scores = jnp.where(causal, _dot_nt(q_dec, k_inv), 0.0)
            o = _dot(scores.astype(BF16), v) + _dot_nt(q_dec, st.astype(BF16))
            st = st * jnp.exp(cum_last) + _dot_tn(v, k_end)
            o = o * lax.rsqrt(jnp.mean(o * o, axis=-1, keepdims=True) + EPS) * og
            o = o * _silu(g_ref[rows, hv].astype(F32))
            o_ref[rows, hv] = o.astype(BF16)
        st_ref[h] = st


def _gla(q, k, v, g, la, out_gain, batch, seq, blk):
    t = q.shape[0]
    nb = seq // blk
    row = lambda b, i: (b * nb + i, 0)
    return pl.pallas_call(
        functools.partial(_gla_kernel, n_chunks=blk // GLA_CHUNK),
        grid=(batch, nb),
        in_specs=[
            pl.BlockSpec((blk, GLA_DK), row),
            pl.BlockSpec((blk, GLA_DK), row),
            pl.BlockSpec((blk, GLA_DV), row),
            pl.BlockSpec((blk, GLA_DV), row),
            pl.BlockSpec((blk, GLA_DK), row),
            pl.BlockSpec((1, GLA_HV), lambda b, i: (0, 0)),
        ],
        out_specs=pl.BlockSpec((blk, GLA_DV), row),
        out_shape=jax.ShapeDtypeStruct((t, GLA_DV), BF16),
        scratch_shapes=[pltpu.VMEM((GLA_HEADS, GLA_HV, GLA_HK), F32)],
        compiler_params=_params(("parallel", "arbitrary")),
        name="gla_core",
    )(q, k, v, g, la, out_gain)


def _proj_residual_kernel(x_ref, o_ref, w_ref, y_ref):
    y_ref[...] = x_ref[...] + _dot(o_ref[...], w_ref[...])


def _proj_residual(x, o, w, tm):
    t, d = x.shape
    kdim = o.shape[1]
    row = lambda i: (i, 0)
    return pl.pallas_call(
        _proj_residual_kernel,
        grid=(t // tm,),
        in_specs=[
            pl.BlockSpec((tm, d), row),
            pl.BlockSpec((tm, kdim), row),
            pl.BlockSpec((kdim, d), lambda i: (0, 0)),
        ],
        out_specs=pl.BlockSpec((tm, d), row),
        out_shape=jax.ShapeDtypeStruct((t, d), F32),
        compiler_params=_params(("parallel",)),
        name="proj_residual",
    )(x, o, w)


def _swiglu_kernel(*refs, gated):
    if gated:
        x_ref, gain_ref, gate_ref, wg_ref, wu_ref, wd_ref, y_ref, h_ref = refs
    else:
        x_ref, gain_ref, wg_ref, wu_ref, wd_ref, y_ref, h_ref = refs
    first = (pl.program_id(1) == 0) & (pl.program_id(2) == 0)

    @pl.when(first)
    def _():
        x = x_ref[...]
        h_ref[...] = _rms(x, gain_ref[...]).astype(BF16)
        y_ref[...] = x

    h = h_ref[...]
    a = _silu(_dot(h, wg_ref[...])) * _dot(h, wu_ref[...])
    if gated:
        a = a * gate_ref[...]
    y_ref[...] += _dot(a.astype(BF16), wd_ref[...])


def _swiglu_residual(x, gain, gates, w_gate, w_up, w_down, tm, tf):
    t, d = x.shape
    n_exp, _, ff = w_gate.shape
    gated = gates is not None
    in_specs = [
        pl.BlockSpec((tm, d), lambda i, e, j: (i, 0)),
        pl.BlockSpec((1, d), lambda i, e, j: (0, 0)),
    ]
    args = [x, gain]
    if gated:
        in_specs.append(pl.BlockSpec((None, tm, 1), lambda i, e, j: (e, i, 0)))
        args.append(gates)
    in_specs += [
        pl.BlockSpec((None, d, tf), lambda i, e, j: (e, 0, j)),
        pl.BlockSpec((None, d, tf), lambda i, e, j: (e, 0, j)),
        pl.BlockSpec((None, tf, d), lambda i, e, j: (e, j, 0)),
    ]
    args += [w_gate, w_up, w_down]
    return pl.pallas_call(
        functools.partial(_swiglu_kernel, gated=gated),
        grid=(t // tm, n_exp, ff // tf),
        in_specs=in_specs,
        out_specs=pl.BlockSpec((tm, d), lambda i, e, j: (i, 0)),
        out_shape=jax.ShapeDtypeStruct((t, d), F32),
        scratch_shapes=[pltpu.VMEM((tm, d), BF16)],
        compiler_params=_params(("parallel", "arbitrary", "arbitrary"), 48 * 1024 * 1024),
        name="moe_swiglu" if gated else "ffn_swiglu",
    )(*args)


def _moba_qkv_kernel(x_ref, gain_ref, w_ref, qg_ref, kg_ref, q_ref, k_ref, v_ref):
    h = _rms(x_ref[...], gain_ref[...]).astype(BF16)
    d = MOBA_HEADS * MOBA_HD
    qg = qg_ref[...]
    kg = kg_ref[...]
    q = _dot(h, w_ref[:, 0:d])
    k = _dot(h, w_ref[:, d:2 * d])
    for hd in range(MOBA_HEADS):
        cols = slice(hd * MOBA_HD, (hd + 1) * MOBA_HD)
        q_ref[:, cols] = _rms(q[:, cols], qg).astype(BF16)
        k_ref[:, cols] = _rms(k[:, cols], kg).astype(BF16)
    v_ref[...] = _dot(h, w_ref[:, 2 * d:]).astype(BF16)


def _moba_qkv(x, gain, w, q_gain, k_gain, tm):
    t = x.shape[0]
    d = MOBA_HEADS * MOBA_HD
    row = lambda i: (i, 0)
    fixed = lambda i: (0, 0)
    return pl.pallas_call(
        _moba_qkv_kernel,
        grid=(t // tm,),
        in_specs=[
            pl.BlockSpec((tm, D_MODEL), row),
            pl.BlockSpec((1, D_MODEL), fixed),
            pl.BlockSpec((D_MODEL, 3 * d), fixed),
            pl.BlockSpec((1, MOBA_HD), fixed),
            pl.BlockSpec((1, MOBA_HD), fixed),
        ],
        out_specs=[pl.BlockSpec((tm, d), row)] * 3,
        out_shape=[jax.ShapeDtypeStruct((t, d), BF16)] * 3,
        compiler_params=_params(("parallel",), 48 * 1024 * 1024),
        name="moba_qkv",
    )(x, gain, w, q_gain, k_gain)


def _moba_kernel(q_ref, k_ref, v_ref, o_ref, *, n_blocks):
    seq = n_blocks * MOBA_BLOCK
    scale = MOBA_HD ** -0.5
    blk_of_key = lax.broadcasted_iota(jnp.int32, (128, seq), 1) // MOBA_BLOCK
    blk_row = lax.broadcasted_iota(jnp.int32, (128, seq), 0)
    indicator = jnp.where(blk_of_key == blk_row, 1.0 / MOBA_BLOCK, 0.0).astype(BF16)
    k_mean = _dot(indicator, k_ref[...])
    km_hi, km_lo = _split(k_mean)

    lane = lax.broadcasted_iota(jnp.int32, (MOBA_BLOCK, 128), 1)
    r = lax.broadcasted_iota(jnp.int32, (MOBA_BLOCK, MOBA_BLOCK), 0)
    c = lax.broadcasted_iota(jnp.int32, (MOBA_BLOCK, MOBA_BLOCK), 1)
    causal = c <= r

    for i in range(n_blocks):
        rows = slice(i * MOBA_BLOCK, (i + 1) * MOBA_BLOCK)
        q = q_ref[rows, :]
        selected = None
        if i > MOBA_TOPK:
            gate = _dot_nt(q, km_hi) + _dot_nt(q, km_lo)
            gate = jnp.where(lane < i, gate, NEG_INF)
            rank = jnp.zeros_like(gate)
            for jp in range(i):
                col = gate[:, jp:jp + 1]
                beats = (col > gate) | ((col == gate) & (jp < lane))
                rank = rank + beats.astype(F32)
            selected = jnp.where((rank < MOBA_TOPK) & (lane < i), 1.0, 0.0)
        s_blocks = []
        for j in range(i + 1):
            keys = slice(j * MOBA_BLOCK, (j + 1) * MOBA_BLOCK)
            s = _dot_nt(q, k_ref[keys, :]) * scale
            if j == i:
                s = jnp.where(causal, s, NEG_INF)
            elif selected is not None:
                s = jnp.where(selected[:, j:j + 1] > 0.5, s, NEG_INF)
            s_blocks.append(s)
        m = s_blocks[0].max(axis=-1, keepdims=True)
        for s in s_blocks[1:]:
            m = jnp.maximum(m, s.max(axis=-1, keepdims=True))
        denom = jnp.zeros((MOBA_BLOCK, 1), F32)
        acc = jnp.zeros((MOBA_BLOCK, MOBA_HD), F32)
        for j, s in enumerate(s_blocks):
            keys = slice(j * MOBA_BLOCK, (j + 1) * MOBA_BLOCK)
            p = jnp.exp(s - m)
            denom = denom + p.sum(axis=-1, keepdims=True)
            acc = acc + _dot(p.astype(BF16), v_ref[keys, :])
        o_ref[rows, :] = (acc / denom).astype(BF16)


def _moba(q, k, v, batch, seq):
    t, d = q.shape
    spec = pl.BlockSpec((seq, MOBA_HD), lambda b, h: (b, h))
    return pl.pallas_call(
        functools.partial(_moba_kernel, n_blocks=seq // MOBA_BLOCK),
        grid=(batch, MOBA_HEADS),
        in_specs=[spec, spec, spec],
        out_specs=spec,
        out_shape=jax.ShapeDtypeStruct((t, d), BF16),
        compiler_params=_params(("parallel", "parallel"), 48 * 1024 * 1024),
        name="moba_attn",
    )(q, k, v)


def _router_kernel(x_ref, gain_ref, w_ref, gates_ref):
    h = _rms(x_ref[...], gain_ref[...])
    h_hi, h_lo = _split(h)
    w_hi, w_lo = _split(w_ref[...])
    logits = _dot(h_hi, w_hi) + _dot(h_lo, w_hi) + _dot(h_hi, w_lo)
    e = lax.broadcasted_iota(jnp.int32, logits.shape, 1)
    m1 = logits.max(axis=-1, keepdims=True)
    i1 = jnp.where(logits == m1, e, N_EXPERTS).min(axis=-1, keepdims=True)
    first = e == i1
    rest = jnp.where(first, -jnp.inf, logits)
    m2 = rest.max(axis=-1, keepdims=True)
    i2 = jnp.where(rest == m2, e, N_EXPERTS).min(axis=-1, keepdims=True)
    second = e == i2
    e2 = jnp.exp(m2 - m1)
    denom = 1.0 + e2
    gates_ref[...] = jnp.where(first, 1.0 / denom, 0.0) + jnp.where(second, e2 / denom, 0.0)


def _router(x, gain, w, tm):
    t = x.shape[0]
    return pl.pallas_call(
        _router_kernel,
        grid=(t // tm,),
        in_specs=[
            pl.BlockSpec((tm, D_MODEL), lambda i: (i, 0)),
            pl.BlockSpec((1, D_MODEL), lambda i: (0, 0)),
            pl.BlockSpec((D_MODEL, N_EXPERTS), lambda i: (0, 0)),
        ],
        out_specs=pl.BlockSpec((tm, N_EXPERTS), lambda i: (i, 0)),
        out_shape=jax.ShapeDtypeStruct((t, N_EXPERTS), F32),
        compiler_params=_params(("parallel",)),
        name="moe_router",
    )(x, gain, w)


def kernel(x, norm_mix, norm_ffn, gla_w_in, gla_w_gate2, gla_b_gate, gla_out_gain, gla_w_out,
           moba_w_qkv, moba_q_gain, moba_k_gain, moba_w_out, ffn_w_gate, ffn_w_up, ffn_w_down,
           moe_w_router, moe_w_gate, moe_w_up, moe_w_down):
    batch, seq, d = x.shape
    t = batch * seq
    xt = x.reshape(t, d)

    n_main = 2 * GLA_DK + 2 * GLA_DV
    w_in = gla_w_in[0]
    q, k, v, g, la = _gla_inproj(
        xt, norm_mix[0:1], w_in[:, :n_main].astype(BF16), w_in[:, n_main:].astype(BF16),
        gla_w_gate2[0], gla_b_gate[0:1], tm=512)
    o = _gla(q, k, v, g, la, gla_out_gain[0:1], batch, seq, blk=256)
    xt = _proj_residual(xt, o, gla_w_out[0].astype(BF16), tm=512)
    xt = _swiglu_residual(xt, norm_ffn[0:1], None, ffn_w_gate.astype(BF16), ffn_w_up.astype(BF16),
                          ffn_w_down.astype(BF16), tm=1024, tf=512)

    q, k, v = _moba_qkv(xt, norm_mix[1:2], moba_w_qkv[0].astype(BF16), moba_q_gain[0:1],
                        moba_k_gain[0:1], tm=512)
    o = _moba(q, k, v, batch, seq)
    xt = _proj_residual(xt, o, moba_w_out[0].astype(BF16), tm=512)
    gates = _router(xt, norm_ffn[1:2], moe_w_router[0], tm=1024)
    gates = gates.T.reshape(N_EXPERTS, t, 1)
    xt = _swiglu_residual(xt, norm_ffn[1:2], gates, moe_w_gate[0].astype(BF16), moe_w_up[0].astype(BF16),
                          moe_w_down[0].astype(BF16), tm=1024, tf=512)
    return xt.reshape(batch, seq, d)
```

```python
import functools

import jax
import jax.numpy as jnp
from jax import lax
from jax.experimental import pallas as pl
from jax.experimental.pallas import tpu as pltpu

F32 = jnp.float32
BF16 = jnp.bfloat16

EPS = 1e-6
NEG_INF = -1e30

D_MODEL = 1024
GLA_HEADS = 4
GLA_DK = 512
GLA_DV = 1024
GLA_HK = 128
GLA_HV = 256
GLA_RANK = 16
GLA_NORMALIZER = 16.0
GLA_CHUNK = 64

MOBA_HEADS = 8
MOBA_HD = 128
MOBA_BLOCK = 256
MOBA_TOPK = 3

N_EXPERTS = 8
TOP_K = 2

LANES = 128
BF16_ROWS = 16

MOE_TM = 512
MOE_TG = 256
TOK_BLK = LANES
GATHER_WB = 10
COMB_TS = 512
COMB_WC = 256
COMB_MAXW = 16


def _params(semantics, vmem_bytes=None):
    return pltpu.CompilerParams(dimension_semantics=semantics, vmem_limit_bytes=vmem_bytes)


def _rms(x, gain):
    y = x * lax.rsqrt(jnp.mean(x * x, axis=-1, keepdims=True) + EPS)
    return y * gain


def _dot(a, b):
    return jnp.dot(a, b, preferred_element_type=F32)


def _dot_nt(a, b):
    return lax.dot_general(a, b, (((1,), (1,)), ((), ())), preferred_element_type=F32)


def _dot_tn(a, b):
    return lax.dot_general(a, b, (((0,), (0,)), ((), ())), preferred_element_type=F32)


def _split(a):
    hi = a.astype(BF16)
    lo = (a - hi.astype(F32)).astype(BF16)
    return hi, lo


def _log_sigmoid(z):
    return jnp.minimum(z, 0.0) - jnp.log1p(jnp.exp(-jnp.abs(z)))


def _silu(z):
    return z / (1.0 + jnp.exp(-z))


def _gla_inproj_kernel(x_ref, gain_ref, w_ref, wlr_ref, wg2_ref, b_ref,
                       q_ref, k_ref, v_ref, g_ref, la_ref):
    h = _rms(x_ref[...], gain_ref[...]).astype(BF16)
    q_ref[...] = _dot(h, w_ref[:, 0:GLA_DK]).astype(BF16)
    k_ref[...] = _dot(h, w_ref[:, GLA_DK:2 * GLA_DK]).astype(BF16)
    v_ref[...] = _dot(h, w_ref[:, 2 * GLA_DK:2 * GLA_DK + GLA_DV]).astype(BF16)
    g_ref[...] = _dot(h, w_ref[:, 2 * GLA_DK + GLA_DV:]).astype(BF16)
    a_lr = _dot(h, wlr_ref[...])
    a_hi, a_lo = _split(a_lr)
    w_hi, w_lo = _split(wg2_ref[...])
    z = _dot(a_hi, w_hi) + _dot(a_lo, w_hi) + _dot(a_hi, w_lo) + b_ref[...]
    la_ref[...] = _log_sigmoid(z) / GLA_NORMALIZER


def _gla_inproj(x, gain, w_main, w_lr, w_gate2, b_gate, tm):
    t = x.shape[0]
    n_main = w_main.shape[1]
    row = lambda i: (i, 0)
    fixed = lambda i: (0, 0)
    return pl.pallas_call(
        _gla_inproj_kernel,
        grid=(t // tm,),
        in_specs=[
            pl.BlockSpec((tm, D_MODEL), row),
            pl.BlockSpec((1, D_MODEL), fixed),
            pl.BlockSpec((D_MODEL, n_main), fixed),
            pl.BlockSpec((D_MODEL, GLA_RANK), fixed),
            pl.BlockSpec((GLA_RANK, GLA_DK), fixed),
            pl.BlockSpec((1, GLA_DK), fixed),
        ],
        out_specs=[
            pl.BlockSpec((tm, GLA_DK), row),
            pl.BlockSpec((tm, GLA_DK), row),
            pl.BlockSpec((tm, GLA_DV), row),
            pl.BlockSpec((tm, GLA_DV), row),
            pl.BlockSpec((tm, GLA_DK), row),
        ],
        out_shape=[
            jax.ShapeDtypeStruct((t, GLA_DK), BF16),
            jax.ShapeDtypeStruct((t, GLA_DK), BF16),
            jax.ShapeDtypeStruct((t, GLA_DV), BF16),
            jax.ShapeDtypeStruct((t, GLA_DV), BF16),
            jax.ShapeDtypeStruct((t, GLA_DK), F32),
        ],
        compiler_params=_params(("parallel",), 48 * 1024 * 1024),
        name="gla_inproj",
    )(x, gain, w_main, w_lr, w_gate2, b_gate)


def _gla_kernel(q_ref, k_ref, v_ref, g_ref, la_ref, og_ref, o_ref, st_ref, *, n_chunks):
    @pl.when(pl.program_id(1) == 0)
    def _():
        st_ref[...] = jnp.zeros_like(st_ref)

    r = lax.broadcasted_iota(jnp.int32, (GLA_CHUNK, GLA_CHUNK), 0)
    c = lax.broadcasted_iota(jnp.int32, (GLA_CHUNK, GLA_CHUNK), 1)
    causal = r >= c
    tril = causal.astype(BF16)
    og = og_ref[...]
    for h in range(GLA_HEADS):
        hk = slice(h * GLA_HK, (h + 1) * GLA_HK)
        hv = slice(h * GLA_HV, (h + 1) * GLA_HV)
        st = st_ref[h]
        for ci in range(n_chunks):
            rows = slice(ci * GLA_CHUNK, (ci + 1) * GLA_CHUNK)
            la = la_ref[rows, hk]
            la_hi, la_rest = _split(la)
            la_mid = la_rest
            la_lo = (la - la_hi.astype(F32) - la_mid.astype(F32)).astype(BF16)
            cum = _dot(tril, la_hi) + _dot(tril, la_mid) + _dot(tril, la_lo)
            cum_last = cum[GLA_CHUNK - 1:GLA_CHUNK, :]
            q = q_ref[rows, hk].astype(F32) * (GLA_HK ** -0.5)
            k = k_ref[rows, hk].astype(F32)
            v = v_ref[rows, hv]
            q_dec = (q * jnp.exp(cum)).astype(BF16)
            k_inv = (k * jnp.exp(-cum)).astype(BF16)
            k_end = (k * jnp.exp(cum_last - cum)).astype(BF16)
            scores = jnp.where(causal, _dot_nt(q_dec, k_inv), 0.0)
            o = _dot(scores.astype(BF16), v) + _dot_nt(q_dec, st.astype(BF16))
            st = st * jnp.exp(cum_last) + _dot_tn(v, k_end)
            o = o * lax.rsqrt(jnp.mean(o * o, axis=-1, keepdims=True) + EPS) * og
            o = o * _silu(g_ref[rows, hv].astype(F32))
            o_ref[rows, hv] = o.astype(BF16)
        st_ref[h] = st


def _gla(q, k, v, g, la, out_gain, batch, seq, blk):
    t = q.shape[0]
    nb = seq // blk
    row = lambda b, i: (b * nb + i, 0)
    return pl.pallas_call(
        functools.partial(_gla_kernel, n_chunks=blk // GLA_CHUNK),
        grid=(batch, nb),
        in_specs=[
            pl.BlockSpec((blk, GLA_DK), row),
            pl.BlockSpec((blk, GLA_DK), row),
            pl.BlockSpec((blk, GLA_DV), row),
            pl.BlockSpec((blk, GLA_DV), row),
            pl.BlockSpec((blk, GLA_DK), row),
            pl.BlockSpec((1, GLA_HV), lambda b, i: (0, 0)),
        ],
        out_specs=pl.BlockSpec((blk, GLA_DV), row),
        out_shape=jax.ShapeDtypeStruct((t, GLA_DV), BF16),
        scratch_shapes=[pltpu.VMEM((GLA_HEADS, GLA_HV, GLA_HK), F32)],
        compiler_params=_params(("parallel", "arbitrary")),
        name="gla_core",
    )(q, k, v, g, la, out_gain)


def _proj_residual_kernel(x_ref, o_ref, w_ref, y_ref):
    y_ref[...] = x_ref[...] + _dot(o_ref[...], w_ref[...])


def _proj_residual(x, o, w, tm):
    t, d = x.shape
    kdim = o.shape[1]
    row = lambda i: (i, 0)
    return pl.pallas_call(
        _proj_residual_kernel,
        grid=(t // tm,),
        in_specs=[
            pl.BlockSpec((tm, d), row),
            pl.BlockSpec((tm, kdim), row),
            pl.BlockSpec((kdim, d), lambda i: (0, 0)),
        ],
        out_specs=pl.BlockSpec((tm, d), row),
        out_shape=jax.ShapeDtypeStruct((t, d), F32),
        compiler_params=_params(("parallel",)),
        name="proj_residual",
    )(x, o, w)


def _ffn_kernel(x_ref, gain_ref, wg_ref, wu_ref, wd_ref, y_ref, h_ref):
    @pl.when(pl.program_id(1) == 0)
    def _():
        x = x_ref[...]
        h_ref[...] = _rms(x, gain_ref[...]).astype(BF16)
        y_ref[...] = x

    h = h_ref[...]
    a = _silu(_dot(h, wg_ref[...])) * _dot(h, wu_ref[...])
    y_ref[...] += _dot(a.astype(BF16), wd_ref[...])


def _ffn_residual(x, gain, w_gate, w_up, w_down, tm, tf):
    t, d = x.shape
    ff = w_gate.shape[1]
    return pl.pallas_call(
        _ffn_kernel,
        grid=(t // tm, ff // tf),
        in_specs=[
            pl.BlockSpec((tm, d), lambda i, j: (i, 0)),
            pl.BlockSpec((1, d), lambda i, j: (0, 0)),
            pl.BlockSpec((d, tf), lambda i, j: (0, j)),
            pl.BlockSpec((d, tf), lambda i, j: (0, j)),
            pl.BlockSpec((tf, d), lambda i, j: (j, 0)),
        ],
        out_specs=pl.BlockSpec((tm, d), lambda i, j: (i, 0)),
        out_shape=jax.ShapeDtypeStruct((t, d), F32),
        scratch_shapes=[pltpu.VMEM((tm, d), BF16)],
        compiler_params=_params(("parallel", "arbitrary"), 48 * 1024 * 1024),
        name="ffn_swiglu",
    )(x, gain, w_gate, w_up, w_down)


def _moba_qkv_kernel(x_ref, gain_ref, w_ref, qg_ref, kg_ref, q_ref, k_ref, v_ref):
    h = _rms(x_ref[...], gain_ref[...]).astype(BF16)
    d = MOBA_HEADS * MOBA_HD
    qg = qg_ref[...]
    kg = kg_ref[...]
    q = _dot(h, w_ref[:, 0:d])
    k = _dot(h, w_ref[:, d:2 * d])
    for hd in range(MOBA_HEADS):
        cols = slice(hd * MOBA_HD, (hd + 1) * MOBA_HD)
        q_ref[:, cols] = _rms(q[:, cols], qg).astype(BF16)
        k_ref[:, cols] = _rms(k[:, cols], kg).astype(BF16)
    v_ref[...] = _dot(h, w_ref[:, 2 * d:]).astype(BF16)


def _moba_qkv(x, gain, w, q_gain, k_gain, tm):
    t = x.shape[0]
    d = MOBA_HEADS * MOBA_HD
    row = lambda i: (i, 0)
    fixed = lambda i: (0, 0)
    return pl.pallas_call(
        _moba_qkv_kernel,
        grid=(t // tm,),
        in_specs=[
            pl.BlockSpec((tm, D_MODEL), row),
            pl.BlockSpec((1, D_MODEL), fixed),
            pl.BlockSpec((D_MODEL, 3 * d), fixed),
            pl.BlockSpec((1, MOBA_HD), fixed),
            pl.BlockSpec((1, MOBA_HD), fixed),
        ],
        out_specs=[pl.BlockSpec((tm, d), row)] * 3,
        out_shape=[jax.ShapeDtypeStruct((t, d), BF16)] * 3,
        compiler_params=_params(("parallel",), 48 * 1024 * 1024),
        name="moba_qkv",
    )(x, gain, w, q_gain, k_gain)


def _moba_kernel(q_ref, k_ref, v_ref, o_ref, *, n_blocks):
    seq = n_blocks * MOBA_BLOCK
    scale = MOBA_HD ** -0.5
    blk_of_key = lax.broadcasted_iota(jnp.int32, (128, seq), 1) // MOBA_BLOCK
    blk_row = lax.broadcasted_iota(jnp.int32, (128, seq), 0)
    indicator = jnp.where(blk_of_key == blk_row, 1.0 / MOBA_BLOCK, 0.0).astype(BF16)
    k_mean = _dot(indicator, k_ref[...])
    km_hi, km_lo = _split(k_mean)

    lane = lax.broadcasted_iota(jnp.int32, (MOBA_BLOCK, 128), 1)
    r = lax.broadcasted_iota(jnp.int32, (MOBA_BLOCK, MOBA_BLOCK), 0)
    c = lax.broadcasted_iota(jnp.int32, (MOBA_BLOCK, MOBA_BLOCK), 1)
    causal = c <= r

    for i in range(n_blocks):
        rows = slice(i * MOBA_BLOCK, (i + 1) * MOBA_BLOCK)
        q = q_ref[rows, :]
        selected = None
        if i > MOBA_TOPK:
            gate = _dot_nt(q, km_hi) + _dot_nt(q, km_lo)
            gate = jnp.where(lane < i, gate, NEG_INF)
            rank = jnp.zeros_like(gate)
            for jp in range(i):
                col = gate[:, jp:jp + 1]
                beats = (col > gate) | ((col == gate) & (jp < lane))
                rank = rank + beats.astype(F32)
            selected = jnp.where((rank < MOBA_TOPK) & (lane < i), 1.0, 0.0)
        s_blocks = []
        for j in range(i + 1):
            keys = slice(j * MOBA_BLOCK, (j + 1) * MOBA_BLOCK)
            s = _dot_nt(q, k_ref[keys, :]) * scale
            if j == i:
                s = jnp.where(causal, s, NEG_INF)
            elif selected is not None:
                s = jnp.where(selected[:, j:j + 1] > 0.5, s, NEG_INF)
            s_blocks.append(s)
        m = s_blocks[0].max(axis=-1, keepdims=True)
        for s in s_blocks[1:]:
            m = jnp.maximum(m, s.max(axis=-1, keepdims=True))
        denom = jnp.zeros((MOBA_BLOCK, 1), F32)
        acc = jnp.zeros((MOBA_BLOCK, MOBA_HD), F32)
        for j, s in enumerate(s_blocks):
            keys = slice(j * MOBA_BLOCK, (j + 1) * MOBA_BLOCK)
            p = jnp.exp(s - m)
            denom = denom + p.sum(axis=-1, keepdims=True)
            acc = acc + _dot(p.astype(BF16), v_ref[keys, :])
        o_ref[rows, :] = (acc / denom).astype(BF16)


def _moba(q, k, v, batch, seq):
    t, d = q.shape
    spec = pl.BlockSpec((seq, MOBA_HD), lambda b, h: (b, h))
    return pl.pallas_call(
        functools.partial(_moba_kernel, n_blocks=seq // MOBA_BLOCK),
        grid=(batch, MOBA_HEADS),
        in_specs=[spec, spec, spec],
        out_specs=spec,
        out_shape=jax.ShapeDtypeStruct((t, d), BF16),
        compiler_params=_params(("parallel", "parallel"), 48 * 1024 * 1024),
        name="moba_attn",
    )(q, k, v)


def _router_kernel(x_ref, gain_ref, w_ref, h_ref, info_ref, cnt_ref, carry_ref):
    @pl.when(pl.program_id(0) == 0)
    def _():
        carry_ref[...] = jnp.zeros_like(carry_ref)

    tr = x_ref.shape[0]
    h = _rms(x_ref[...], gain_ref[...])
    h_hi, h_lo = _split(h)
    h_ref[...] = h_hi
    w_hi, w_lo = _split(w_ref[...])
    logits = _dot(h_hi, w_hi) + _dot(h_lo, w_hi) + _dot(h_hi, w_lo)
    e = lax.broadcasted_iota(jnp.int32, logits.shape, 1)
    logits = jnp.where(e < N_EXPERTS, logits, -jnp.inf)
    m1 = logits.max(axis=-1, keepdims=True)
    i1 = jnp.where(logits == m1, e, LANES).min(axis=-1, keepdims=True)
    first = e == i1
    rest = jnp.where(first, -jnp.inf, logits)
    m2 = rest.max(axis=-1, keepdims=True)
    i2 = jnp.where(rest == m2, e, LANES).min(axis=-1, keepdims=True)
    second = e == i2
    e2 = jnp.exp(m2 - m1)
    denom = 1.0 + e2
    sel = jnp.where(first | second, 1.0, 0.0).astype(BF16)
    r = lax.broadcasted_iota(jnp.int32, (tr, tr), 0)
    c = lax.broadcasted_iota(jnp.int32, (tr, tr), 1)
    before = _dot((c < r).astype(BF16), sel) + carry_ref[...]
    rank1 = jnp.where(first, before, 0.0).sum(axis=-1, keepdims=True)
    rank2 = jnp.where(second, before, 0.0).sum(axis=-1, keepdims=True)
    nb = tr // TOK_BLK
    blk_row = lax.broadcasted_iota(jnp.int32, (nb, tr), 0)
    blk_tok = lax.broadcasted_iota(jnp.int32, (nb, tr), 1) // TOK_BLK
    cnt = _dot((blk_row == blk_tok).astype(BF16), sel)
    cnt_ref[...] = cnt
    carry_ref[...] += cnt.sum(axis=0, keepdims=True)
    col = lax.broadcasted_iota(jnp.int32, (tr, 8), 1)
    fields = (rank1, rank2, i1.astype(F32), i2.astype(F32), 1.0 / denom, e2 / denom)
    info = jnp.zeros((tr, 8), F32)
    for n, f in enumerate(fields):
        info = jnp.where(col == n, f, info)
    info_ref[...] = info


def _router(x, gain, w_padded, tr):
    t = x.shape[0]
    nb = tr // TOK_BLK
    return pl.pallas_call(
        _router_kernel,
        grid=(t // tr,),
        in_specs=[
            pl.BlockSpec((tr, D_MODEL), lambda i: (i, 0)),
            pl.BlockSpec((1, D_MODEL), lambda i: (0, 0)),
            pl.BlockSpec((D_MODEL, LANES), lambda i: (0, 0)),
        ],
        out_specs=[
            pl.BlockSpec((tr, D_MODEL), lambda i: (i, 0)),
            pl.BlockSpec((tr, 8), lambda i: (i, 0)),
            pl.BlockSpec((nb, LANES), lambda i: (i, 0)),
        ],
        out_shape=[
            jax.ShapeDtypeStruct((t, D_MODEL), BF16),
            jax.ShapeDtypeStruct((t, 8), F32),
            jax.ShapeDtypeStruct((t // TOK_BLK, LANES), F32),
        ],
        scratch_shapes=[pltpu.VMEM((1, LANES), F32)],
        compiler_params=_params(("arbitrary",), 48 * 1024 * 1024),
        name="moe_router",
    )(x, gain, w_padded)


def _moe_plan(info, cnt, t):
    i32 = jnp.int32
    e_ids = jnp.arange(N_EXPERTS, dtype=i32)
    cnt_blk = cnt[:, :N_EXPERTS].astype(i32)
    cum_blk = jnp.cumsum(cnt_blk, axis=0)
    counts = cum_blk[-1]
    tiles_e = (counts + MOE_TM - 1) // MOE_TM
    tile_end = jnp.cumsum(tiles_e)
    tile_off = tile_end - tiles_e
    row_off = tile_off * MOE_TM
    n_used = tile_end[-1]
    nt = _moe_num_tiles(t)
    r = jnp.arange(nt, dtype=i32)
    tile_valid = (r < n_used).astype(i32)
    last_e = jnp.max(jnp.where(tiles_e > 0, e_ids, 0))
    te = jnp.minimum(jnp.sum((r[:, None] >= tile_end[None, :]).astype(i32), axis=1), N_EXPERTS - 1)
    tile_expert = jnp.where(tile_valid > 0, te, last_e)

    rank1, rank2, idx1, idx2 = (info[:, n].astype(i32) for n in range(4))
    off_of = lambda idx: jnp.sum(jnp.where(idx[:, None] == e_ids[None, :], row_off[None, :], 0), axis=1)
    pos1 = rank1 + off_of(idx1)
    pos2 = rank2 + off_of(idx2)

    per = MOE_TM // MOE_TG
    rg = jnp.arange(nt * per, dtype=i32)
    r5 = rg // per
    eg = tile_expert[r5]
    k0 = (r5 - tile_off[eg]) * MOE_TM + (rg % per) * MOE_TG
    k1 = jnp.minimum(k0 + MOE_TG, counts[eg])
    has = (tile_valid[r5] > 0) & (k1 > k0)
    cum_e = cum_blk.T[eg]
    s_lo = jnp.sum((cum_e <= k0[:, None]).astype(i32), axis=1)
    s_hi = jnp.sum((cum_e <= (k1 - 1)[:, None]).astype(i32), axis=1)
    g_nwin = jnp.where(has, (s_hi - s_lo) // GATHER_WB + 1, 0)
    g_slo = jnp.where(has, s_lo, 0)

    nts = t // COMB_TS
    cnt_ts = cnt_blk.reshape(nts, COMB_TS // TOK_BLK, N_EXPERTS).sum(axis=1)
    seg_lo = row_off[None, :] + jnp.cumsum(cnt_ts, axis=0) - cnt_ts
    ws0 = (seg_lo // BF16_ROWS) * BF16_ROWS
    nw = jnp.where(cnt_ts > 0, (seg_lo + cnt_ts - ws0 + COMB_WC - 1) // COMB_WC, 0)
    nw_end = jnp.cumsum(nw, axis=1)
    k = jnp.arange(COMB_MAXW, dtype=i32)
    ek = jnp.minimum(jnp.sum((k[None, :, None] >= nw_end[:, None, :]).astype(i32), axis=2), N_EXPERTS - 1)
    pick = lambda a: jnp.take_along_axis(a, ek, axis=1)
    live = k[None, :] < nw_end[:, -1:]
    c_start = jnp.where(live, pick(ws0) + (k[None, :] - pick(nw_end - nw)) * COMB_WC, 0)
    c_lo = jnp.where(live, pick(seg_lo), 0)
    c_hi = jnp.where(live, pick(seg_lo + cnt_ts), 0)
    return dict(
        tile_expert=tile_expert, tile_valid=tile_valid, pos1=pos1, pos2=pos2,
        g_slo=g_slo, g_nwin=g_nwin,
        c_start=c_start.reshape(-1), c_lo=c_lo.reshape(-1), c_hi=c_hi.reshape(-1), c_nwin=nw_end[:, -1])


def _moe_num_tiles(t):
    return TOP_K * t // MOE_TM + N_EXPERTS + 1


def _moe_gather_kernel(slo_ref, nwin_ref, p1_ref, p2_ref, w1_ref, w2_ref, h_ref, hs_ref, gs_ref):
    r = pl.program_id(0)
    n_blk = p1_ref.shape[0]
    row_id = r * MOE_TG + lax.broadcasted_iota(jnp.int32, (MOE_TG, TOK_BLK), 0)
    hs_ref[...] = jnp.zeros_like(hs_ref)
    gs_ref[...] = jnp.zeros_like(gs_ref)

    def body(k, carry):
        nominal = slo_ref[r] + k * GATHER_WB
        sb = jnp.minimum(nominal, n_blk - GATHER_WB)
        fresh = sb + lax.broadcasted_iota(jnp.int32, (GATHER_WB, TOK_BLK), 0) >= nominal
        p1 = jnp.where(fresh, p1_ref[pl.ds(sb, GATHER_WB), :], -1)
        p2 = jnp.where(fresh, p2_ref[pl.ds(sb, GATHER_WB), :], -1)
        w1 = w1_ref[pl.ds(sb, GATHER_WB), :]
        w2 = w2_ref[pl.ds(sb, GATHER_WB), :]
        pieces = []
        g = jnp.zeros((MOE_TG, TOK_BLK), F32)
        for a in range(GATHER_WB):
            m1 = p1[a:a + 1, :] == row_id
            m2 = p2[a:a + 1, :] == row_id
            pieces.append(jnp.where(m1 | m2, 1.0, 0.0).astype(BF16))
            g = g + jnp.where(m1, w1[a:a + 1, :], 0.0) + jnp.where(m2, w2[a:a + 1, :], 0.0)
        onehot = jnp.concatenate(pieces, axis=1)
        tok0 = pl.multiple_of(sb * TOK_BLK, TOK_BLK)
        hs_ref[...] += _dot(onehot, h_ref[pl.ds(tok0, GATHER_WB * TOK_BLK), :]).astype(BF16)
        gs_ref[...] += g.sum(axis=-1, keepdims=True)
        return carry

    lax.fori_loop(0, nwin_ref[r], body, 0)


def _moe_gather(plan, h, w1, w2, n_rows):
    t = h.shape[0]
    n_blk = t // TOK_BLK
    as_blocks = lambda a: a.reshape(n_blk, TOK_BLK)
    whole = lambda shape: pl.BlockSpec(shape, lambda r, *_: (0, 0), pipeline_mode=pl.Buffered(1))
    grid_spec = pltpu.PrefetchScalarGridSpec(
        num_scalar_prefetch=2,
        grid=(n_rows // MOE_TG,),
        in_specs=[whole((n_blk, TOK_BLK))] * 4 + [whole((t, D_MODEL))],
        out_specs=[
            pl.BlockSpec((MOE_TG, D_MODEL), lambda r, *_: (r, 0)),
            pl.BlockSpec((MOE_TG, 1), lambda r, *_: (r, 0)),
        ],
    )
    return pl.pallas_call(
        _moe_gather_kernel,
        grid_spec=grid_spec,
        out_shape=[
            jax.ShapeDtypeStruct((n_rows, D_MODEL), BF16),
            jax.ShapeDtypeStruct((n_rows, 1), F32),
        ],
        compiler_params=_params(("parallel",), 52 * 1024 * 1024),
        name="moe_gather",
    )(plan["g_slo"], plan["g_nwin"], as_blocks(plan["pos1"]), as_blocks(plan["pos2"]),
      as_blocks(w1), as_blocks(w2), h)


def _moe_expert_kernel(te_ref, tv_ref, hs_ref, gs_ref, wg_ref, wu_ref, wd_ref, y_ref, acc_ref, *, nj, chunk):
    r = pl.program_id(0)
    j = pl.program_id(1)
    valid = tv_ref[r] > 0
    tf = wg_ref.shape[1]

    @pl.when(valid)
    def _():
        h = hs_ref[...]
        part = None
        for c0 in range(0, tf, chunk):
            cols = slice(c0, c0 + chunk)
            a = _silu(_dot(h, wg_ref[:, cols])) * _dot(h, wu_ref[:, cols])
            d = _dot(a.astype(BF16), wd_ref[cols, :])
            part = d if part is None else part + d
        if nj == 1:
            y_ref[...] = (part * gs_ref[...]).astype(BF16)
        else:
            @pl.when(j == 0)
            def _():
                acc_ref[...] = part

            @pl.when((j > 0) & (j < nj - 1))
            def _():
                acc_ref[...] += part

            @pl.when(j == nj - 1)
            def _():
                y_ref[...] = ((acc_ref[...] + part) * gs_ref[...]).astype(BF16)

    @pl.when(jnp.logical_not(valid) & (j == nj - 1))
    def _():
        y_ref[...] = jnp.zeros_like(y_ref)


def _moe_experts(plan, hs, gs, w_gate, w_up, w_down, tf, chunk):
    n_rows, d = hs.shape
    ff = w_gate.shape[2]
    nj = ff // tf
    jj = lambda r, j, te, tv: jnp.where(tv[r] > 0, j, nj - 1)
    grid_spec = pltpu.PrefetchScalarGridSpec(
        num_scalar_prefetch=2,
        grid=(n_rows // MOE_TM, nj),
        in_specs=[
            pl.BlockSpec((MOE_TM, d), lambda r, j, te, tv: (r, 0)),
            pl.BlockSpec((MOE_TM, 1), lambda r, j, te, tv: (r, 0)),
            pl.BlockSpec((None, d, tf), lambda r, j, te, tv: (te[r], 0, jj(r, j, te, tv))),
            pl.BlockSpec((None, d, tf), lambda r, j, te, tv: (te[r], 0, jj(r, j, te, tv))),
            pl.BlockSpec((None, tf, d), lambda r, j, te, tv: (te[r], jj(r, j, te, tv), 0)),
        ],
        out_specs=pl.BlockSpec((MOE_TM, d), lambda r, j, te, tv: (r, 0)),
        scratch_shapes=[pltpu.VMEM((MOE_TM, d), F32)],
    )
    return pl.pallas_call(
        functools.partial(_moe_expert_kernel, nj=nj, chunk=chunk),
        grid_spec=grid_spec,
        out_shape=jax.ShapeDtypeStruct((n_rows, d), BF16),
        compiler_params=_params(("arbitrary", "arbitrary"), 56 * 1024 * 1024),
        name="moe_experts",
    )(plan["tile_expert"], plan["tile_valid"], hs, gs, w_gate, w_up, w_down)


def _moe_combine_kernel(start_ref, lo_ref, hi_ref, nwin_ref, x_ref, pos_ref, y_hbm, o_ref, ybuf, sem):
    i = pl.program_id(0)
    n = nwin_ref[i]
    base = i * COMB_MAXW

    def window_copy(k, slot):
        start = pl.multiple_of(start_ref[base + k], BF16_ROWS)
        return pltpu.make_async_copy(y_hbm.at[pl.ds(start, COMB_WC), :], ybuf.at[slot], sem.at[slot])

    o_ref[...] = x_ref[...]

    @pl.when(n > 0)
    def _():
        window_copy(0, 0).start()

    pos1 = pos_ref[:, 0:1]
    pos2 = pos_ref[:, 1:2]
    lane = lax.broadcasted_iota(jnp.int32, (COMB_TS, COMB_WC), 1)

    def body(k, carry):
        slot = k % 2
        window_copy(k, slot).wait()

        @pl.when(k + 1 < n)
        def _():
            window_copy(k + 1, 1 - slot).start()

        start = start_ref[base + k]
        lo = lo_ref[base + k]
        hi = hi_ref[base + k]
        q1 = jnp.where((pos1 >= lo) & (pos1 < hi), pos1 - start, -1)
        q2 = jnp.where((pos2 >= lo) & (pos2 < hi), pos2 - start, -1)
        onehot = jnp.where((q1 == lane) | (q2 == lane), 1.0, 0.0).astype(BF16)
        o_ref[...] += _dot(onehot, ybuf[slot])
        return carry

    lax.fori_loop(0, n, body, 0)


def _moe_combine(plan, x, y):
    t, d = x.shape
    pos = jnp.stack([plan["pos1"], plan["pos2"]], axis=1)
    grid_spec = pltpu.PrefetchScalarGridSpec(
        num_scalar_prefetch=4,
        grid=(t // COMB_TS,),
        in_specs=[
            pl.BlockSpec((COMB_TS, d), lambda i, *_: (i, 0)),
            pl.BlockSpec((COMB_TS, 2), lambda i, *_: (i, 0)),
            pl.BlockSpec(memory_space=pl.ANY),
        ],
        out_specs=pl.BlockSpec((COMB_TS, d), lambda i, *_: (i, 0)),
        scratch_shapes=[pltpu.VMEM((2, COMB_WC, d), BF16), pltpu.SemaphoreType.DMA((2,))],
    )
    return pl.pallas_call(
        _moe_combine_kernel,
        grid_spec=grid_spec,
        out_shape=jax.ShapeDtypeStruct((t, d), F32),
        compiler_params=_params(("arbitrary",)),
        name="moe_combine",
    )(plan["c_start"], plan["c_lo"], plan["c_hi"], plan["c_nwin"], x, pos, y)


def kernel(x, norm_mix, norm_ffn, gla_w_in, gla_w_gate2, gla_b_gate, gla_out_gain, gla_w_out,
           moba_w_qkv, moba_q_gain, moba_k_gain, moba_w_out, ffn_w_gate, ffn_w_up, ffn_w_down,
           moe_w_router, moe_w_gate, moe_w_up, moe_w_down):
    batch, seq, d = x.shape
    t = batch * seq
    xt = x.reshape(t, d)

    n_main = 2 * GLA_DK + 2 * GLA_DV
    w_in = gla_w_in[0]
    q, k, v, g, la = _gla_inproj(
        xt, norm_mix[0:1], w_in[:, :n_main].astype(BF16), w_in[:, n_main:].astype(BF16),
        gla_w_gate2[0], gla_b_gate[0:1], tm=512)
    o = _gla(q, k, v, g, la, gla_out_gain[0:1], batch, seq, blk=256)
    xt = _proj_residual(xt, o, gla_w_out[0].astype(BF16), tm=512)
    xt = _ffn_residual(xt, norm_ffn[0:1], ffn_w_gate[0].astype(BF16), ffn_w_up[0].astype(BF16),
                       ffn_w_down[0].astype(BF16), tm=1024, tf=512)

    q, k, v = _moba_qkv(xt, norm_mix[1:2], moba_w_qkv[0].astype(BF16), moba_q_gain[0:1],
                        moba_k_gain[0:1], tm=512)
    o = _moba(q, k, v, batch, seq)
    xt = _proj_residual(xt, o, moba_w_out[0].astype(BF16), tm=512)
    w_router = jnp.pad(moe_w_router[0], ((0, 0), (0, LANES - N_EXPERTS)))
    h, info, cnt = _router(xt, norm_ffn[1:2], w_router, tr=1024)
    plan = _moe_plan(info, cnt, t)
    n_rows = _moe_num_tiles(t) * MOE_TM
    hs, gs = _moe_gather(plan, h, info[:, 4], info[:, 5], n_rows)
    y = _moe_experts(plan, hs, gs, moe_w_gate[0].astype(BF16), moe_w_up[0].astype(BF16),
                     moe_w_down[0].astype(BF16), tf=1792, chunk=896)
    xt = _moe_combine(plan, xt, y)
    return xt.reshape(batch, seq, d)
```

```python
import functools

import jax
import jax.numpy as jnp
from jax import lax
from jax.experimental import pallas as pl
from jax.experimental.pallas import tpu as pltpu

F32 = jnp.float32
BF16 = jnp.bfloat16
I32 = jnp.int32

EPS = 1e-6
NEG_INF = -1e30

D_MODEL = 1024
GLA_HEADS = 4
GLA_DK = 512
GLA_DV = 1024
GLA_HK = 128
GLA_HV = 256
GLA_RANK = 16
GLA_NORMALIZER = 16.0
GLA_CHUNK = 64

MOBA_HEADS = 8
MOBA_HD = 128
MOBA_BLOCK = 256
MOBA_TOPK = 3
MOBA_Q_SCALE = MOBA_HD ** -0.5 * 1.4426950408889634

N_EXPERTS = 8
TOP_K = 2

LANES = 128
BF16_ROWS = 16

MOE_TM = 512
MOE_TG = 256
TOK_BLK = LANES
GATHER_WB = 10
COMB_TS = 256
COMB_WC = 128
COMB_GROUP = 8
COMB_MAXW = 16


def _params(semantics, vmem_bytes=None):
    return pltpu.CompilerParams(dimension_semantics=semantics, vmem_limit_bytes=vmem_bytes)


def _rms(x, gain):
    y = x * lax.rsqrt(jnp.mean(x * x, axis=-1, keepdims=True) + EPS)
    return y * gain


def _dot(a, b):
    return jnp.dot(a, b, preferred_element_type=F32)


def _dot_nt(a, b):
    return lax.dot_general(a, b, (((1,), (1,)), ((), ())), preferred_element_type=F32)


def _dot_tn(a, b):
    return lax.dot_general(a, b, (((0,), (0,)), ((), ())), preferred_element_type=F32)


def _split(a):
    hi = a.astype(BF16)
    lo = (a - hi.astype(F32)).astype(BF16)
    return hi, lo


def _log_sigmoid(z):
    return jnp.minimum(z, 0.0) - jnp.log1p(jnp.exp(-jnp.abs(z)))


def _silu(z):
    return z / (1.0 + jnp.exp(-z))


def _gla_inproj_kernel(x_ref, gain_ref, w_ref, wlr_ref, wg2_ref, b_ref,
                       q_ref, k_ref, v_ref, g_ref, la_ref):
    h = _rms(x_ref[...], gain_ref[...]).astype(BF16)
    q_ref[...] = _dot(h, w_ref[:, 0:GLA_DK]).astype(BF16)
    k_ref[...] = _dot(h, w_ref[:, GLA_DK:2 * GLA_DK]).astype(BF16)
    v_ref[...] = _dot(h, w_ref[:, 2 * GLA_DK:2 * GLA_DK + GLA_DV]).astype(BF16)
    g_ref[...] = _dot(h, w_ref[:, 2 * GLA_DK + GLA_DV:]).astype(BF16)
    a_lr = _dot(h, wlr_ref[...])
    a_hi, a_lo = _split(a_lr)
    w_hi, w_lo = _split(wg2_ref[...])
    z = _dot(a_hi, w_hi) + _dot(a_lo, w_hi) + _dot(a_hi, w_lo) + b_ref[...]
    la_ref[...] = _log_sigmoid(z) / GLA_NORMALIZER


def _gla_inproj(x, gain, w_main, w_lr, w_gate2, b_gate, tm):
    t = x.shape[0]
    n_main = w_main.shape[1]
    row = lambda i: (i, 0)
    fixed = lambda i: (0, 0)
    return pl.pallas_call(
        _gla_inproj_kernel,
        grid=(t // tm,),
        in_specs=[
            pl.BlockSpec((tm, D_MODEL), row),
            pl.BlockSpec((1, D_MODEL), fixed),
            pl.BlockSpec((D_MODEL, n_main), fixed),
            pl.BlockSpec((D_MODEL, GLA_RANK), fixed),
            pl.BlockSpec((GLA_RANK, GLA_DK), fixed),
            pl.BlockSpec((1, GLA_DK), fixed),
        ],
        out_specs=[
            pl.BlockSpec((tm, GLA_DK), row),
            pl.BlockSpec((tm, GLA_DK), row),
            pl.BlockSpec((tm, GLA_DV), row),
            pl.BlockSpec((tm, GLA_DV), row),
            pl.BlockSpec((tm, GLA_DK), row),
        ],
        out_shape=[
            jax.ShapeDtypeStruct((t, GLA_DK), BF16),
            jax.ShapeDtypeStruct((t, GLA_DK), BF16),
            jax.ShapeDtypeStruct((t, GLA_DV), BF16),
            jax.ShapeDtypeStruct((t, GLA_DV), BF16),
            jax.ShapeDtypeStruct((t, GLA_DK), F32),
        ],
        compiler_params=_params(("parallel",), 48 * 1024 * 1024),
        name="gla_inproj",
    )(x, gain, w_main, w_lr, w_gate2, b_gate)


def _gla_kernel(q_ref, k_ref, v_ref, g_ref, la_ref, og_ref, o_ref, st_ref, *, n_chunks):
    @pl.when(pl.program_id(1) == 0)
    def _():
        st_ref[...] = jnp.zeros_like(st_ref)

    blk = n_chunks * GLA_CHUNK
    r = lax.broadcasted_iota(I32, (blk, blk), 0)
    c = lax.broadcasted_iota(I32, (blk, blk), 1)
    causal = (r >= c) & (r // GLA_CHUNK == c // GLA_CHUNK)
    tril = causal.astype(BF16)
    la = la_ref[...]
    la_hi, la_mid = _split(la)
    la_lo = (la - la_hi.astype(F32) - la_mid.astype(F32)).astype(BF16)
    cum = _dot(tril, la_hi) + _dot(tril, la_mid) + _dot(tril, la_lo)
    ends = [cum[(ci + 1) * GLA_CHUNK - 1:(ci + 1) * GLA_CHUNK, :] for ci in range(n_chunks)]
    last = jnp.concatenate([jnp.broadcast_to(e, (GLA_CHUNK, GLA_DK)) for e in ends], axis=0)
    q = q_ref[...].astype(F32) * (GLA_HK ** -0.5)
    k = k_ref[...].astype(F32)
    q_dec = (q * jnp.exp(cum)).astype(BF16)
    k_inv = (k * jnp.exp(-cum)).astype(BF16)
    k_end = (k * jnp.exp(last - cum)).astype(BF16)
    chunk_decay = [jnp.exp(e) for e in ends]
    og = og_ref[...]
    heads = range(GLA_HEADS)
    hk = [slice(h * GLA_HK, (h + 1) * GLA_HK) for h in heads]
    hv = [slice(h * GLA_HV, (h + 1) * GLA_HV) for h in heads]
    scores = [jnp.where(causal, _dot_nt(q_dec[:, hk[h]], k_inv[:, hk[h]]), 0.0).astype(BF16) for h in heads]
    state_in = [[_dot_tn(v_ref[ci * GLA_CHUNK:(ci + 1) * GLA_CHUNK, hv[h]],
                         k_end[ci * GLA_CHUNK:(ci + 1) * GLA_CHUNK, hk[h]])
                 for ci in range(n_chunks)] for h in heads]
    intra = [_dot(scores[h], v_ref[:, hv[h]]) for h in heads]
    for h in heads:
        st = st_ref[h]
        inter = []
        for ci in range(n_chunks):
            rows = slice(ci * GLA_CHUNK, (ci + 1) * GLA_CHUNK)
            inter.append(_dot_nt(q_dec[rows, hk[h]], st.astype(BF16)))
            st = st * chunk_decay[ci][:, hk[h]] + state_in[h][ci]
        st_ref[h] = st
        o = intra[h] + jnp.concatenate(inter, axis=0)
        o = o * lax.rsqrt(jnp.mean(o * o, axis=-1, keepdims=True) + EPS) * og
        o = o * _silu(g_ref[:, hv[h]].astype(F32))
        o_ref[:, hv[h]] = o.astype(BF16)


def _gla(q, k, v, g, la, out_gain, batch, seq, blk):
    t = q.shape[0]
    nb = seq // blk
    row = lambda b, i: (b * nb + i, 0)
    return pl.pallas_call(
        functools.partial(_gla_kernel, n_chunks=blk // GLA_CHUNK),
        grid=(batch, nb),
        in_specs=[
            pl.BlockSpec((blk, GLA_DK), row),
            pl.BlockSpec((blk, GLA_DK), row),
            pl.BlockSpec((blk, GLA_DV), row),
            pl.BlockSpec((blk, GLA_DV), row),
            pl.BlockSpec((blk, GLA_DK), row),
            pl.BlockSpec((1, GLA_HV), lambda b, i: (0, 0)),
        ],
        out_specs=pl.BlockSpec((blk, GLA_DV), row),
        out_shape=jax.ShapeDtypeStruct((t, GLA_DV), BF16),
        scratch_shapes=[pltpu.VMEM((GLA_HEADS, GLA_HV, GLA_HK), F32)],
        compiler_params=_params(("parallel", "arbitrary")),
        name="gla_core",
    )(q, k, v, g, la, out_gain)


def _proj_residual_kernel(x_ref, o_ref, w_ref, y_ref):
    y_ref[...] = x_ref[...] + _dot(o_ref[...], w_ref[...])


def _proj_residual(x, o, w, tm):
    t, d = x.shape
    kdim = o.shape[1]
    row = lambda i: (i, 0)
    return pl.pallas_call(
        _proj_residual_kernel,
        grid=(t // tm,),
        in_specs=[
            pl.BlockSpec((tm, d), row),
            pl.BlockSpec((tm, kdim), row),
            pl.BlockSpec((kdim, d), lambda i: (0, 0)),
        ],
        out_specs=pl.BlockSpec((tm, d), row),
        out_shape=jax.ShapeDtypeStruct((t, d), F32),
        compiler_params=_params(("parallel",)),
        name="proj_residual",
    )(x, o, w)


def _ffn_kernel(x_ref, gain_ref, wg_ref, wu_ref, wd_ref, y_ref, h_ref):
    @pl.when(pl.program_id(1) == 0)
    def _():
        x = x_ref[...]
        h_ref[...] = _rms(x, gain_ref[...]).astype(BF16)
        y_ref[...] = x

    h = h_ref[...]
    a = _silu(_dot(h, wg_ref[...])) * _dot(h, wu_ref[...])
    y_ref[...] += _dot(a.astype(BF16), wd_ref[...])


def _ffn_residual(x, gain, w_gate, w_up, w_down, tm, tf):
    t, d = x.shape
    ff = w_gate.shape[1]
    return pl.pallas_call(
        _ffn_kernel,
        grid=(t // tm, ff // tf),
        in_specs=[
            pl.BlockSpec((tm, d), lambda i, j: (i, 0)),
            pl.BlockSpec((1, d), lambda i, j: (0, 0)),
            pl.BlockSpec((d, tf), lambda i, j: (0, j)),
            pl.BlockSpec((d, tf), lambda i, j: (0, j)),
            pl.BlockSpec((tf, d), lambda i, j: (j, 0)),
        ],
        out_specs=pl.BlockSpec((tm, d), lambda i, j: (i, 0)),
        out_shape=jax.ShapeDtypeStruct((t, d), F32),
        scratch_shapes=[pltpu.VMEM((tm, d), BF16)],
        compiler_params=_params(("parallel", "arbitrary"), 48 * 1024 * 1024),
        name="ffn_swiglu",
    )(x, gain, w_gate, w_up, w_down)


def _moba_qkv_kernel(x_ref, gain_ref, w_ref, qg_ref, kg_ref, q_ref, k_ref, v_ref):
    h = _rms(x_ref[...], gain_ref[...]).astype(BF16)
    d = MOBA_HEADS * MOBA_HD
    qg = qg_ref[...]
    kg = kg_ref[...]
    q = _dot(h, w_ref[:, 0:d])
    k = _dot(h, w_ref[:, d:2 * d])
    for hd in range(MOBA_HEADS):
        cols = slice(hd * MOBA_HD, (hd + 1) * MOBA_HD)
        q_ref[:, cols] = (_rms(q[:, cols], qg) * MOBA_Q_SCALE).astype(BF16)
        k_ref[:, cols] = _rms(k[:, cols], kg).astype(BF16)
    v_ref[...] = _dot(h, w_ref[:, 2 * d:]).astype(BF16)


def _moba_qkv(x, gain, w, q_gain, k_gain, tm):
    t = x.shape[0]
    d = MOBA_HEADS * MOBA_HD
    row = lambda i: (i, 0)
    fixed = lambda i: (0, 0)
    return pl.pallas_call(
        _moba_qkv_kernel,
        grid=(t // tm,),
        in_specs=[
            pl.BlockSpec((tm, D_MODEL), row),
            pl.BlockSpec((1, D_MODEL), fixed),
            pl.BlockSpec((D_MODEL, 3 * d), fixed),
            pl.BlockSpec((1, MOBA_HD), fixed),
            pl.BlockSpec((1, MOBA_HD), fixed),
        ],
        out_specs=[pl.BlockSpec((tm, d), row)] * 3,
        out_shape=[jax.ShapeDtypeStruct((t, d), BF16)] * 3,
        compiler_params=_params(("parallel",), 48 * 1024 * 1024),
        name="moba_qkv",
    )(x, gain, w, q_gain, k_gain)


def _reduce_rows(x, op, stop=8):
    while x.shape[0] > stop:
        half = x.shape[0] // 2
        x = op(x[:half], x[half:])
    return x


def _moba_kernel(q_ref, k_ref, v_ref, o_ref, *, n_blocks):
    seq = n_blocks * MOBA_BLOCK
    n_pad = 16
    v_t = v_ref[...].astype(F32).T.astype(BF16)
    blk_of_key = lax.broadcasted_iota(I32, (n_pad, seq), 1) // MOBA_BLOCK
    blk_row = lax.broadcasted_iota(I32, (n_pad, seq), 0)
    indicator = jnp.where(blk_of_key == blk_row, 1.0 / MOBA_BLOCK, 0.0).astype(BF16)
    k_mean = _dot(indicator, k_ref[...])
    km_hi, km_lo = _split(k_mean)

    blk_id = lax.broadcasted_iota(I32, (n_pad, MOBA_BLOCK), 0)
    key_pos = lax.broadcasted_iota(I32, (MOBA_BLOCK, MOBA_BLOCK), 0)
    query_pos = lax.broadcasted_iota(I32, (MOBA_BLOCK, MOBA_BLOCK), 1)
    causal = key_pos <= query_pos

    def masked_scores(i):
        rows = slice(i * MOBA_BLOCK, (i + 1) * MOBA_BLOCK)
        n_keys = (i + 1) * MOBA_BLOCK
        q = q_ref[rows, :]
        s = _dot_nt(k_ref[0:n_keys, :], q)
        selected = None
        if i > MOBA_TOPK:
            gate = _dot_nt(km_hi, q) + _dot_nt(km_lo, q)
            gate = jnp.where(blk_id < i, gate, NEG_INF)
            rank = jnp.zeros_like(gate)
            for jp in range(i):
                row = gate[jp:jp + 1, :]
                beats = (row > gate) | ((row == gate) & (jp < blk_id))
                rank = rank + beats.astype(F32)
            selected = jnp.where((rank < MOBA_TOPK) & (blk_id < i), 1.0, 0.0)
        pieces = []
        for j in range(i + 1):
            sj = s[j * MOBA_BLOCK:(j + 1) * MOBA_BLOCK, :]
            if j == i:
                sj = jnp.where(causal, sj, NEG_INF)
            elif selected is not None:
                sj = jnp.where(selected[j:j + 1, :] > 0.5, sj, NEG_INF)
            pieces.append(sj)
        return pieces

    upcoming = masked_scores(0)
    for i in range(n_blocks):
        rows = slice(i * MOBA_BLOCK, (i + 1) * MOBA_BLOCK)
        n_keys = (i + 1) * MOBA_BLOCK
        pieces = upcoming
        if i + 1 < n_blocks:
            upcoming = masked_scores(i + 1)
        m = _reduce_rows(functools.reduce(jnp.maximum, pieces), jnp.maximum).max(axis=0, keepdims=True)
        probs = [jnp.exp2(sj - m) for sj in pieces]
        denom = _reduce_rows(functools.reduce(jnp.add, probs), jnp.add).sum(axis=0, keepdims=True)
        p = probs[0] if len(probs) == 1 else jnp.concatenate(probs, axis=0)
        o_t = _dot(v_t[:, 0:n_keys], p.astype(BF16))
        o_ref[rows, :] = (o_t / denom).T.astype(BF16)


def _moba(q, k, v, batch, seq):
    t, d = q.shape
    spec = pl.BlockSpec((seq, MOBA_HD), lambda b, h: (b, h))
    return pl.pallas_call(
        functools.partial(_moba_kernel, n_blocks=seq // MOBA_BLOCK),
        grid=(batch, MOBA_HEADS),
        in_specs=[spec, spec, spec],
        out_specs=spec,
        out_shape=jax.ShapeDtypeStruct((t, d), BF16),
        compiler_params=_params(("parallel", "parallel"), 48 * 1024 * 1024),
        name="moba_attn",
    )(q, k, v)


def _router_kernel(x_ref, gain_ref, w_ref, h_ref, info_ref, cnt_ref, carry_ref, tri_ref, *, region_rows):
    tr = x_ref.shape[0]

    @pl.when(pl.program_id(0) == 0)
    def _():
        carry_ref[...] = jnp.zeros_like(carry_ref)
        r = lax.broadcasted_iota(I32, (tr, tr), 0)
        c = lax.broadcasted_iota(I32, (tr, tr), 1)
        tri_ref[...] = (c < r).astype(BF16)

    h = _rms(x_ref[...], gain_ref[...])
    h_hi, h_lo = _split(h)
    h_ref[...] = h_hi
    both = _dot(h_hi, w_ref[...])
    logits = both[:, :LANES] + both[:, LANES:] + _dot(h_lo, w_ref[:, :LANES])
    e = lax.broadcasted_iota(I32, logits.shape, 1)
    logits = jnp.where(e < N_EXPERTS, logits, -jnp.inf)
    m1 = logits.max(axis=-1, keepdims=True)
    i1 = jnp.where(logits == m1, e, LANES).min(axis=-1, keepdims=True)
    first = e == i1
    rest = jnp.where(first, -jnp.inf, logits)
    m2 = rest.max(axis=-1, keepdims=True)
    i2 = jnp.where(rest == m2, e, LANES).min(axis=-1, keepdims=True)
    second = e == i2
    e2 = jnp.exp(m2 - m1)
    denom = 1.0 + e2
    sel = jnp.where(first | second, 1.0, 0.0).astype(BF16)
    before = _dot(tri_ref[...], sel) + carry_ref[...]
    rank1 = jnp.where(first, before, 0.0).sum(axis=-1, keepdims=True)
    rank2 = jnp.where(second, before, 0.0).sum(axis=-1, keepdims=True)
    nb = tr // TOK_BLK
    blk_row = lax.broadcasted_iota(I32, (nb, tr), 0)
    blk_tok = lax.broadcasted_iota(I32, (nb, tr), 1) // TOK_BLK
    cnt = _dot((blk_row == blk_tok).astype(BF16), sel)
    cnt_ref[...] = cnt
    carry_ref[...] += cnt.sum(axis=0, keepdims=True)
    col = lax.broadcasted_iota(I32, (tr, 4), 1)
    fields = (i1.astype(F32) * region_rows + rank1, i2.astype(F32) * region_rows + rank2,
              1.0 / denom, e2 / denom)
    info = jnp.zeros((tr, 4), F32)
    for n, f in enumerate(fields):
        info = jnp.where(col == n, f, info)
    info_ref[...] = info


def _router(x, gain, w_router, tr):
    t = x.shape[0]
    nb = tr // TOK_BLK
    pad = ((0, 0), (0, LANES - N_EXPERTS))
    w_hi = w_router.astype(BF16)
    w_lo = (w_router - w_hi.astype(F32)).astype(BF16)
    w_both = jnp.concatenate([jnp.pad(w_hi, pad), jnp.pad(w_lo, pad)], axis=1)
    return pl.pallas_call(
        functools.partial(_router_kernel, region_rows=float(t)),
        grid=(t // tr,),
        in_specs=[
            pl.BlockSpec((tr, D_MODEL), lambda i: (i, 0)),
            pl.BlockSpec((1, D_MODEL), lambda i: (0, 0)),
            pl.BlockSpec((D_MODEL, 2 * LANES), lambda i: (0, 0)),
        ],
        out_specs=[
            pl.BlockSpec((tr, D_MODEL), lambda i: (i, 0)),
            pl.BlockSpec((tr, 4), lambda i: (i, 0)),
            pl.BlockSpec((nb, LANES), lambda i: (i, 0)),
        ],
        out_shape=[
            jax.ShapeDtypeStruct((t, D_MODEL), BF16),
            jax.ShapeDtypeStruct((t, 4), F32),
            jax.ShapeDtypeStruct((t // TOK_BLK, LANES), F32),
        ],
        scratch_shapes=[pltpu.VMEM((1, LANES), F32), pltpu.VMEM((tr, tr), BF16)],
        compiler_params=_params(("arbitrary",), 48 * 1024 * 1024),
        name="moe_router",
    )(x, gain, w_both)


def _moe_plan(cnt, t):
    e_ids = jnp.arange(N_EXPERTS, dtype=I32)
    cnt_blk = cnt[:, :N_EXPERTS].astype(I32)
    cum_blk = jnp.cumsum(cnt_blk, axis=0)
    counts = cum_blk[-1]
    tiles_e = (counts + MOE_TM - 1) // MOE_TM
    tile_end = jnp.cumsum(tiles_e)
    tile_off = tile_end - tiles_e
    nt = TOP_K * t // MOE_TM + N_EXPERTS
    r = jnp.arange(nt, dtype=I32)
    valid = r < tile_end[-1]
    te = jnp.minimum(jnp.sum((r[:, None] >= tile_end[None, :]).astype(I32), axis=1), N_EXPERTS - 1)
    local = r - tile_off[te]
    last_e = jnp.max(jnp.where(tiles_e > 0, e_ids, 0))
    tile_expert = jnp.where(valid, te, last_e)
    shift = e_ids * t - tile_off * MOE_TM

    per = MOE_TM // MOE_TG
    rg = jnp.arange(nt * per, dtype=I32)
    r5 = rg // per
    eg = te[r5]
    k0 = local[r5] * MOE_TM + (rg % per) * MOE_TG
    k1 = jnp.minimum(k0 + MOE_TG, counts[eg])
    has = valid[r5] & (k1 > k0)
    cum_e = cum_blk.T[eg]
    s_lo = jnp.sum((cum_e <= k0[:, None]).astype(I32), axis=1)
    s_hi = jnp.sum((cum_e <= (k1 - 1)[:, None]).astype(I32), axis=1)
    g_nwin = jnp.where(has, (s_hi - s_lo) // GATHER_WB + 1, 0)
    g_slo = jnp.where(has, s_lo, 0)
    g_code = eg * t + k0

    nts = t // COMB_TS
    cnt_ts = cnt_blk.reshape(nts, COMB_TS // TOK_BLK, N_EXPERTS).sum(axis=1)
    seg_lo = (tile_off * MOE_TM)[None, :] + jnp.cumsum(cnt_ts, axis=0) - cnt_ts
    seg_hi = seg_lo + cnt_ts
    ws0 = (seg_lo // BF16_ROWS) * BF16_ROWS
    nw = jnp.where(cnt_ts > 0, (seg_hi - ws0 + COMB_WC - 1) // COMB_WC, 0)
    nw_end = jnp.cumsum(nw, axis=1)
    k = jnp.arange(COMB_MAXW, dtype=I32)
    ek = jnp.minimum(jnp.sum((k[None, :, None] >= nw_end[:, None, :]).astype(I32), axis=2), N_EXPERTS - 1)
    pick = lambda a: jnp.take_along_axis(a, ek, axis=1)
    live = k[None, :] < nw_end[:, -1:]
    nominal = pick(ws0) + (k[None, :] - pick(nw_end - nw)) * COMB_WC
    c_start = jnp.minimum(nominal, (tile_end * MOE_TM)[ek] - COMB_WC)
    c_off = shift[ek] + c_start
    c_lo = jnp.maximum(pick(seg_lo), nominal) + shift[ek]
    c_hi = jnp.minimum(pick(seg_hi), nominal + COMB_WC) + shift[ek]
    flat = lambda a: jnp.where(live, a, 0).reshape(-1)
    return dict(
        tile_expert=tile_expert, tile_valid=valid.astype(I32),
        g_code=g_code, g_slo=g_slo, g_nwin=g_nwin,
        c_start=flat(c_start), c_off=flat(c_off), c_lo=flat(c_lo), c_hi=flat(c_hi), c_nwin=nw_end[:, -1])


def _moe_gather_kernel(code_ref, slo_ref, nwin_ref, rows_ref, h_ref, hs_ref, gs_ref):
    r = pl.program_id(0)
    n_blk = rows_ref.shape[1]
    row_id = (code_ref[r] + lax.broadcasted_iota(I32, (MOE_TG, TOK_BLK), 0)).astype(F32)
    hs_ref[...] = jnp.zeros_like(hs_ref)
    gs_ref[...] = jnp.zeros_like(gs_ref)

    def body(k, carry):
        nominal = slo_ref[r] + k * GATHER_WB
        sb = jnp.minimum(nominal, n_blk - GATHER_WB)
        fresh = sb + lax.broadcasted_iota(I32, (GATHER_WB, TOK_BLK), 0) >= nominal
        p1 = jnp.where(fresh, rows_ref[0, pl.ds(sb, GATHER_WB), :], -1.0)
        p2 = jnp.where(fresh, rows_ref[1, pl.ds(sb, GATHER_WB), :], -1.0)
        w1 = rows_ref[2, pl.ds(sb, GATHER_WB), :]
        w2 = rows_ref[3, pl.ds(sb, GATHER_WB), :]
        pieces = []
        g = jnp.zeros((MOE_TG, TOK_BLK), F32)
        for a in range(GATHER_WB):
            m1 = p1[a:a + 1, :] == row_id
            m2 = p2[a:a + 1, :] == row_id
            pieces.append(jnp.where(m1 | m2, 1.0, 0.0).astype(BF16))
            g = g + jnp.where(m1, w1[a:a + 1, :], 0.0) + jnp.where(m2, w2[a:a + 1, :], 0.0)
        onehot = jnp.concatenate(pieces, axis=1)
        tok0 = pl.multiple_of(sb * TOK_BLK, TOK_BLK)
        hs_ref[...] += _dot(onehot, h_ref[pl.ds(tok0, GATHER_WB * TOK_BLK), :]).astype(BF16)
        gs_ref[...] += g.sum(axis=-1, keepdims=True)
        return carry

    lax.fori_loop(0, nwin_ref[r], body, 0)


def _moe_gather(plan, h, rows):
    t = h.shape[0]
    n_tiles = plan["g_code"].shape[0]
    n_rows = n_tiles * MOE_TG
    n_blk = t // TOK_BLK
    grid_spec = pltpu.PrefetchScalarGridSpec(
        num_scalar_prefetch=3,
        grid=(n_tiles,),
        in_specs=[
            pl.BlockSpec((4, n_blk, TOK_BLK), lambda r, *_: (0, 0, 0), pipeline_mode=pl.Buffered(1)),
            pl.BlockSpec((t, D_MODEL), lambda r, *_: (0, 0), pipeline_mode=pl.Buffered(1)),
        ],
        out_specs=[
            pl.BlockSpec((MOE_TG, D_MODEL), lambda r, *_: (r, 0)),
            pl.BlockSpec((MOE_TG, 1), lambda r, *_: (r, 0)),
        ],
    )
    return pl.pallas_call(
        _moe_gather_kernel,
        grid_spec=grid_spec,
        out_shape=[
            jax.ShapeDtypeStruct((n_rows, D_MODEL), BF16),
            jax.ShapeDtypeStruct((n_rows, 1), F32),
        ],
        compiler_params=_params(("parallel",), 52 * 1024 * 1024),
        name="moe_gather",
    )(plan["g_code"], plan["g_slo"], plan["g_nwin"], rows, h)


def _moe_expert_kernel(te_ref, tv_ref, hs_ref, gs_ref, wg_ref, wu_ref, wd_ref, y_ref, acc_ref, *, nj, chunk):
    r = pl.program_id(0)
    j = pl.program_id(1)
    valid = tv_ref[r] > 0
    tf = wg_ref.shape[1]

    @pl.when(valid)
    def _():
        h = hs_ref[...]
        part = None
        for c0 in range(0, tf, chunk):
            cols = slice(c0, c0 + chunk)
            a = _silu(_dot(h, wg_ref[:, cols])) * _dot(h, wu_ref[:, cols])
            d = _dot(a.astype(BF16), wd_ref[cols, :])
            part = d if part is None else part + d
        if nj == 1:
            y_ref[...] = (part * gs_ref[...]).astype(BF16)
        else:
            @pl.when(j == 0)
            def _():
                acc_ref[...] = part

            @pl.when((j > 0) & (j < nj - 1))
            def _():
                acc_ref[...] += part

            @pl.when(j == nj - 1)
            def _():
                y_ref[...] = ((acc_ref[...] + part) * gs_ref[...]).astype(BF16)

    @pl.when(jnp.logical_not(valid) & (j == nj - 1))
    def _():
        y_ref[...] = jnp.zeros_like(y_ref)


def _moe_experts(plan, hs, gs, w_gate, w_up, w_down, tf, chunk):
    n_rows, d = hs.shape
    ff = w_gate.shape[2]
    nj = ff // tf
    jj = lambda r, j, tv: jnp.where(tv[r] > 0, j, nj - 1)
    grid_spec = pltpu.PrefetchScalarGridSpec(
        num_scalar_prefetch=2,
        grid=(n_rows // MOE_TM, nj),
        in_specs=[
            pl.BlockSpec((MOE_TM, d), lambda r, j, te, tv: (r, 0)),
            pl.BlockSpec((MOE_TM, 1), lambda r, j, te, tv: (r, 0)),
            pl.BlockSpec((None, d, tf), lambda r, j, te, tv: (te[r], 0, jj(r, j, tv))),
            pl.BlockSpec((None, d, tf), lambda r, j, te, tv: (te[r], 0, jj(r, j, tv))),
            pl.BlockSpec((None, tf, d), lambda r, j, te, tv: (te[r], jj(r, j, tv), 0)),
        ],
        out_specs=pl.BlockSpec((MOE_TM, d), lambda r, j, te, tv: (r, 0)),
        scratch_shapes=[pltpu.VMEM((MOE_TM, d), F32)],
    )
    return pl.pallas_call(
        functools.partial(_moe_expert_kernel, nj=nj, chunk=chunk),
        grid_spec=grid_spec,
        out_shape=jax.ShapeDtypeStruct((n_rows, d), BF16),
        compiler_params=_params(("arbitrary", "arbitrary"), 56 * 1024 * 1024),
        name="moe_experts",
    )(plan["tile_expert"], plan["tile_valid"], hs, gs, w_gate, w_up, w_down)


def _moe_combine_kernel(start_ref, off_ref, lo_ref, hi_ref, nwin_ref, x_ref, info_ref, y_hbm, o_ref,
                        ybuf, ybuf_more, sem, sem_more):
    i = pl.program_id(0)
    n_tiles = pl.num_programs(0)

    def window_copy(idx, dst, dsem):
        start = pl.multiple_of(start_ref[idx], BF16_ROWS)
        return pltpu.make_async_copy(y_hbm.at[pl.ds(start, COMB_WC), :], dst, dsem)

    def first_group(tile, wait):
        buf = tile % 2
        for s in range(COMB_GROUP):
            @pl.when(s < nwin_ref[tile])
            def _():
                cp = window_copy(tile * COMB_MAXW + s, ybuf.at[buf, pl.ds(s * COMB_WC, COMB_WC), :],
                                 sem.at[buf, s])
                cp.wait() if wait else cp.start()

    @pl.when(i == 0)
    def _():
        ybuf[...] = jnp.zeros_like(ybuf)
        ybuf_more[...] = jnp.zeros_like(ybuf_more)
        first_group(0, wait=False)

    @pl.when(i + 1 < n_tiles)
    def _():
        first_group(i + 1, wait=False)

    first_group(i, wait=True)

    code1 = info_ref[:, 0:1].astype(I32)
    code2 = info_ref[:, 1:2].astype(I32)
    lane = lax.broadcasted_iota(I32, (COMB_TS, COMB_WC), 1)

    def onehot_group(g):
        pieces = []
        for s in range(COMB_GROUP):
            idx = i * COMB_MAXW + g * COMB_GROUP + s
            off = off_ref[idx]
            lo = lo_ref[idx]
            hi = hi_ref[idx]
            q1 = jnp.where((code1 >= lo) & (code1 < hi), code1 - off, -1)
            q2 = jnp.where((code2 >= lo) & (code2 < hi), code2 - off, -1)
            pieces.append(jnp.where((q1 == lane) | (q2 == lane), 1.0, 0.0).astype(BF16))
        return jnp.concatenate(pieces, axis=1)

    o_ref[...] = x_ref[...] + _dot(onehot_group(0), ybuf[i % 2])

    n = nwin_ref[i]

    def more(g, carry):
        for wait in (False, True):
            for s in range(COMB_GROUP):
                @pl.when(g * COMB_GROUP + s < n)
                def _():
                    cp = window_copy(i * COMB_MAXW + g * COMB_GROUP + s,
                                     ybuf_more.at[pl.ds(s * COMB_WC, COMB_WC), :], sem_more.at[s])
                    cp.wait() if wait else cp.start()
        o_ref[...] += _dot(onehot_group(g), ybuf_more[...])
        return carry

    lax.fori_loop(1, (n + COMB_GROUP - 1) // COMB_GROUP, more, 0)


def _moe_combine(plan, x, info, y):
    t, d = x.shape
    width = COMB_GROUP * COMB_WC
    grid_spec = pltpu.PrefetchScalarGridSpec(
        num_scalar_prefetch=5,
        grid=(t // COMB_TS,),
        in_specs=[
            pl.BlockSpec((COMB_TS, d), lambda i, *_: (i, 0)),
            pl.BlockSpec((COMB_TS, 4), lambda i, *_: (i, 0)),
            pl.BlockSpec(memory_space=pl.ANY),
        ],
        out_specs=pl.BlockSpec((COMB_TS, d), lambda i, *_: (i, 0)),
        scratch_shapes=[
            pltpu.VMEM((2, width, d), BF16),
            pltpu.VMEM((width, d), BF16),
            pltpu.SemaphoreType.DMA((2, COMB_GROUP)),
            pltpu.SemaphoreType.DMA((COMB_GROUP,)),
        ],
    )
    return pl.pallas_call(
        _moe_combine_kernel,
        grid_spec=grid_spec,
        out_shape=jax.ShapeDtypeStruct((t, d), F32),
        compiler_params=_params(("arbitrary",)),
        name="moe_combine",
    )(plan["c_start"], plan["c_off"], plan["c_lo"], plan["c_hi"], plan["c_nwin"], x, info, y)


def kernel(x, norm_mix, norm_ffn, gla_w_in, gla_w_gate2, gla_b_gate, gla_out_gain, gla_w_out,
           moba_w_qkv, moba_q_gain, moba_k_gain, moba_w_out, ffn_w_gate, ffn_w_up, ffn_w_down,
           moe_w_router, moe_w_gate, moe_w_up, moe_w_down):
    batch, seq, d = x.shape
    t = batch * seq
    xt = x.reshape(t, d)

    n_main = 2 * GLA_DK + 2 * GLA_DV
    w_in = gla_w_in[0]
    q, k, v, g, la = _gla_inproj(
        xt, norm_mix[0:1], w_in[:, :n_main].astype(BF16), w_in[:, n_main:].astype(BF16),
        gla_w_gate2[0], gla_b_gate[0:1], tm=512)
    o = _gla(q, k, v, g, la, gla_out_gain[0:1], batch, seq, blk=256)
    xt = _proj_residual(xt, o, gla_w_out[0].astype(BF16), tm=512)
    xt = _ffn_residual(xt, norm_ffn[0:1], ffn_w_gate[0].astype(BF16), ffn_w_up[0].astype(BF16),
                       ffn_w_down[0].astype(BF16), tm=1024, tf=512)

    q, k, v = _moba_qkv(xt, norm_mix[1:2], moba_w_qkv[0].astype(BF16), moba_q_gain[0:1],
                        moba_k_gain[0:1], tm=512)
    o = _moba(q, k, v, batch, seq)
    xt = _proj_residual(xt, o, moba_w_out[0].astype(BF16), tm=512)
    h, info, cnt = _router(xt, norm_ffn[1:2], moe_w_router[0], tr=1024)
    plan = _moe_plan(cnt, t)
    rows = info.T.reshape(4, t // TOK_BLK, TOK_BLK)
    hs, gs = _moe_gather(plan, h, rows)
    y = _moe_experts(plan, hs, gs, moe_w_gate[0].astype(BF16), moe_w_up[0].astype(BF16),
                     moe_w_down[0].astype(BF16), tf=1792, chunk=896)
    xt = _moe_combine(plan, xt, info, y)
    return xt.reshape(batch, seq, d)
```

```python
import functools

import jax
import jax.numpy as jnp
from jax import lax
from jax.experimental import pallas as pl
from jax.experimental.pallas import tpu as pltpu

F32 = jnp.float32
BF16 = jnp.bfloat16
I32 = jnp.int32

EPS = 1e-6
NEG_INF = -1e30

D_MODEL = 1024
GLA_HEADS = 4
GLA_DK = 512
GLA_DV = 1024
GLA_HK = 128
GLA_HV = 256
GLA_RANK = 16
GLA_NORMALIZER = 16.0
GLA_CHUNK = 64

MOBA_HEADS = 8
MOBA_HD = 128
MOBA_BLOCK = 256
MOBA_TOPK = 3
MOBA_Q_SCALE = MOBA_HD ** -0.5 * 1.4426950408889634

N_EXPERTS = 8
TOP_K = 2

LANES = 128
BF16_ROWS = 16

MOE_TM = 512
MOE_TG = 128
TOK_BLK = LANES
GATHER_WB = 6
COMB_TS = 256
COMB_WC = 128
COMB_GROUP = 8
COMB_MAXW = 16


def _params(semantics, vmem_bytes=None):
    return pltpu.CompilerParams(dimension_semantics=semantics, vmem_limit_bytes=vmem_bytes)


def _rms(x, gain):
    y = x * lax.rsqrt(jnp.mean(x * x, axis=-1, keepdims=True) + EPS)
    return y * gain


def _dot(a, b):
    return jnp.dot(a, b, preferred_element_type=F32)


def _dot_nt(a, b):
    return lax.dot_general(a, b, (((1,), (1,)), ((), ())), preferred_element_type=F32)


def _dot_tn(a, b):
    return lax.dot_general(a, b, (((0,), (0,)), ((), ())), preferred_element_type=F32)


def _split(a):
    hi = a.astype(BF16)
    lo = (a - hi.astype(F32)).astype(BF16)
    return hi, lo


def _log_sigmoid(z):
    return jnp.minimum(z, 0.0) - jnp.log1p(jnp.exp(-jnp.abs(z)))


def _silu(z):
    return z / (1.0 + jnp.exp(-z))


def _ride_along_specs(arrays, n_steps, step):
    specs, shapes = [], []
    for w in arrays:
        n_e, rows, cols = w.shape
        per_e = n_steps // n_e
        if rows % (per_e * BF16_ROWS) == 0:
            blk = (None, rows // per_e, cols)
            imap = lambda *ids, per_e=per_e: (step(*ids) // per_e, step(*ids) % per_e, 0)
        else:
            assert cols % (per_e * LANES) == 0, (w.shape, n_steps)
            blk = (None, rows, cols // per_e)
            imap = lambda *ids, per_e=per_e: (step(*ids) // per_e, 0, step(*ids) % per_e)
        specs.append(pl.BlockSpec(blk, imap))
        shapes.append(jax.ShapeDtypeStruct(w.shape, BF16))
    return specs, shapes


def _gla_inproj_kernel(x_ref, gain_ref, w_ref, wlr_ref, wg2_ref, b_ref, *rest, n_cast):
    cast_in, (q_ref, k_ref, v_ref, g_ref, la_ref, *cast_out) = rest[:n_cast], rest[n_cast:]
    for src, dst in zip(cast_in, cast_out):
        dst[...] = src[...].astype(BF16)
    h = _rms(x_ref[...], gain_ref[...]).astype(BF16)
    q_ref[...] = _dot(h, w_ref[:, 0:GLA_DK]).astype(BF16)
    k_ref[...] = _dot(h, w_ref[:, GLA_DK:2 * GLA_DK]).astype(BF16)
    v_ref[...] = _dot(h, w_ref[:, 2 * GLA_DK:2 * GLA_DK + GLA_DV]).astype(BF16)
    g_ref[...] = _dot(h, w_ref[:, 2 * GLA_DK + GLA_DV:]).astype(BF16)
    a_lr = _dot(h, wlr_ref[...])
    a_hi, a_lo = _split(a_lr)
    w_hi, w_lo = _split(wg2_ref[...])
    z = _dot(a_hi, w_hi) + _dot(a_lo, w_hi) + _dot(a_hi, w_lo) + b_ref[...]
    la_ref[...] = _log_sigmoid(z) / GLA_NORMALIZER


def _gla_inproj(x, gain, w_main, w_lr, w_gate2, b_gate, ride_along, tm):
    t = x.shape[0]
    n_main = w_main.shape[1]
    row = lambda i: (i, 0)
    fixed = lambda i: (0, 0)
    cast_specs, cast_shapes = _ride_along_specs(ride_along, t // tm, lambda i: i)
    outs = pl.pallas_call(
        functools.partial(_gla_inproj_kernel, n_cast=len(ride_along)),
        grid=(t // tm,),
        in_specs=[
            pl.BlockSpec((tm, D_MODEL), row),
            pl.BlockSpec((1, D_MODEL), fixed),
            pl.BlockSpec((D_MODEL, n_main), fixed),
            pl.BlockSpec((D_MODEL, GLA_RANK), fixed),
            pl.BlockSpec((GLA_RANK, GLA_DK), fixed),
            pl.BlockSpec((1, GLA_DK), fixed),
        ] + cast_specs,
        out_specs=[
            pl.BlockSpec((tm, GLA_DK), row),
            pl.BlockSpec((tm, GLA_DK), row),
            pl.BlockSpec((tm, GLA_DV), row),
            pl.BlockSpec((tm, GLA_DV), row),
            pl.BlockSpec((tm, GLA_DK), row),
        ] + cast_specs,
        out_shape=[
            jax.ShapeDtypeStruct((t, GLA_DK), BF16),
            jax.ShapeDtypeStruct((t, GLA_DK), BF16),
            jax.ShapeDtypeStruct((t, GLA_DV), BF16),
            jax.ShapeDtypeStruct((t, GLA_DV), BF16),
            jax.ShapeDtypeStruct((t, GLA_DK), F32),
        ] + cast_shapes,
        compiler_params=_params(("arbitrary",), 48 * 1024 * 1024),
        name="gla_inproj",
    )(x, gain, w_main, w_lr, w_gate2, b_gate, *ride_along)
    return outs[:5], outs[5:]


def _gla_kernel(q_ref, k_ref, v_ref, g_ref, la_ref, og_ref, o_ref, st_ref, *, n_chunks):
    @pl.when(pl.program_id(1) == 0)
    def _():
        st_ref[...] = jnp.zeros_like(st_ref)

    blk = n_chunks * GLA_CHUNK
    r = lax.broadcasted_iota(I32, (blk, blk), 0)
    c = lax.broadcasted_iota(I32, (blk, blk), 1)
    causal = (r >= c) & (r // GLA_CHUNK == c // GLA_CHUNK)
    tril = causal.astype(BF16)
    la = la_ref[...]
    la_hi, la_mid = _split(la)
    la_lo = (la - la_hi.astype(F32) - la_mid.astype(F32)).astype(BF16)
    cum = _dot(tril, la_hi) + _dot(tril, la_mid) + _dot(tril, la_lo)
    ends = [cum[(ci + 1) * GLA_CHUNK - 1:(ci + 1) * GLA_CHUNK, :] for ci in range(n_chunks)]
    last = jnp.concatenate([jnp.broadcast_to(e, (GLA_CHUNK, GLA_DK)) for e in ends], axis=0)
    q = q_ref[...].astype(F32) * (GLA_HK ** -0.5)
    k = k_ref[...].astype(F32)
    q_dec = (q * jnp.exp(cum)).astype(BF16)
    k_inv = (k * jnp.exp(-cum)).astype(BF16)
    k_end = (k * jnp.exp(last - cum)).astype(BF16)
    chunk_decay = [jnp.exp(e) for e in ends]
    og = og_ref[...]
    heads = range(GLA_HEADS)
    hk = [slice(h * GLA_HK, (h + 1) * GLA_HK) for h in heads]
    hv = [slice(h * GLA_HV, (h + 1) * GLA_HV) for h in heads]
    scores = [jnp.where(causal, _dot_nt(q_dec[:, hk[h]], k_inv[:, hk[h]]), 0.0).astype(BF16) for h in heads]
    state_in = [[_dot_tn(v_ref[ci * GLA_CHUNK:(ci + 1) * GLA_CHUNK, hv[h]],
                         k_end[ci * GLA_CHUNK:(ci + 1) * GLA_CHUNK, hk[h]])
                 for ci in range(n_chunks)] for h in heads]
    intra = [_dot(scores[h], v_ref[:, hv[h]]) for h in heads]
    for h in heads:
        st = st_ref[h]
        inter = []
        for ci in range(n_chunks):
            rows = slice(ci * GLA_CHUNK, (ci + 1) * GLA_CHUNK)
            inter.append(_dot_nt(q_dec[rows, hk[h]], st.astype(BF16)))
            st = st * chunk_decay[ci][:, hk[h]] + state_in[h][ci]
        st_ref[h] = st
        o = intra[h] + jnp.concatenate(inter, axis=0)
        o = o * lax.rsqrt(jnp.mean(o * o, axis=-1, keepdims=True) + EPS) * og
        o = o * _silu(g_ref[:, hv[h]].astype(F32))
        o_ref[:, hv[h]] = o.astype(BF16)


def _gla(q, k, v, g, la, out_gain, batch, seq, blk):
    t = q.shape[0]
    nb = seq // blk
    row = lambda b, i: (b * nb + i, 0)
    return pl.pallas_call(
        functools.partial(_gla_kernel, n_chunks=blk // GLA_CHUNK),
        grid=(batch, nb),
        in_specs=[
            pl.BlockSpec((blk, GLA_DK), row),
            pl.BlockSpec((blk, GLA_DK), row),
            pl.BlockSpec((blk, GLA_DV), row),
            pl.BlockSpec((blk, GLA_DV), row),
            pl.BlockSpec((blk, GLA_DK), row),
            pl.BlockSpec((1, GLA_HV), lambda b, i: (0, 0)),
        ],
        out_specs=pl.BlockSpec((blk, GLA_DV), row),
        out_shape=jax.ShapeDtypeStruct((t, GLA_DV), BF16),
        scratch_shapes=[pltpu.VMEM((GLA_HEADS, GLA_HV, GLA_HK), F32)],
        compiler_params=_params(("parallel", "arbitrary")),
        name="gla_core",
    )(q, k, v, g, la, out_gain)


def _ffn_kernel(x_ref, o_ref, wo_ref, gain_ref, wg_ref, wu_ref, wd_ref, *rest, n_cast):
    cast_in, (y_ref, *cast_out), (h_ref,) = rest[:n_cast], rest[n_cast:2 * n_cast + 1], rest[2 * n_cast + 1:]

    @pl.when(pl.program_id(1) == 0)
    def _():
        x1 = x_ref[...] + _dot(o_ref[...], wo_ref[...])
        h_ref[...] = _rms(x1, gain_ref[...]).astype(BF16)
        y_ref[...] = x1

    h = h_ref[...]
    a = _silu(_dot(h, wg_ref[...])) * _dot(h, wu_ref[...])
    y_ref[...] += _dot(a.astype(BF16), wd_ref[...])
    for src, dst in zip(cast_in, cast_out):
        dst[...] = src[...].astype(BF16)


def _ffn_residual(x, o, w_out, gain, w_gate, w_up, w_down, ride_along, tm, tf):
    t, d = x.shape
    ff = w_gate.shape[1]
    ni, nj = t // tm, ff // tf
    cast_specs, cast_shapes = _ride_along_specs(ride_along, ni * nj, lambda i, j: i * nj + j)
    outs = pl.pallas_call(
        functools.partial(_ffn_kernel, n_cast=len(ride_along)),
        grid=(ni, nj),
        in_specs=[
            pl.BlockSpec((tm, d), lambda i, j: (i, 0)),
            pl.BlockSpec((tm, d), lambda i, j: (i, 0)),
            pl.BlockSpec((d, d), lambda i, j: (0, 0)),
            pl.BlockSpec((1, d), lambda i, j: (0, 0)),
            pl.BlockSpec((d, tf), lambda i, j: (0, j)),
            pl.BlockSpec((d, tf), lambda i, j: (0, j)),
            pl.BlockSpec((tf, d), lambda i, j: (j, 0)),
        ] + cast_specs,
        out_specs=[pl.BlockSpec((tm, d), lambda i, j: (i, 0))] + cast_specs,
        out_shape=[jax.ShapeDtypeStruct((t, d), F32)] + cast_shapes,
        scratch_shapes=[pltpu.VMEM((tm, d), BF16)],
        compiler_params=_params(("arbitrary", "arbitrary"), 56 * 1024 * 1024),
        name="ffn_swiglu",
    )(x, o, w_out, gain, w_gate, w_up, w_down, *ride_along)
    return outs[0], outs[1:]


def _moba_qkv_kernel(x_ref, gain_ref, w_ref, qg_ref, kg_ref, q_ref, k_ref, v_ref):
    h = _rms(x_ref[...], gain_ref[...]).astype(BF16)
    d = MOBA_HEADS * MOBA_HD
    qg = qg_ref[...]
    kg = kg_ref[...]
    q = _dot(h, w_ref[:, 0:d])
    k = _dot(h, w_ref[:, d:2 * d])
    for hd in range(MOBA_HEADS):
        cols = slice(hd * MOBA_HD, (hd + 1) * MOBA_HD)
        q_ref[:, cols] = (_rms(q[:, cols], qg) * MOBA_Q_SCALE).astype(BF16)
        k_ref[:, cols] = _rms(k[:, cols], kg).astype(BF16)
    v_ref[...] = _dot(h, w_ref[:, 2 * d:]).astype(BF16)


def _moba_qkv(x, gain, w, q_gain, k_gain, tm):
    t = x.shape[0]
    d = MOBA_HEADS * MOBA_HD
    row = lambda i: (i, 0)
    fixed = lambda i: (0, 0)
    return pl.pallas_call(
        _moba_qkv_kernel,
        grid=(t // tm,),
        in_specs=[
            pl.BlockSpec((tm, D_MODEL), row),
            pl.BlockSpec((1, D_MODEL), fixed),
            pl.BlockSpec((D_MODEL, 3 * d), fixed),
            pl.BlockSpec((1, MOBA_HD), fixed),
            pl.BlockSpec((1, MOBA_HD), fixed),
        ],
        out_specs=[pl.BlockSpec((tm, d), row)] * 3,
        out_shape=[jax.ShapeDtypeStruct((t, d), BF16)] * 3,
        compiler_params=_params(("parallel",), 48 * 1024 * 1024),
        name="moba_qkv",
    )(x, gain, w, q_gain, k_gain)


def _reduce_rows(x, op, stop=8):
    while x.shape[0] > stop:
        half = x.shape[0] // 2
        x = op(x[:half], x[half:])
    return x


def _moba_kernel(q_ref, k_ref, v_ref, o_ref, *, n_blocks):
    seq = n_blocks * MOBA_BLOCK
    n_pad = 16
    v_t = v_ref[...].astype(F32).T.astype(BF16)
    blk_of_key = lax.broadcasted_iota(I32, (n_pad, seq), 1) // MOBA_BLOCK
    blk_row = lax.broadcasted_iota(I32, (n_pad, seq), 0)
    indicator = jnp.where(blk_of_key == blk_row, 1.0 / MOBA_BLOCK, 0.0).astype(BF16)
    k_mean = _dot(indicator, k_ref[...])
    km_hi, km_lo = _split(k_mean)

    blk_id = lax.broadcasted_iota(I32, (n_pad, MOBA_BLOCK), 0)
    key_pos = lax.broadcasted_iota(I32, (MOBA_BLOCK, MOBA_BLOCK), 0)
    query_pos = lax.broadcasted_iota(I32, (MOBA_BLOCK, MOBA_BLOCK), 1)
    causal = key_pos <= query_pos

    def masked_scores(i):
        rows = slice(i * MOBA_BLOCK, (i + 1) * MOBA_BLOCK)
        n_keys = (i + 1) * MOBA_BLOCK
        q = q_ref[rows, :]
        s = _dot_nt(k_ref[0:n_keys, :], q)
        selected = None
        if i > MOBA_TOPK:
            gate = _dot_nt(km_hi, q) + _dot_nt(km_lo, q)
            gate = jnp.where(blk_id < i, gate, NEG_INF)
            rank = jnp.zeros_like(gate)
            for jp in range(i):
                row = gate[jp:jp + 1, :]
                beats = (row > gate) | ((row == gate) & (jp < blk_id))
                rank = rank + beats.astype(F32)
            selected = jnp.where((rank < MOBA_TOPK) & (blk_id < i), 1.0, 0.0)
        pieces = []
        for j in range(i + 1):
            sj = s[j * MOBA_BLOCK:(j + 1) * MOBA_BLOCK, :]
            if j == i:
                sj = jnp.where(causal, sj, NEG_INF)
            elif selected is not None:
                sj = jnp.where(selected[j:j + 1, :] > 0.5, sj, NEG_INF)
            pieces.append(sj)
        return pieces

    upcoming = masked_scores(0)
    for i in range(n_blocks):
        rows = slice(i * MOBA_BLOCK, (i + 1) * MOBA_BLOCK)
        n_keys = (i + 1) * MOBA_BLOCK
        pieces = upcoming
        if i + 1 < n_blocks:
            upcoming = masked_scores(i + 1)
        m = _reduce_rows(functools.reduce(jnp.maximum, pieces), jnp.maximum).max(axis=0, keepdims=True)
        probs = [jnp.exp2(sj - m) for sj in pieces]
        denom = _reduce_rows(functools.reduce(jnp.add, probs), jnp.add).sum(axis=0, keepdims=True)
        p = probs[0] if len(probs) == 1 else jnp.concatenate(probs, axis=0)
        o_t = _dot(v_t[:, 0:n_keys], p.astype(BF16))
        o_ref[rows, :] = (o_t / denom).T.astype(BF16)


def _moba(q, k, v, batch, seq):
    t, d = q.shape
    spec = pl.BlockSpec((seq, MOBA_HD), lambda b, h: (b, h))
    return pl.pallas_call(
        functools.partial(_moba_kernel, n_blocks=seq // MOBA_BLOCK),
        grid=(batch, MOBA_HEADS),
        in_specs=[spec, spec, spec],
        out_specs=spec,
        out_shape=jax.ShapeDtypeStruct((t, d), BF16),
        compiler_params=_params(("parallel", "parallel"), 48 * 1024 * 1024),
        name="moba_attn",
    )(q, k, v)


def _router_kernel(x_ref, o_ref, wo_ref, gain_ref, w_ref, x2_ref, h_ref, info_ref, cnt_ref, carry_ref, tri_ref,
                   *, region_rows):
    tr = x_ref.shape[0]

    @pl.when(pl.program_id(0) == 0)
    def _():
        carry_ref[...] = jnp.zeros_like(carry_ref)
        r = lax.broadcasted_iota(I32, (tr, tr), 0)
        c = lax.broadcasted_iota(I32, (tr, tr), 1)
        tri_ref[...] = (c < r).astype(BF16)

    x2 = x_ref[...] + _dot(o_ref[...], wo_ref[...])
    x2_ref[...] = x2
    h = _rms(x2, gain_ref[...])
    h_hi, h_lo = _split(h)
    h_ref[...] = h_hi
    both = _dot(h_hi, w_ref[...])
    logits = both[:, :LANES] + both[:, LANES:] + _dot(h_lo, w_ref[:, :LANES])
    e = lax.broadcasted_iota(I32, logits.shape, 1)
    logits = jnp.where(e < N_EXPERTS, logits, -jnp.inf)
    m1 = logits.max(axis=-1, keepdims=True)
    i1 = jnp.where(logits == m1, e, LANES).min(axis=-1, keepdims=True)
    first = e == i1
    rest = jnp.where(first, -jnp.inf, logits)
    m2 = rest.max(axis=-1, keepdims=True)
    i2 = jnp.where(rest == m2, e, LANES).min(axis=-1, keepdims=True)
    second = e == i2
    e2 = jnp.exp(m2 - m1)
    denom = 1.0 + e2
    sel = jnp.where(first | second, 1.0, 0.0).astype(BF16)
    before = _dot(tri_ref[...], sel) + carry_ref[...]
    rank1 = jnp.where(first, before, 0.0).sum(axis=-1, keepdims=True)
    rank2 = jnp.where(second, before, 0.0).sum(axis=-1, keepdims=True)
    nb = tr // TOK_BLK
    blk_row = lax.broadcasted_iota(I32, (nb, tr), 0)
    blk_tok = lax.broadcasted_iota(I32, (nb, tr), 1) // TOK_BLK
    cnt = _dot((blk_row == blk_tok).astype(BF16), sel)
    cnt_ref[...] = cnt
    carry_ref[...] += cnt.sum(axis=0, keepdims=True)
    col = lax.broadcasted_iota(I32, (tr, 4), 1)
    fields = (i1.astype(F32) * region_rows + rank1, i2.astype(F32) * region_rows + rank2,
              1.0 / denom, e2 / denom)
    info = jnp.zeros((tr, 4), F32)
    for n, f in enumerate(fields):
        info = jnp.where(col == n, f, info)
    info_ref[...] = info


def _router(x, o, w_out, gain, w_router, tr):
    t = x.shape[0]
    nb = tr // TOK_BLK
    pad = ((0, 0), (0, LANES - N_EXPERTS))
    w_hi = w_router.astype(BF16)
    w_lo = (w_router - w_hi.astype(F32)).astype(BF16)
    w_both = jnp.concatenate([jnp.pad(w_hi, pad), jnp.pad(w_lo, pad)], axis=1)
    row = lambda i: (i, 0)
    fixed = lambda i: (0, 0)
    return pl.pallas_call(
        functools.partial(_router_kernel, region_rows=float(t)),
        grid=(t // tr,),
        in_specs=[
            pl.BlockSpec((tr, D_MODEL), row),
            pl.BlockSpec((tr, D_MODEL), row),
            pl.BlockSpec((D_MODEL, D_MODEL), fixed),
            pl.BlockSpec((1, D_MODEL), fixed),
            pl.BlockSpec((D_MODEL, 2 * LANES), fixed),
        ],
        out_specs=[
            pl.BlockSpec((tr, D_MODEL), row),
            pl.BlockSpec((tr, D_MODEL), row),
            pl.BlockSpec((tr, 4), row),
            pl.BlockSpec((nb, LANES), row),
        ],
        out_shape=[
            jax.ShapeDtypeStruct((t, D_MODEL), F32),
            jax.ShapeDtypeStruct((t, D_MODEL), BF16),
            jax.ShapeDtypeStruct((t, 4), F32),
            jax.ShapeDtypeStruct((t // TOK_BLK, LANES), F32),
        ],
        scratch_shapes=[pltpu.VMEM((1, LANES), F32), pltpu.VMEM((tr, tr), BF16)],
        compiler_params=_params(("arbitrary",), 56 * 1024 * 1024),
        name="moe_router",
    )(x, o, w_out, gain, w_both)


def _moe_plan(cnt, t):
    e_ids = jnp.arange(N_EXPERTS, dtype=I32)
    cnt_blk = cnt[:, :N_EXPERTS].astype(I32)
    cum_blk = jnp.cumsum(cnt_blk, axis=0)
    counts = cum_blk[-1]
    tiles_e = (counts + MOE_TM - 1) // MOE_TM
    tile_end = jnp.cumsum(tiles_e)
    tile_off = tile_end - tiles_e
    nt = TOP_K * t // MOE_TM + N_EXPERTS
    r = jnp.arange(nt, dtype=I32)
    valid = r < tile_end[-1]
    te = jnp.minimum(jnp.sum((r[:, None] >= tile_end[None, :]).astype(I32), axis=1), N_EXPERTS - 1)
    local = r - tile_off[te]
    last_e = jnp.max(jnp.where(tiles_e > 0, e_ids, 0))
    tile_expert = jnp.where(valid, te, last_e)
    shift = e_ids * t - tile_off * MOE_TM

    per = MOE_TM // MOE_TG
    rg = jnp.arange(nt * per, dtype=I32)
    r5 = rg // per
    eg = te[r5]
    k0 = local[r5] * MOE_TM + (rg % per) * MOE_TG
    k1 = jnp.minimum(k0 + MOE_TG, counts[eg])
    has = valid[r5] & (k1 > k0)
    cum_e = cum_blk.T[eg]
    s_lo = jnp.sum((cum_e <= k0[:, None]).astype(I32), axis=1)
    s_hi = jnp.sum((cum_e <= (k1 - 1)[:, None]).astype(I32), axis=1)
    g_nwin = jnp.where(has, (s_hi - s_lo) // GATHER_WB + 1, 0)
    g_slo = jnp.where(has, s_lo, 0)
    g_code = eg * t + k0

    nts = t // COMB_TS
    cnt_ts = cnt_blk.reshape(nts, COMB_TS // TOK_BLK, N_EXPERTS).sum(axis=1)
    seg_lo = (tile_off * MOE_TM)[None, :] + jnp.cumsum(cnt_ts, axis=0) - cnt_ts
    seg_hi = seg_lo + cnt_ts
    ws0 = (seg_lo // BF16_ROWS) * BF16_ROWS
    nw = jnp.where(cnt_ts > 0, (seg_hi - ws0 + COMB_WC - 1) // COMB_WC, 0)
    nw_end = jnp.cumsum(nw, axis=1)
    k = jnp.arange(COMB_MAXW, dtype=I32)
    ek = jnp.minimum(jnp.sum((k[None, :, None] >= nw_end[:, None, :]).astype(I32), axis=2), N_EXPERTS - 1)
    pick = lambda a: jnp.take_along_axis(a, ek, axis=1)
    live = k[None, :] < nw_end[:, -1:]
    nominal = pick(ws0) + (k[None, :] - pick(nw_end - nw)) * COMB_WC
    c_start = jnp.minimum(nominal, (tile_end * MOE_TM)[ek] - COMB_WC)
    c_off = shift[ek] + c_start
    c_lo = jnp.maximum(pick(seg_lo), nominal) + shift[ek]
    c_hi = jnp.minimum(pick(seg_hi), nominal + COMB_WC) + shift[ek]
    flat = lambda a: jnp.where(live, a, 0).reshape(-1)
    return dict(
        tile_expert=tile_expert, tile_valid=valid.astype(I32),
        g_code=g_code, g_slo=g_slo, g_nwin=g_nwin,
        c_start=flat(c_start), c_off=flat(c_off), c_lo=flat(c_lo), c_hi=flat(c_hi), c_nwin=nw_end[:, -1])


def _moe_gather_kernel(code_ref, slo_ref, nwin_ref, rows_ref, h_ref, hs_ref, gs_ref):
    n_blk = rows_ref.shape[1]
    hs_ref[...] = jnp.zeros_like(hs_ref)
    gs_ref[...] = jnp.zeros_like(gs_ref)
    for sub in range(hs_ref.shape[0] // MOE_TG):
        _gather_subtile(pl.program_id(0) * (hs_ref.shape[0] // MOE_TG) + sub,
                        slice(sub * MOE_TG, (sub + 1) * MOE_TG),
                        code_ref, slo_ref, nwin_ref, rows_ref, h_ref, hs_ref, gs_ref, n_blk)


def _gather_subtile(r, out_rows, code_ref, slo_ref, nwin_ref, rows_ref, h_ref, hs_ref, gs_ref, n_blk):
    row_id = (code_ref[r] + lax.broadcasted_iota(I32, (MOE_TG, TOK_BLK), 0)).astype(F32)

    def body(k, carry):
        nominal = slo_ref[r] + k * GATHER_WB
        sb = jnp.minimum(nominal, n_blk - GATHER_WB)
        fresh = sb + lax.broadcasted_iota(I32, (GATHER_WB, TOK_BLK), 0) >= nominal
        p1 = jnp.where(fresh, rows_ref[0, pl.ds(sb, GATHER_WB), :], -1.0)
        p2 = jnp.where(fresh, rows_ref[1, pl.ds(sb, GATHER_WB), :], -1.0)
        w1 = rows_ref[2, pl.ds(sb, GATHER_WB), :]
        w2 = rows_ref[3, pl.ds(sb, GATHER_WB), :]
        pieces = []
        g = jnp.zeros((MOE_TG, TOK_BLK), F32)
        for a in range(GATHER_WB):
            m1 = p1[a:a + 1, :] == row_id
            m2 = p2[a:a + 1, :] == row_id
            pieces.append(jnp.where(m1 | m2, 1.0, 0.0).astype(BF16))
            g = g + jnp.where(m1, w1[a:a + 1, :], 0.0) + jnp.where(m2, w2[a:a + 1, :], 0.0)
        onehot = jnp.concatenate(pieces, axis=1)
        tok0 = pl.multiple_of(sb * TOK_BLK, TOK_BLK)
        hs_ref[out_rows, :] += _dot(onehot, h_ref[pl.ds(tok0, GATHER_WB * TOK_BLK), :]).astype(BF16)
        gs_ref[out_rows, :] += g.sum(axis=-1, keepdims=True)
        return carry

    lax.fori_loop(0, nwin_ref[r], body, 0)


def _moe_gather(plan, h, rows):
    t = h.shape[0]
    n_rows = plan["g_code"].shape[0] * MOE_TG
    n_blk = t // TOK_BLK
    grid_spec = pltpu.PrefetchScalarGridSpec(
        num_scalar_prefetch=3,
        grid=(n_rows // MOE_TM,),
        in_specs=[
            pl.BlockSpec((4, n_blk, TOK_BLK), lambda r, *_: (0, 0, 0), pipeline_mode=pl.Buffered(1)),
            pl.BlockSpec((t, D_MODEL), lambda r, *_: (0, 0), pipeline_mode=pl.Buffered(1)),
        ],
        out_specs=[
            pl.BlockSpec((MOE_TM, D_MODEL), lambda r, *_: (r, 0)),
            pl.BlockSpec((MOE_TM, 1), lambda r, *_: (r, 0)),
        ],
    )
    return pl.pallas_call(
        _moe_gather_kernel,
        grid_spec=grid_spec,
        out_shape=[
            jax.ShapeDtypeStruct((n_rows, D_MODEL), BF16),
            jax.ShapeDtypeStruct((n_rows, 1), F32),
        ],
        compiler_params=_params(("parallel",), 52 * 1024 * 1024),
        name="moe_gather",
    )(plan["g_code"], plan["g_slo"], plan["g_nwin"], rows, h)


def _moe_expert_kernel(te_ref, tv_ref, hs_ref, gs_ref, wg_ref, wu_ref, wd_ref, y_ref, acc_ref, *, nj, chunk):
    r = pl.program_id(0)
    j = pl.program_id(1)
    valid = tv_ref[r] > 0
    tf = wg_ref.shape[1]

    @pl.when(valid)
    def _():
        h = hs_ref[...]
        part = None
        for c0 in range(0, tf, chunk):
            cols = slice(c0, c0 + chunk)
            a = _silu(_dot(h, wg_ref[:, cols])) * _dot(h, wu_ref[:, cols])
            d = _dot(a.astype(BF16), wd_ref[cols, :])
            part = d if part is None else part + d
        if nj == 1:
            y_ref[...] = (part * gs_ref[...]).astype(BF16)
        else:
            @pl.when(j == 0)
            def _():
                acc_ref[...] = part

            @pl.when((j > 0) & (j < nj - 1))
            def _():
                acc_ref[...] += part

            @pl.when(j == nj - 1)
            def _():
                y_ref[...] = ((acc_ref[...] + part) * gs_ref[...]).astype(BF16)

    @pl.when(jnp.logical_not(valid) & (j == nj - 1))
    def _():
        y_ref[...] = jnp.zeros_like(y_ref)


def _moe_experts(plan, hs, gs, w_gate, w_up, w_down, tf, chunk):
    n_rows, d = hs.shape
    ff = w_gate.shape[2]
    nj = ff // tf
    jj = lambda r, j, tv: jnp.where(tv[r] > 0, j, nj - 1)
    grid_spec = pltpu.PrefetchScalarGridSpec(
        num_scalar_prefetch=2,
        grid=(n_rows // MOE_TM, nj),
        in_specs=[
            pl.BlockSpec((MOE_TM, d), lambda r, j, te, tv: (r, 0)),
            pl.BlockSpec((MOE_TM, 1), lambda r, j, te, tv: (r, 0)),
            pl.BlockSpec((None, d, tf), lambda r, j, te, tv: (te[r], 0, jj(r, j, tv))),
            pl.BlockSpec((None, d, tf), lambda r, j, te, tv: (te[r], 0, jj(r, j, tv))),
            pl.BlockSpec((None, tf, d), lambda r, j, te, tv: (te[r], jj(r, j, tv), 0)),
        ],
        out_specs=pl.BlockSpec((MOE_TM, d), lambda r, j, te, tv: (r, 0)),
        scratch_shapes=[pltpu.VMEM((MOE_TM, d), F32)],
    )
    return pl.pallas_call(
        functools.partial(_moe_expert_kernel, nj=nj, chunk=chunk),
        grid_spec=grid_spec,
        out_shape=jax.ShapeDtypeStruct((n_rows, d), BF16),
        compiler_params=_params(("arbitrary", "arbitrary"), 56 * 1024 * 1024),
        name="moe_experts",
    )(plan["tile_expert"], plan["tile_valid"], hs, gs, w_gate, w_up, w_down)


def _moe_combine_kernel(start_ref, off_ref, lo_ref, hi_ref, nwin_ref, x_ref, info_ref, y_hbm, o_ref,
                        ybuf, ybuf_more, sem, sem_more):
    i = pl.program_id(0)
    n_tiles = pl.num_programs(0)

    def window_copy(idx, dst, dsem):
        start = pl.multiple_of(start_ref[idx], BF16_ROWS)
        return pltpu.make_async_copy(y_hbm.at[pl.ds(start, COMB_WC), :], dst, dsem)

    def first_group(tile, wait):
        buf = tile % 2
        for s in range(COMB_GROUP):
            @pl.when(s < nwin_ref[tile])
            def _():
                cp = window_copy(tile * COMB_MAXW + s, ybuf.at[buf, pl.ds(s * COMB_WC, COMB_WC), :],
                                 sem.at[buf, s])
                cp.wait() if wait else cp.start()

    @pl.when(i == 0)
    def _():
        ybuf[...] = jnp.zeros_like(ybuf)
        ybuf_more[...] = jnp.zeros_like(ybuf_more)
        first_group(0, wait=False)

    @pl.when(i + 1 < n_tiles)
    def _():
        first_group(i + 1, wait=False)

    first_group(i, wait=True)

    code1 = info_ref[:, 0:1].astype(I32)
    code2 = info_ref[:, 1:2].astype(I32)
    lane = lax.broadcasted_iota(I32, (COMB_TS, COMB_WC), 1)

    def onehot_group(g):
        pieces = []
        for s in range(COMB_GROUP):
            idx = i * COMB_MAXW + g * COMB_GROUP + s
            off = off_ref[idx]
            lo = lo_ref[idx]
            hi = hi_ref[idx]
            q1 = jnp.where((code1 >= lo) & (code1 < hi), code1 - off, -1)
            q2 = jnp.where((code2 >= lo) & (code2 < hi), code2 - off, -1)
            pieces.append(jnp.where((q1 == lane) | (q2 == lane), 1.0, 0.0).astype(BF16))
        return jnp.concatenate(pieces, axis=1)

    o_ref[...] = x_ref[...] + _dot(onehot_group(0), ybuf[i % 2])

    n = nwin_ref[i]

    def more(g, carry):
        for wait in (False, True):
            for s in range(COMB_GROUP):
                @pl.when(g * COMB_GROUP + s < n)
                def _():
                    cp = window_copy(i * COMB_MAXW + g * COMB_GROUP + s,
                                     ybuf_more.at[pl.ds(s * COMB_WC, COMB_WC), :], sem_more.at[s])
                    cp.wait() if wait else cp.start()
        o_ref[...] += _dot(onehot_group(g), ybuf_more[...])
        return carry

    lax.fori_loop(1, (n + COMB_GROUP - 1) // COMB_GROUP, more, 0)


def _moe_combine(plan, x, info, y):
    t, d = x.shape
    width = COMB_GROUP * COMB_WC
    grid_spec = pltpu.PrefetchScalarGridSpec(
        num_scalar_prefetch=5,
        grid=(t // COMB_TS,),
        in_specs=[
            pl.BlockSpec((COMB_TS, d), lambda i, *_: (i, 0)),
            pl.BlockSpec((COMB_TS, 4), lambda i, *_: (i, 0)),
            pl.BlockSpec(memory_space=pl.ANY),
        ],
        out_specs=pl.BlockSpec((COMB_TS, d), lambda i, *_: (i, 0)),
        scratch_shapes=[
            pltpu.VMEM((2, width, d), BF16),
            pltpu.VMEM((width, d), BF16),
            pltpu.SemaphoreType.DMA((2, COMB_GROUP)),
            pltpu.SemaphoreType.DMA((COMB_GROUP,)),
        ],
    )
    return pl.pallas_call(
        _moe_combine_kernel,
        grid_spec=grid_spec,
        out_shape=jax.ShapeDtypeStruct((t, d), F32),
        compiler_params=_params(("arbitrary",)),
        name="moe_combine",
    )(plan["c_start"], plan["c_off"], plan["c_lo"], plan["c_hi"], plan["c_nwin"], x, info, y)


def kernel(x, norm_mix, norm_ffn, gla_w_in, gla_w_gate2, gla_b_gate, gla_out_gain, gla_w_out,
           moba_w_qkv, moba_q_gain, moba_k_gain, moba_w_out, ffn_w_gate, ffn_w_up, ffn_w_down,
           moe_w_router, moe_w_gate, moe_w_up, moe_w_down):
    batch, seq, d = x.shape
    t = batch * seq
    xt = x.reshape(t, d)

    n_main = 2 * GLA_DK + 2 * GLA_DV
    w_in = gla_w_in[0]
    (q, k, v, g, la), (ffn_wg, ffn_wu, ffn_wd, gla_wo, moba_wqkv, moba_wo) = _gla_inproj(
        xt, norm_mix[0:1], w_in[:, :n_main].astype(BF16), w_in[:, n_main:].astype(BF16),
        gla_w_gate2[0], gla_b_gate[0:1],
        ride_along=(ffn_w_gate, ffn_w_up, ffn_w_down, gla_w_out, moba_w_qkv, moba_w_out), tm=512)
    o = _gla(q, k, v, g, la, gla_out_gain[0:1], batch, seq, blk=256)
    xt, (moe_wg, moe_wu, moe_wd) = _ffn_residual(
        xt, o, gla_wo[0], norm_ffn[0:1], ffn_wg[0], ffn_wu[0], ffn_wd[0],
        ride_along=(moe_w_gate[0], moe_w_up[0], moe_w_down[0]), tm=1024, tf=512)

    q, k, v = _moba_qkv(xt, norm_mix[1:2], moba_wqkv[0], moba_q_gain[0:1], moba_k_gain[0:1], tm=512)
    o = _moba(q, k, v, batch, seq)
    xt, h, info, cnt = _router(xt, o, moba_wo[0], norm_ffn[1:2], moe_w_router[0], tr=1024)
    plan = _moe_plan(cnt, t)
    rows = info.T.reshape(4, t // TOK_BLK, TOK_BLK)
    hs, gs = _moe_gather(plan, h, rows)
    y = _moe_experts(plan, hs, gs, moe_wg, moe_wu, moe_wd, tf=1792, chunk=1792)
    xt = _moe_combine(plan, xt, info, y)
    return xt.reshape(batch, seq, d)
```

```python
import functools

import jax
import jax.numpy as jnp
from jax import lax
from jax.experimental import pallas as pl
from jax.experimental.pallas import tpu as pltpu

F32 = jnp.float32
BF16 = jnp.bfloat16
I32 = jnp.int32

EPS = 1e-6
NEG_INF = -1e30

D_MODEL = 1024
GLA_HEADS = 4
GLA_DK = 512
GLA_DV = 1024
GLA_HK = 128
GLA_HV = 256
GLA_RANK = 16
GLA_NORMALIZER = 16.0
GLA_CHUNK = 64

MOBA_HEADS = 8
MOBA_HD = 128
MOBA_BLOCK = 256
MOBA_TOPK = 3
MOBA_Q_SCALE = MOBA_HD ** -0.5 * 1.4426950408889634

N_EXPERTS = 8
TOP_K = 2

LANES = 128
BF16_ROWS = 16

FFN_PIECES = (256, 256)
MOE_PIECES = (512, 512, 768)
MOE_TM = 512
MOE_TG = 128
TOK_BLK = LANES
GATHER_WB = 6
COMB_TS = 256
COMB_WC = 128
COMB_GROUP = 8
COMB_MAXW = 16


def _params(semantics, vmem_bytes=None):
    return pltpu.CompilerParams(dimension_semantics=semantics, vmem_limit_bytes=vmem_bytes)


def _rms(x, gain):
    y = x * lax.rsqrt(jnp.mean(x * x, axis=-1, keepdims=True) + EPS)
    return y * gain


def _dot(a, b):
    return jnp.dot(a, b, preferred_element_type=F32)


def _dot_nt(a, b):
    return lax.dot_general(a, b, (((1,), (1,)), ((), ())), preferred_element_type=F32)


def _dot_tn(a, b):
    return lax.dot_general(a, b, (((0,), (0,)), ((), ())), preferred_element_type=F32)


def _split(a):
    hi = a.astype(BF16)
    lo = (a - hi.astype(F32)).astype(BF16)
    return hi, lo


def _log_sigmoid(z):
    return jnp.minimum(z, 0.0) - jnp.log1p(jnp.exp(-jnp.abs(z)))


def _silu(z):
    return z / (1.0 + jnp.exp(-z))


def _swiglu_pieces(h, wg_ref, wu_ref, wd_ref, widths):
    bounds = [sum(widths[:n]) for n in range(len(widths) + 1)]
    assert bounds[-1] == wg_ref.shape[1], (widths, wg_ref.shape)
    gate_up = lambda c: (_dot(h, wg_ref[:, bounds[c]:bounds[c + 1]]), _dot(h, wu_ref[:, bounds[c]:bounds[c + 1]]))
    out = None
    upcoming = gate_up(0)
    for c in range(len(widths)):
        g, u = upcoming
        if c + 1 < len(widths):
            upcoming = gate_up(c + 1)
        d = _dot((_silu(g) * u).astype(BF16), wd_ref[bounds[c]:bounds[c + 1], :])
        out = d if out is None else out + d
    return out


def _ride_along_specs(arrays, n_steps, step):
    specs, shapes = [], []
    for w in arrays:
        n_e, rows, cols = w.shape
        per_e = n_steps // n_e
        if rows % (per_e * BF16_ROWS) == 0:
            blk = (None, rows // per_e, cols)
            imap = lambda *ids, per_e=per_e: (step(*ids) // per_e, step(*ids) % per_e, 0)
        else:
            assert cols % (per_e * LANES) == 0, (w.shape, n_steps)
            blk = (None, rows, cols // per_e)
            imap = lambda *ids, per_e=per_e: (step(*ids) // per_e, 0, step(*ids) % per_e)
        specs.append(pl.BlockSpec(blk, imap))
        shapes.append(jax.ShapeDtypeStruct(w.shape, BF16))
    return specs, shapes


def _gla_inproj_kernel(x_ref, gain_ref, w_ref, wg2_ref, b_ref, *rest, n_cast):
    cast_in, (q_ref, k_ref, v_ref, g_ref, la_ref, *cast_out), (wb_ref,) = (
        rest[:n_cast], rest[n_cast:2 * n_cast + 5], rest[2 * n_cast + 5:])

    @pl.when(pl.program_id(0) == 0)
    def _():
        wb_ref[...] = w_ref[...].astype(BF16)

    for src, dst in zip(cast_in, cast_out):
        dst[...] = src[...].astype(BF16)
    n_main = 2 * GLA_DK + 2 * GLA_DV
    h = _rms(x_ref[...], gain_ref[...]).astype(BF16)
    q_ref[...] = _dot(h, wb_ref[:, 0:GLA_DK]).astype(BF16)
    k_ref[...] = _dot(h, wb_ref[:, GLA_DK:2 * GLA_DK]).astype(BF16)
    v_ref[...] = _dot(h, wb_ref[:, 2 * GLA_DK:2 * GLA_DK + GLA_DV]).astype(BF16)
    g_ref[...] = _dot(h, wb_ref[:, 2 * GLA_DK + GLA_DV:n_main]).astype(BF16)
    a_lr = _dot(h, wb_ref[:, n_main:])
    a_hi, a_lo = _split(a_lr)
    w_hi, w_lo = _split(wg2_ref[...])
    z = _dot(jnp.concatenate([a_hi, a_lo, a_hi], axis=1), jnp.concatenate([w_hi, w_hi, w_lo], axis=0))
    la_ref[...] = _log_sigmoid(z + b_ref[...]) / GLA_NORMALIZER


def _gla_inproj(x, gain, w_in, w_gate2, b_gate, ride_along, tm):
    t = x.shape[0]
    row = lambda i: (i, 0)
    fixed = lambda i: (0, 0)
    cast_specs, cast_shapes = _ride_along_specs(ride_along, t // tm, lambda i: i)
    outs = pl.pallas_call(
        functools.partial(_gla_inproj_kernel, n_cast=len(ride_along)),
        grid=(t // tm,),
        in_specs=[
            pl.BlockSpec((tm, D_MODEL), row),
            pl.BlockSpec((1, D_MODEL), fixed),
            pl.BlockSpec((None,) + w_in.shape[1:], lambda i: (0, 0, 0), pipeline_mode=pl.Buffered(1)),
            pl.BlockSpec((GLA_RANK, GLA_DK), fixed),
            pl.BlockSpec((1, GLA_DK), fixed),
        ] + cast_specs,
        out_specs=[
            pl.BlockSpec((tm, GLA_DK), row),
            pl.BlockSpec((tm, GLA_DK), row),
            pl.BlockSpec((tm, GLA_DV), row),
            pl.BlockSpec((tm, GLA_DV), row),
            pl.BlockSpec((tm, GLA_DK), row),
        ] + cast_specs,
        out_shape=[
            jax.ShapeDtypeStruct((t, GLA_DK), BF16),
            jax.ShapeDtypeStruct((t, GLA_DK), BF16),
            jax.ShapeDtypeStruct((t, GLA_DV), BF16),
            jax.ShapeDtypeStruct((t, GLA_DV), BF16),
            jax.ShapeDtypeStruct((t, GLA_DK), F32),
        ] + cast_shapes,
        scratch_shapes=[pltpu.VMEM(w_in.shape[1:], BF16)],
        compiler_params=_params(("arbitrary",), 48 * 1024 * 1024),
        name="gla_inproj",
    )(x, gain, w_in, w_gate2, b_gate, *ride_along)
    return outs[:5], outs[5:]


def _gla_kernel(q_ref, k_ref, v_ref, g_ref, la_ref, og_ref, o_ref, st_ref, *, n_chunks):
    @pl.when(pl.program_id(1) == 0)
    def _():
        st_ref[...] = jnp.zeros_like(st_ref)

    blk = n_chunks * GLA_CHUNK
    r = lax.broadcasted_iota(I32, (blk, blk), 0)
    c = lax.broadcasted_iota(I32, (blk, blk), 1)
    causal = (r >= c) & (r // GLA_CHUNK == c // GLA_CHUNK)
    tril = causal.astype(BF16)
    la = la_ref[...]
    la_hi, la_mid = _split(la)
    la_lo = (la - la_hi.astype(F32) - la_mid.astype(F32)).astype(BF16)
    cum = _dot(tril, la_hi) + _dot(tril, la_mid) + _dot(tril, la_lo)
    ends = [cum[(ci + 1) * GLA_CHUNK - 1:(ci + 1) * GLA_CHUNK, :] for ci in range(n_chunks)]
    last = jnp.concatenate([jnp.broadcast_to(e, (GLA_CHUNK, GLA_DK)) for e in ends], axis=0)
    q = q_ref[...].astype(F32) * (GLA_HK ** -0.5)
    k = k_ref[...].astype(F32)
    q_dec = (q * jnp.exp(cum)).astype(BF16)
    k_inv = (k * jnp.exp(-cum)).astype(BF16)
    k_end = (k * jnp.exp(last - cum)).astype(BF16)
    chunk_decay = [jnp.exp(e) for e in ends]
    og = og_ref[...]
    heads = range(GLA_HEADS)
    hk = [slice(h * GLA_HK, (h + 1) * GLA_HK) for h in heads]
    hv = [slice(h * GLA_HV, (h + 1) * GLA_HV) for h in heads]
    scores = [jnp.where(causal, _dot_nt(q_dec[:, hk[h]], k_inv[:, hk[h]]), 0.0).astype(BF16) for h in heads]
    state_in = [[_dot_tn(v_ref[ci * GLA_CHUNK:(ci + 1) * GLA_CHUNK, hv[h]],
                         k_end[ci * GLA_CHUNK:(ci + 1) * GLA_CHUNK, hk[h]])
                 for ci in range(n_chunks)] for h in heads]
    intra = [_dot(scores[h], v_ref[:, hv[h]]) for h in heads]
    for h in heads:
        st = st_ref[h]
        inter = []
        for ci in range(n_chunks):
            rows = slice(ci * GLA_CHUNK, (ci + 1) * GLA_CHUNK)
            inter.append(_dot_nt(q_dec[rows, hk[h]], st.astype(BF16)))
            st = st * chunk_decay[ci][:, hk[h]] + state_in[h][ci]
        st_ref[h] = st
        o = intra[h] + jnp.concatenate(inter, axis=0)
        o = o * lax.rsqrt(jnp.mean(o * o, axis=-1, keepdims=True) + EPS) * og
        o = o * _silu(g_ref[:, hv[h]].astype(F32))
        o_ref[:, hv[h]] = o.astype(BF16)


def _gla(q, k, v, g, la, out_gain, batch, seq, blk):
    t = q.shape[0]
    nb = seq // blk
    row = lambda b, i: (b * nb + i, 0)
    return pl.pallas_call(
        functools.partial(_gla_kernel, n_chunks=blk // GLA_CHUNK),
        grid=(batch, nb),
        in_specs=[
            pl.BlockSpec((blk, GLA_DK), row),
            pl.BlockSpec((blk, GLA_DK), row),
            pl.BlockSpec((blk, GLA_DV), row),
            pl.BlockSpec((blk, GLA_DV), row),
            pl.BlockSpec((blk, GLA_DK), row),
            pl.BlockSpec((1, GLA_HV), lambda b, i: (0, 0)),
        ],
        out_specs=pl.BlockSpec((blk, GLA_DV), row),
        out_shape=jax.ShapeDtypeStruct((t, GLA_DV), BF16),
        scratch_shapes=[pltpu.VMEM((GLA_HEADS, GLA_HV, GLA_HK), F32)],
        compiler_params=_params(("parallel", "arbitrary")),
        name="gla_core",
    )(q, k, v, g, la, out_gain)


def _ffn_kernel(x_ref, o_ref, wo_ref, gain_ref, wg_ref, wu_ref, wd_ref, *rest, n_cast):
    cast_in, (y_ref, *cast_out), (h_ref,) = rest[:n_cast], rest[n_cast:2 * n_cast + 1], rest[2 * n_cast + 1:]

    @pl.when(pl.program_id(1) == 0)
    def _():
        x1 = x_ref[...] + _dot(o_ref[...], wo_ref[...])
        h_ref[...] = _rms(x1, gain_ref[...]).astype(BF16)
        y_ref[...] = x1

    h = h_ref[...]
    y_ref[...] += _swiglu_pieces(h, wg_ref, wu_ref, wd_ref, FFN_PIECES)
    for src, dst in zip(cast_in, cast_out):
        dst[...] = src[...].astype(BF16)


def _ffn_residual(x, o, w_out, gain, w_gate, w_up, w_down, ride_along, tm, tf):
    t, d = x.shape
    ff = w_gate.shape[1]
    ni, nj = t // tm, ff // tf
    cast_specs, cast_shapes = _ride_along_specs(ride_along, ni * nj, lambda i, j: i * nj + j)
    outs = pl.pallas_call(
        functools.partial(_ffn_kernel, n_cast=len(ride_along)),
        grid=(ni, nj),
        in_specs=[
            pl.BlockSpec((tm, d), lambda i, j: (i, 0)),
            pl.BlockSpec((tm, d), lambda i, j: (i, 0)),
            pl.BlockSpec((d, d), lambda i, j: (0, 0)),
            pl.BlockSpec((1, d), lambda i, j: (0, 0)),
            pl.BlockSpec((d, tf), lambda i, j: (0, j)),
            pl.BlockSpec((d, tf), lambda i, j: (0, j)),
            pl.BlockSpec((tf, d), lambda i, j: (j, 0)),
        ] + cast_specs,
        out_specs=[pl.BlockSpec((tm, d), lambda i, j: (i, 0))] + cast_specs,
        out_shape=[jax.ShapeDtypeStruct((t, d), F32)] + cast_shapes,
        scratch_shapes=[pltpu.VMEM((tm, d), BF16)],
        compiler_params=_params(("arbitrary", "arbitrary"), 56 * 1024 * 1024),
        name="ffn_swiglu",
    )(x, o, w_out, gain, w_gate, w_up, w_down, *ride_along)
    return outs[0], outs[1:]


def _moba_qkv_kernel(x_ref, gain_ref, w_ref, qg_ref, kg_ref, q_ref, k_ref, v_ref):
    h = _rms(x_ref[...], gain_ref[...]).astype(BF16)
    d = MOBA_HEADS * MOBA_HD
    qg = qg_ref[...]
    kg = kg_ref[...]
    q = _dot(h, w_ref[:, 0:d])
    k = _dot(h, w_ref[:, d:2 * d])
    for hd in range(MOBA_HEADS):
        cols = slice(hd * MOBA_HD, (hd + 1) * MOBA_HD)
        q_ref[:, cols] = (_rms(q[:, cols], qg) * MOBA_Q_SCALE).astype(BF16)
        k_ref[:, cols] = _rms(k[:, cols], kg).astype(BF16)
    v_ref[...] = _dot(h, w_ref[:, 2 * d:]).astype(BF16)


def _moba_qkv(x, gain, w, q_gain, k_gain, tm):
    t = x.shape[0]
    d = MOBA_HEADS * MOBA_HD
    row = lambda i: (i, 0)
    fixed = lambda i: (0, 0)
    return pl.pallas_call(
        _moba_qkv_kernel,
        grid=(t // tm,),
        in_specs=[
            pl.BlockSpec((tm, D_MODEL), row),
            pl.BlockSpec((1, D_MODEL), fixed),
            pl.BlockSpec((D_MODEL, 3 * d), fixed),
            pl.BlockSpec((1, MOBA_HD), fixed),
            pl.BlockSpec((1, MOBA_HD), fixed),
        ],
        out_specs=[pl.BlockSpec((tm, d), row)] * 3,
        out_shape=[jax.ShapeDtypeStruct((t, d), BF16)] * 3,
        compiler_params=_params(("parallel",), 48 * 1024 * 1024),
        name="moba_qkv",
    )(x, gain, w, q_gain, k_gain)


def _reduce_rows(x, op, stop=8):
    while x.shape[0] > stop:
        half = x.shape[0] // 2
        x = op(x[:half], x[half:])
    return x


def _moba_kernel(q_ref, k_ref, v_ref, o_ref, *, n_blocks):
    seq = n_blocks * MOBA_BLOCK
    n_pad = 16
    v_t = v_ref[...].astype(F32).T.astype(BF16)
    blk_of_key = lax.broadcasted_iota(I32, (n_pad, seq), 1) // MOBA_BLOCK
    blk_row = lax.broadcasted_iota(I32, (n_pad, seq), 0)
    indicator = jnp.where(blk_of_key == blk_row, 1.0 / MOBA_BLOCK, 0.0).astype(BF16)
    k_mean = _dot(indicator, k_ref[...])
    km_hi, km_lo = _split(k_mean)

    blk_id = lax.broadcasted_iota(I32, (n_pad, MOBA_BLOCK), 0)
    key_pos = lax.broadcasted_iota(I32, (MOBA_BLOCK, MOBA_BLOCK), 0)
    query_pos = lax.broadcasted_iota(I32, (MOBA_BLOCK, MOBA_BLOCK), 1)
    causal = key_pos <= query_pos

    def masked_scores(i):
        rows = slice(i * MOBA_BLOCK, (i + 1) * MOBA_BLOCK)
        n_keys = (i + 1) * MOBA_BLOCK
        q = q_ref[rows, :]
        s = _dot_nt(k_ref[0:n_keys, :], q)
        selected = None
        if i > MOBA_TOPK:
            gate = _dot_nt(km_hi, q) + _dot_nt(km_lo, q)
            gate = jnp.where(blk_id < i, gate, NEG_INF)
            rank = jnp.zeros_like(gate)
            for jp in range(i):
                row = gate[jp:jp + 1, :]
                beats = (row > gate) | ((row == gate) & (jp < blk_id))
                rank = rank + beats.astype(F32)
            selected = jnp.where((rank < MOBA_TOPK) & (blk_id < i), 1.0, 0.0)
        pieces = []
        for j in range(i + 1):
            sj = s[j * MOBA_BLOCK:(j + 1) * MOBA_BLOCK, :]
            if j == i:
                sj = jnp.where(causal, sj, NEG_INF)
            elif selected is not None:
                sj = jnp.where(selected[j:j + 1, :] > 0.5, sj, NEG_INF)
            pieces.append(sj)
        return pieces

    upcoming = masked_scores(0)
    for i in range(n_blocks):
        rows = slice(i * MOBA_BLOCK, (i + 1) * MOBA_BLOCK)
        n_keys = (i + 1) * MOBA_BLOCK
        pieces = upcoming
        if i + 1 < n_blocks:
            upcoming = masked_scores(i + 1)
        m = _reduce_rows(functools.reduce(jnp.maximum, pieces), jnp.maximum).max(axis=0, keepdims=True)
        probs = [jnp.exp2(sj - m) for sj in pieces]
        denom = _reduce_rows(functools.reduce(jnp.add, probs), jnp.add).sum(axis=0, keepdims=True)
        p = probs[0] if len(probs) == 1 else jnp.concatenate(probs, axis=0)
        o_t = _dot(v_t[:, 0:n_keys], p.astype(BF16))
        o_ref[rows, :] = (o_t / denom).T.astype(BF16)


def _moba(q, k, v, batch, seq):
    t, d = q.shape
    spec = pl.BlockSpec((seq, MOBA_HD), lambda b, h: (b, h))
    return pl.pallas_call(
        functools.partial(_moba_kernel, n_blocks=seq // MOBA_BLOCK),
        grid=(batch, MOBA_HEADS),
        in_specs=[spec, spec, spec],
        out_specs=spec,
        out_shape=jax.ShapeDtypeStruct((t, d), BF16),
        compiler_params=_params(("parallel", "parallel"), 48 * 1024 * 1024),
        name="moba_attn",
    )(q, k, v)


def _router_kernel(x_ref, o_ref, wo_ref, gain_ref, w_ref, x2_ref, h_ref, info_ref, cnt_ref, carry_ref, tri_ref,
                   *, region_rows):
    tr = x_ref.shape[0]

    @pl.when(pl.program_id(0) == 0)
    def _():
        carry_ref[...] = jnp.zeros_like(carry_ref)
        r = lax.broadcasted_iota(I32, (tr, tr), 0)
        c = lax.broadcasted_iota(I32, (tr, tr), 1)
        tri_ref[...] = (c < r).astype(BF16)

    x2 = x_ref[...] + _dot(o_ref[...], wo_ref[...])
    x2_ref[...] = x2
    h = _rms(x2, gain_ref[...])
    h_hi, h_lo = _split(h)
    h_ref[...] = h_hi
    both = _dot(h_hi, w_ref[...])
    logits = both[:, :LANES] + both[:, LANES:] + _dot(h_lo, w_ref[:, :LANES])
    e = lax.broadcasted_iota(I32, logits.shape, 1)
    logits = jnp.where(e < N_EXPERTS, logits, -jnp.inf)
    m1 = logits.max(axis=-1, keepdims=True)
    i1 = jnp.where(logits == m1, e, LANES).min(axis=-1, keepdims=True)
    first = e == i1
    rest = jnp.where(first, -jnp.inf, logits)
    m2 = rest.max(axis=-1, keepdims=True)
    i2 = jnp.where(rest == m2, e, LANES).min(axis=-1, keepdims=True)
    second = e == i2
    e2 = jnp.exp(m2 - m1)
    denom = 1.0 + e2
    sel = jnp.where(first | second, 1.0, 0.0).astype(BF16)
    before = _dot(tri_ref[...], sel) + carry_ref[...]
    rank1 = jnp.where(first, before, 0.0).sum(axis=-1, keepdims=True)
    rank2 = jnp.where(second, before, 0.0).sum(axis=-1, keepdims=True)
    nb = tr // TOK_BLK
    blk_row = lax.broadcasted_iota(I32, (nb, tr), 0)
    blk_tok = lax.broadcasted_iota(I32, (nb, tr), 1) // TOK_BLK
    cnt = _dot((blk_row == blk_tok).astype(BF16), sel)
    cnt_ref[...] = cnt
    carry_ref[...] += cnt.sum(axis=0, keepdims=True)
    col = lax.broadcasted_iota(I32, (tr, 4), 1)
    fields = (i1.astype(F32) * region_rows + rank1, i2.astype(F32) * region_rows + rank2,
              1.0 / denom, e2 / denom)
    info = jnp.zeros((tr, 4), F32)
    for n, f in enumerate(fields):
        info = jnp.where(col == n, f, info)
    info_ref[...] = info


def _router(x, o, w_out, gain, w_router, tr):
    t = x.shape[0]
    nb = tr // TOK_BLK
    pad = ((0, 0), (0, LANES - N_EXPERTS))
    w_hi = w_router.astype(BF16)
    w_lo = (w_router - w_hi.astype(F32)).astype(BF16)
    w_both = jnp.concatenate([jnp.pad(w_hi, pad), jnp.pad(w_lo, pad)], axis=1)
    row = lambda i: (i, 0)
    fixed = lambda i: (0, 0)
    return pl.pallas_call(
        functools.partial(_router_kernel, region_rows=float(t)),
        grid=(t // tr,),
        in_specs=[
            pl.BlockSpec((tr, D_MODEL), row),
            pl.BlockSpec((tr, D_MODEL), row),
            pl.BlockSpec((D_MODEL, D_MODEL), fixed),
            pl.BlockSpec((1, D_MODEL), fixed),
            pl.BlockSpec((D_MODEL, 2 * LANES), fixed),
        ],
        out_specs=[
            pl.BlockSpec((tr, D_MODEL), row),
            pl.BlockSpec((tr, D_MODEL), row),
            pl.BlockSpec((tr, 4), row),
            pl.BlockSpec((nb, LANES), row),
        ],
        out_shape=[
            jax.ShapeDtypeStruct((t, D_MODEL), F32),
            jax.ShapeDtypeStruct((t, D_MODEL), BF16),
            jax.ShapeDtypeStruct((t, 4), F32),
            jax.ShapeDtypeStruct((t // TOK_BLK, LANES), F32),
        ],
        scratch_shapes=[pltpu.VMEM((1, LANES), F32), pltpu.VMEM((tr, tr), BF16)],
        compiler_params=_params(("arbitrary",), 56 * 1024 * 1024),
        name="moe_router",
    )(x, o, w_out, gain, w_both)


def _moe_plan(cnt, t):
    e_ids = jnp.arange(N_EXPERTS, dtype=I32)
    by_expert = lambda table, idx: jnp.sum(jnp.where(idx[..., None] == e_ids, table, 0), axis=-1)
    cnt_blk = cnt[:, :N_EXPERTS].astype(I32)
    cum_blk = jnp.cumsum(cnt_blk, axis=0)
    counts = cum_blk[-1]
    tiles_e = (counts + MOE_TM - 1) // MOE_TM
    tile_end = jnp.cumsum(tiles_e)
    tile_off = tile_end - tiles_e
    nt = TOP_K * t // MOE_TM + N_EXPERTS
    r = jnp.arange(nt, dtype=I32)
    valid = r < tile_end[-1]
    te = jnp.minimum(jnp.sum((r[:, None] >= tile_end[None, :]).astype(I32), axis=1), N_EXPERTS - 1)
    local = r - by_expert(tile_off, te)
    last_e = jnp.max(jnp.where(tiles_e > 0, e_ids, 0))
    tile_expert = jnp.where(valid, te, last_e)
    shift = e_ids * t - tile_off * MOE_TM

    per = MOE_TM // MOE_TG
    rg = jnp.arange(nt * per, dtype=I32)
    eg = jnp.repeat(te, per)
    k0 = jnp.repeat(local, per) * MOE_TM + (rg % per) * MOE_TG
    k1 = jnp.minimum(k0 + MOE_TG, by_expert(counts, eg))
    has = jnp.repeat(valid, per) & (k1 > k0)
    cum_e = by_expert(cum_blk[None, :, :], eg[:, None])
    s_lo = jnp.sum((cum_e <= k0[:, None]).astype(I32), axis=1)
    s_hi = jnp.sum((cum_e <= (k1 - 1)[:, None]).astype(I32), axis=1)
    g_nwin = jnp.where(has, (s_hi - s_lo) // GATHER_WB + 1, 0)
    g_slo = jnp.where(has, s_lo, 0)
    g_code = eg * t + k0

    nts = t // COMB_TS
    cnt_ts = cnt_blk.reshape(nts, COMB_TS // TOK_BLK, N_EXPERTS).sum(axis=1)
    seg_lo = (tile_off * MOE_TM)[None, :] + jnp.cumsum(cnt_ts, axis=0) - cnt_ts
    seg_hi = seg_lo + cnt_ts
    ws0 = (seg_lo // BF16_ROWS) * BF16_ROWS
    nw = jnp.where(cnt_ts > 0, (seg_hi - ws0 + COMB_WC - 1) // COMB_WC, 0)
    nw_end = jnp.cumsum(nw, axis=1)
    k = jnp.arange(COMB_MAXW, dtype=I32)
    ek = jnp.minimum(jnp.sum((k[None, :, None] >= nw_end[:, None, :]).astype(I32), axis=2), N_EXPERTS - 1)
    pick = lambda a: by_expert(a[:, None, :], ek)
    live = k[None, :] < nw_end[:, -1:]
    nominal = pick(ws0) + (k[None, :] - pick(nw_end - nw)) * COMB_WC
    shift_k = by_expert(shift, ek)
    c_start = jnp.minimum(nominal, by_expert(tile_end * MOE_TM, ek) - COMB_WC)
    c_off = shift_k + c_start
    c_lo = jnp.maximum(pick(seg_lo), nominal) + shift_k
    c_hi = jnp.minimum(pick(seg_hi), nominal + COMB_WC) + shift_k
    flat = lambda a: jnp.where(live, a, 0).reshape(-1)
    return dict(
        tile_expert=tile_expert, tile_valid=valid.astype(I32),
        g_code=g_code, g_slo=g_slo, g_nwin=g_nwin,
        c_start=flat(c_start), c_off=flat(c_off), c_lo=flat(c_lo), c_hi=flat(c_hi), c_nwin=nw_end[:, -1])


def _moe_gather_kernel(code_ref, slo_ref, nwin_ref, rows_ref, h_ref, hs_ref, gs_ref):
    n_blk = rows_ref.shape[1]
    hs_ref[...] = jnp.zeros_like(hs_ref)
    gs_ref[...] = jnp.zeros_like(gs_ref)
    for sub in range(hs_ref.shape[0] // MOE_TG):
        _gather_subtile(pl.program_id(0) * (hs_ref.shape[0] // MOE_TG) + sub,
                        slice(sub * MOE_TG, (sub + 1) * MOE_TG),
                        code_ref, slo_ref, nwin_ref, rows_ref, h_ref, hs_ref, gs_ref, n_blk)


def _gather_subtile(r, out_rows, code_ref, slo_ref, nwin_ref, rows_ref, h_ref, hs_ref, gs_ref, n_blk):
    row_id = (code_ref[r] + lax.broadcasted_iota(I32, (MOE_TG, TOK_BLK), 0)).astype(F32)

    def body(k, carry):
        nominal = slo_ref[r] + k * GATHER_WB
        sb = jnp.minimum(nominal, n_blk - GATHER_WB)
        fresh = sb + lax.broadcasted_iota(I32, (GATHER_WB, TOK_BLK), 0) >= nominal
        p1 = jnp.where(fresh, rows_ref[0, pl.ds(sb, GATHER_WB), :], -1.0)
        p2 = jnp.where(fresh, rows_ref[1, pl.ds(sb, GATHER_WB), :], -1.0)
        w1 = rows_ref[2, pl.ds(sb, GATHER_WB), :]
        w2 = rows_ref[3, pl.ds(sb, GATHER_WB), :]
        pieces = []
        g = jnp.zeros((MOE_TG, TOK_BLK), F32)
        for a in range(GATHER_WB):
            m1 = p1[a:a + 1, :] == row_id
            m2 = p2[a:a + 1, :] == row_id
            pieces.append(jnp.where(m1 | m2, 1.0, 0.0).astype(BF16))
            g = g + jnp.where(m1, w1[a:a + 1, :], 0.0) + jnp.where(m2, w2[a:a + 1, :], 0.0)
        onehot = jnp.concatenate(pieces, axis=1)
        tok0 = pl.multiple_of(sb * TOK_BLK, TOK_BLK)
        hs_ref[out_rows, :] += _dot(onehot, h_ref[pl.ds(tok0, GATHER_WB * TOK_BLK), :]).astype(BF16)
        gs_ref[out_rows, :] += g.sum(axis=-1, keepdims=True)
        return carry

    lax.fori_loop(0, nwin_ref[r], body, 0)


def _moe_gather(plan, h, rows):
    t = h.shape[0]
    n_rows = plan["g_code"].shape[0] * MOE_TG
    n_blk = t // TOK_BLK
    grid_spec = pltpu.PrefetchScalarGridSpec(
        num_scalar_prefetch=3,
        grid=(n_rows // MOE_TM,),
        in_specs=[
            pl.BlockSpec((4, n_blk, TOK_BLK), lambda r, *_: (0, 0, 0), pipeline_mode=pl.Buffered(1)),
            pl.BlockSpec((t, D_MODEL), lambda r, *_: (0, 0), pipeline_mode=pl.Buffered(1)),
        ],
        out_specs=[
            pl.BlockSpec((MOE_TM, D_MODEL), lambda r, *_: (r, 0)),
            pl.BlockSpec((MOE_TM, 1), lambda r, *_: (r, 0)),
        ],
    )
    return pl.pallas_call(
        _moe_gather_kernel,
        grid_spec=grid_spec,
        out_shape=[
            jax.ShapeDtypeStruct((n_rows, D_MODEL), BF16),
            jax.ShapeDtypeStruct((n_rows, 1), F32),
        ],
        compiler_params=_params(("parallel",), 52 * 1024 * 1024),
        name="moe_gather",
    )(plan["g_code"], plan["g_slo"], plan["g_nwin"], rows, h)


def _moe_expert_kernel(te_ref, tv_ref, hs_ref, gs_ref, wg_ref, wu_ref, wd_ref, y_ref, acc_ref, *, nj, pieces):
    r = pl.program_id(0)
    j = pl.program_id(1)
    valid = tv_ref[r] > 0
    part = lambda: _swiglu_pieces(hs_ref[...], wg_ref, wu_ref, wd_ref, pieces)

    assert nj >= 2, nj

    @pl.when(valid & (j == 0))
    def _():
        acc_ref[...] = part()

    if nj > 2:
        @pl.when(valid & (j > 0) & (j < nj - 1))
        def _():
            acc_ref[...] += part()

    @pl.when(valid & (j == nj - 1))
    def _():
        y_ref[...] = ((acc_ref[...] + part()) * gs_ref[...]).astype(BF16)

    @pl.when(jnp.logical_not(valid) & (j == nj - 1))
    def _():
        y_ref[...] = jnp.zeros_like(y_ref)


def _moe_experts(plan, hs, gs, w_gate, w_up, w_down, pieces):
    tf = sum(pieces)
    n_rows, d = hs.shape
    ff = w_gate.shape[2]
    nj = ff // tf
    jj = lambda r, j, tv: jnp.where(tv[r] > 0, j, nj - 1)
    grid_spec = pltpu.PrefetchScalarGridSpec(
        num_scalar_prefetch=2,
        grid=(n_rows // MOE_TM, nj),
        in_specs=[
            pl.BlockSpec((MOE_TM, d), lambda r, j, te, tv: (r, 0)),
            pl.BlockSpec((MOE_TM, 1), lambda r, j, te, tv: (r, 0)),
            pl.BlockSpec((None, d, tf), lambda r, j, te, tv: (te[r], 0, jj(r, j, tv))),
            pl.BlockSpec((None, d, tf), lambda r, j, te, tv: (te[r], 0, jj(r, j, tv))),
            pl.BlockSpec((None, tf, d), lambda r, j, te, tv: (te[r], jj(r, j, tv), 0)),
        ],
        out_specs=pl.BlockSpec((MOE_TM, d), lambda r, j, te, tv: (r, 0)),
        scratch_shapes=[pltpu.VMEM((MOE_TM, d), F32)],
    )
    return pl.pallas_call(
        functools.partial(_moe_expert_kernel, nj=nj, pieces=pieces),
        grid_spec=grid_spec,
        out_shape=jax.ShapeDtypeStruct((n_rows, d), BF16),
        compiler_params=_params(("arbitrary", "arbitrary"), 56 * 1024 * 1024),
        name="moe_experts",
    )(plan["tile_expert"], plan["tile_valid"], hs, gs, w_gate, w_up, w_down)


def _moe_combine_kernel(start_ref, off_ref, lo_ref, hi_ref, nwin_ref, x_ref, info_ref, y_hbm, o_ref,
                        ybuf, ybuf_more, sem, sem_more):
    i = pl.program_id(0)
    n_tiles = pl.num_programs(0)

    def window_copy(idx, dst, dsem):
        start = pl.multiple_of(start_ref[idx], BF16_ROWS)
        return pltpu.make_async_copy(y_hbm.at[pl.ds(start, COMB_WC), :], dst, dsem)

    def first_group(tile, wait):
        buf = tile % 2
        for s in range(COMB_GROUP):
            @pl.when(s < nwin_ref[tile])
            def _():
                cp = window_copy(tile * COMB_MAXW + s, ybuf.at[buf, pl.ds(s * COMB_WC, COMB_WC), :],
                                 sem.at[buf, s])
                cp.wait() if wait else cp.start()

    @pl.when(i == 0)
    def _():
        ybuf[...] = jnp.zeros_like(ybuf)
        ybuf_more[...] = jnp.zeros_like(ybuf_more)
        first_group(0, wait=False)

    @pl.when(i + 1 < n_tiles)
    def _():
        first_group(i + 1, wait=False)

    first_group(i, wait=True)

    code1 = info_ref[:, 0:1].astype(I32)
    code2 = info_ref[:, 1:2].astype(I32)
    lane = lax.broadcasted_iota(I32, (COMB_TS, COMB_WC), 1)

    def onehot_group(g):
        pieces = []
        for s in range(COMB_GROUP):
            idx = i * COMB_MAXW + g * COMB_GROUP + s
            off = off_ref[idx]
            lo = lo_ref[idx]
            hi = hi_ref[idx]
            q1 = jnp.where((code1 >= lo) & (code1 < hi), code1 - off, -1)
            q2 = jnp.where((code2 >= lo) & (code2 < hi), code2 - off, -1)
            pieces.append(jnp.where((q1 == lane) | (q2 == lane), 1.0, 0.0).astype(BF16))
        return jnp.concatenate(pieces, axis=1)

    o_ref[...] = x_ref[...] + _dot(onehot_group(0), ybuf[i % 2])

    n = nwin_ref[i]

    def more(g, carry):
        for wait in (False, True):
            for s in range(COMB_GROUP):
                @pl.when(g * COMB_GROUP + s < n)
                def _():
                    cp = window_copy(i * COMB_MAXW + g * COMB_GROUP + s,
                                     ybuf_more.at[pl.ds(s * COMB_WC, COMB_WC), :], sem_more.at[s])
                    cp.wait() if wait else cp.start()
        o_ref[...] += _dot(onehot_group(g), ybuf_more[...])
        return carry

    lax.fori_loop(1, (n + COMB_GROUP - 1) // COMB_GROUP, more, 0)


def _moe_combine(plan, x, info, y):
    t, d = x.shape
    width = COMB_GROUP * COMB_WC
    grid_spec = pltpu.PrefetchScalarGridSpec(
        num_scalar_prefetch=5,
        grid=(t // COMB_TS,),
        in_specs=[
            pl.BlockSpec((COMB_TS, d), lambda i, *_: (i, 0)),
            pl.BlockSpec((COMB_TS, 4), lambda i, *_: (i, 0)),
            pl.BlockSpec(memory_space=pl.ANY),
        ],
        out_specs=pl.BlockSpec((COMB_TS, d), lambda i, *_: (i, 0)),
        scratch_shapes=[
            pltpu.VMEM((2, width, d), BF16),
            pltpu.VMEM((width, d), BF16),
            pltpu.SemaphoreType.DMA((2, COMB_GROUP)),
            pltpu.SemaphoreType.DMA((COMB_GROUP,)),
        ],
    )
    return pl.pallas_call(
        _moe_combine_kernel,
        grid_spec=grid_spec,
        out_shape=jax.ShapeDtypeStruct((t, d), F32),
        compiler_params=_params(("arbitrary",)),
        name="moe_combine",
    )(plan["c_start"], plan["c_off"], plan["c_lo"], plan["c_hi"], plan["c_nwin"], x, info, y)


def kernel(x, norm_mix, norm_ffn, gla_w_in, gla_w_gate2, gla_b_gate, gla_out_gain, gla_w_out,
           moba_w_qkv, moba_q_gain, moba_k_gain, moba_w_out, ffn_w_gate, ffn_w_up, ffn_w_down,
           moe_w_router, moe_w_gate, moe_w_up, moe_w_down):
    batch, seq, d = x.shape
    t = batch * seq
    xt = x.reshape(t, d)

    (q, k, v, g, la), (ffn_wg, ffn_wu, ffn_wd, gla_wo, moba_wqkv, moba_wo) = _gla_inproj(
        xt, norm_mix[0:1], gla_w_in, gla_w_gate2[0], gla_b_gate[0:1],
        ride_along=(ffn_w_gate, ffn_w_up, ffn_w_down, gla_w_out, moba_w_qkv, moba_w_out), tm=512)
    o = _gla(q, k, v, g, la, gla_out_gain[0:1], batch, seq, blk=256)
    xt, (moe_wg, moe_wu, moe_wd) = _ffn_residual(
        xt, o, gla_wo[0], norm_ffn[0:1], ffn_wg[0], ffn_wu[0], ffn_wd[0],
        ride_along=(moe_w_gate[0], moe_w_up[0], moe_w_down[0]), tm=1024, tf=512)

    q, k, v = _moba_qkv(xt, norm_mix[1:2], moba_wqkv[0], moba_q_gain[0:1], moba_k_gain[0:1], tm=512)
    o = _moba(q, k, v, batch, seq)
    xt, h, info, cnt = _router(xt, o, moba_wo[0], norm_ffn[1:2], moe_w_router[0], tr=1024)
    plan = _moe_plan(cnt, t)
    rows = info.T.reshape(4, t // TOK_BLK, TOK_BLK)
    hs, gs = _moe_gather(plan, h, rows)
    y = _moe_experts(plan, hs, gs, moe_wg, moe_wu, moe_wd, pieces=MOE_PIECES)
    xt = _moe_combine(plan, xt, info, y)
    return xt.reshape(batch, seq, d)
```

```python
import functools

import jax
import jax.numpy as jnp
from jax import lax
from jax.experimental import pallas as pl
from jax.experimental.pallas import tpu as pltpu

F32 = jnp.float32
BF16 = jnp.bfloat16
I32 = jnp.int32

EPS = 1e-6
NEG_INF = -1e30

D_MODEL = 1024
GLA_HEADS = 4
GLA_DK = 512
GLA_DV = 1024
GLA_HK = 128
GLA_HV = 256
GLA_RANK = 16
GLA_NORMALIZER = 16.0
GLA_CHUNK = 64

MOBA_HEADS = 8
MOBA_HD = 128
MOBA_BLOCK = 256
MOBA_TOPK = 3
MOBA_Q_SCALE = MOBA_HD ** -0.5 * 1.4426950408889634

N_EXPERTS = 8
TOP_K = 2

LANES = 128
BF16_ROWS = 16

MOE_PIECES = (512, 512, 768)
MOE_TM = 512
MOE_TG = 128
TOK_BLK = LANES
GATHER_WB = 6
COMB_TS = 256
COMB_WC = 128
COMB_GROUP = 8
COMB_MAXW = 16


def _params(semantics, vmem_bytes=None):
    return pltpu.CompilerParams(dimension_semantics=semantics, vmem_limit_bytes=vmem_bytes)


def _rms(x, gain):
    y = x * lax.rsqrt(jnp.mean(x * x, axis=-1, keepdims=True) + EPS)
    return y * gain


def _dot(a, b):
    return jnp.dot(a, b, preferred_element_type=F32)


def _dot_nt(a, b):
    return lax.dot_general(a, b, (((1,), (1,)), ((), ())), preferred_element_type=F32)


def _dot_tn(a, b):
    return lax.dot_general(a, b, (((0,), (0,)), ((), ())), preferred_element_type=F32)


def _split(a):
    hi = a.astype(BF16)
    lo = (a - hi.astype(F32)).astype(BF16)
    return hi, lo


def _log_sigmoid(z):
    return jnp.minimum(z, 0.0) - jnp.log1p(jnp.exp(-jnp.abs(z)))


def _silu(z):
    return z / (1.0 + jnp.exp(-z))


def _swiglu_pieces(h, wg_ref, wu_ref, wd_ref, widths):
    bounds = [sum(widths[:n]) for n in range(len(widths) + 1)]
    assert bounds[-1] == wg_ref.shape[1], (widths, wg_ref.shape)
    gate_up = lambda c: (_dot(h, wg_ref[:, bounds[c]:bounds[c + 1]]), _dot(h, wu_ref[:, bounds[c]:bounds[c + 1]]))
    out = None
    upcoming = gate_up(0)
    for c in range(len(widths)):
        g, u = upcoming
        if c + 1 < len(widths):
            upcoming = gate_up(c + 1)
        d = _dot((_silu(g) * u).astype(BF16), wd_ref[bounds[c]:bounds[c + 1], :])
        out = d if out is None else out + d
    return out


def _ride_along_specs(arrays, n_steps, step):
    specs, shapes = [], []
    for w in arrays:
        n_e, rows, cols = w.shape
        per_e = n_steps // n_e
        if rows % (per_e * BF16_ROWS) == 0:
            blk = (None, rows // per_e, cols)
            imap = lambda *ids, per_e=per_e: (step(*ids) // per_e, step(*ids) % per_e, 0)
        else:
            assert cols % (per_e * LANES) == 0, (w.shape, n_steps)
            blk = (None, rows, cols // per_e)
            imap = lambda *ids, per_e=per_e: (step(*ids) // per_e, 0, step(*ids) % per_e)
        specs.append(pl.BlockSpec(blk, imap))
        shapes.append(jax.ShapeDtypeStruct(w.shape, BF16))
    return specs, shapes


def _with_ride_along(body, n_in, n_out, n_cast):
    def kernel(*refs):
        ins, refs = refs[:n_in], refs[n_in:]
        cast_in, refs = refs[:n_cast], refs[n_cast:]
        outs, refs = refs[:n_out], refs[n_out:]
        cast_out, scratch = refs[:n_cast], refs[n_cast:]
        for src, dst in zip(cast_in, cast_out):
            dst[...] = src[...].astype(BF16)
        body(*ins, *outs, *scratch)
    return kernel


def _gla_inproj_kernel(x_ref, gain_ref, w_ref, wg2_ref, b_ref, q_ref, k_ref, v_ref, g_ref, la_ref, wb_ref):
    @pl.when(pl.program_id(0) == 0)
    def _():
        wb_ref[...] = w_ref[...].astype(BF16)

    n_main = 2 * GLA_DK + 2 * GLA_DV
    h = _rms(x_ref[...], gain_ref[...]).astype(BF16)
    q_ref[...] = _dot(h, wb_ref[:, 0:GLA_DK]).astype(BF16)
    k_ref[...] = _dot(h, wb_ref[:, GLA_DK:2 * GLA_DK]).astype(BF16)
    v_ref[...] = _dot(h, wb_ref[:, 2 * GLA_DK:2 * GLA_DK + GLA_DV]).astype(BF16)
    g_ref[...] = _dot(h, wb_ref[:, 2 * GLA_DK + GLA_DV:n_main]).astype(BF16)
    a_lr = _dot(h, wb_ref[:, n_main:])
    a_hi, a_lo = _split(a_lr)
    w_hi, w_lo = _split(wg2_ref[...])
    z = _dot(jnp.concatenate([a_hi, a_lo, a_hi], axis=1), jnp.concatenate([w_hi, w_hi, w_lo], axis=0))
    la_ref[...] = _log_sigmoid(z + b_ref[...]) / GLA_NORMALIZER


def _gla_inproj(x, gain, w_in, w_gate2, b_gate, ride_along, tm):
    t = x.shape[0]
    row = lambda i: (i, 0)
    fixed = lambda i: (0, 0)
    cast_specs, cast_shapes = _ride_along_specs(ride_along, t // tm, lambda i: i)
    outs = pl.pallas_call(
        _with_ride_along(_gla_inproj_kernel, 5, 5, len(ride_along)),
        grid=(t // tm,),
        in_specs=[
            pl.BlockSpec((tm, D_MODEL), row),
            pl.BlockSpec((1, D_MODEL), fixed),
            pl.BlockSpec((None,) + w_in.shape[1:], lambda i: (0, 0, 0), pipeline_mode=pl.Buffered(1)),
            pl.BlockSpec((GLA_RANK, GLA_DK), fixed),
            pl.BlockSpec((1, GLA_DK), fixed),
        ] + cast_specs,
        out_specs=[
            pl.BlockSpec((tm, GLA_DK), row),
            pl.BlockSpec((tm, GLA_DK), row),
            pl.BlockSpec((tm, GLA_DV), row),
            pl.BlockSpec((tm, GLA_DV), row),
            pl.BlockSpec((tm, GLA_DK), row),
        ] + cast_specs,
        out_shape=[
            jax.ShapeDtypeStruct((t, GLA_DK), BF16),
            jax.ShapeDtypeStruct((t, GLA_DK), BF16),
            jax.ShapeDtypeStruct((t, GLA_DV), BF16),
            jax.ShapeDtypeStruct((t, GLA_DV), BF16),
            jax.ShapeDtypeStruct((t, GLA_DK), F32),
        ] + cast_shapes,
        scratch_shapes=[pltpu.VMEM(w_in.shape[1:], BF16)],
        compiler_params=_params(("arbitrary",), 48 * 1024 * 1024),
        name="gla_inproj",
    )(x, gain, w_in, w_gate2, b_gate, *ride_along)
    return outs[:5], outs[5:]


def _gla_kernel(q_ref, k_ref, v_ref, g_ref, la_ref, og_ref, o_ref, st_ref, *, n_chunks):
    @pl.when(pl.program_id(1) == 0)
    def _():
        st_ref[...] = jnp.zeros_like(st_ref)

    blk = n_chunks * GLA_CHUNK
    r = lax.broadcasted_iota(I32, (blk, blk), 0)
    c = lax.broadcasted_iota(I32, (blk, blk), 1)
    causal = (r >= c) & (r // GLA_CHUNK == c // GLA_CHUNK)
    tril = causal.astype(BF16)
    la = la_ref[...]
    la_hi, la_mid = _split(la)
    la_lo = (la - la_hi.astype(F32) - la_mid.astype(F32)).astype(BF16)
    cum = _dot(tril, la_hi) + _dot(tril, la_mid) + _dot(tril, la_lo)
    ends = [cum[(ci + 1) * GLA_CHUNK - 1:(ci + 1) * GLA_CHUNK, :] for ci in range(n_chunks)]
    last = jnp.concatenate([jnp.broadcast_to(e, (GLA_CHUNK, GLA_DK)) for e in ends], axis=0)
    q = q_ref[...].astype(F32) * (GLA_HK ** -0.5)
    k = k_ref[...].astype(F32)
    q_dec = (q * jnp.exp(cum)).astype(BF16)
    k_inv = (k * jnp.exp(-cum)).astype(BF16)
    k_end = (k * jnp.exp(last - cum)).astype(BF16)
    chunk_decay = [jnp.exp(e) for e in ends]
    og = og_ref[...]
    heads = range(GLA_HEADS)
    hk = [slice(h * GLA_HK, (h + 1) * GLA_HK) for h in heads]
    hv = [slice(h * GLA_HV, (h + 1) * GLA_HV) for h in heads]
    scores = [jnp.where(causal, _dot_nt(q_dec[:, hk[h]], k_inv[:, hk[h]]), 0.0).astype(BF16) for h in heads]
    state_in = [[_dot_tn(v_ref[ci * GLA_CHUNK:(ci + 1) * GLA_CHUNK, hv[h]],
                         k_end[ci * GLA_CHUNK:(ci + 1) * GLA_CHUNK, hk[h]])
                 for ci in range(n_chunks)] for h in heads]
    intra = [_dot(scores[h], v_ref[:, hv[h]]) for h in heads]
    for h in heads:
        st = st_ref[h]
        inter = []
        for ci in range(n_chunks):
            rows = slice(ci * GLA_CHUNK, (ci + 1) * GLA_CHUNK)
            inter.append(_dot_nt(q_dec[rows, hk[h]], st.astype(BF16)))
            st = st * chunk_decay[ci][:, hk[h]] + state_in[h][ci]
        st_ref[h] = st
        o = intra[h] + jnp.concatenate(inter, axis=0)
        o = o * lax.rsqrt(jnp.mean(o * o, axis=-1, keepdims=True) + EPS) * og
        o = o * _silu(g_ref[:, hv[h]].astype(F32))
        o_ref[:, hv[h]] = o.astype(BF16)


def _gla(q, k, v, g, la, out_gain, ride_along, batch, seq, blk):
    t = q.shape[0]
    nb = seq // blk
    row = lambda b, i: (b * nb + i, 0)
    cast_specs, cast_shapes = _ride_along_specs(ride_along, batch * nb, lambda b, i: b * nb + i)
    outs = pl.pallas_call(
        _with_ride_along(functools.partial(_gla_kernel, n_chunks=blk // GLA_CHUNK), 6, 1, len(ride_along)),
        grid=(batch, nb),
        in_specs=[
            pl.BlockSpec((blk, GLA_DK), row),
            pl.BlockSpec((blk, GLA_DK), row),
            pl.BlockSpec((blk, GLA_DV), row),
            pl.BlockSpec((blk, GLA_DV), row),
            pl.BlockSpec((blk, GLA_DK), row),
            pl.BlockSpec((1, GLA_HV), lambda b, i: (0, 0)),
        ] + cast_specs,
        out_specs=[pl.BlockSpec((blk, GLA_DV), row)] + cast_specs,
        out_shape=[jax.ShapeDtypeStruct((t, GLA_DV), BF16)] + cast_shapes,
        scratch_shapes=[pltpu.VMEM((GLA_HEADS, GLA_HV, GLA_HK), F32)],
        compiler_params=_params(("arbitrary", "arbitrary"), 48 * 1024 * 1024),
        name="gla_core",
    )(q, k, v, g, la, out_gain, *ride_along)
    return outs[0], outs[1:]


def _ffn_kernel(x_ref, o_ref, wo_ref, gain_ref, wg_ref, wu_ref, wd_ref, y_ref, h_ref, *, pieces):
    j = pl.program_id(1)

    @pl.when(j == 0)
    def _():
        x1 = x_ref[...] + _dot(o_ref[...], wo_ref[...])
        h = _rms(x1, gain_ref[...]).astype(BF16)
        h_ref[...] = h
        y_ref[...] = x1 + _swiglu_pieces(h, wg_ref, wu_ref, wd_ref, pieces)

    @pl.when(j > 0)
    def _():
        y_ref[...] += _swiglu_pieces(h_ref[...], wg_ref, wu_ref, wd_ref, pieces)


def _ffn_residual(x, o, w_out, gain, w_gate, w_up, w_down, tm, pieces):
    t, d = x.shape
    ff = w_gate.shape[1]
    tf = sum(pieces)
    return pl.pallas_call(
        functools.partial(_ffn_kernel, pieces=pieces),
        grid=(t // tm, ff // tf),
        in_specs=[
            pl.BlockSpec((tm, d), lambda i, j: (i, 0)),
            pl.BlockSpec((tm, d), lambda i, j: (i, 0)),
            pl.BlockSpec((d, d), lambda i, j: (0, 0), pipeline_mode=pl.Buffered(1)),
            pl.BlockSpec((1, d), lambda i, j: (0, 0)),
            pl.BlockSpec((d, tf), lambda i, j: (0, j)),
            pl.BlockSpec((d, tf), lambda i, j: (0, j)),
            pl.BlockSpec((tf, d), lambda i, j: (j, 0)),
        ],
        out_specs=pl.BlockSpec((tm, d), lambda i, j: (i, 0)),
        out_shape=jax.ShapeDtypeStruct((t, d), F32),
        scratch_shapes=[pltpu.VMEM((tm, d), BF16)],
        compiler_params=_params(("arbitrary", "arbitrary"), 56 * 1024 * 1024),
        name="ffn_swiglu",
    )(x, o, w_out, gain, w_gate, w_up, w_down)


def _moba_qkv_kernel(x_ref, gain_ref, w_ref, qg_ref, kg_ref, q_ref, k_ref, v_ref):
    h = _rms(x_ref[...], gain_ref[...]).astype(BF16)
    d = MOBA_HEADS * MOBA_HD
    qg = qg_ref[...]
    kg = kg_ref[...]
    q = _dot(h, w_ref[:, 0:d])
    k = _dot(h, w_ref[:, d:2 * d])
    for hd in range(MOBA_HEADS):
        cols = slice(hd * MOBA_HD, (hd + 1) * MOBA_HD)
        q_ref[:, cols] = (_rms(q[:, cols], qg) * MOBA_Q_SCALE).astype(BF16)
        k_ref[:, cols] = _rms(k[:, cols], kg).astype(BF16)
    v_ref[...] = _dot(h, w_ref[:, 2 * d:]).astype(BF16)


def _moba_qkv(x, gain, w, q_gain, k_gain, tm):
    t = x.shape[0]
    d = MOBA_HEADS * MOBA_HD
    row = lambda i: (i, 0)
    fixed = lambda i: (0, 0)
    return pl.pallas_call(
        _moba_qkv_kernel,
        grid=(t // tm,),
        in_specs=[
            pl.BlockSpec((tm, D_MODEL), row),
            pl.BlockSpec((1, D_MODEL), fixed),
            pl.BlockSpec((D_MODEL, 3 * d), fixed),
            pl.BlockSpec((1, MOBA_HD), fixed),
            pl.BlockSpec((1, MOBA_HD), fixed),
        ],
        out_specs=[pl.BlockSpec((tm, d), row)] * 3,
        out_shape=[jax.ShapeDtypeStruct((t, d), BF16)] * 3,
        compiler_params=_params(("parallel",), 48 * 1024 * 1024),
        name="moba_qkv",
    )(x, gain, w, q_gain, k_gain)


def _reduce_rows(x, op, stop=8):
    while x.shape[0] > stop:
        half = x.shape[0] // 2
        x = op(x[:half], x[half:])
    return x


def _moba_kernel(q_ref, k_ref, v_ref, o_ref, *, n_blocks):
    seq = n_blocks * MOBA_BLOCK
    n_pad = 16
    v_t = v_ref[...].astype(F32).T.astype(BF16)
    blk_of_key = lax.broadcasted_iota(I32, (n_pad, seq), 1) // MOBA_BLOCK
    blk_row = lax.broadcasted_iota(I32, (n_pad, seq), 0)
    indicator = jnp.where(blk_of_key == blk_row, 1.0 / MOBA_BLOCK, 0.0).astype(BF16)
    k_mean = _dot(indicator, k_ref[...])
    km_hi, km_lo = _split(k_mean)

    blk_id = lax.broadcasted_iota(I32, (n_pad, MOBA_BLOCK), 0)
    key_pos = lax.broadcasted_iota(I32, (MOBA_BLOCK, MOBA_BLOCK), 0)
    query_pos = lax.broadcasted_iota(I32, (MOBA_BLOCK, MOBA_BLOCK), 1)
    causal = key_pos <= query_pos

    def masked_scores(i):
        rows = slice(i * MOBA_BLOCK, (i + 1) * MOBA_BLOCK)
        n_keys = (i + 1) * MOBA_BLOCK
        q = q_ref[rows, :]
        s = _dot_nt(k_ref[0:n_keys, :], q)
        selected = None
        if i > MOBA_TOPK:
            gate = _dot_nt(km_hi, q) + _dot_nt(km_lo, q)
            gate = jnp.where(blk_id < i, gate, NEG_INF)
            rank = jnp.zeros_like(gate)
            for jp in range(i):
                row = gate[jp:jp + 1, :]
                beats = (row > gate) | ((row == gate) & (jp < blk_id))
                rank = rank + beats.astype(F32)
            selected = jnp.where((rank < MOBA_TOPK) & (blk_id < i), 1.0, 0.0)
        pieces = []
        for j in range(i + 1):
            sj = s[j * MOBA_BLOCK:(j + 1) * MOBA_BLOCK, :]
            if j == i:
                sj = jnp.where(causal, sj, NEG_INF)
            elif selected is not None:
                sj = jnp.where(selected[j:j + 1, :] > 0.5, sj, NEG_INF)
            pieces.append(sj)
        return pieces

    upcoming = masked_scores(0)
    for i in range(n_blocks):
        rows = slice(i * MOBA_BLOCK, (i + 1) * MOBA_BLOCK)
        n_keys = (i + 1) * MOBA_BLOCK
        pieces = upcoming
        if i + 1 < n_blocks:
            upcoming = masked_scores(i + 1)
        m = _reduce_rows(functools.reduce(jnp.maximum, pieces), jnp.maximum).max(axis=0, keepdims=True)
        probs = [jnp.exp2(sj - m) for sj in pieces]
        denom = _reduce_rows(functools.reduce(jnp.add, probs), jnp.add).sum(axis=0, keepdims=True)
        p = probs[0] if len(probs) == 1 else jnp.concatenate(probs, axis=0)
        o_t = _dot(v_t[:, 0:n_keys], p.astype(BF16))
        o_ref[rows, :] = (o_t / denom).T.astype(BF16)


def _moba(q, k, v, ride_along, batch, seq):
    t, d = q.shape
    spec = pl.BlockSpec((seq, MOBA_HD), lambda b, h: (b, h))
    cast_specs, cast_shapes = _ride_along_specs(ride_along, batch * MOBA_HEADS, lambda b, h: b * MOBA_HEADS + h)
    outs = pl.pallas_call(
        _with_ride_along(functools.partial(_moba_kernel, n_blocks=seq // MOBA_BLOCK), 3, 1, len(ride_along)),
        grid=(batch, MOBA_HEADS),
        in_specs=[spec, spec, spec] + cast_specs,
        out_specs=[spec] + cast_specs,
        out_shape=[jax.ShapeDtypeStruct((t, d), BF16)] + cast_shapes,
        compiler_params=_params(("arbitrary", "arbitrary"), 48 * 1024 * 1024),
        name="moba_attn",
    )(q, k, v, *ride_along)
    return outs[0], outs[1:]


def _router_kernel(x_ref, o_ref, wo_ref, gain_ref, w_ref, x2_ref, h_ref, info_ref, cnt_ref, carry_ref, tri_ref,
                   *, region_rows):
    tr = x_ref.shape[0]

    @pl.when(pl.program_id(0) == 0)
    def _():
        carry_ref[...] = jnp.zeros_like(carry_ref)
        r = lax.broadcasted_iota(I32, (tr, tr), 0)
        c = lax.broadcasted_iota(I32, (tr, tr), 1)
        tri_ref[...] = (c < r).astype(BF16)

    x2 = x_ref[...] + _dot(o_ref[...], wo_ref[...])
    x2_ref[...] = x2
    h = _rms(x2, gain_ref[...])
    h_hi, h_lo = _split(h)
    h_ref[...] = h_hi
    both = _dot(h_hi, w_ref[...])
    logits = both[:, :LANES] + both[:, LANES:] + _dot(h_lo, w_ref[:, :LANES])
    e = lax.broadcasted_iota(I32, logits.shape, 1)
    logits = jnp.where(e < N_EXPERTS, logits, -jnp.inf)
    m1 = logits.max(axis=-1, keepdims=True)
    i1 = jnp.where(logits == m1, e, LANES).min(axis=-1, keepdims=True)
    first = e == i1
    rest = jnp.where(first, -jnp.inf, logits)
    m2 = rest.max(axis=-1, keepdims=True)
    i2 = jnp.where(rest == m2, e, LANES).min(axis=-1, keepdims=True)
    second = e == i2
    e2 = jnp.exp(m2 - m1)
    denom = 1.0 + e2
    sel = jnp.where(first | second, 1.0, 0.0).astype(BF16)
    before = _dot(tri_ref[...], sel) + carry_ref[...]
    rank1 = jnp.where(first, before, 0.0).sum(axis=-1, keepdims=True)
    rank2 = jnp.where(second, before, 0.0).sum(axis=-1, keepdims=True)
    nb = tr // TOK_BLK
    blk_row = lax.broadcasted_iota(I32, (nb, tr), 0)
    blk_tok = lax.broadcasted_iota(I32, (nb, tr), 1) // TOK_BLK
    cnt = _dot((blk_row == blk_tok).astype(BF16), sel)
    cnt_ref[...] = cnt
    carry_ref[...] += cnt.sum(axis=0, keepdims=True)
    col = lax.broadcasted_iota(I32, (tr, 4), 1)
    fields = (i1.astype(F32) * region_rows + rank1, i2.astype(F32) * region_rows + rank2,
              1.0 / denom, e2 / denom)
    info = jnp.zeros((tr, 4), F32)
    for n, f in enumerate(fields):
        info = jnp.where(col == n, f, info)
    info_ref[...] = info


def _router(x, o, w_out, gain, w_router, tr):
    t = x.shape[0]
    nb = tr // TOK_BLK
    pad = ((0, 0), (0, LANES - N_EXPERTS))
    w_hi = w_router.astype(BF16)
    w_lo = (w_router - w_hi.astype(F32)).astype(BF16)
    w_both = jnp.concatenate([jnp.pad(w_hi, pad), jnp.pad(w_lo, pad)], axis=1)
    row = lambda i: (i, 0)
    fixed = lambda i: (0, 0)
    return pl.pallas_call(
        functools.partial(_router_kernel, region_rows=float(t)),
        grid=(t // tr,),
        in_specs=[
            pl.BlockSpec((tr, D_MODEL), row),
            pl.BlockSpec((tr, D_MODEL), row),
            pl.BlockSpec((D_MODEL, D_MODEL), fixed),
            pl.BlockSpec((1, D_MODEL), fixed),
            pl.BlockSpec((D_MODEL, 2 * LANES), fixed),
        ],
        out_specs=[
            pl.BlockSpec((tr, D_MODEL), row),
            pl.BlockSpec((tr, D_MODEL), row),
            pl.BlockSpec((tr, 4), row),
            pl.BlockSpec((nb, LANES), row),
        ],
        out_shape=[
            jax.ShapeDtypeStruct((t, D_MODEL), F32),
            jax.ShapeDtypeStruct((t, D_MODEL), BF16),
            jax.ShapeDtypeStruct((t, 4), F32),
            jax.ShapeDtypeStruct((t // TOK_BLK, LANES), F32),
        ],
        scratch_shapes=[pltpu.VMEM((1, LANES), F32), pltpu.VMEM((tr, tr), BF16)],
        compiler_params=_params(("arbitrary",), 56 * 1024 * 1024),
        name="moe_router",
    )(x, o, w_out, gain, w_both)


def _moe_plan(cnt, t):
    e_ids = jnp.arange(N_EXPERTS, dtype=I32)
    by_expert = lambda table, idx: jnp.sum(jnp.where(idx[..., None] == e_ids, table, 0), axis=-1)
    cnt_blk = cnt[:, :N_EXPERTS].astype(I32)
    cum_blk = jnp.cumsum(cnt_blk, axis=0)
    counts = cum_blk[-1]
    tiles_e = (counts + MOE_TM - 1) // MOE_TM
    tile_end = jnp.cumsum(tiles_e)
    tile_off = tile_end - tiles_e
    nt = TOP_K * t // MOE_TM + N_EXPERTS
    r = jnp.arange(nt, dtype=I32)
    valid = r < tile_end[-1]
    te = jnp.minimum(jnp.sum((r[:, None] >= tile_end[None, :]).astype(I32), axis=1), N_EXPERTS - 1)
    local = r - by_expert(tile_off, te)
    last_e = jnp.max(jnp.where(tiles_e > 0, e_ids, 0))
    tile_expert = jnp.where(valid, te, last_e)
    shift = e_ids * t - tile_off * MOE_TM

    per = MOE_TM // MOE_TG
    rg = jnp.arange(nt * per, dtype=I32)
    eg = jnp.repeat(te, per)
    k0 = jnp.repeat(local, per) * MOE_TM + (rg % per) * MOE_TG
    k1 = jnp.minimum(k0 + MOE_TG, by_expert(counts, eg))
    has = jnp.repeat(valid, per) & (k1 > k0)
    cum_e = by_expert(cum_blk[None, :, :], eg[:, None])
    s_lo = jnp.sum((cum_e <= k0[:, None]).astype(I32), axis=1)
    s_hi = jnp.sum((cum_e <= (k1 - 1)[:, None]).astype(I32), axis=1)
    g_nwin = jnp.where(has, (s_hi - s_lo) // GATHER_WB + 1, 0)
    g_slo = jnp.where(has, s_lo, 0)
    g_code = eg * t + k0

    nts = t // COMB_TS
    cnt_ts = cnt_blk.reshape(nts, COMB_TS // TOK_BLK, N_EXPERTS).sum(axis=1)
    seg_lo = (tile_off * MOE_TM)[None, :] + jnp.cumsum(cnt_ts, axis=0) - cnt_ts
    seg_hi = seg_lo + cnt_ts
    ws0 = (seg_lo // BF16_ROWS) * BF16_ROWS
    nw = jnp.where(cnt_ts > 0, (seg_hi - ws0 + COMB_WC - 1) // COMB_WC, 0)
    nw_end = jnp.cumsum(nw, axis=1)
    k = jnp.arange(COMB_MAXW, dtype=I32)
    ek = jnp.minimum(jnp.sum((k[None, :, None] >= nw_end[:, None, :]).astype(I32), axis=2), N_EXPERTS - 1)
    pick = lambda a: by_expert(a[:, None, :], ek)
    live = k[None, :] < nw_end[:, -1:]
    nominal = pick(ws0) + (k[None, :] - pick(nw_end - nw)) * COMB_WC
    shift_k = by_expert(shift, ek)
    c_start = jnp.minimum(nominal, by_expert(tile_end * MOE_TM, ek) - COMB_WC)
    c_off = shift_k + c_start
    c_lo = jnp.maximum(pick(seg_lo), nominal) + shift_k
    c_hi = jnp.minimum(pick(seg_hi), nominal + COMB_WC) + shift_k
    flat = lambda a: jnp.where(live, a, 0).reshape(-1)
    return dict(
        tile_expert=tile_expert, tile_valid=valid.astype(I32),
        g_code=g_code, g_slo=g_slo, g_nwin=g_nwin,
        c_start=flat(c_start), c_off=flat(c_off), c_lo=flat(c_lo), c_hi=flat(c_hi), c_nwin=nw_end[:, -1])


def _moe_gather_kernel(code_ref, slo_ref, nwin_ref, rows_ref, h_ref, hs_ref, gs_ref):
    n_blk = rows_ref.shape[1]
    hs_ref[...] = jnp.zeros_like(hs_ref)
    gs_ref[...] = jnp.zeros_like(gs_ref)
    for sub in range(hs_ref.shape[0] // MOE_TG):
        _gather_subtile(pl.program_id(0) * (hs_ref.shape[0] // MOE_TG) + sub,
                        slice(sub * MOE_TG, (sub + 1) * MOE_TG),
                        code_ref, slo_ref, nwin_ref, rows_ref, h_ref, hs_ref, gs_ref, n_blk)


def _gather_subtile(r, out_rows, code_ref, slo_ref, nwin_ref, rows_ref, h_ref, hs_ref, gs_ref, n_blk):
    row_id = (code_ref[r] + lax.broadcasted_iota(I32, (MOE_TG, TOK_BLK), 0)).astype(F32)

    def body(k, carry):
        nominal = slo_ref[r] + k * GATHER_WB
        sb = jnp.minimum(nominal, n_blk - GATHER_WB)
        fresh = sb + lax.broadcasted_iota(I32, (GATHER_WB, TOK_BLK), 0) >= nominal
        p1 = jnp.where(fresh, rows_ref[0, pl.ds(sb, GATHER_WB), :], -1.0)
        p2 = jnp.where(fresh, rows_ref[1, pl.ds(sb, GATHER_WB), :], -1.0)
        w1 = rows_ref[2, pl.ds(sb, GATHER_WB), :]
        w2 = rows_ref[3, pl.ds(sb, GATHER_WB), :]
        pieces = []
        g = jnp.zeros((MOE_TG, TOK_BLK), F32)
        for a in range(GATHER_WB):
            m1 = p1[a:a + 1, :] == row_id
            m2 = p2[a:a + 1, :] == row_id
            pieces.append(jnp.where(m1 | m2, 1.0, 0.0).astype(BF16))
            g = g + jnp.where(m1, w1[a:a + 1, :], 0.0) + jnp.where(m2, w2[a:a + 1, :], 0.0)
        onehot = jnp.concatenate(pieces, axis=1)
        tok0 = pl.multiple_of(sb * TOK_BLK, TOK_BLK)
        hs_ref[out_rows, :] += _dot(onehot, h_ref[pl.ds(tok0, GATHER_WB * TOK_BLK), :]).astype(BF16)
        gs_ref[out_rows, :] += g.sum(axis=-1, keepdims=True)
        return carry

    lax.fori_loop(0, nwin_ref[r], body, 0)


def _moe_gather(plan, h, rows):
    t = h.shape[0]
    n_rows = plan["g_code"].shape[0] * MOE_TG
    n_blk = t // TOK_BLK
    grid_spec = pltpu.PrefetchScalarGridSpec(
        num_scalar_prefetch=3,
        grid=(n_rows // MOE_TM,),
        in_specs=[
            pl.BlockSpec((4, n_blk, TOK_BLK), lambda r, *_: (0, 0, 0), pipeline_mode=pl.Buffered(1)),
            pl.BlockSpec((t, D_MODEL), lambda r, *_: (0, 0), pipeline_mode=pl.Buffered(1)),
        ],
        out_specs=[
            pl.BlockSpec((MOE_TM, D_MODEL), lambda r, *_: (r, 0)),
            pl.BlockSpec((MOE_TM, 1), lambda r, *_: (r, 0)),
        ],
    )
    return pl.pallas_call(
        _moe_gather_kernel,
        grid_spec=grid_spec,
        out_shape=[
            jax.ShapeDtypeStruct((n_rows, D_MODEL), BF16),
            jax.ShapeDtypeStruct((n_rows, 1), F32),
        ],
        compiler_params=_params(("parallel",), 52 * 1024 * 1024),
        name="moe_gather",
    )(plan["g_code"], plan["g_slo"], plan["g_nwin"], rows, h)


def _moe_expert_kernel(te_ref, tv_ref, hs_ref, gs_ref, wg_ref, wu_ref, wd_ref, y_ref, acc_ref, *, nj, pieces):
    r = pl.program_id(0)
    j = pl.program_id(1)
    valid = tv_ref[r] > 0
    part = lambda: _swiglu_pieces(hs_ref[...], wg_ref, wu_ref, wd_ref, pieces)

    assert nj >= 2, nj

    @pl.when(valid & (j == 0))
    def _():
        acc_ref[...] = part()

    if nj > 2:
        @pl.when(valid & (j > 0) & (j < nj - 1))
        def _():
            acc_ref[...] += part()

    @pl.when(valid & (j == nj - 1))
    def _():
        y_ref[...] = ((acc_ref[...] + part()) * gs_ref[...]).astype(BF16)

    @pl.when(jnp.logical_not(valid) & (j == nj - 1))
    def _():
        y_ref[...] = jnp.zeros_like(y_ref)


def _moe_experts(plan, hs, gs, w_gate, w_up, w_down, pieces):
    tf = sum(pieces)
    n_rows, d = hs.shape
    ff = w_gate.shape[2]
    nj = ff // tf
    jj = lambda r, j, tv: jnp.where(tv[r] > 0, j, nj - 1)
    grid_spec = pltpu.PrefetchScalarGridSpec(
        num_scalar_prefetch=2,
        grid=(n_rows // MOE_TM, nj),
        in_specs=[
            pl.BlockSpec((MOE_TM, d), lambda r, j, te, tv: (r, 0)),
            pl.BlockSpec((MOE_TM, 1), lambda r, j, te, tv: (r, 0)),
            pl.BlockSpec((None, d, tf), lambda r, j, te, tv: (te[r], 0, jj(r, j, tv))),
            pl.BlockSpec((None, d, tf), lambda r, j, te, tv: (te[r], 0, jj(r, j, tv))),
            pl.BlockSpec((None, tf, d), lambda r, j, te, tv: (te[r], jj(r, j, tv), 0)),
        ],
        out_specs=pl.BlockSpec((MOE_TM, d), lambda r, j, te, tv: (r, 0)),
        scratch_shapes=[pltpu.VMEM((MOE_TM, d), F32)],
    )
    return pl.pallas_call(
        functools.partial(_moe_expert_kernel, nj=nj, pieces=pieces),
        grid_spec=grid_spec,
        out_shape=jax.ShapeDtypeStruct((n_rows, d), BF16),
        compiler_params=_params(("arbitrary", "arbitrary"), 56 * 1024 * 1024),
        name="moe_experts",
    )(plan["tile_expert"], plan["tile_valid"], hs, gs, w_gate, w_up, w_down)


def _moe_combine_kernel(start_ref, off_ref, lo_ref, hi_ref, nwin_ref, x_ref, info_ref, y_hbm, o_ref,
                        ybuf, ybuf_more, sem, sem_more):
    i = pl.program_id(0)
    n_tiles = pl.num_programs(0)

    def window_copy(idx, dst, dsem):
        start = pl.multiple_of(start_ref[idx], BF16_ROWS)
        return pltpu.make_async_copy(y_hbm.at[pl.ds(start, COMB_WC), :], dst, dsem)

    def first_group(tile, wait):
        buf = tile % 2
        for s in range(COMB_GROUP):
            @pl.when(s < nwin_ref[tile])
            def _():
                cp = window_copy(tile * COMB_MAXW + s, ybuf.at[buf, pl.ds(s * COMB_WC, COMB_WC), :],
                                 sem.at[buf, s])
                cp.wait() if wait else cp.start()

    @pl.when(i == 0)
    def _():
        ybuf[...] = jnp.zeros_like(ybuf)
        ybuf_more[...] = jnp.zeros_like(ybuf_more)
        first_group(0, wait=False)

    @pl.when(i + 1 < n_tiles)
    def _():
        first_group(i + 1, wait=False)

    first_group(i, wait=True)

    code1 = jnp.broadcast_to(info_ref[:, 0:1].astype(I32), (COMB_TS, COMB_WC))
    code2 = jnp.broadcast_to(info_ref[:, 1:2].astype(I32), (COMB_TS, COMB_WC))
    lane = lax.broadcasted_iota(I32, (COMB_TS, COMB_WC), 1)
    lane_row = lax.broadcasted_iota(I32, (1, COMB_WC), 1)

    def onehot_group(g):
        pieces = []
        for s in range(COMB_GROUP):
            idx = i * COMB_MAXW + g * COMB_GROUP + s
            off = off_ref[idx]
            in_span = (lane_row >= lo_ref[idx] - off) & (lane_row < hi_ref[idx] - off)
            target = lane + off
            hit = ((code1 == target) | (code2 == target)) & in_span
            pieces.append(jnp.where(hit, 1.0, 0.0).astype(BF16))
        return jnp.concatenate(pieces, axis=1)

    o_ref[...] = x_ref[...] + _dot(onehot_group(0), ybuf[i % 2])

    n = nwin_ref[i]

    def more(g, carry):
        for wait in (False, True):
            for s in range(COMB_GROUP):
                @pl.when(g * COMB_GROUP + s < n)
                def _():
                    cp = window_copy(i * COMB_MAXW + g * COMB_GROUP + s,
                                     ybuf_more.at[pl.ds(s * COMB_WC, COMB_WC), :], sem_more.at[s])
                    cp.wait() if wait else cp.start()
        o_ref[...] += _dot(onehot_group(g), ybuf_more[...])
        return carry

    lax.fori_loop(1, (n + COMB_GROUP - 1) // COMB_GROUP, more, 0)


def _moe_combine(plan, x, info, y):
    t, d = x.shape
    width = COMB_GROUP * COMB_WC
    grid_spec = pltpu.PrefetchScalarGridSpec(
        num_scalar_prefetch=5,
        grid=(t // COMB_TS,),
        in_specs=[
            pl.BlockSpec((COMB_TS, d), lambda i, *_: (i, 0)),
            pl.BlockSpec((COMB_TS, 4), lambda i, *_: (i, 0)),
            pl.BlockSpec(memory_space=pl.ANY),
        ],
        out_specs=pl.BlockSpec((COMB_TS, d), lambda i, *_: (i, 0)),
        scratch_shapes=[
            pltpu.VMEM((2, width, d), BF16),
            pltpu.VMEM((width, d), BF16),
            pltpu.SemaphoreType.DMA((2, COMB_GROUP)),
            pltpu.SemaphoreType.DMA((COMB_GROUP,)),
        ],
    )
    return pl.pallas_call(
        _moe_combine_kernel,
        grid_spec=grid_spec,
        out_shape=jax.ShapeDtypeStruct((t, d), F32),
        compiler_params=_params(("arbitrary",)),
        name="moe_combine",
    )(plan["c_start"], plan["c_off"], plan["c_lo"], plan["c_hi"], plan["c_nwin"], x, info, y)


def kernel(x, norm_mix, norm_ffn, gla_w_in, gla_w_gate2, gla_b_gate, gla_out_gain, gla_w_out,
           moba_w_qkv, moba_q_gain, moba_k_gain, moba_w_out, ffn_w_gate, ffn_w_up, ffn_w_down,
           moe_w_router, moe_w_gate, moe_w_up, moe_w_down):
    batch, seq, d = x.shape
    t = batch * seq
    xt = x.reshape(t, d)

    (q, k, v, g, la), (ffn_wg, ffn_wu, ffn_wd, gla_wo, moba_wqkv, moba_wo) = _gla_inproj(
        xt, norm_mix[0:1], gla_w_in, gla_w_gate2[0], gla_b_gate[0:1],
        ride_along=(ffn_w_gate, ffn_w_up, ffn_w_down, gla_w_out, moba_w_qkv, moba_w_out), tm=512)
    o, (moe_wg, moe_wu) = _gla(q, k, v, g, la, gla_out_gain[0:1], (moe_w_gate[0], moe_w_up[0]),
                               batch, seq, blk=256)
    xt = _ffn_residual(xt, o, gla_wo[0], norm_ffn[0:1], ffn_wg[0], ffn_wu[0], ffn_wd[0],
                       tm=512, pieces=MOE_PIECES)

    q, k, v = _moba_qkv(xt, norm_mix[1:2], moba_wqkv[0], moba_q_gain[0:1], moba_k_gain[0:1], tm=512)
    o, (moe_wd,) = _moba(q, k, v, (moe_w_down[0],), batch, seq)
    xt, h, info, cnt = _router(xt, o, moba_wo[0], norm_ffn[1:2], moe_w_router[0], tr=1024)
    plan = _moe_plan(cnt, t)
    rows = info.T.reshape(4, t // TOK_BLK, TOK_BLK)
    hs, gs = _moe_gather(plan, h, rows)
    y = _moe_experts(plan, hs, gs, moe_wg, moe_wu, moe_wd, pieces=MOE_PIECES)
    xt = _moe_combine(plan, xt, info, y)
    return xt.reshape(batch, seq, d)
```

```python
import functools

import jax
import jax.numpy as jnp
from jax import lax
from jax.experimental import pallas as pl
from jax.experimental.pallas import tpu as pltpu

F32 = jnp.float32
BF16 = jnp.bfloat16
I32 = jnp.int32

EPS = 1e-6
NEG_INF = -1e30

D_MODEL = 1024
GLA_HEADS = 4
GLA_DK = 512
GLA_DV = 1024
GLA_HK = 128
GLA_HV = 256
GLA_RANK = 16
GLA_NORMALIZER = 16.0
GLA_CHUNK = 64

MOBA_HEADS = 8
MOBA_HD = 128
MOBA_BLOCK = 256
MOBA_TOPK = 3
MOBA_Q_SCALE = MOBA_HD ** -0.5 * 1.4426950408889634

N_EXPERTS = 8
TOP_K = 2

LANES = 128
BF16_ROWS = 16

MOE_PIECES = (512, 512, 768)
MOE_TM = 512
MOE_TG = 128
TOK_BLK = LANES
GATHER_WB = 6
COMB_TS = 256
COMB_WC = 128
COMB_GROUP = 8
COMB_MAXW = 16


def _params(semantics, vmem_bytes=None):
    return pltpu.CompilerParams(dimension_semantics=semantics, vmem_limit_bytes=vmem_bytes)


def _rms(x, gain):
    y = x * lax.rsqrt(jnp.mean(x * x, axis=-1, keepdims=True) + EPS)
    return y * gain


def _dot(a, b):
    return jnp.dot(a, b, preferred_element_type=F32)


def _dot_nt(a, b):
    return lax.dot_general(a, b, (((1,), (1,)), ((), ())), preferred_element_type=F32)


def _dot_tn(a, b):
    return lax.dot_general(a, b, (((0,), (0,)), ((), ())), preferred_element_type=F32)


def _split(a):
    hi = a.astype(BF16)
    lo = (a - hi.astype(F32)).astype(BF16)
    return hi, lo


def _log_sigmoid(z):
    return jnp.minimum(z, 0.0) - jnp.log(1.0 + jnp.exp(-jnp.abs(z)))


def _silu(z):
    return z / (1.0 + jnp.exp(-z))


def _swiglu_pieces(h, wg_ref, wu_ref, wd_ref, widths):
    bounds = [sum(widths[:n]) for n in range(len(widths) + 1)]
    assert bounds[-1] == wg_ref.shape[1], (widths, wg_ref.shape)
    gate_up = lambda c: (_dot(h, wg_ref[:, bounds[c]:bounds[c + 1]]), _dot(h, wu_ref[:, bounds[c]:bounds[c + 1]]))
    out = None
    upcoming = gate_up(0)
    for c in range(len(widths)):
        g, u = upcoming
        if c + 1 < len(widths):
            upcoming = gate_up(c + 1)
        d = _dot((_silu(g) * u).astype(BF16), wd_ref[bounds[c]:bounds[c + 1], :])
        out = d if out is None else out + d
    return out


def _ride_along_specs(arrays, n_steps, step):
    specs, shapes = [], []
    for w in arrays:
        n_e, rows, cols = w.shape
        per_e = n_steps // n_e
        if rows % (per_e * BF16_ROWS) == 0:
            blk = (None, rows // per_e, cols)
            imap = lambda *ids, per_e=per_e: (step(*ids) // per_e, step(*ids) % per_e, 0)
        else:
            assert cols % (per_e * LANES) == 0, (w.shape, n_steps)
            blk = (None, rows, cols // per_e)
            imap = lambda *ids, per_e=per_e: (step(*ids) // per_e, 0, step(*ids) % per_e)
        specs.append(pl.BlockSpec(blk, imap))
        shapes.append(jax.ShapeDtypeStruct(w.shape, BF16))
    return specs, shapes


def _with_ride_along(body, n_in, n_out, n_cast):
    def kernel(*refs):
        ins, refs = refs[:n_in], refs[n_in:]
        cast_in, refs = refs[:n_cast], refs[n_cast:]
        outs, refs = refs[:n_out], refs[n_out:]
        cast_out, scratch = refs[:n_cast], refs[n_cast:]
        for src, dst in zip(cast_in, cast_out):
            dst[...] = src[...].astype(BF16)
        body(*ins, *outs, *scratch)
    return kernel


def _gla_inproj_kernel(x_next_ref, x_first_ref, gain_ref, w_ref, wg2_ref, b_ref,
                       q_ref, k_ref, v_ref, g_ref, la_ref, wb_ref, h_ref):
    i = pl.program_id(0)

    @pl.when(i == 0)
    def _():
        wb_ref[...] = w_ref[...].astype(BF16)
        h_ref[0] = _rms(x_first_ref[...], gain_ref[...]).astype(BF16)

    n_main = 2 * GLA_DK + 2 * GLA_DV

    def step(cur, nxt):
        h = h_ref[cur]
        h_ref[nxt] = _rms(x_next_ref[...], gain_ref[...]).astype(BF16)
        a_lr = _dot(h, wb_ref[:, n_main:])
        a_hi, a_lo = _split(a_lr)
        w_hi, w_lo = _split(wg2_ref[...])
        z = _dot(jnp.concatenate([a_hi, a_lo, a_hi], axis=1), jnp.concatenate([w_hi, w_hi, w_lo], axis=0))
        la_ref[...] = _log_sigmoid(z + b_ref[...]) / GLA_NORMALIZER
        q_ref[...] = _dot(h, wb_ref[:, 0:GLA_DK]).astype(BF16)
        k_ref[...] = _dot(h, wb_ref[:, GLA_DK:2 * GLA_DK]).astype(BF16)
        v_ref[...] = _dot(h, wb_ref[:, 2 * GLA_DK:2 * GLA_DK + GLA_DV]).astype(BF16)
        g_ref[...] = _dot(h, wb_ref[:, 2 * GLA_DK + GLA_DV:n_main]).astype(BF16)

    @pl.when(i % 2 == 0)
    def _():
        step(0, 1)

    @pl.when(i % 2 == 1)
    def _():
        step(1, 0)


def _gla_inproj(x, gain, w_in, w_gate2, b_gate, ride_along, tm):
    t = x.shape[0]
    n_steps = t // tm
    row = lambda i: (i, 0)
    fixed = lambda i: (0, 0)
    cast_specs, cast_shapes = _ride_along_specs(ride_along, n_steps, lambda i: i)
    outs = pl.pallas_call(
        _with_ride_along(_gla_inproj_kernel, 6, 5, len(ride_along)),
        grid=(n_steps,),
        in_specs=[
            pl.BlockSpec((tm, D_MODEL), lambda i: (jnp.minimum(i + 1, n_steps - 1), 0)),
            pl.BlockSpec((tm, D_MODEL), fixed),
            pl.BlockSpec((1, D_MODEL), fixed),
            pl.BlockSpec((None,) + w_in.shape[1:], lambda i: (0, 0, 0), pipeline_mode=pl.Buffered(1)),
            pl.BlockSpec((GLA_RANK, GLA_DK), fixed),
            pl.BlockSpec((1, GLA_DK), fixed),
        ] + cast_specs,
        out_specs=[
            pl.BlockSpec((tm, GLA_DK), row),
            pl.BlockSpec((tm, GLA_DK), row),
            pl.BlockSpec((tm, GLA_DV), row),
            pl.BlockSpec((tm, GLA_DV), row),
            pl.BlockSpec((tm, GLA_DK), row),
        ] + cast_specs,
        out_shape=[
            jax.ShapeDtypeStruct((t, GLA_DK), BF16),
            jax.ShapeDtypeStruct((t, GLA_DK), BF16),
            jax.ShapeDtypeStruct((t, GLA_DV), BF16),
            jax.ShapeDtypeStruct((t, GLA_DV), BF16),
            jax.ShapeDtypeStruct((t, GLA_DK), F32),
        ] + cast_shapes,
        scratch_shapes=[pltpu.VMEM(w_in.shape[1:], BF16), pltpu.VMEM((2, tm, D_MODEL), BF16)],
        compiler_params=_params(("arbitrary",), 48 * 1024 * 1024),
        name="gla_inproj",
    )(x, x, gain, w_in, w_gate2, b_gate, *ride_along)
    return outs[:5], outs[5:]


def _gla_kernel(q_ref, k_ref, v_ref, g_ref, la_ref, og_ref, o_ref, st_ref, *, n_chunks):
    @pl.when(pl.program_id(1) == 0)
    def _():
        st_ref[...] = jnp.zeros_like(st_ref)

    blk = n_chunks * GLA_CHUNK
    r = lax.broadcasted_iota(I32, (blk, blk), 0)
    c = lax.broadcasted_iota(I32, (blk, blk), 1)
    causal = (r >= c) & (r // GLA_CHUNK == c // GLA_CHUNK)
    tril = causal.astype(BF16)
    la = la_ref[...]
    la_hi, la_mid = _split(la)
    la_lo = (la - la_hi.astype(F32) - la_mid.astype(F32)).astype(BF16)
    cum = _dot(tril, la_hi) + _dot(tril, la_mid) + _dot(tril, la_lo)
    ends = [cum[(ci + 1) * GLA_CHUNK - 1:(ci + 1) * GLA_CHUNK, :] for ci in range(n_chunks)]
    last = jnp.concatenate([jnp.broadcast_to(e, (GLA_CHUNK, GLA_DK)) for e in ends], axis=0)
    q = q_ref[...].astype(F32) * (GLA_HK ** -0.5)
    k = k_ref[...].astype(F32)
    q_dec = (q * jnp.exp(cum)).astype(BF16)
    k_inv = (k * jnp.exp(-cum)).astype(BF16)
    k_end = (k * jnp.exp(last - cum)).astype(BF16)
    chunk_decay = [jnp.exp(e) for e in ends]
    og = og_ref[...]
    heads = range(GLA_HEADS)
    hk = [slice(h * GLA_HK, (h + 1) * GLA_HK) for h in heads]
    hv = [slice(h * GLA_HV, (h + 1) * GLA_HV) for h in heads]
    scores = [jnp.where(causal, _dot_nt(q_dec[:, hk[h]], k_inv[:, hk[h]]), 0.0).astype(BF16) for h in heads]
    state_in = [[_dot_tn(v_ref[ci * GLA_CHUNK:(ci + 1) * GLA_CHUNK, hv[h]],
                         k_end[ci * GLA_CHUNK:(ci + 1) * GLA_CHUNK, hk[h]])
                 for ci in range(n_chunks)] for h in heads]
    intra = [_dot(scores[h], v_ref[:, hv[h]]) for h in heads]
    for h in heads:
        st = st_ref[h]
        inter = []
        for ci in range(n_chunks):
            rows = slice(ci * GLA_CHUNK, (ci + 1) * GLA_CHUNK)
            inter.append(_dot_nt(q_dec[rows, hk[h]], st.astype(BF16)))
            st = st * chunk_decay[ci][:, hk[h]] + state_in[h][ci]
        st_ref[h] = st
        o = intra[h] + jnp.concatenate(inter, axis=0)
        o = o * lax.rsqrt(jnp.mean(o * o, axis=-1, keepdims=True) + EPS) * og
        o = o * _silu(g_ref[:, hv[h]].astype(F32))
        o_ref[:, hv[h]] = o.astype(BF16)


def _gla(q, k, v, g, la, out_gain, ride_along, batch, seq, blk):
    t = q.shape[0]
    nb = seq // blk
    row = lambda b, i: (b * nb + i, 0)
    cast_specs, cast_shapes = _ride_along_specs(ride_along, batch * nb, lambda b, i: b * nb + i)
    outs = pl.pallas_call(
        _with_ride_along(functools.partial(_gla_kernel, n_chunks=blk // GLA_CHUNK), 6, 1, len(ride_along)),
        grid=(batch, nb),
        in_specs=[
            pl.BlockSpec((blk, GLA_DK), row),
            pl.BlockSpec((blk, GLA_DK), row),
            pl.BlockSpec((blk, GLA_DV), row),
            pl.BlockSpec((blk, GLA_DV), row),
            pl.BlockSpec((blk, GLA_DK), row),
            pl.BlockSpec((1, GLA_HV), lambda b, i: (0, 0)),
        ] + cast_specs,
        out_specs=[pl.BlockSpec((blk, GLA_DV), row)] + cast_specs,
        out_shape=[jax.ShapeDtypeStruct((t, GLA_DV), BF16)] + cast_shapes,
        scratch_shapes=[pltpu.VMEM((GLA_HEADS, GLA_HV, GLA_HK), F32)],
        compiler_params=_params(("arbitrary", "arbitrary"), 48 * 1024 * 1024),
        name="gla_core",
    )(q, k, v, g, la, out_gain, *ride_along)
    return outs[0], outs[1:]


def _ffn_kernel(x_ref, o_ref, wo_ref, gain_ref, wg_ref, wu_ref, wd_ref, y_ref, h_ref, *, pieces):
    j = pl.program_id(1)

    @pl.when(j == 0)
    def _():
        x1 = x_ref[...] + _dot(o_ref[...], wo_ref[...])
        h = _rms(x1, gain_ref[...]).astype(BF16)
        h_ref[...] = h
        y_ref[...] = x1 + _swiglu_pieces(h, wg_ref, wu_ref, wd_ref, pieces)

    @pl.when(j > 0)
    def _():
        y_ref[...] += _swiglu_pieces(h_ref[...], wg_ref, wu_ref, wd_ref, pieces)


def _ffn_residual(x, o, w_out, gain, w_gate, w_up, w_down, tm, pieces):
    t, d = x.shape
    ff = w_gate.shape[1]
    tf = sum(pieces)
    return pl.pallas_call(
        functools.partial(_ffn_kernel, pieces=pieces),
        grid=(t // tm, ff // tf),
        in_specs=[
            pl.BlockSpec((tm, d), lambda i, j: (i, 0)),
            pl.BlockSpec((tm, d), lambda i, j: (i, 0)),
            pl.BlockSpec((d, d), lambda i, j: (0, 0), pipeline_mode=pl.Buffered(1)),
            pl.BlockSpec((1, d), lambda i, j: (0, 0)),
            pl.BlockSpec((d, tf), lambda i, j: (0, j)),
            pl.BlockSpec((d, tf), lambda i, j: (0, j)),
            pl.BlockSpec((tf, d), lambda i, j: (j, 0)),
        ],
        out_specs=pl.BlockSpec((tm, d), lambda i, j: (i, 0)),
        out_shape=jax.ShapeDtypeStruct((t, d), F32),
        scratch_shapes=[pltpu.VMEM((tm, d), BF16)],
        compiler_params=_params(("arbitrary", "arbitrary"), 56 * 1024 * 1024),
        name="ffn_swiglu",
    )(x, o, w_out, gain, w_gate, w_up, w_down)


def _moba_qkv_kernel(x_ref, gain_ref, w_ref, qg_ref, kg_ref, q_ref, k_ref, v_ref):
    h = _rms(x_ref[...], gain_ref[...]).astype(BF16)
    d = MOBA_HEADS * MOBA_HD
    qg = qg_ref[...]
    kg = kg_ref[...]
    q = _dot(h, w_ref[:, 0:d])
    k = _dot(h, w_ref[:, d:2 * d])
    for hd in range(MOBA_HEADS):
        cols = slice(hd * MOBA_HD, (hd + 1) * MOBA_HD)
        q_ref[:, cols] = (_rms(q[:, cols], qg) * MOBA_Q_SCALE).astype(BF16)
        k_ref[:, cols] = _rms(k[:, cols], kg).astype(BF16)
    v_ref[...] = _dot(h, w_ref[:, 2 * d:]).astype(BF16)


def _moba_qkv(x, gain, w, q_gain, k_gain, tm):
    t = x.shape[0]
    d = MOBA_HEADS * MOBA_HD
    row = lambda i: (i, 0)
    fixed = lambda i: (0, 0)
    return pl.pallas_call(
        _moba_qkv_kernel,
        grid=(t // tm,),
        in_specs=[
            pl.BlockSpec((tm, D_MODEL), row),
            pl.BlockSpec((1, D_MODEL), fixed),
            pl.BlockSpec((D_MODEL, 3 * d), fixed),
            pl.BlockSpec((1, MOBA_HD), fixed),
            pl.BlockSpec((1, MOBA_HD), fixed),
        ],
        out_specs=[pl.BlockSpec((tm, d), row)] * 3,
        out_shape=[jax.ShapeDtypeStruct((t, d), BF16)] * 3,
        compiler_params=_params(("parallel",), 48 * 1024 * 1024),
        name="moba_qkv",
    )(x, gain, w, q_gain, k_gain)


def _reduce_rows(x, op, stop=8):
    while x.shape[0] > stop:
        half = x.shape[0] // 2
        x = op(x[:half], x[half:])
    return x


def _moba_kernel(q_ref, k_ref, v_ref, o_ref, *, n_blocks):
    seq = n_blocks * MOBA_BLOCK
    n_pad = 16
    ones_rows = jnp.where(lax.broadcasted_iota(I32, (BF16_ROWS, seq), 0) == 0, 1.0, 0.0)
    v_t = jnp.concatenate([v_ref[...].astype(F32).T, ones_rows], axis=0).astype(BF16)
    blk_of_key = lax.broadcasted_iota(I32, (n_pad, seq), 1) // MOBA_BLOCK
    blk_row = lax.broadcasted_iota(I32, (n_pad, seq), 0)
    indicator = jnp.where(blk_of_key == blk_row, 1.0 / MOBA_BLOCK, 0.0).astype(BF16)
    k_mean = _dot(indicator, k_ref[...])
    km_hi, km_lo = _split(k_mean)

    blk_id = lax.broadcasted_iota(I32, (n_pad, MOBA_BLOCK), 0)
    key_pos = lax.broadcasted_iota(I32, (MOBA_BLOCK, MOBA_BLOCK), 0)
    query_pos = lax.broadcasted_iota(I32, (MOBA_BLOCK, MOBA_BLOCK), 1)
    causal = key_pos <= query_pos

    def masked_scores(i):
        rows = slice(i * MOBA_BLOCK, (i + 1) * MOBA_BLOCK)
        n_keys = (i + 1) * MOBA_BLOCK
        q = q_ref[rows, :]
        s = _dot_nt(k_ref[0:n_keys, :], q)
        selected = None
        if i > MOBA_TOPK:
            gate = _dot_nt(km_hi, q) + _dot_nt(km_lo, q)
            gate = jnp.where(blk_id < i, gate, NEG_INF)
            rank = jnp.zeros_like(gate)
            for jp in range(i):
                row = gate[jp:jp + 1, :]
                beats = (row > gate) | ((row == gate) & (jp < blk_id))
                rank = rank + beats.astype(F32)
            selected = jnp.where((rank < MOBA_TOPK) & (blk_id < i), 1.0, 0.0)
        pieces = []
        for j in range(i + 1):
            sj = s[j * MOBA_BLOCK:(j + 1) * MOBA_BLOCK, :]
            if j == i:
                sj = jnp.where(causal, sj, NEG_INF)
            elif selected is not None:
                sj = jnp.where(selected[j:j + 1, :] > 0.5, sj, NEG_INF)
            pieces.append(sj)
        return pieces

    upcoming = masked_scores(0)
    for i in range(n_blocks):
        rows = slice(i * MOBA_BLOCK, (i + 1) * MOBA_BLOCK)
        n_keys = (i + 1) * MOBA_BLOCK
        pieces = upcoming
        if i + 1 < n_blocks:
            upcoming = masked_scores(i + 1)
        m = _reduce_rows(functools.reduce(jnp.maximum, pieces), jnp.maximum).max(axis=0, keepdims=True)
        probs = [jnp.exp2(sj - m).astype(BF16) for sj in pieces]
        p = probs[0] if len(probs) == 1 else jnp.concatenate(probs, axis=0)
        o_t = _dot(v_t[:, 0:n_keys], p)
        o_ref[rows, :] = (o_t[:MOBA_HD, :] / o_t[MOBA_HD:MOBA_HD + 1, :]).T.astype(BF16)


def _moba(q, k, v, ride_along, batch, seq):
    t, d = q.shape
    spec = pl.BlockSpec((seq, MOBA_HD), lambda b, h: (b, h))
    cast_specs, cast_shapes = _ride_along_specs(ride_along, batch * MOBA_HEADS, lambda b, h: b * MOBA_HEADS + h)
    outs = pl.pallas_call(
        _with_ride_along(functools.partial(_moba_kernel, n_blocks=seq // MOBA_BLOCK), 3, 1, len(ride_along)),
        grid=(batch, MOBA_HEADS),
        in_specs=[spec, spec, spec] + cast_specs,
        out_specs=[spec] + cast_specs,
        out_shape=[jax.ShapeDtypeStruct((t, d), BF16)] + cast_shapes,
        compiler_params=_params(("arbitrary", "arbitrary"), 48 * 1024 * 1024),
        name="moba_attn",
    )(q, k, v, *ride_along)
    return outs[0], outs[1:]


def _router_kernel(x_ref, o_ref, wo_ref, gain_ref, w_ref, x2_ref, h_ref, info_ref, cnt_ref, carry_ref, tri_ref,
                   *, region_rows):
    tr = x_ref.shape[0]

    @pl.when(pl.program_id(0) == 0)
    def _():
        carry_ref[...] = jnp.zeros_like(carry_ref)
        r = lax.broadcasted_iota(I32, (TOK_BLK, TOK_BLK), 0)
        c = lax.broadcasted_iota(I32, (TOK_BLK, TOK_BLK), 1)
        tri_ref[...] = (c < r).astype(BF16)

    x2 = x_ref[...] + _dot(o_ref[...], wo_ref[...])
    x2_ref[...] = x2
    h = _rms(x2, gain_ref[...])
    h_hi, h_lo = _split(h)
    h_ref[...] = h_hi
    both = _dot(h_hi, w_ref[...])
    logits = both[:, :LANES] + both[:, LANES:] + _dot(h_lo, w_ref[:, :LANES])
    e = lax.broadcasted_iota(I32, logits.shape, 1)
    logits = jnp.where(e < N_EXPERTS, logits, -jnp.inf)
    m1 = logits.max(axis=-1, keepdims=True)
    i1 = jnp.where(logits == m1, e, LANES).min(axis=-1, keepdims=True)
    first = e == i1
    rest = jnp.where(first, -jnp.inf, logits)
    m2 = rest.max(axis=-1, keepdims=True)
    i2 = jnp.where(rest == m2, e, LANES).min(axis=-1, keepdims=True)
    second = e == i2
    e2 = jnp.exp(m2 - m1)
    denom = 1.0 + e2
    sel = jnp.where(first | second, 1.0, 0.0).astype(BF16)
    nb = tr // TOK_BLK
    blk_row = lax.broadcasted_iota(I32, (nb, tr), 0)
    blk_tok = lax.broadcasted_iota(I32, (nb, tr), 1) // TOK_BLK
    cnt = _dot((blk_row == blk_tok).astype(BF16), sel)
    cnt_ref[...] = cnt
    offset = carry_ref[...]
    before = []
    for b in range(nb):
        before.append(_dot(tri_ref[...], sel[b * TOK_BLK:(b + 1) * TOK_BLK, :]) + offset)
        offset = offset + cnt[b:b + 1, :]
    carry_ref[...] = offset
    before = jnp.concatenate(before, axis=0)
    rank1 = jnp.where(first, before, 0.0).sum(axis=-1, keepdims=True)
    rank2 = jnp.where(second, before, 0.0).sum(axis=-1, keepdims=True)
    col = lax.broadcasted_iota(I32, (tr, 4), 1)
    fields = (i1.astype(F32) * region_rows + rank1, i2.astype(F32) * region_rows + rank2,
              1.0 / denom, e2 / denom)
    info = jnp.zeros((tr, 4), F32)
    for n, f in enumerate(fields):
        info = jnp.where(col == n, f, info)
    info_ref[...] = info


def _router(x, o, w_out, gain, w_router, tr):
    t = x.shape[0]
    nb = tr // TOK_BLK
    pad = ((0, 0), (0, LANES - N_EXPERTS))
    w_hi = w_router.astype(BF16)
    w_lo = (w_router - w_hi.astype(F32)).astype(BF16)
    w_both = jnp.concatenate([jnp.pad(w_hi, pad), jnp.pad(w_lo, pad)], axis=1)
    row = lambda i: (i, 0)
    fixed = lambda i: (0, 0)
    return pl.pallas_call(
        functools.partial(_router_kernel, region_rows=float(t)),
        grid=(t // tr,),
        in_specs=[
            pl.BlockSpec((tr, D_MODEL), row),
            pl.BlockSpec((tr, D_MODEL), row),
            pl.BlockSpec((D_MODEL, D_MODEL), fixed),
            pl.BlockSpec((1, D_MODEL), fixed),
            pl.BlockSpec((D_MODEL, 2 * LANES), fixed),
        ],
        out_specs=[
            pl.BlockSpec((tr, D_MODEL), row),
            pl.BlockSpec((tr, D_MODEL), row),
            pl.BlockSpec((tr, 4), row),
            pl.BlockSpec((nb, LANES), row),
        ],
        out_shape=[
            jax.ShapeDtypeStruct((t, D_MODEL), F32),
            jax.ShapeDtypeStruct((t, D_MODEL), BF16),
            jax.ShapeDtypeStruct((t, 4), F32),
            jax.ShapeDtypeStruct((t // TOK_BLK, LANES), F32),
        ],
        scratch_shapes=[pltpu.VMEM((1, LANES), F32), pltpu.VMEM((TOK_BLK, TOK_BLK), BF16)],
        compiler_params=_params(("arbitrary",), 56 * 1024 * 1024),
        name="moe_router",
    )(x, o, w_out, gain, w_both)


def _moe_plan(cnt, t):
    e_ids = jnp.arange(N_EXPERTS, dtype=I32)
    by_expert = lambda table, idx: jnp.sum(jnp.where(idx[..., None] == e_ids, table, 0), axis=-1)
    cnt_blk = cnt[:, :N_EXPERTS].astype(I32)
    cum_blk = jnp.cumsum(cnt_blk, axis=0)
    counts = cum_blk[-1]
    tiles_e = (counts + MOE_TM - 1) // MOE_TM
    tile_end = jnp.cumsum(tiles_e)
    tile_off = tile_end - tiles_e
    nt = TOP_K * t // MOE_TM + N_EXPERTS
    r = jnp.arange(nt, dtype=I32)
    valid = r < tile_end[-1]
    te = jnp.minimum(jnp.sum((r[:, None] >= tile_end[None, :]).astype(I32), axis=1), N_EXPERTS - 1)
    local = r - by_expert(tile_off, te)
    last_e = jnp.max(jnp.where(tiles_e > 0, e_ids, 0))
    tile_expert = jnp.where(valid, te, last_e)
    shift = e_ids * t - tile_off * MOE_TM

    per = MOE_TM // MOE_TG
    rg = jnp.arange(nt * per, dtype=I32)
    eg = jnp.repeat(te, per)
    k0 = jnp.repeat(local, per) * MOE_TM + (rg % per) * MOE_TG
    k1 = jnp.minimum(k0 + MOE_TG, by_expert(counts, eg))
    has = jnp.repeat(valid, per) & (k1 > k0)
    cum_e = by_expert(cum_blk[None, :, :], eg[:, None])
    s_lo = jnp.sum((cum_e <= k0[:, None]).astype(I32), axis=1)
    s_hi = jnp.sum((cum_e <= (k1 - 1)[:, None]).astype(I32), axis=1)
    g_nwin = jnp.where(has, (s_hi - s_lo) // GATHER_WB + 1, 0)
    g_slo = jnp.where(has, s_lo, 0)
    g_code = eg * t + k0

    nts = t // COMB_TS
    cnt_ts = cnt_blk.reshape(nts, COMB_TS // TOK_BLK, N_EXPERTS).sum(axis=1)
    seg_lo = (tile_off * MOE_TM)[None, :] + jnp.cumsum(cnt_ts, axis=0) - cnt_ts
    seg_hi = seg_lo + cnt_ts
    ws0 = (seg_lo // BF16_ROWS) * BF16_ROWS
    nw = jnp.where(cnt_ts > 0, (seg_hi - ws0 + COMB_WC - 1) // COMB_WC, 0)
    nw_end = jnp.cumsum(nw, axis=1)
    k = jnp.arange(COMB_MAXW, dtype=I32)
    ek = jnp.minimum(jnp.sum((k[None, :, None] >= nw_end[:, None, :]).astype(I32), axis=2), N_EXPERTS - 1)
    pick = lambda a: by_expert(a[:, None, :], ek)
    live = k[None, :] < nw_end[:, -1:]
    nominal = pick(ws0) + (k[None, :] - pick(nw_end - nw)) * COMB_WC
    shift_k = by_expert(shift, ek)
    c_start = jnp.minimum(nominal, by_expert(tile_end * MOE_TM, ek) - COMB_WC)
    c_off = shift_k + c_start
    c_lo = jnp.maximum(pick(seg_lo), nominal) + shift_k
    c_hi = jnp.minimum(pick(seg_hi), nominal + COMB_WC) + shift_k
    flat = lambda a: jnp.where(live, a, 0).reshape(-1)
    return dict(
        tile_expert=tile_expert, tile_valid=valid.astype(I32),
        g_code=g_code, g_slo=g_slo, g_nwin=g_nwin,
        c_start=flat(c_start), c_off=flat(c_off), c_lo=flat(c_lo), c_hi=flat(c_hi), c_nwin=nw_end[:, -1])


def _moe_gather_kernel(code_ref, slo_ref, nwin_ref, rows_ref, h_ref, hs_ref, gs_ref):
    n_blk = rows_ref.shape[1]
    hs_ref[...] = jnp.zeros_like(hs_ref)
    gs_ref[...] = jnp.zeros_like(gs_ref)
    for sub in range(hs_ref.shape[0] // MOE_TG):
        _gather_subtile(pl.program_id(0) * (hs_ref.shape[0] // MOE_TG) + sub,
                        slice(sub * MOE_TG, (sub + 1) * MOE_TG),
                        code_ref, slo_ref, nwin_ref, rows_ref, h_ref, hs_ref, gs_ref, n_blk)


def _gather_subtile(r, out_rows, code_ref, slo_ref, nwin_ref, rows_ref, h_ref, hs_ref, gs_ref, n_blk):
    row_id = (code_ref[r] + lax.broadcasted_iota(I32, (MOE_TG, TOK_BLK), 0)).astype(F32)

    def body(k, carry):
        nominal = slo_ref[r] + k * GATHER_WB
        sb = jnp.minimum(nominal, n_blk - GATHER_WB)
        fresh = sb + lax.broadcasted_iota(I32, (GATHER_WB, TOK_BLK), 0) >= nominal
        p1 = jnp.where(fresh, rows_ref[0, pl.ds(sb, GATHER_WB), :], -1.0)
        p2 = jnp.where(fresh, rows_ref[1, pl.ds(sb, GATHER_WB), :], -1.0)
        w1 = rows_ref[2, pl.ds(sb, GATHER_WB), :]
        w2 = rows_ref[3, pl.ds(sb, GATHER_WB), :]
        pieces = []
        g = jnp.zeros((MOE_TG, TOK_BLK), F32)
        for a in range(GATHER_WB):
            m1 = p1[a:a + 1, :] == row_id
            m2 = p2[a:a + 1, :] == row_id
            pieces.append(jnp.where(m1 | m2, 1.0, 0.0).astype(BF16))
            g = g + jnp.where(m1, w1[a:a + 1, :], 0.0) + jnp.where(m2, w2[a:a + 1, :], 0.0)
        onehot = jnp.concatenate(pieces, axis=1)
        tok0 = pl.multiple_of(sb * TOK_BLK, TOK_BLK)
        hs_ref[out_rows, :] += _dot(onehot, h_ref[pl.ds(tok0, GATHER_WB * TOK_BLK), :]).astype(BF16)
        gs_ref[out_rows, :] += g.sum(axis=-1, keepdims=True)
        return carry

    lax.fori_loop(0, nwin_ref[r], body, 0)


def _moe_gather(plan, h, rows):
    t = h.shape[0]
    n_rows = plan["g_code"].shape[0] * MOE_TG
    n_blk = t // TOK_BLK
    grid_spec = pltpu.PrefetchScalarGridSpec(
        num_scalar_prefetch=3,
        grid=(n_rows // MOE_TM,),
        in_specs=[
            pl.BlockSpec((4, n_blk, TOK_BLK), lambda r, *_: (0, 0, 0), pipeline_mode=pl.Buffered(1)),
            pl.BlockSpec((t, D_MODEL), lambda r, *_: (0, 0), pipeline_mode=pl.Buffered(1)),
        ],
        out_specs=[
            pl.BlockSpec((MOE_TM, D_MODEL), lambda r, *_: (r, 0)),
            pl.BlockSpec((MOE_TM, 1), lambda r, *_: (r, 0)),
        ],
    )
    return pl.pallas_call(
        _moe_gather_kernel,
        grid_spec=grid_spec,
        out_shape=[
            jax.ShapeDtypeStruct((n_rows, D_MODEL), BF16),
            jax.ShapeDtypeStruct((n_rows, 1), F32),
        ],
        compiler_params=_params(("parallel",), 52 * 1024 * 1024),
        name="moe_gather",
    )(plan["g_code"], plan["g_slo"], plan["g_nwin"], rows, h)


def _moe_expert_kernel(te_ref, tv_ref, hs_ref, gs_ref, wg_ref, wu_ref, wd_ref, y_ref, acc_ref, *, nj, pieces):
    r = pl.program_id(0)
    j = pl.program_id(1)
    valid = tv_ref[r] > 0
    part = lambda: _swiglu_pieces(hs_ref[...], wg_ref, wu_ref, wd_ref, pieces)

    assert nj >= 2, nj

    @pl.when(valid & (j == 0))
    def _():
        acc_ref[...] = part()

    if nj > 2:
        @pl.when(valid & (j > 0) & (j < nj - 1))
        def _():
            acc_ref[...] += part()

    @pl.when(valid & (j == nj - 1))
    def _():
        y_ref[...] = ((acc_ref[...] + part()) * gs_ref[...]).astype(BF16)

    @pl.when(jnp.logical_not(valid) & (j == nj - 1))
    def _():
        y_ref[...] = jnp.zeros_like(y_ref)


def _moe_experts(plan, hs, gs, w_gate, w_up, w_down, pieces):
    tf = sum(pieces)
    n_rows, d = hs.shape
    ff = w_gate.shape[2]
    nj = ff // tf
    jj = lambda r, j, tv: jnp.where(tv[r] > 0, j, nj - 1)
    grid_spec = pltpu.PrefetchScalarGridSpec(
        num_scalar_prefetch=2,
        grid=(n_rows // MOE_TM, nj),
        in_specs=[
            pl.BlockSpec((MOE_TM, d), lambda r, j, te, tv: (r, 0)),
            pl.BlockSpec((MOE_TM, 1), lambda r, j, te, tv: (r, 0)),
            pl.BlockSpec((None, d, tf), lambda r, j, te, tv: (te[r], 0, jj(r, j, tv))),
            pl.BlockSpec((None, d, tf), lambda r, j, te, tv: (te[r], 0, jj(r, j, tv))),
            pl.BlockSpec((None, tf, d), lambda r, j, te, tv: (te[r], jj(r, j, tv), 0)),
        ],
        out_specs=pl.BlockSpec((MOE_TM, d), lambda r, j, te, tv: (r, 0)),
        scratch_shapes=[pltpu.VMEM((MOE_TM, d), F32)],
    )
    return pl.pallas_call(
        functools.partial(_moe_expert_kernel, nj=nj, pieces=pieces),
        grid_spec=grid_spec,
        out_shape=jax.ShapeDtypeStruct((n_rows, d), BF16),
        compiler_params=_params(("arbitrary", "arbitrary"), 56 * 1024 * 1024),
        name="moe_experts",
    )(plan["tile_expert"], plan["tile_valid"], hs, gs, w_gate, w_up, w_down)


def _moe_combine_kernel(start_ref, off_ref, lo_ref, hi_ref, nwin_ref, x_ref, info_ref, y_hbm, o_ref,
                        ybuf, ybuf_more, sem, sem_more):
    i = pl.program_id(0)
    n_tiles = pl.num_programs(0)

    def window_copy(idx, dst, dsem):
        start = pl.multiple_of(start_ref[idx], BF16_ROWS)
        return pltpu.make_async_copy(y_hbm.at[pl.ds(start, COMB_WC), :], dst, dsem)

    def first_group(tile, wait):
        buf = tile % 2
        for s in range(COMB_GROUP):
            @pl.when(s < nwin_ref[tile])
            def _():
                cp = window_copy(tile * COMB_MAXW + s, ybuf.at[buf, pl.ds(s * COMB_WC, COMB_WC), :],
                                 sem.at[buf, s])
                cp.wait() if wait else cp.start()

    @pl.when(i == 0)
    def _():
        ybuf[...] = jnp.zeros_like(ybuf)
        ybuf_more[...] = jnp.zeros_like(ybuf_more)
        first_group(0, wait=False)

    @pl.when(i + 1 < n_tiles)
    def _():
        first_group(i + 1, wait=False)

    first_group(i, wait=True)

    code1 = jnp.broadcast_to(info_ref[:, 0:1].astype(I32), (COMB_TS, COMB_WC))
    code2 = jnp.broadcast_to(info_ref[:, 1:2].astype(I32), (COMB_TS, COMB_WC))
    lane = lax.broadcasted_iota(I32, (COMB_TS, COMB_WC), 1)
    lane_row = lax.broadcasted_iota(I32, (1, COMB_WC), 1)

    def onehot_group(g):
        pieces = []
        for s in range(COMB_GROUP):
            idx = i * COMB_MAXW + g * COMB_GROUP + s
            off = off_ref[idx]
            in_span = (lane_row >= lo_ref[idx] - off) & (lane_row < hi_ref[idx] - off)
            target = lane + off
            hit = ((code1 == target) | (code2 == target)) & in_span
            pieces.append(jnp.where(hit, 1.0, 0.0).astype(BF16))
        return jnp.concatenate(pieces, axis=1)

    o_ref[...] = x_ref[...] + _dot(onehot_group(0), ybuf[i % 2])

    n = nwin_ref[i]

    def more(g, carry):
        for wait in (False, True):
            for s in range(COMB_GROUP):
                @pl.when(g * COMB_GROUP + s < n)
                def _():
                    cp = window_copy(i * COMB_MAXW + g * COMB_GROUP + s,
                                     ybuf_more.at[pl.ds(s * COMB_WC, COMB_WC), :], sem_more.at[s])
                    cp.wait() if wait else cp.start()
        o_ref[...] += _dot(onehot_group(g), ybuf_more[...])
        return carry

    lax.fori_loop(1, (n + COMB_GROUP - 1) // COMB_GROUP, more, 0)


def _moe_combine(plan, x, info, y):
    t, d = x.shape
    width = COMB_GROUP * COMB_WC
    grid_spec = pltpu.PrefetchScalarGridSpec(
        num_scalar_prefetch=5,
        grid=(t // COMB_TS,),
        in_specs=[
            pl.BlockSpec((COMB_TS, d), lambda i, *_: (i, 0)),
            pl.BlockSpec((COMB_TS, 4), lambda i, *_: (i, 0)),
            pl.BlockSpec(memory_space=pl.ANY),
        ],
        out_specs=pl.BlockSpec((COMB_TS, d), lambda i, *_: (i, 0)),
        scratch_shapes=[
            pltpu.VMEM((2, width, d), BF16),
            pltpu.VMEM((width, d), BF16),
            pltpu.SemaphoreType.DMA((2, COMB_GROUP)),
            pltpu.SemaphoreType.DMA((COMB_GROUP,)),
        ],
    )
    return pl.pallas_call(
        _moe_combine_kernel,
        grid_spec=grid_spec,
        out_shape=jax.ShapeDtypeStruct((t, d), F32),
        compiler_params=_params(("arbitrary",)),
        name="moe_combine",
    )(plan["c_start"], plan["c_off"], plan["c_lo"], plan["c_hi"], plan["c_nwin"], x, info, y)


def kernel(x, norm_mix, norm_ffn, gla_w_in, gla_w_gate2, gla_b_gate, gla_out_gain, gla_w_out,
           moba_w_qkv, moba_q_gain, moba_k_gain, moba_w_out, ffn_w_gate, ffn_w_up, ffn_w_down,
           moe_w_router, moe_w_gate, moe_w_up, moe_w_down):
    batch, seq, d = x.shape
    t = batch * seq
    xt = x.reshape(t, d)

    (q, k, v, g, la), (ffn_wg, ffn_wu, ffn_wd, gla_wo, moba_wqkv, moba_wo) = _gla_inproj(
        xt, norm_mix[0:1], gla_w_in, gla_w_gate2[0], gla_b_gate[0:1],
        ride_along=(ffn_w_gate, ffn_w_up, ffn_w_down, gla_w_out, moba_w_qkv, moba_w_out), tm=512)
    o, _ = _gla(q, k, v, g, la, gla_out_gain[0:1], (), batch, seq, blk=256)
    xt = _ffn_residual(xt, o, gla_wo[0], norm_ffn[0:1], ffn_wg[0], ffn_wu[0], ffn_wd[0],
                       tm=512, pieces=MOE_PIECES)

    q, k, v = _moba_qkv(xt, norm_mix[1:2], moba_wqkv[0], moba_q_gain[0:1], moba_k_gain[0:1], tm=512)
    o, (moe_wg, moe_wu, moe_wd) = _moba(q, k, v, (moe_w_gate[0], moe_w_up[0], moe_w_down[0]), batch, seq)
    xt, h, info, cnt = _router(xt, o, moba_wo[0], norm_ffn[1:2], moe_w_router[0], tr=1024)
    plan = _moe_plan(cnt, t)
    rows = info.T.reshape(4, t // TOK_BLK, TOK_BLK)
    hs, gs = _moe_gather(plan, h, rows)
    y = _moe_experts(plan, hs, gs, moe_wg, moe_wu, moe_wd, pieces=MOE_PIECES)
    xt = _moe_combine(plan, xt, info, y)
    return xt.reshape(batch, seq, d)
```

```python
import functools

import jax
import jax.numpy as jnp
from jax import lax
from jax.experimental import pallas as pl
from jax.experimental.pallas import tpu as pltpu

F32 = jnp.float32
BF16 = jnp.bfloat16
I32 = jnp.int32

EPS = 1e-6
NEG_INF = -1e30

D_MODEL = 1024
GLA_HEADS = 4
GLA_DK = 512
GLA_DV = 1024
GLA_HK = 128
GLA_HV = 256
GLA_RANK = 16
GLA_NORMALIZER = 16.0
GLA_CHUNK = 64

MOBA_HEADS = 8
MOBA_HD = 128
MOBA_BLOCK = 256
MOBA_TOPK = 3
MOBA_Q_SCALE = MOBA_HD ** -0.5 * 1.4426950408889634

N_EXPERTS = 8
TOP_K = 2

LANES = 128
BF16_ROWS = 16

MOE_PIECES = (512, 512, 768)
MOE_TM = 256
MOE_TG = 128
GATHER_ROWS = 512
TOK_BLK = LANES
GATHER_WB = 6
COMB_TS = 256
COMB_WC = 128
COMB_GROUP = 8
COMB_AHEAD_BUFS = 3
COMB_MAXW = 16


def _params(semantics, vmem_bytes=None):
    return pltpu.CompilerParams(dimension_semantics=semantics, vmem_limit_bytes=vmem_bytes)


def _rms(x, gain):
    y = x * lax.rsqrt(jnp.mean(x * x, axis=-1, keepdims=True) + EPS)
    return y * gain


def _dot(a, b):
    return jnp.dot(a, b, preferred_element_type=F32)


def _dot_nt(a, b):
    return lax.dot_general(a, b, (((1,), (1,)), ((), ())), preferred_element_type=F32)


def _dot_tn(a, b):
    return lax.dot_general(a, b, (((0,), (0,)), ((), ())), preferred_element_type=F32)


def _split(a):
    hi = a.astype(BF16)
    lo = (a - hi.astype(F32)).astype(BF16)
    return hi, lo


def _log_sigmoid(z):
    return jnp.minimum(z, 0.0) - jnp.log(1.0 + jnp.exp(-jnp.abs(z)))


def _silu(z):
    return z / (1.0 + jnp.exp(-z))


def _swiglu_pieces(h, wg_ref, wu_ref, wd_ref, widths):
    bounds = [sum(widths[:n]) for n in range(len(widths) + 1)]
    assert bounds[-1] == wg_ref.shape[1], (widths, wg_ref.shape)
    gate_up = lambda c: (_dot(h, wg_ref[:, bounds[c]:bounds[c + 1]]), _dot(h, wu_ref[:, bounds[c]:bounds[c + 1]]))
    out = None
    upcoming = gate_up(0)
    for c in range(len(widths)):
        g, u = upcoming
        if c + 1 < len(widths):
            upcoming = gate_up(c + 1)
        d = _dot((_silu(g) * u).astype(BF16), wd_ref[bounds[c]:bounds[c + 1], :])
        out = d if out is None else out + d
    return out


def _ride_along_specs(arrays, n_steps, step):
    specs, shapes = [], []
    for w in arrays:
        n_e, rows, cols = w.shape
        per_e = n_steps // n_e
        if rows % (per_e * BF16_ROWS) == 0:
            blk = (None, rows // per_e, cols)
            imap = lambda *ids, per_e=per_e: (step(*ids) // per_e, step(*ids) % per_e, 0)
        else:
            assert cols % (per_e * LANES) == 0, (w.shape, n_steps)
            blk = (None, rows, cols // per_e)
            imap = lambda *ids, per_e=per_e: (step(*ids) // per_e, 0, step(*ids) % per_e)
        specs.append(pl.BlockSpec(blk, imap))
        shapes.append(jax.ShapeDtypeStruct(w.shape, BF16))
    return specs, shapes


def _with_ride_along(body, n_in, n_out, n_cast):
    def kernel(*refs):
        ins, refs = refs[:n_in], refs[n_in:]
        cast_in, refs = refs[:n_cast], refs[n_cast:]
        outs, refs = refs[:n_out], refs[n_out:]
        cast_out, scratch = refs[:n_cast], refs[n_cast:]
        for src, dst in zip(cast_in, cast_out):
            dst[...] = src[...].astype(BF16)
        body(*ins, *outs, *scratch)
    return kernel


def _gla_inproj_kernel(x_next_ref, x_first_ref, gain_ref, w_ref, wg2_ref, b_ref,
                       q_ref, k_ref, v_ref, g_ref, la_ref, wb_ref, h_ref):
    i = pl.program_id(0)

    @pl.when(i == 0)
    def _():
        wb_ref[...] = w_ref[...].astype(BF16)
        h_ref[0] = _rms(x_first_ref[...], gain_ref[...]).astype(BF16)

    n_main = 2 * GLA_DK + 2 * GLA_DV

    def step(cur, nxt):
        h = h_ref[cur]
        h_ref[nxt] = _rms(x_next_ref[...], gain_ref[...]).astype(BF16)
        a_lr = _dot(h, wb_ref[:, n_main:])
        a_hi, a_lo = _split(a_lr)
        w_hi, w_lo = _split(wg2_ref[...])
        z = _dot(jnp.concatenate([a_hi, a_lo, a_hi], axis=1), jnp.concatenate([w_hi, w_hi, w_lo], axis=0))
        la_ref[...] = _log_sigmoid(z + b_ref[...]) / GLA_NORMALIZER
        q_ref[...] = _dot(h, wb_ref[:, 0:GLA_DK]).astype(BF16)
        k_ref[...] = _dot(h, wb_ref[:, GLA_DK:2 * GLA_DK]).astype(BF16)
        v_ref[...] = _dot(h, wb_ref[:, 2 * GLA_DK:2 * GLA_DK + GLA_DV]).astype(BF16)
        g_ref[...] = _dot(h, wb_ref[:, 2 * GLA_DK + GLA_DV:n_main]).astype(BF16)

    @pl.when(i % 2 == 0)
    def _():
        step(0, 1)

    @pl.when(i % 2 == 1)
    def _():
        step(1, 0)


def _gla_inproj(x, gain, w_in, w_gate2, b_gate, ride_along, tm):
    t = x.shape[0]
    n_steps = t // tm
    row = lambda i: (i, 0)
    fixed = lambda i: (0, 0)
    cast_specs, cast_shapes = _ride_along_specs(ride_along, n_steps, lambda i: i)
    outs = pl.pallas_call(
        _with_ride_along(_gla_inproj_kernel, 6, 5, len(ride_along)),
        grid=(n_steps,),
        in_specs=[
            pl.BlockSpec((tm, D_MODEL), lambda i: (jnp.minimum(i + 1, n_steps - 1), 0)),
            pl.BlockSpec((tm, D_MODEL), fixed),
            pl.BlockSpec((1, D_MODEL), fixed),
            pl.BlockSpec((None,) + w_in.shape[1:], lambda i: (0, 0, 0), pipeline_mode=pl.Buffered(1)),
            pl.BlockSpec((GLA_RANK, GLA_DK), fixed),
            pl.BlockSpec((1, GLA_DK), fixed),
        ] + cast_specs,
        out_specs=[
            pl.BlockSpec((tm, GLA_DK), row),
            pl.BlockSpec((tm, GLA_DK), row),
            pl.BlockSpec((tm, GLA_DV), row),
            pl.BlockSpec((tm, GLA_DV), row),
            pl.BlockSpec((tm, GLA_DK), row),
        ] + cast_specs,
        out_shape=[
            jax.ShapeDtypeStruct((t, GLA_DK), BF16),
            jax.ShapeDtypeStruct((t, GLA_DK), BF16),
            jax.ShapeDtypeStruct((t, GLA_DV), BF16),
            jax.ShapeDtypeStruct((t, GLA_DV), BF16),
            jax.ShapeDtypeStruct((t, GLA_DK), F32),
        ] + cast_shapes,
        scratch_shapes=[pltpu.VMEM(w_in.shape[1:], BF16), pltpu.VMEM((2, tm, D_MODEL), BF16)],
        compiler_params=_params(("arbitrary",), 48 * 1024 * 1024),
        name="gla_inproj",
    )(x, x, gain, w_in, w_gate2, b_gate, *ride_along)
    return outs[:5], outs[5:]


def _gla_kernel(q_ref, k_ref, v_ref, g_ref, la_ref, og_ref, o_ref, st_ref, *, n_chunks):
    @pl.when(pl.program_id(1) == 0)
    def _():
        st_ref[...] = jnp.zeros_like(st_ref)

    blk = n_chunks * GLA_CHUNK
    r = lax.broadcasted_iota(I32, (blk, blk), 0)
    c = lax.broadcasted_iota(I32, (blk, blk), 1)
    causal = (r >= c) & (r // GLA_CHUNK == c // GLA_CHUNK)
    tril = causal.astype(BF16)
    la = la_ref[...]
    la_hi, la_mid = _split(la)
    la_lo = (la - la_hi.astype(F32) - la_mid.astype(F32)).astype(BF16)
    cum = _dot(tril, la_hi) + _dot(tril, la_mid) + _dot(tril, la_lo)
    ends = [cum[(ci + 1) * GLA_CHUNK - 1:(ci + 1) * GLA_CHUNK, :] for ci in range(n_chunks)]
    last = jnp.concatenate([jnp.broadcast_to(e, (GLA_CHUNK, GLA_DK)) for e in ends], axis=0)
    q = q_ref[...].astype(F32) * (GLA_HK ** -0.5)
    k = k_ref[...].astype(F32)
    q_dec = (q * jnp.exp(cum)).astype(BF16)
    k_inv = (k * jnp.exp(-cum)).astype(BF16)
    k_end = (k * jnp.exp(last - cum)).astype(BF16)
    chunk_decay = [jnp.exp(e) for e in ends]
    og = og_ref[...]
    heads = range(GLA_HEADS)
    hk = [slice(h * GLA_HK, (h + 1) * GLA_HK) for h in heads]
    hv = [slice(h * GLA_HV, (h + 1) * GLA_HV) for h in heads]
    scores = [jnp.where(causal, _dot_nt(q_dec[:, hk[h]], k_inv[:, hk[h]]), 0.0).astype(BF16) for h in heads]
    state_in = [[_dot_tn(v_ref[ci * GLA_CHUNK:(ci + 1) * GLA_CHUNK, hv[h]],
                         k_end[ci * GLA_CHUNK:(ci + 1) * GLA_CHUNK, hk[h]])
                 for ci in range(n_chunks)] for h in heads]
    intra = [_dot(scores[h], v_ref[:, hv[h]]) for h in heads]
    for h in heads:
        st = st_ref[h]
        inter = []
        for ci in range(n_chunks):
            rows = slice(ci * GLA_CHUNK, (ci + 1) * GLA_CHUNK)
            inter.append(_dot_nt(q_dec[rows, hk[h]], st.astype(BF16)))
            st = st * chunk_decay[ci][:, hk[h]] + state_in[h][ci]
        st_ref[h] = st
        o = intra[h] + jnp.concatenate(inter, axis=0)
        o = o * lax.rsqrt(jnp.mean(o * o, axis=-1, keepdims=True) + EPS) * og
        o = o * _silu(g_ref[:, hv[h]].astype(F32))
        o_ref[:, hv[h]] = o.astype(BF16)


def _gla(q, k, v, g, la, out_gain, ride_along, batch, seq, blk):
    t = q.shape[0]
    nb = seq // blk
    row = lambda b, i: (b * nb + i, 0)
    cast_specs, cast_shapes = _ride_along_specs(ride_along, batch * nb, lambda b, i: b * nb + i)
    outs = pl.pallas_call(
        _with_ride_along(functools.partial(_gla_kernel, n_chunks=blk // GLA_CHUNK), 6, 1, len(ride_along)),
        grid=(batch, nb),
        in_specs=[
            pl.BlockSpec((blk, GLA_DK), row),
            pl.BlockSpec((blk, GLA_DK), row),
            pl.BlockSpec((blk, GLA_DV), row),
            pl.BlockSpec((blk, GLA_DV), row),
            pl.BlockSpec((blk, GLA_DK), row),
            pl.BlockSpec((1, GLA_HV), lambda b, i: (0, 0)),
        ] + cast_specs,
        out_specs=[pl.BlockSpec((blk, GLA_DV), row)] + cast_specs,
        out_shape=[jax.ShapeDtypeStruct((t, GLA_DV), BF16)] + cast_shapes,
        scratch_shapes=[pltpu.VMEM((GLA_HEADS, GLA_HV, GLA_HK), F32)],
        compiler_params=_params(("arbitrary", "arbitrary"), 48 * 1024 * 1024),
        name="gla_core",
    )(q, k, v, g, la, out_gain, *ride_along)
    return outs[0], outs[1:]


def _ffn_kernel(x_ref, o_ref, wo_ref, gain_ref, wg_ref, wu_ref, wd_ref, y_ref, h_ref, *, pieces):
    j = pl.program_id(1)

    @pl.when(j == 0)
    def _():
        x1 = x_ref[...] + _dot(o_ref[...], wo_ref[...])
        h = _rms(x1, gain_ref[...]).astype(BF16)
        h_ref[...] = h
        y_ref[...] = x1 + _swiglu_pieces(h, wg_ref, wu_ref, wd_ref, pieces)

    @pl.when(j > 0)
    def _():
        y_ref[...] += _swiglu_pieces(h_ref[...], wg_ref, wu_ref, wd_ref, pieces)


def _ffn_residual(x, o, w_out, gain, w_gate, w_up, w_down, tm, pieces):
    t, d = x.shape
    ff = w_gate.shape[1]
    tf = sum(pieces)
    return pl.pallas_call(
        functools.partial(_ffn_kernel, pieces=pieces),
        grid=(t // tm, ff // tf),
        in_specs=[
            pl.BlockSpec((tm, d), lambda i, j: (i, 0)),
            pl.BlockSpec((tm, d), lambda i, j: (i, 0)),
            pl.BlockSpec((d, d), lambda i, j: (0, 0), pipeline_mode=pl.Buffered(1)),
            pl.BlockSpec((1, d), lambda i, j: (0, 0)),
            pl.BlockSpec((d, tf), lambda i, j: (0, j)),
            pl.BlockSpec((d, tf), lambda i, j: (0, j)),
            pl.BlockSpec((tf, d), lambda i, j: (j, 0)),
        ],
        out_specs=pl.BlockSpec((tm, d), lambda i, j: (i, 0)),
        out_shape=jax.ShapeDtypeStruct((t, d), F32),
        scratch_shapes=[pltpu.VMEM((tm, d), BF16)],
        compiler_params=_params(("arbitrary", "arbitrary"), 56 * 1024 * 1024),
        name="ffn_swiglu",
    )(x, o, w_out, gain, w_gate, w_up, w_down)


def _moba_qkv_kernel(x_ref, gain_ref, w_ref, qg_ref, kg_ref, q_ref, k_ref, v_ref):
    h = _rms(x_ref[...], gain_ref[...]).astype(BF16)
    d = MOBA_HEADS * MOBA_HD
    qg = qg_ref[...]
    kg = kg_ref[...]
    q = _dot(h, w_ref[:, 0:d])
    k = _dot(h, w_ref[:, d:2 * d])
    for hd in range(MOBA_HEADS):
        cols = slice(hd * MOBA_HD, (hd + 1) * MOBA_HD)
        q_ref[:, cols] = (_rms(q[:, cols], qg) * MOBA_Q_SCALE).astype(BF16)
        k_ref[:, cols] = _rms(k[:, cols], kg).astype(BF16)
    v_ref[...] = _dot(h, w_ref[:, 2 * d:]).astype(BF16)


def _moba_qkv(x, gain, w, q_gain, k_gain, tm):
    t = x.shape[0]
    d = MOBA_HEADS * MOBA_HD
    row = lambda i: (i, 0)
    fixed = lambda i: (0, 0)
    return pl.pallas_call(
        _moba_qkv_kernel,
        grid=(t // tm,),
        in_specs=[
            pl.BlockSpec((tm, D_MODEL), row),
            pl.BlockSpec((1, D_MODEL), fixed),
            pl.BlockSpec((D_MODEL, 3 * d), fixed),
            pl.BlockSpec((1, MOBA_HD), fixed),
            pl.BlockSpec((1, MOBA_HD), fixed),
        ],
        out_specs=[pl.BlockSpec((tm, d), row)] * 3,
        out_shape=[jax.ShapeDtypeStruct((t, d), BF16)] * 3,
        compiler_params=_params(("parallel",), 48 * 1024 * 1024),
        name="moba_qkv",
    )(x, gain, w, q_gain, k_gain)


def _reduce_rows(x, op, stop=8):
    while x.shape[0] > stop:
        half = x.shape[0] // 2
        x = op(x[:half], x[half:])
    return x


def _moba_kernel(q_ref, k_ref, v_ref, o_ref, *, n_blocks):
    seq = n_blocks * MOBA_BLOCK
    n_pad = 16
    ones_rows = jnp.where(lax.broadcasted_iota(I32, (BF16_ROWS, seq), 0) == 0, 1.0, 0.0)
    v_t = jnp.concatenate([v_ref[...].astype(F32).T, ones_rows], axis=0).astype(BF16)
    blk_of_key = lax.broadcasted_iota(I32, (n_pad, seq), 1) // MOBA_BLOCK
    blk_row = lax.broadcasted_iota(I32, (n_pad, seq), 0)
    indicator = jnp.where(blk_of_key == blk_row, 1.0 / MOBA_BLOCK, 0.0).astype(BF16)
    k_mean = _dot(indicator, k_ref[...])
    km_hi, km_lo = _split(k_mean)

    blk_id = lax.broadcasted_iota(I32, (n_pad, MOBA_BLOCK), 0)
    key_pos = lax.broadcasted_iota(I32, (MOBA_BLOCK, MOBA_BLOCK), 0)
    query_pos = lax.broadcasted_iota(I32, (MOBA_BLOCK, MOBA_BLOCK), 1)
    causal = key_pos <= query_pos

    def masked_scores(i):
        rows = slice(i * MOBA_BLOCK, (i + 1) * MOBA_BLOCK)
        n_keys = (i + 1) * MOBA_BLOCK
        q = q_ref[rows, :]
        s = _dot_nt(k_ref[0:n_keys, :], q)
        selected = None
        if i > MOBA_TOPK:
            gate = _dot_nt(km_hi, q) + _dot_nt(km_lo, q)
            gate = jnp.where(blk_id < i, gate, NEG_INF)
            rank = jnp.zeros_like(gate)
            for jp in range(i):
                row = gate[jp:jp + 1, :]
                beats = (row > gate) | ((row == gate) & (jp < blk_id))
                rank = rank + beats.astype(F32)
            selected = jnp.where((rank < MOBA_TOPK) & (blk_id < i), 1.0, 0.0)
        pieces = []
        for j in range(i + 1):
            sj = s[j * MOBA_BLOCK:(j + 1) * MOBA_BLOCK, :]
            if j == i:
                sj = jnp.where(causal, sj, NEG_INF)
            elif selected is not None:
                sj = jnp.where(selected[j:j + 1, :] > 0.5, sj, NEG_INF)
            pieces.append(sj)
        return pieces

    upcoming = masked_scores(0)
    for i in range(n_blocks):
        rows = slice(i * MOBA_BLOCK, (i + 1) * MOBA_BLOCK)
        n_keys = (i + 1) * MOBA_BLOCK
        pieces = upcoming
        if i + 1 < n_blocks:
            upcoming = masked_scores(i + 1)
        m = _reduce_rows(functools.reduce(jnp.maximum, pieces), jnp.maximum).max(axis=0, keepdims=True)
        probs = [jnp.exp2(sj - m).astype(BF16) for sj in pieces]
        p = probs[0] if len(probs) == 1 else jnp.concatenate(probs, axis=0)
        o_t = _dot(v_t[:, 0:n_keys], p)
        o_ref[rows, :] = (o_t[:MOBA_HD, :] / o_t[MOBA_HD:MOBA_HD + 1, :]).T.astype(BF16)


def _moba(q, k, v, ride_along, batch, seq):
    t, d = q.shape
    spec = pl.BlockSpec((seq, MOBA_HD), lambda b, h: (b, h))
    cast_specs, cast_shapes = _ride_along_specs(ride_along, batch * MOBA_HEADS, lambda b, h: b * MOBA_HEADS + h)
    outs = pl.pallas_call(
        _with_ride_along(functools.partial(_moba_kernel, n_blocks=seq // MOBA_BLOCK), 3, 1, len(ride_along)),
        grid=(batch, MOBA_HEADS),
        in_specs=[spec, spec, spec] + cast_specs,
        out_specs=[spec] + cast_specs,
        out_shape=[jax.ShapeDtypeStruct((t, d), BF16)] + cast_shapes,
        compiler_params=_params(("arbitrary", "arbitrary"), 48 * 1024 * 1024),
        name="moba_attn",
    )(q, k, v, *ride_along)
    return outs[0], outs[1:]


def _router_kernel(x_ref, o_ref, wo_ref, gain_ref, w_ref, x2_ref, h_ref, info_ref, cnt_ref, carry_ref, tri_ref,
                   *, region_rows):
    tr = x_ref.shape[0]

    @pl.when(pl.program_id(0) == 0)
    def _():
        carry_ref[...] = jnp.zeros_like(carry_ref)
        r = lax.broadcasted_iota(I32, (TOK_BLK, TOK_BLK), 0)
        c = lax.broadcasted_iota(I32, (TOK_BLK, TOK_BLK), 1)
        tri_ref[...] = (c < r).astype(BF16)

    x2 = x_ref[...] + _dot(o_ref[...], wo_ref[...])
    x2_ref[...] = x2
    h = _rms(x2, gain_ref[...])
    h_hi, h_lo = _split(h)
    h_ref[...] = h_hi
    both = _dot(h_hi, w_ref[...])
    logits = both[:, :LANES] + both[:, LANES:] + _dot(h_lo, w_ref[:, :LANES])
    e = lax.broadcasted_iota(I32, logits.shape, 1)
    logits = jnp.where(e < N_EXPERTS, logits, -jnp.inf)
    m1 = logits.max(axis=-1, keepdims=True)
    i1 = jnp.where(logits == m1, e, LANES).min(axis=-1, keepdims=True)
    first = e == i1
    rest = jnp.where(first, -jnp.inf, logits)
    m2 = rest.max(axis=-1, keepdims=True)
    i2 = jnp.where(rest == m2, e, LANES).min(axis=-1, keepdims=True)
    second = e == i2
    e2 = jnp.exp(m2 - m1)
    denom = 1.0 + e2
    sel = jnp.where(first | second, 1.0, 0.0).astype(BF16)
    nb = tr // TOK_BLK
    blk_row = lax.broadcasted_iota(I32, (nb, tr), 0)
    blk_tok = lax.broadcasted_iota(I32, (nb, tr), 1) // TOK_BLK
    cnt = _dot((blk_row == blk_tok).astype(BF16), sel)
    cnt_ref[...] = cnt
    offset = carry_ref[...]
    before = []
    for b in range(nb):
        before.append(_dot(tri_ref[...], sel[b * TOK_BLK:(b + 1) * TOK_BLK, :]) + offset)
        offset = offset + cnt[b:b + 1, :]
    carry_ref[...] = offset
    before = jnp.concatenate(before, axis=0)
    rank1 = jnp.where(first, before, 0.0).sum(axis=-1, keepdims=True)
    rank2 = jnp.where(second, before, 0.0).sum(axis=-1, keepdims=True)
    col = lax.broadcasted_iota(I32, (tr, 4), 1)
    fields = (i1.astype(F32) * region_rows + rank1, i2.astype(F32) * region_rows + rank2,
              1.0 / denom, e2 / denom)
    info = jnp.zeros((tr, 4), F32)
    for n, f in enumerate(fields):
        info = jnp.where(col == n, f, info)
    info_ref[...] = info


def _router(x, o, w_out, gain, w_router, tr):
    t = x.shape[0]
    nb = tr // TOK_BLK
    pad = ((0, 0), (0, LANES - N_EXPERTS))
    w_hi = w_router.astype(BF16)
    w_lo = (w_router - w_hi.astype(F32)).astype(BF16)
    w_both = jnp.concatenate([jnp.pad(w_hi, pad), jnp.pad(w_lo, pad)], axis=1)
    row = lambda i: (i, 0)
    fixed = lambda i: (0, 0)
    return pl.pallas_call(
        functools.partial(_router_kernel, region_rows=float(t)),
        grid=(t // tr,),
        in_specs=[
            pl.BlockSpec((tr, D_MODEL), row),
            pl.BlockSpec((tr, D_MODEL), row),
            pl.BlockSpec((D_MODEL, D_MODEL), fixed),
            pl.BlockSpec((1, D_MODEL), fixed),
            pl.BlockSpec((D_MODEL, 2 * LANES), fixed),
        ],
        out_specs=[
            pl.BlockSpec((tr, D_MODEL), row),
            pl.BlockSpec((tr, D_MODEL), row),
            pl.BlockSpec((tr, 4), row),
            pl.BlockSpec((nb, LANES), row),
        ],
        out_shape=[
            jax.ShapeDtypeStruct((t, D_MODEL), F32),
            jax.ShapeDtypeStruct((t, D_MODEL), BF16),
            jax.ShapeDtypeStruct((t, 4), F32),
            jax.ShapeDtypeStruct((t // TOK_BLK, LANES), F32),
        ],
        scratch_shapes=[pltpu.VMEM((1, LANES), F32), pltpu.VMEM((TOK_BLK, TOK_BLK), BF16)],
        compiler_params=_params(("arbitrary",), 56 * 1024 * 1024),
        name="moe_router",
    )(x, o, w_out, gain, w_both)


def _moe_plan(cnt, t):
    e_ids = jnp.arange(N_EXPERTS, dtype=I32)
    by_expert = lambda table, idx: jnp.sum(jnp.where(idx[..., None] == e_ids, table, 0), axis=-1)
    cnt_blk = cnt[:, :N_EXPERTS].astype(I32)
    cum_blk = jnp.cumsum(cnt_blk, axis=0)
    counts = cum_blk[-1]
    tiles_e = (counts + MOE_TM - 1) // MOE_TM
    tile_end = jnp.cumsum(tiles_e)
    tile_off = tile_end - tiles_e
    nt = TOP_K * t // MOE_TM + N_EXPERTS
    r = jnp.arange(nt, dtype=I32)
    valid = r < tile_end[-1]
    te = jnp.minimum(jnp.sum((r[:, None] >= tile_end[None, :]).astype(I32), axis=1), N_EXPERTS - 1)
    local = r - by_expert(tile_off, te)
    last_e = jnp.max(jnp.where(tiles_e > 0, e_ids, 0))
    tile_expert = jnp.where(valid, te, last_e)
    shift = e_ids * t - tile_off * MOE_TM

    per = MOE_TM // MOE_TG
    rg = jnp.arange(nt * per, dtype=I32)
    eg = jnp.repeat(te, per)
    k0 = jnp.repeat(local, per) * MOE_TM + (rg % per) * MOE_TG
    k1 = jnp.minimum(k0 + MOE_TG, by_expert(counts, eg))
    has = jnp.repeat(valid, per) & (k1 > k0)
    cum_e = by_expert(cum_blk[None, :, :], eg[:, None])
    s_lo = jnp.sum((cum_e <= k0[:, None]).astype(I32), axis=1)
    s_hi = jnp.sum((cum_e <= (k1 - 1)[:, None]).astype(I32), axis=1)
    g_nwin = jnp.where(has, (s_hi - s_lo) // GATHER_WB + 1, 0)
    g_slo = jnp.where(has, s_lo, 0)
    g_code = eg * t + k0

    nts = t // COMB_TS
    cnt_ts = cnt_blk.reshape(nts, COMB_TS // TOK_BLK, N_EXPERTS).sum(axis=1)
    seg_lo = (tile_off * MOE_TM)[None, :] + jnp.cumsum(cnt_ts, axis=0) - cnt_ts
    seg_hi = seg_lo + cnt_ts
    ws0 = (seg_lo // BF16_ROWS) * BF16_ROWS
    nw = jnp.where(cnt_ts > 0, (seg_hi - ws0 + COMB_WC - 1) // COMB_WC, 0)
    nw_end = jnp.cumsum(nw, axis=1)
    k = jnp.arange(COMB_MAXW, dtype=I32)
    ek = jnp.minimum(jnp.sum((k[None, :, None] >= nw_end[:, None, :]).astype(I32), axis=2), N_EXPERTS - 1)
    pick = lambda a: by_expert(a[:, None, :], ek)
    live = k[None, :] < nw_end[:, -1:]
    nominal = pick(ws0) + (k[None, :] - pick(nw_end - nw)) * COMB_WC
    shift_k = by_expert(shift, ek)
    c_start = jnp.minimum(nominal, by_expert(tile_end * MOE_TM, ek) - COMB_WC)
    c_off = shift_k + c_start
    c_lo = jnp.maximum(pick(seg_lo), nominal) + shift_k
    c_hi = jnp.minimum(pick(seg_hi), nominal + COMB_WC) + shift_k
    flat = lambda a: jnp.where(live, a, 0).reshape(-1)
    return dict(
        tile_expert=tile_expert, tile_valid=valid.astype(I32),
        g_code=g_code, g_slo=g_slo, g_nwin=g_nwin,
        c_start=flat(c_start), c_off=flat(c_off), c_lo=flat(c_lo), c_hi=flat(c_hi), c_nwin=nw_end[:, -1])


def _moe_gather_kernel(code_ref, slo_ref, nwin_ref, rows_ref, h_ref, hs_ref, gs_ref):
    n_blk = rows_ref.shape[1]
    for sub in range(hs_ref.shape[0] // MOE_TG):
        _gather_subtile(pl.program_id(0) * (hs_ref.shape[0] // MOE_TG) + sub,
                        slice(sub * MOE_TG, (sub + 1) * MOE_TG),
                        code_ref, slo_ref, nwin_ref, rows_ref, h_ref, hs_ref, gs_ref, n_blk)


def _gather_subtile(r, out_rows, code_ref, slo_ref, nwin_ref, rows_ref, h_ref, hs_ref, gs_ref, n_blk):
    row_id = (code_ref[r] + lax.broadcasted_iota(I32, (MOE_TG, TOK_BLK), 0)).astype(F32)

    def window(k, first):
        nominal = slo_ref[r] + k * GATHER_WB
        sb = jnp.minimum(nominal, n_blk - GATHER_WB)
        fresh = sb + lax.broadcasted_iota(I32, (GATHER_WB, TOK_BLK), 0) >= nominal
        p1 = jnp.where(fresh, rows_ref[0, pl.ds(sb, GATHER_WB), :], -1.0)
        p2 = jnp.where(fresh, rows_ref[1, pl.ds(sb, GATHER_WB), :], -1.0)
        w1 = rows_ref[2, pl.ds(sb, GATHER_WB), :]
        w2 = rows_ref[3, pl.ds(sb, GATHER_WB), :]
        pieces = []
        g = jnp.zeros((MOE_TG, TOK_BLK), F32)
        for a in range(GATHER_WB):
            m1 = p1[a:a + 1, :] == row_id
            m2 = p2[a:a + 1, :] == row_id
            pieces.append(jnp.where(m1 | m2, 1.0, 0.0).astype(BF16))
            g = g + jnp.where(m1, w1[a:a + 1, :], 0.0) + jnp.where(m2, w2[a:a + 1, :], 0.0)
        onehot = jnp.concatenate(pieces, axis=1)
        tok0 = pl.multiple_of(sb * TOK_BLK, TOK_BLK)
        rows = _dot(onehot, h_ref[pl.ds(tok0, GATHER_WB * TOK_BLK), :]).astype(BF16)
        weight = g.sum(axis=-1, keepdims=True)
        if first:
            hs_ref[out_rows, :] = rows
            gs_ref[out_rows, :] = weight
        else:
            hs_ref[out_rows, :] += rows
            gs_ref[out_rows, :] += weight

    window(0, True)

    def more(k, carry):
        window(k, False)
        return carry

    lax.fori_loop(1, nwin_ref[r], more, 0)


def _moe_gather(plan, h, rows):
    t = h.shape[0]
    n_rows = plan["g_code"].shape[0] * MOE_TG
    n_blk = t // TOK_BLK
    grid_spec = pltpu.PrefetchScalarGridSpec(
        num_scalar_prefetch=3,
        grid=(n_rows // GATHER_ROWS,),
        in_specs=[
            pl.BlockSpec((4, n_blk, TOK_BLK), lambda r, *_: (0, 0, 0), pipeline_mode=pl.Buffered(1)),
            pl.BlockSpec((t, D_MODEL), lambda r, *_: (0, 0), pipeline_mode=pl.Buffered(1)),
        ],
        out_specs=[
            pl.BlockSpec((GATHER_ROWS, D_MODEL), lambda r, *_: (r, 0)),
            pl.BlockSpec((GATHER_ROWS, 1), lambda r, *_: (r, 0)),
        ],
    )
    return pl.pallas_call(
        _moe_gather_kernel,
        grid_spec=grid_spec,
        out_shape=[
            jax.ShapeDtypeStruct((n_rows, D_MODEL), BF16),
            jax.ShapeDtypeStruct((n_rows, 1), F32),
        ],
        compiler_params=_params(("parallel",), 52 * 1024 * 1024),
        name="moe_gather",
    )(plan["g_code"], plan["g_slo"], plan["g_nwin"], rows, h)


def _moe_expert_kernel(te_ref, tv_ref, hs_ref, gs_ref, wg_ref, wu_ref, wd_ref, y_ref, acc_ref, *, nj, pieces):
    r = pl.program_id(0)
    j = pl.program_id(1)
    valid = tv_ref[r] > 0
    part = lambda: _swiglu_pieces(hs_ref[...], wg_ref, wu_ref, wd_ref, pieces)

    if nj == 1:
        @pl.when(valid)
        def _():
            y_ref[...] = (part() * gs_ref[...]).astype(BF16)
    else:
        @pl.when(valid & (j == 0))
        def _():
            acc_ref[...] = part()

        if nj > 2:
            @pl.when(valid & (j > 0) & (j < nj - 1))
            def _():
                acc_ref[...] += part()

        @pl.when(valid & (j == nj - 1))
        def _():
            y_ref[...] = ((acc_ref[...] + part()) * gs_ref[...]).astype(BF16)

    @pl.when(jnp.logical_not(valid) & (j == nj - 1))
    def _():
        y_ref[...] = jnp.zeros_like(y_ref)


def _moe_experts(plan, hs, gs, w_gate, w_up, w_down, pieces):
    tf = sum(pieces)
    n_rows, d = hs.shape
    ff = w_gate.shape[2]
    nj = ff // tf
    jj = lambda r, j, tv: jnp.where(tv[r] > 0, j, nj - 1)
    grid_spec = pltpu.PrefetchScalarGridSpec(
        num_scalar_prefetch=2,
        grid=(n_rows // MOE_TM, nj),
        in_specs=[
            pl.BlockSpec((MOE_TM, d), lambda r, j, te, tv: (r, 0)),
            pl.BlockSpec((MOE_TM, 1), lambda r, j, te, tv: (r, 0)),
            pl.BlockSpec((None, d, tf), lambda r, j, te, tv: (te[r], 0, jj(r, j, tv))),
            pl.BlockSpec((None, d, tf), lambda r, j, te, tv: (te[r], 0, jj(r, j, tv))),
            pl.BlockSpec((None, tf, d), lambda r, j, te, tv: (te[r], jj(r, j, tv), 0)),
        ],
        out_specs=pl.BlockSpec((MOE_TM, d), lambda r, j, te, tv: (r, 0)),
        scratch_shapes=[pltpu.VMEM((MOE_TM if nj > 1 else 8, d), F32)],
    )
    return pl.pallas_call(
        functools.partial(_moe_expert_kernel, nj=nj, pieces=pieces),
        grid_spec=grid_spec,
        out_shape=jax.ShapeDtypeStruct((n_rows, d), BF16),
        compiler_params=_params(("arbitrary", "arbitrary"), 56 * 1024 * 1024),
        name="moe_experts",
    )(plan["tile_expert"], plan["tile_valid"], hs, gs, w_gate, w_up, w_down)


def _moe_combine_kernel(start_ref, off_ref, lo_ref, hi_ref, nwin_ref, x_ref, info_ref, y_hbm, o_ref,
                        ybuf, ybuf_more, sem, sem_more):
    i = pl.program_id(0)
    n_tiles = pl.num_programs(0)

    def window_copy(idx, dst, dsem):
        start = pl.multiple_of(start_ref[idx], BF16_ROWS)
        return pltpu.make_async_copy(y_hbm.at[pl.ds(start, COMB_WC), :], dst, dsem)

    def first_group(tile, wait):
        buf = tile % COMB_AHEAD_BUFS
        for s in range(COMB_GROUP):
            @pl.when(s < nwin_ref[tile])
            def _():
                cp = window_copy(tile * COMB_MAXW + s, ybuf.at[buf, pl.ds(s * COMB_WC, COMB_WC), :],
                                 sem.at[buf, s])
                cp.wait() if wait else cp.start()

    @pl.when(i == 0)
    def _():
        ybuf[...] = jnp.zeros_like(ybuf)
        ybuf_more[...] = jnp.zeros_like(ybuf_more)
        for ahead in range(COMB_AHEAD_BUFS - 1):
            @pl.when(ahead < n_tiles)
            def _():
                first_group(ahead, wait=False)

    @pl.when(i + COMB_AHEAD_BUFS - 1 < n_tiles)
    def _():
        first_group(i + COMB_AHEAD_BUFS - 1, wait=False)

    first_group(i, wait=True)

    code1 = jnp.broadcast_to(info_ref[:, 0:1].astype(I32), (COMB_TS, COMB_WC))
    code2 = jnp.broadcast_to(info_ref[:, 1:2].astype(I32), (COMB_TS, COMB_WC))
    lane = lax.broadcasted_iota(I32, (COMB_TS, COMB_WC), 1)
    lane_row = lax.broadcasted_iota(I32, (1, COMB_WC), 1)

    def onehot_group(g):
        pieces = []
        for s in range(COMB_GROUP):
            idx = i * COMB_MAXW + g * COMB_GROUP + s
            off = off_ref[idx]
            in_span = (lane_row >= lo_ref[idx] - off) & (lane_row < hi_ref[idx] - off)
            target = lane + off
            hit = ((code1 == target) | (code2 == target)) & in_span
            pieces.append(jnp.where(hit, 1.0, 0.0).astype(BF16))
        return jnp.concatenate(pieces, axis=1)

    o_ref[...] = x_ref[...] + _dot(onehot_group(0), ybuf[i % COMB_AHEAD_BUFS])

    n = nwin_ref[i]

    def more(g, carry):
        for wait in (False, True):
            for s in range(COMB_GROUP):
                @pl.when(g * COMB_GROUP + s < n)
                def _():
                    cp = window_copy(i * COMB_MAXW + g * COMB_GROUP + s,
                                     ybuf_more.at[pl.ds(s * COMB_WC, COMB_WC), :], sem_more.at[s])
                    cp.wait() if wait else cp.start()
        o_ref[...] += _dot(onehot_group(g), ybuf_more[...])
        return carry

    lax.fori_loop(1, (n + COMB_GROUP - 1) // COMB_GROUP, more, 0)


def _moe_combine(plan, x, info, y):
    t, d = x.shape
    width = COMB_GROUP * COMB_WC
    grid_spec = pltpu.PrefetchScalarGridSpec(
        num_scalar_prefetch=5,
        grid=(t // COMB_TS,),
        in_specs=[
            pl.BlockSpec((COMB_TS, d), lambda i, *_: (i, 0)),
            pl.BlockSpec((COMB_TS, 4), lambda i, *_: (i, 0)),
            pl.BlockSpec(memory_space=pl.ANY),
        ],
        out_specs=pl.BlockSpec((COMB_TS, d), lambda i, *_: (i, 0)),
        scratch_shapes=[
            pltpu.VMEM((COMB_AHEAD_BUFS, width, d), BF16),
            pltpu.VMEM((width, d), BF16),
            pltpu.SemaphoreType.DMA((COMB_AHEAD_BUFS, COMB_GROUP)),
            pltpu.SemaphoreType.DMA((COMB_GROUP,)),
        ],
    )
    return pl.pallas_call(
        _moe_combine_kernel,
        grid_spec=grid_spec,
        out_shape=jax.ShapeDtypeStruct((t, d), F32),
        compiler_params=_params(("arbitrary",)),
        name="moe_combine",
    )(plan["c_start"], plan["c_off"], plan["c_lo"], plan["c_hi"], plan["c_nwin"], x, info, y)


def kernel(x, norm_mix, norm_ffn, gla_w_in, gla_w_gate2, gla_b_gate, gla_out_gain, gla_w_out,
           moba_w_qkv, moba_q_gain, moba_k_gain, moba_w_out, ffn_w_gate, ffn_w_up, ffn_w_down,
           moe_w_router, moe_w_gate, moe_w_up, moe_w_down):
    batch, seq, d = x.shape
    t = batch * seq
    xt = x.reshape(t, d)

    (q, k, v, g, la), (ffn_wg, ffn_wu, ffn_wd, gla_wo, moba_wqkv, moba_wo) = _gla_inproj(
        xt, norm_mix[0:1], gla_w_in, gla_w_gate2[0], gla_b_gate[0:1],
        ride_along=(ffn_w_gate, ffn_w_up, ffn_w_down, gla_w_out, moba_w_qkv, moba_w_out), tm=512)
    o, _ = _gla(q, k, v, g, la, gla_out_gain[0:1], (), batch, seq, blk=256)
    xt = _ffn_residual(xt, o, gla_wo[0], norm_ffn[0:1], ffn_wg[0], ffn_wu[0], ffn_wd[0],
                       tm=512, pieces=MOE_PIECES)

    q, k, v = _moba_qkv(xt, norm_mix[1:2], moba_wqkv[0], moba_q_gain[0:1], moba_k_gain[0:1], tm=512)
    o, (moe_wg, moe_wu, moe_wd) = _moba(q, k, v, (moe_w_gate[0], moe_w_up[0], moe_w_down[0]), batch, seq)
    xt, h, info, cnt = _router(xt, o, moba_wo[0], norm_ffn[1:2], moe_w_router[0], tr=1024)
    plan = _moe_plan(cnt, t)
    rows = info.T.reshape(4, t // TOK_BLK, TOK_BLK)
    hs, gs = _moe_gather(plan, h, rows)
    y = _moe_experts(plan, hs, gs, moe_wg, moe_wu, moe_wd, pieces=MOE_PIECES + MOE_PIECES)
    xt = _moe_combine(plan, xt, info, y)
    return xt.reshape(batch, seq, d)
```

```python
import functools

import jax
import jax.numpy as jnp
from jax import lax
from jax.experimental import pallas as pl
from jax.experimental.pallas import tpu as pltpu

F32 = jnp.float32
BF16 = jnp.bfloat16
I32 = jnp.int32

EPS = 1e-6
NEG_INF = -1e30

D_MODEL = 1024
GLA_HEADS = 4
GLA_DK = 512
GLA_DV = 1024
GLA_HK = 128
GLA_HV = 256
GLA_RANK = 16
GLA_NORMALIZER = 16.0
GLA_CHUNK = 64

MOBA_HEADS = 8
MOBA_HD = 128
MOBA_BLOCK = 256
MOBA_TOPK = 3
MOBA_Q_SCALE = MOBA_HD ** -0.5 * 1.4426950408889634

N_EXPERTS = 8
TOP_K = 2

LANES = 128
BF16_ROWS = 16

MOE_PIECES = (512, 512, 768)
MOE_TM = 256
MOE_TG = 128
GATHER_ROWS = 512
TOK_BLK = LANES
GATHER_WB = 6
COMB_TS = 256
COMB_WC = 128
COMB_GROUP = 8
COMB_AHEAD_BUFS = 3
COMB_MAXW = 16


def _params(semantics, vmem_bytes=None):
    return pltpu.CompilerParams(dimension_semantics=semantics, vmem_limit_bytes=vmem_bytes)


def _rms(x, gain):
    y = x * lax.rsqrt(jnp.mean(x * x, axis=-1, keepdims=True) + EPS)
    return y * gain


def _dot(a, b):
    return jnp.dot(a, b, preferred_element_type=F32)


def _dot_nt(a, b):
    return lax.dot_general(a, b, (((1,), (1,)), ((), ())), preferred_element_type=F32)


def _dot_tn(a, b):
    return lax.dot_general(a, b, (((0,), (0,)), ((), ())), preferred_element_type=F32)


def _split(a):
    hi = a.astype(BF16)
    lo = (a - hi.astype(F32)).astype(BF16)
    return hi, lo


def _log_sigmoid(z):
    return jnp.minimum(z, 0.0) - jnp.log(1.0 + jnp.exp(-jnp.abs(z)))


def _silu(z):
    return z / (1.0 + jnp.exp(-z))


def _swiglu_pieces(h, wg_ref, wu_ref, wd_ref, widths):
    bounds = [sum(widths[:n]) for n in range(len(widths) + 1)]
    assert bounds[-1] == wg_ref.shape[1], (widths, wg_ref.shape)
    gate_up = lambda c: (_dot(h, wg_ref[:, bounds[c]:bounds[c + 1]]), _dot(h, wu_ref[:, bounds[c]:bounds[c + 1]]))
    out = None
    upcoming = gate_up(0)
    for c in range(len(widths)):
        g, u = upcoming
        if c + 1 < len(widths):
            upcoming = gate_up(c + 1)
        d = _dot((_silu(g) * u).astype(BF16), wd_ref[bounds[c]:bounds[c + 1], :])
        out = d if out is None else out + d
    return out


def _ride_along_specs(arrays, n_steps, step):
    specs, shapes = [], []
    for w in arrays:
        n_e, rows, cols = w.shape
        per_e = n_steps // n_e
        if rows % (per_e * BF16_ROWS) == 0:
            blk = (None, rows // per_e, cols)
            imap = lambda *ids, per_e=per_e: (step(*ids) // per_e, step(*ids) % per_e, 0)
        else:
            assert cols % (per_e * LANES) == 0, (w.shape, n_steps)
            blk = (None, rows, cols // per_e)
            imap = lambda *ids, per_e=per_e: (step(*ids) // per_e, 0, step(*ids) % per_e)
        specs.append(pl.BlockSpec(blk, imap))
        shapes.append(jax.ShapeDtypeStruct(w.shape, BF16))
    return specs, shapes


def _with_ride_along(body, n_in, n_out, n_cast):
    def kernel(*refs):
        ins, refs = refs[:n_in], refs[n_in:]
        cast_in, refs = refs[:n_cast], refs[n_cast:]
        outs, refs = refs[:n_out], refs[n_out:]
        cast_out, scratch = refs[:n_cast], refs[n_cast:]
        for src, dst in zip(cast_in, cast_out):
            dst[...] = src[...].astype(BF16)
        body(*ins, *outs, *scratch)
    return kernel


def _gla_inproj_kernel(x_next_ref, x_first_ref, gain_ref, w_ref, wg2_ref, b_ref,
                       q_ref, k_ref, v_ref, g_ref, la_ref, wb_ref, h_ref):
    i = pl.program_id(0)

    @pl.when(i == 0)
    def _():
        wb_ref[...] = w_ref[...].astype(BF16)
        h_ref[0] = _rms(x_first_ref[...], gain_ref[...]).astype(BF16)

    n_main = 2 * GLA_DK + 2 * GLA_DV

    def step(cur, nxt):
        h = h_ref[cur]
        h_ref[nxt] = _rms(x_next_ref[...], gain_ref[...]).astype(BF16)
        a_lr = _dot(h, wb_ref[:, n_main:])
        a_hi, a_lo = _split(a_lr)
        w_hi, w_lo = _split(wg2_ref[...])
        z = _dot(jnp.concatenate([a_hi, a_lo, a_hi], axis=1), jnp.concatenate([w_hi, w_hi, w_lo], axis=0))
        la_ref[...] = _log_sigmoid(z + b_ref[...]) / GLA_NORMALIZER
        q_ref[...] = _dot(h, wb_ref[:, 0:GLA_DK]).astype(BF16)
        k_ref[...] = _dot(h, wb_ref[:, GLA_DK:2 * GLA_DK]).astype(BF16)
        v_ref[...] = _dot(h, wb_ref[:, 2 * GLA_DK:2 * GLA_DK + GLA_DV]).astype(BF16)
        g_ref[...] = _dot(h, wb_ref[:, 2 * GLA_DK + GLA_DV:n_main]).astype(BF16)

    @pl.when(i % 2 == 0)
    def _():
        step(0, 1)

    @pl.when(i % 2 == 1)
    def _():
        step(1, 0)


def _gla_inproj(x, gain, w_in, w_gate2, b_gate, ride_along, tm):
    t = x.shape[0]
    n_steps = t // tm
    row = lambda i: (i, 0)
    fixed = lambda i: (0, 0)
    cast_specs, cast_shapes = _ride_along_specs(ride_along, n_steps, lambda i: i)
    outs = pl.pallas_call(
        _with_ride_along(_gla_inproj_kernel, 6, 5, len(ride_along)),
        grid=(n_steps,),
        in_specs=[
            pl.BlockSpec((tm, D_MODEL), lambda i: (jnp.minimum(i + 1, n_steps - 1), 0)),
            pl.BlockSpec((tm, D_MODEL), fixed),
            pl.BlockSpec((1, D_MODEL), fixed),
            pl.BlockSpec((None,) + w_in.shape[1:], lambda i: (0, 0, 0), pipeline_mode=pl.Buffered(1)),
            pl.BlockSpec((GLA_RANK, GLA_DK), fixed),
            pl.BlockSpec((1, GLA_DK), fixed),
        ] + cast_specs,
        out_specs=[
            pl.BlockSpec((tm, GLA_DK), row),
            pl.BlockSpec((tm, GLA_DK), row),
            pl.BlockSpec((tm, GLA_DV), row),
            pl.BlockSpec((tm, GLA_DV), row),
            pl.BlockSpec((tm, GLA_DK), row),
        ] + cast_specs,
        out_shape=[
            jax.ShapeDtypeStruct((t, GLA_DK), BF16),
            jax.ShapeDtypeStruct((t, GLA_DK), BF16),
            jax.ShapeDtypeStruct((t, GLA_DV), BF16),
            jax.ShapeDtypeStruct((t, GLA_DV), BF16),
            jax.ShapeDtypeStruct((t, GLA_DK), F32),
        ] + cast_shapes,
        scratch_shapes=[pltpu.VMEM(w_in.shape[1:], BF16), pltpu.VMEM((2, tm, D_MODEL), BF16)],
        compiler_params=_params(("arbitrary",), 48 * 1024 * 1024),
        name="gla_inproj",
    )(x, x, gain, w_in, w_gate2, b_gate, *ride_along)
    return outs[:5], outs[5:]


def _gla_kernel(q_ref, k_ref, v_ref, g_ref, la_ref, og_ref, o_ref, st_ref, *, n_chunks):
    @pl.when(pl.program_id(1) == 0)
    def _():
        st_ref[...] = jnp.zeros_like(st_ref)

    blk = n_chunks * GLA_CHUNK
    r = lax.broadcasted_iota(I32, (blk, blk), 0)
    c = lax.broadcasted_iota(I32, (blk, blk), 1)
    causal = (r >= c) & (r // GLA_CHUNK == c // GLA_CHUNK)
    tril = causal.astype(BF16)
    la = la_ref[...]
    la_hi, la_mid = _split(la)
    la_lo = (la - la_hi.astype(F32) - la_mid.astype(F32)).astype(BF16)
    cum = _dot(tril, la_hi) + _dot(tril, la_mid) + _dot(tril, la_lo)
    ends = [cum[(ci + 1) * GLA_CHUNK - 1:(ci + 1) * GLA_CHUNK, :] for ci in range(n_chunks)]
    last = jnp.concatenate([jnp.broadcast_to(e, (GLA_CHUNK, GLA_DK)) for e in ends], axis=0)
    q = q_ref[...].astype(F32) * (GLA_HK ** -0.5)
    k = k_ref[...].astype(F32)
    q_dec = (q * jnp.exp(cum)).astype(BF16)
    k_inv = (k * jnp.exp(-cum)).astype(BF16)
    k_end = (k * jnp.exp(last - cum)).astype(BF16)
    chunk_decay = [jnp.exp(e) for e in ends]
    og = og_ref[...]
    heads = range(GLA_HEADS)
    hk = [slice(h * GLA_HK, (h + 1) * GLA_HK) for h in heads]
    hv = [slice(h * GLA_HV, (h + 1) * GLA_HV) for h in heads]
    scores = [jnp.where(causal, _dot_nt(q_dec[:, hk[h]], k_inv[:, hk[h]]), 0.0).astype(BF16) for h in heads]
    state_in = [[_dot_tn(v_ref[ci * GLA_CHUNK:(ci + 1) * GLA_CHUNK, hv[h]],
                         k_end[ci * GLA_CHUNK:(ci + 1) * GLA_CHUNK, hk[h]])
                 for ci in range(n_chunks)] for h in heads]
    intra = [_dot(scores[h], v_ref[:, hv[h]]) for h in heads]
    for h in heads:
        st = st_ref[h]
        inter = []
        for ci in range(n_chunks):
            rows = slice(ci * GLA_CHUNK, (ci + 1) * GLA_CHUNK)
            inter.append(_dot_nt(q_dec[rows, hk[h]], st.astype(BF16)))
            st = st * chunk_decay[ci][:, hk[h]] + state_in[h][ci]
        st_ref[h] = st
        o = intra[h] + jnp.concatenate(inter, axis=0)
        o = o * lax.rsqrt(jnp.mean(o * o, axis=-1, keepdims=True) + EPS) * og
        o = o * _silu(g_ref[:, hv[h]].astype(F32))
        o_ref[:, hv[h]] = o.astype(BF16)


def _gla(q, k, v, g, la, out_gain, ride_along, batch, seq, blk):
    t = q.shape[0]
    nb = seq // blk
    row = lambda b, i: (b * nb + i, 0)
    cast_specs, cast_shapes = _ride_along_specs(ride_along, batch * nb, lambda b, i: b * nb + i)
    outs = pl.pallas_call(
        _with_ride_along(functools.partial(_gla_kernel, n_chunks=blk // GLA_CHUNK), 6, 1, len(ride_along)),
        grid=(batch, nb),
        in_specs=[
            pl.BlockSpec((blk, GLA_DK), row),
            pl.BlockSpec((blk, GLA_DK), row),
            pl.BlockSpec((blk, GLA_DV), row),
            pl.BlockSpec((blk, GLA_DV), row),
            pl.BlockSpec((blk, GLA_DK), row),
            pl.BlockSpec((1, GLA_HV), lambda b, i: (0, 0)),
        ] + cast_specs,
        out_specs=[pl.BlockSpec((blk, GLA_DV), row)] + cast_specs,
        out_shape=[jax.ShapeDtypeStruct((t, GLA_DV), BF16)] + cast_shapes,
        scratch_shapes=[pltpu.VMEM((GLA_HEADS, GLA_HV, GLA_HK), F32)],
        compiler_params=_params(("arbitrary", "arbitrary"), 48 * 1024 * 1024),
        name="gla_core",
    )(q, k, v, g, la, out_gain, *ride_along)
    return outs[0], outs[1:]


def _ffn_kernel(x_ref, o_ref, wo_ref, gain_ref, wg_ref, wu_ref, wd_ref, y_ref, *, pieces):
    x1 = x_ref[...] + _dot(o_ref[...], wo_ref[...])
    h = _rms(x1, gain_ref[...]).astype(BF16)
    y_ref[...] = x1 + _swiglu_pieces(h, wg_ref, wu_ref, wd_ref, pieces)


def _ffn_residual(x, o, w_out, gain, w_gate, w_up, w_down, tm, pieces):
    t, d = x.shape
    row = lambda i: (i, 0)
    resident = lambda a: pl.BlockSpec(a.shape, lambda i: (0, 0), pipeline_mode=pl.Buffered(1))
    return pl.pallas_call(
        functools.partial(_ffn_kernel, pieces=pieces),
        grid=(t // tm,),
        in_specs=[pl.BlockSpec((tm, d), row), pl.BlockSpec((tm, d), row), resident(w_out), resident(gain),
                  resident(w_gate), resident(w_up), resident(w_down)],
        out_specs=pl.BlockSpec((tm, d), row),
        out_shape=jax.ShapeDtypeStruct((t, d), F32),
        compiler_params=_params(("parallel",), 56 * 1024 * 1024),
        name="ffn_swiglu",
    )(x, o, w_out, gain, w_gate, w_up, w_down)


def _moba_qkv_kernel(x_ref, gain_ref, w_ref, qg_ref, kg_ref, q_ref, k_ref, v_ref):
    h = _rms(x_ref[...], gain_ref[...]).astype(BF16)
    d = MOBA_HEADS * MOBA_HD
    qg = qg_ref[...]
    kg = kg_ref[...]
    q = _dot(h, w_ref[:, 0:d])
    k = _dot(h, w_ref[:, d:2 * d])
    for hd in range(MOBA_HEADS):
        cols = slice(hd * MOBA_HD, (hd + 1) * MOBA_HD)
        q_ref[:, cols] = (_rms(q[:, cols], qg) * MOBA_Q_SCALE).astype(BF16)
        k_ref[:, cols] = _rms(k[:, cols], kg).astype(BF16)
    v_ref[...] = _dot(h, w_ref[:, 2 * d:]).astype(BF16)


def _moba_qkv(x, gain, w, q_gain, k_gain, tm):
    t = x.shape[0]
    d = MOBA_HEADS * MOBA_HD
    row = lambda i: (i, 0)
    fixed = lambda i: (0, 0)
    return pl.pallas_call(
        _moba_qkv_kernel,
        grid=(t // tm,),
        in_specs=[
            pl.BlockSpec((tm, D_MODEL), row),
            pl.BlockSpec((1, D_MODEL), fixed),
            pl.BlockSpec((D_MODEL, 3 * d), fixed),
            pl.BlockSpec((1, MOBA_HD), fixed),
            pl.BlockSpec((1, MOBA_HD), fixed),
        ],
        out_specs=[pl.BlockSpec((tm, d), row)] * 3,
        out_shape=[jax.ShapeDtypeStruct((t, d), BF16)] * 3,
        compiler_params=_params(("parallel",), 48 * 1024 * 1024),
        name="moba_qkv",
    )(x, gain, w, q_gain, k_gain)


def _reduce_rows(x, op, stop=8):
    while x.shape[0] > stop:
        half = x.shape[0] // 2
        x = op(x[:half], x[half:])
    return x


def _moba_kernel(q_ref, k_ref, v_ref, o_ref, *, n_blocks):
    seq = n_blocks * MOBA_BLOCK
    n_pad = 16
    ones_rows = jnp.where(lax.broadcasted_iota(I32, (BF16_ROWS, seq), 0) == 0, 1.0, 0.0)
    v_t = jnp.concatenate([v_ref[...].astype(F32).T, ones_rows], axis=0).astype(BF16)
    blk_of_key = lax.broadcasted_iota(I32, (n_pad, seq), 1) // MOBA_BLOCK
    blk_row = lax.broadcasted_iota(I32, (n_pad, seq), 0)
    indicator = jnp.where(blk_of_key == blk_row, 1.0 / MOBA_BLOCK, 0.0).astype(BF16)
    k_mean = _dot(indicator, k_ref[...])
    km_hi, km_lo = _split(k_mean)

    blk_id = lax.broadcasted_iota(I32, (n_pad, MOBA_BLOCK), 0)
    key_pos = lax.broadcasted_iota(I32, (MOBA_BLOCK, MOBA_BLOCK), 0)
    query_pos = lax.broadcasted_iota(I32, (MOBA_BLOCK, MOBA_BLOCK), 1)
    causal = key_pos <= query_pos

    def masked_scores(i):
        rows = slice(i * MOBA_BLOCK, (i + 1) * MOBA_BLOCK)
        n_keys = (i + 1) * MOBA_BLOCK
        q = q_ref[rows, :]
        s = _dot_nt(k_ref[0:n_keys, :], q)
        selected = None
        if i > MOBA_TOPK:
            gate = _dot_nt(km_hi, q) + _dot_nt(km_lo, q)
            gate = jnp.where(blk_id < i, gate, NEG_INF)
            rank = jnp.zeros_like(gate)
            for jp in range(i):
                row = gate[jp:jp + 1, :]
                beats = (row > gate) | ((row == gate) & (jp < blk_id))
                rank = rank + beats.astype(F32)
            selected = jnp.where((rank < MOBA_TOPK) & (blk_id < i), 1.0, 0.0)
        pieces = []
        for j in range(i + 1):
            sj = s[j * MOBA_BLOCK:(j + 1) * MOBA_BLOCK, :]
            if j == i:
                sj = jnp.where(causal, sj, NEG_INF)
            elif selected is not None:
                sj = jnp.where(selected[j:j + 1, :] > 0.5, sj, NEG_INF)
            pieces.append(sj)
        return pieces

    def weighted_values(i, p):
        o_t = _dot(v_t[:, 0:(i + 1) * MOBA_BLOCK], p)
        rows = slice(i * MOBA_BLOCK, (i + 1) * MOBA_BLOCK)
        o_ref[rows, :] = (o_t[:MOBA_HD, :] / o_t[MOBA_HD:MOBA_HD + 1, :]).T.astype(BF16)

    upcoming = masked_scores(0)
    pending = None
    for i in range(n_blocks):
        pieces = upcoming
        if i + 1 < n_blocks:
            upcoming = masked_scores(i + 1)
        if pending is not None:
            weighted_values(i - 1, pending)
        m = _reduce_rows(functools.reduce(jnp.maximum, pieces), jnp.maximum).max(axis=0, keepdims=True)
        probs = [jnp.exp2(sj - m).astype(BF16) for sj in pieces]
        pending = probs[0] if len(probs) == 1 else jnp.concatenate(probs, axis=0)
    weighted_values(n_blocks - 1, pending)


def _moba(q, k, v, ride_along, batch, seq):
    t, d = q.shape
    spec = pl.BlockSpec((seq, MOBA_HD), lambda b, h: (b, h))
    cast_specs, cast_shapes = _ride_along_specs(ride_along, batch * MOBA_HEADS, lambda b, h: b * MOBA_HEADS + h)
    outs = pl.pallas_call(
        _with_ride_along(functools.partial(_moba_kernel, n_blocks=seq // MOBA_BLOCK), 3, 1, len(ride_along)),
        grid=(batch, MOBA_HEADS),
        in_specs=[spec, spec, spec] + cast_specs,
        out_specs=[spec] + cast_specs,
        out_shape=[jax.ShapeDtypeStruct((t, d), BF16)] + cast_shapes,
        compiler_params=_params(("arbitrary", "arbitrary"), 48 * 1024 * 1024),
        name="moba_attn",
    )(q, k, v, *ride_along)
    return outs[0], outs[1:]


def _router_kernel(x_ref, o_ref, wo_ref, gain_ref, w_ref, x2_ref, h_ref, info_ref, rows_ref, cnt_ref,
                   carry_ref, tri_ref,
                   *, region_rows):
    tr = x_ref.shape[0]

    @pl.when(pl.program_id(0) == 0)
    def _():
        carry_ref[...] = jnp.zeros_like(carry_ref)
        r = lax.broadcasted_iota(I32, (TOK_BLK, TOK_BLK), 0)
        c = lax.broadcasted_iota(I32, (TOK_BLK, TOK_BLK), 1)
        tri_ref[...] = (c < r).astype(BF16)

    x2 = x_ref[...] + _dot(o_ref[...], wo_ref[...])
    x2_ref[...] = x2
    h = _rms(x2, gain_ref[...])
    h_hi, h_lo = _split(h)
    h_ref[...] = h_hi
    both = _dot(h_hi, w_ref[...])
    logits = both[:, :LANES] + both[:, LANES:] + _dot(h_lo, w_ref[:, :LANES])
    e = lax.broadcasted_iota(I32, logits.shape, 1)
    logits = jnp.where(e < N_EXPERTS, logits, -jnp.inf)
    m1 = logits.max(axis=-1, keepdims=True)
    i1 = jnp.where(logits == m1, e, LANES).min(axis=-1, keepdims=True)
    first = e == i1
    rest = jnp.where(first, -jnp.inf, logits)
    m2 = rest.max(axis=-1, keepdims=True)
    i2 = jnp.where(rest == m2, e, LANES).min(axis=-1, keepdims=True)
    second = e == i2
    e2 = jnp.exp(m2 - m1)
    denom = 1.0 + e2
    sel = jnp.where(first | second, 1.0, 0.0).astype(BF16)
    nb = tr // TOK_BLK
    blk_row = lax.broadcasted_iota(I32, (nb, tr), 0)
    blk_tok = lax.broadcasted_iota(I32, (nb, tr), 1) // TOK_BLK
    cnt = _dot((blk_row == blk_tok).astype(BF16), sel)
    cnt_ref[...] = cnt
    offset = carry_ref[...]
    before = []
    for b in range(nb):
        before.append(_dot(tri_ref[...], sel[b * TOK_BLK:(b + 1) * TOK_BLK, :]) + offset)
        offset = offset + cnt[b:b + 1, :]
    carry_ref[...] = offset
    before = jnp.concatenate(before, axis=0)
    rank1 = jnp.where(first, before, 0.0).sum(axis=-1, keepdims=True)
    rank2 = jnp.where(second, before, 0.0).sum(axis=-1, keepdims=True)
    fields = (i1.astype(F32) * region_rows + rank1, i2.astype(F32) * region_rows + rank2,
              1.0 / denom, e2 / denom)
    info = jnp.zeros((tr, LANES), F32)
    for n, f in enumerate(fields):
        info = jnp.where(e == n, f, info)
    info_ref[...] = info[:, :len(fields)]
    info_t = info.T
    for n in range(len(fields)):
        for b in range(nb):
            rows_ref[n, b:b + 1, :] = info_t[n:n + 1, b * TOK_BLK:(b + 1) * TOK_BLK]


def _router(x, o, w_out, gain, w_router, tr):
    t = x.shape[0]
    nb = tr // TOK_BLK
    pad = ((0, 0), (0, LANES - N_EXPERTS))
    w_hi = w_router.astype(BF16)
    w_lo = (w_router - w_hi.astype(F32)).astype(BF16)
    w_both = jnp.concatenate([jnp.pad(w_hi, pad), jnp.pad(w_lo, pad)], axis=1)
    row = lambda i: (i, 0)
    fixed = lambda i: (0, 0)
    return pl.pallas_call(
        functools.partial(_router_kernel, region_rows=float(t)),
        grid=(t // tr,),
        in_specs=[
            pl.BlockSpec((tr, D_MODEL), row),
            pl.BlockSpec((tr, D_MODEL), row),
            pl.BlockSpec((D_MODEL, D_MODEL), fixed),
            pl.BlockSpec((1, D_MODEL), fixed),
            pl.BlockSpec((D_MODEL, 2 * LANES), fixed),
        ],
        out_specs=[
            pl.BlockSpec((tr, D_MODEL), row),
            pl.BlockSpec((tr, D_MODEL), row),
            pl.BlockSpec((tr, 4), row),
            pl.BlockSpec((4, nb, TOK_BLK), lambda i: (0, i, 0)),
            pl.BlockSpec((nb, LANES), row),
        ],
        out_shape=[
            jax.ShapeDtypeStruct((t, D_MODEL), F32),
            jax.ShapeDtypeStruct((t, D_MODEL), BF16),
            jax.ShapeDtypeStruct((t, 4), F32),
            jax.ShapeDtypeStruct((4, t // TOK_BLK, TOK_BLK), F32),
            jax.ShapeDtypeStruct((t // TOK_BLK, LANES), F32),
        ],
        scratch_shapes=[pltpu.VMEM((1, LANES), F32), pltpu.VMEM((TOK_BLK, TOK_BLK), BF16)],
        compiler_params=_params(("arbitrary",), 56 * 1024 * 1024),
        name="moe_router",
    )(x, o, w_out, gain, w_both)


def _moe_plan(cnt, t):
    e_ids = jnp.arange(N_EXPERTS, dtype=I32)
    by_expert = lambda table, idx: jnp.sum(jnp.where(idx[..., None] == e_ids, table, 0), axis=-1)
    cnt_blk = cnt[:, :N_EXPERTS].astype(I32)
    cum_blk = jnp.cumsum(cnt_blk, axis=0)
    counts = cum_blk[-1]
    tiles_e = (counts + MOE_TM - 1) // MOE_TM
    tile_end = jnp.cumsum(tiles_e)
    tile_off = tile_end - tiles_e
    nt = TOP_K * t // MOE_TM + N_EXPERTS
    r = jnp.arange(nt, dtype=I32)
    valid = r < tile_end[-1]
    te = jnp.minimum(jnp.sum((r[:, None] >= tile_end[None, :]).astype(I32), axis=1), N_EXPERTS - 1)
    local = r - by_expert(tile_off, te)
    last_e = jnp.max(jnp.where(tiles_e > 0, e_ids, 0))
    tile_expert = jnp.where(valid, te, last_e)
    shift = e_ids * t - tile_off * MOE_TM

    per = MOE_TM // MOE_TG
    rg = jnp.arange(nt * per, dtype=I32)
    eg = jnp.repeat(te, per)
    k0 = jnp.repeat(local, per) * MOE_TM + (rg % per) * MOE_TG
    k1 = jnp.minimum(k0 + MOE_TG, by_expert(counts, eg))
    has = jnp.repeat(valid, per) & (k1 > k0)
    cum_e = by_expert(cum_blk[None, :, :], eg[:, None])
    s_lo = jnp.sum((cum_e <= k0[:, None]).astype(I32), axis=1)
    s_hi = jnp.sum((cum_e <= (k1 - 1)[:, None]).astype(I32), axis=1)
    g_nwin = jnp.where(has, (s_hi - s_lo) // GATHER_WB + 1, 0)
    g_slo = jnp.where(has, s_lo, 0)
    g_code = eg * t + k0

    nts = t // COMB_TS
    cnt_ts = cnt_blk.reshape(nts, COMB_TS // TOK_BLK, N_EXPERTS).sum(axis=1)
    seg_lo = (tile_off * MOE_TM)[None, :] + jnp.cumsum(cnt_ts, axis=0) - cnt_ts
    seg_hi = seg_lo + cnt_ts
    ws0 = (seg_lo // BF16_ROWS) * BF16_ROWS
    nw = jnp.where(cnt_ts > 0, (seg_hi - ws0 + COMB_WC - 1) // COMB_WC, 0)
    nw_end = jnp.cumsum(nw, axis=1)
    k = jnp.arange(COMB_MAXW, dtype=I32)
    ek = jnp.minimum(jnp.sum((k[None, :, None] >= nw_end[:, None, :]).astype(I32), axis=2), N_EXPERTS - 1)
    pick = lambda a: by_expert(a[:, None, :], ek)
    live = k[None, :] < nw_end[:, -1:]
    nominal = pick(ws0) + (k[None, :] - pick(nw_end - nw)) * COMB_WC
    shift_k = by_expert(shift, ek)
    c_start = jnp.minimum(nominal, by_expert(tile_end * MOE_TM, ek) - COMB_WC)
    c_off = shift_k + c_start
    c_lo = jnp.maximum(pick(seg_lo), nominal) + shift_k
    c_hi = jnp.minimum(pick(seg_hi), nominal + COMB_WC) + shift_k
    flat = lambda a: jnp.where(live, a, 0).reshape(-1)
    return dict(
        tile_expert=tile_expert, tile_valid=valid.astype(I32),
        g_code=g_code, g_slo=g_slo, g_nwin=g_nwin,
        c_start=flat(c_start), c_off=flat(c_off), c_lo=flat(c_lo), c_hi=flat(c_hi), c_nwin=nw_end[:, -1])


def _moe_gather_kernel(code_ref, slo_ref, nwin_ref, rows_ref, h_ref, hs_ref, gs_ref):
    n_blk = rows_ref.shape[1]
    for sub in range(hs_ref.shape[0] // MOE_TG):
        _gather_subtile(pl.program_id(0) * (hs_ref.shape[0] // MOE_TG) + sub,
                        slice(sub * MOE_TG, (sub + 1) * MOE_TG),
                        code_ref, slo_ref, nwin_ref, rows_ref, h_ref, hs_ref, gs_ref, n_blk)


def _gather_subtile(r, out_rows, code_ref, slo_ref, nwin_ref, rows_ref, h_ref, hs_ref, gs_ref, n_blk):
    row_id = (code_ref[r] + lax.broadcasted_iota(I32, (MOE_TG, TOK_BLK), 0)).astype(F32)

    def window(k, first):
        nominal = slo_ref[r] + k * GATHER_WB
        sb = jnp.minimum(nominal, n_blk - GATHER_WB)
        fresh = sb + lax.broadcasted_iota(I32, (GATHER_WB, TOK_BLK), 0) >= nominal
        p1 = jnp.where(fresh, rows_ref[0, pl.ds(sb, GATHER_WB), :], -1.0)
        p2 = jnp.where(fresh, rows_ref[1, pl.ds(sb, GATHER_WB), :], -1.0)
        w1 = rows_ref[2, pl.ds(sb, GATHER_WB), :]
        w2 = rows_ref[3, pl.ds(sb, GATHER_WB), :]
        pieces = []
        g = jnp.zeros((MOE_TG, TOK_BLK), F32)
        for a in range(GATHER_WB):
            m1 = p1[a:a + 1, :] == row_id
            m2 = p2[a:a + 1, :] == row_id
            pieces.append(jnp.where(m1 | m2, 1.0, 0.0).astype(BF16))
            g = g + jnp.where(m1, w1[a:a + 1, :], 0.0) + jnp.where(m2, w2[a:a + 1, :], 0.0)
        onehot = jnp.concatenate(pieces, axis=1)
        tok0 = pl.multiple_of(sb * TOK_BLK, TOK_BLK)
        rows = _dot(onehot, h_ref[pl.ds(tok0, GATHER_WB * TOK_BLK), :]).astype(BF16)
        weight = g.sum(axis=-1, keepdims=True)
        if first:
            hs_ref[out_rows, :] = rows
            gs_ref[out_rows, :] = weight
        else:
            hs_ref[out_rows, :] += rows
            gs_ref[out_rows, :] += weight

    window(0, True)

    def more(k, carry):
        window(k, False)
        return carry

    lax.fori_loop(1, nwin_ref[r], more, 0)


def _moe_gather(plan, h, rows):
    t = h.shape[0]
    n_rows = plan["g_code"].shape[0] * MOE_TG
    n_blk = t // TOK_BLK
    grid_spec = pltpu.PrefetchScalarGridSpec(
        num_scalar_prefetch=3,
        grid=(n_rows // GATHER_ROWS,),
        in_specs=[
            pl.BlockSpec((4, n_blk, TOK_BLK), lambda r, *_: (0, 0, 0), pipeline_mode=pl.Buffered(1)),
            pl.BlockSpec((t, D_MODEL), lambda r, *_: (0, 0), pipeline_mode=pl.Buffered(1)),
        ],
        out_specs=[
            pl.BlockSpec((GATHER_ROWS, D_MODEL), lambda r, *_: (r, 0)),
            pl.BlockSpec((GATHER_ROWS, 1), lambda r, *_: (r, 0)),
        ],
    )
    return pl.pallas_call(
        _moe_gather_kernel,
        grid_spec=grid_spec,
        out_shape=[
            jax.ShapeDtypeStruct((n_rows, D_MODEL), BF16),
            jax.ShapeDtypeStruct((n_rows, 1), F32),
        ],
        compiler_params=_params(("parallel",), 52 * 1024 * 1024),
        name="moe_gather",
    )(plan["g_code"], plan["g_slo"], plan["g_nwin"], rows, h)


def _moe_expert_kernel(te_ref, tv_ref, hs_ref, gs_ref, wg_ref, wu_ref, wd_ref, y_ref, acc_ref, *, nj, pieces):
    r = pl.program_id(0)
    j = pl.program_id(1)
    valid = tv_ref[r] > 0
    part = lambda: _swiglu_pieces(hs_ref[...], wg_ref, wu_ref, wd_ref, pieces)

    if nj == 1:
        @pl.when(valid)
        def _():
            y_ref[...] = (part() * gs_ref[...]).astype(BF16)
    else:
        @pl.when(valid & (j == 0))
        def _():
            acc_ref[...] = part()

        if nj > 2:
            @pl.when(valid & (j > 0) & (j < nj - 1))
            def _():
                acc_ref[...] += part()

        @pl.when(valid & (j == nj - 1))
        def _():
            y_ref[...] = ((acc_ref[...] + part()) * gs_ref[...]).astype(BF16)

    @pl.when(jnp.logical_not(valid) & (j == nj - 1))
    def _():
        y_ref[...] = jnp.zeros_like(y_ref)


def _moe_experts(plan, hs, gs, w_gate, w_up, w_down, pieces):
    tf = sum(pieces)
    n_rows, d = hs.shape
    ff = w_gate.shape[2]
    nj = ff // tf
    jj = lambda r, j, tv: jnp.where(tv[r] > 0, j, nj - 1)
    grid_spec = pltpu.PrefetchScalarGridSpec(
        num_scalar_prefetch=2,
        grid=(n_rows // MOE_TM, nj),
        in_specs=[
            pl.BlockSpec((MOE_TM, d), lambda r, j, te, tv: (r, 0)),
            pl.BlockSpec((MOE_TM, 1), lambda r, j, te, tv: (r, 0)),
            pl.BlockSpec((None, d, tf), lambda r, j, te, tv: (te[r], 0, jj(r, j, tv))),
            pl.BlockSpec((None, d, tf), lambda r, j, te, tv: (te[r], 0, jj(r, j, tv))),
            pl.BlockSpec((None, tf, d), lambda r, j, te, tv: (te[r], jj(r, j, tv), 0)),
        ],
        out_specs=pl.BlockSpec((MOE_TM, d), lambda r, j, te, tv: (r, 0)),
        scratch_shapes=[pltpu.VMEM((MOE_TM if nj > 1 else 8, d), F32)],
    )
    return pl.pallas_call(
        functools.partial(_moe_expert_kernel, nj=nj, pieces=pieces),
        grid_spec=grid_spec,
        out_shape=jax.ShapeDtypeStruct((n_rows, d), BF16),
        compiler_params=_params(("arbitrary", "arbitrary"), 56 * 1024 * 1024),
        name="moe_experts",
    )(plan["tile_expert"], plan["tile_valid"], hs, gs, w_gate, w_up, w_down)


def _moe_combine_kernel(start_ref, off_ref, lo_ref, hi_ref, nwin_ref, x_ref, info_ref, y_hbm, o_ref,
                        ybuf, ybuf_more, sem, sem_more):
    i = pl.program_id(0)
    n_tiles = pl.num_programs(0)

    def window_copy(idx, dst, dsem):
        start = pl.multiple_of(start_ref[idx], BF16_ROWS)
        return pltpu.make_async_copy(y_hbm.at[pl.ds(start, COMB_WC), :], dst, dsem)

    def first_group(tile, wait):
        buf = tile % COMB_AHEAD_BUFS
        for s in range(COMB_GROUP):
            @pl.when(s < nwin_ref[tile])
            def _():
                cp = window_copy(tile * COMB_MAXW + s, ybuf.at[buf, pl.ds(s * COMB_WC, COMB_WC), :],
                                 sem.at[buf, s])
                cp.wait() if wait else cp.start()

    @pl.when(i == 0)
    def _():
        ybuf[...] = jnp.zeros_like(ybuf)
        ybuf_more[...] = jnp.zeros_like(ybuf_more)
        for ahead in range(COMB_AHEAD_BUFS - 1):
            @pl.when(ahead < n_tiles)
            def _():
                first_group(ahead, wait=False)

    @pl.when(i + COMB_AHEAD_BUFS - 1 < n_tiles)
    def _():
        first_group(i + COMB_AHEAD_BUFS - 1, wait=False)

    first_group(i, wait=True)

    code1 = jnp.broadcast_to(info_ref[:, 0:1].astype(I32), (COMB_TS, COMB_WC))
    code2 = jnp.broadcast_to(info_ref[:, 1:2].astype(I32), (COMB_TS, COMB_WC))
    lane = lax.broadcasted_iota(I32, (COMB_TS, COMB_WC), 1)
    lane_row = lax.broadcasted_iota(I32, (1, COMB_WC), 1)

    def onehot_group(g):
        pieces = []
        for s in range(COMB_GROUP):
            idx = i * COMB_MAXW + g * COMB_GROUP + s
            off = off_ref[idx]
            in_span = (lane_row >= lo_ref[idx] - off) & (lane_row < hi_ref[idx] - off)
            target = lane + off
            hit = ((code1 == target) | (code2 == target)) & in_span
            pieces.append(jnp.where(hit, 1.0, 0.0).astype(BF16))
        return jnp.concatenate(pieces, axis=1)

    o_ref[...] = x_ref[...] + _dot(onehot_group(0), ybuf[i % COMB_AHEAD_BUFS])

    n = nwin_ref[i]

    def more(g, carry):
        for wait in (False, True):
            for s in range(COMB_GROUP):
                @pl.when(g * COMB_GROUP + s < n)
                def _():
                    cp = window_copy(i * COMB_MAXW + g * COMB_GROUP + s,
                                     ybuf_more.at[pl.ds(s * COMB_WC, COMB_WC), :], sem_more.at[s])
                    cp.wait() if wait else cp.start()
        o_ref[...] += _dot(onehot_group(g), ybuf_more[...])
        return carry

    lax.fori_loop(1, (n + COMB_GROUP - 1) // COMB_GROUP, more, 0)


def _moe_combine(plan, x, info, y):
    t, d = x.shape
    width = COMB_GROUP * COMB_WC
    grid_spec = pltpu.PrefetchScalarGridSpec(
        num_scalar_prefetch=5,
        grid=(t // COMB_TS,),
        in_specs=[
            pl.BlockSpec((COMB_TS, d), lambda i, *_: (i, 0)),
            pl.BlockSpec((COMB_TS, 4), lambda i, *_: (i, 0)),
            pl.BlockSpec(memory_space=pl.ANY),
        ],
        out_specs=pl.BlockSpec((COMB_TS, d), lambda i, *_: (i, 0)),
        scratch_shapes=[
            pltpu.VMEM((COMB_AHEAD_BUFS, width, d), BF16),
            pltpu.VMEM((width, d), BF16),
            pltpu.SemaphoreType.DMA((COMB_AHEAD_BUFS, COMB_GROUP)),
            pltpu.SemaphoreType.DMA((COMB_GROUP,)),
        ],
    )
    return pl.pallas_call(
        _moe_combine_kernel,
        grid_spec=grid_spec,
        out_shape=jax.ShapeDtypeStruct((t, d), F32),
        compiler_params=_params(("arbitrary",)),
        name="moe_combine",
    )(plan["c_start"], plan["c_off"], plan["c_lo"], plan["c_hi"], plan["c_nwin"], x, info, y)


def kernel(x, norm_mix, norm_ffn, gla_w_in, gla_w_gate2, gla_b_gate, gla_out_gain, gla_w_out,
           moba_w_qkv, moba_q_gain, moba_k_gain, moba_w_out, ffn_w_gate, ffn_w_up, ffn_w_down,
           moe_w_router, moe_w_gate, moe_w_up, moe_w_down):
    batch, seq, d = x.shape
    t = batch * seq
    xt = x.reshape(t, d)

    (q, k, v, g, la), (ffn_wg, ffn_wu, ffn_wd, gla_wo, moba_wqkv, moba_wo) = _gla_inproj(
        xt, norm_mix[0:1], gla_w_in, gla_w_gate2[0], gla_b_gate[0:1],
        ride_along=(ffn_w_gate, ffn_w_up, ffn_w_down, gla_w_out, moba_w_qkv, moba_w_out), tm=512)
    o, _ = _gla(q, k, v, g, la, gla_out_gain[0:1], (), batch, seq, blk=256)
    xt = _ffn_residual(xt, o, gla_wo[0], norm_ffn[0:1], ffn_wg[0], ffn_wu[0], ffn_wd[0],
                       tm=512, pieces=MOE_PIECES + MOE_PIECES)

    q, k, v = _moba_qkv(xt, norm_mix[1:2], moba_wqkv[0], moba_q_gain[0:1], moba_k_gain[0:1], tm=512)
    o, (moe_wg, moe_wu, moe_wd) = _moba(q, k, v, (moe_w_gate[0], moe_w_up[0], moe_w_down[0]), batch, seq)
    xt, h, info, rows, cnt = _router(xt, o, moba_wo[0], norm_ffn[1:2], moe_w_router[0], tr=1024)
    plan = _moe_plan(cnt, t)
    hs, gs = _moe_gather(plan, h, rows)
    y = _moe_experts(plan, hs, gs, moe_wg, moe_wu, moe_wd, pieces=MOE_PIECES + MOE_PIECES)
    xt = _moe_combine(plan, xt, info, y)
    return xt.reshape(batch, seq, d)
```

```python
import functools

import jax
import jax.numpy as jnp
from jax import lax
from jax.experimental import pallas as pl
from jax.experimental.pallas import tpu as pltpu

F32 = jnp.float32
BF16 = jnp.bfloat16
I32 = jnp.int32

EPS = 1e-6
NEG_INF = -1e30

D_MODEL = 1024
GLA_HEADS = 4
GLA_DK = 512
GLA_DV = 1024
GLA_HK = 128
GLA_HV = 256
GLA_RANK = 16
GLA_NORMALIZER = 16.0
GLA_CHUNK = 64

MOBA_HEADS = 8
MOBA_HD = 128
MOBA_BLOCK = 256
MOBA_TOPK = 3
MOBA_Q_SCALE = MOBA_HD ** -0.5 * 1.4426950408889634

N_EXPERTS = 8
TOP_K = 2

LANES = 128
BF16_ROWS = 16

MOE_PIECES = (512, 512, 768)
MOE_TM = 256
MOE_TG = 128
GATHER_ROWS = 512
TOK_BLK = LANES
GATHER_WB = 6
COMB_TS = 256
COMB_WC = 128
COMB_GROUP = 8
COMB_AHEAD_BUFS = 3
COMB_MAXW = 16


def _params(semantics, vmem_bytes=None):
    return pltpu.CompilerParams(dimension_semantics=semantics, vmem_limit_bytes=vmem_bytes)


def _rms(x, gain):
    y = x * lax.rsqrt(jnp.mean(x * x, axis=-1, keepdims=True) + EPS)
    return y * gain


def _dot(a, b):
    return jnp.dot(a, b, preferred_element_type=F32)


def _dot_nt(a, b):
    return lax.dot_general(a, b, (((1,), (1,)), ((), ())), preferred_element_type=F32)


def _dot_tn(a, b):
    return lax.dot_general(a, b, (((0,), (0,)), ((), ())), preferred_element_type=F32)


def _split(a):
    hi = a.astype(BF16)
    lo = (a - hi.astype(F32)).astype(BF16)
    return hi, lo


def _log_sigmoid(z):
    return jnp.minimum(z, 0.0) - jnp.log(1.0 + jnp.exp(-jnp.abs(z)))


def _silu(z):
    return z / (1.0 + jnp.exp(-z))


def _swiglu_pieces(h, wg_ref, wu_ref, wd_ref, widths):
    bounds = [sum(widths[:n]) for n in range(len(widths) + 1)]
    assert bounds[-1] == wg_ref.shape[1], (widths, wg_ref.shape)
    gate_up = lambda c: (_dot(h, wg_ref[:, bounds[c]:bounds[c + 1]]), _dot(h, wu_ref[:, bounds[c]:bounds[c + 1]]))
    out = None
    upcoming = gate_up(0)
    for c in range(len(widths)):
        g, u = upcoming
        if c + 1 < len(widths):
            upcoming = gate_up(c + 1)
        d = _dot((_silu(g) * u).astype(BF16), wd_ref[bounds[c]:bounds[c + 1], :])
        out = d if out is None else out + d
    return out


def _ride_along_specs(arrays, n_steps, step):
    specs, shapes = [], []
    for w in arrays:
        n_e, rows, cols = w.shape
        per_e = n_steps // n_e
        if rows % (per_e * BF16_ROWS) == 0:
            blk = (None, rows // per_e, cols)
            imap = lambda *ids, per_e=per_e: (step(*ids) // per_e, step(*ids) % per_e, 0)
        else:
            assert cols % (per_e * LANES) == 0, (w.shape, n_steps)
            blk = (None, rows, cols // per_e)
            imap = lambda *ids, per_e=per_e: (step(*ids) // per_e, 0, step(*ids) % per_e)
        specs.append(pl.BlockSpec(blk, imap))
        shapes.append(jax.ShapeDtypeStruct(w.shape, BF16))
    return specs, shapes


def _with_ride_along(body, n_in, n_out, n_cast):
    def kernel(*refs):
        ins, refs = refs[:n_in], refs[n_in:]
        cast_in, refs = refs[:n_cast], refs[n_cast:]
        outs, refs = refs[:n_out], refs[n_out:]
        cast_out, scratch = refs[:n_cast], refs[n_cast:]
        for src, dst in zip(cast_in, cast_out):
            dst[...] = src[...].astype(BF16)
        body(*ins, *outs, *scratch)
    return kernel


def _gla_inproj_kernel(x_next_ref, x_first_ref, gain_ref, w_ref, wg2_ref, b_ref,
                       q_ref, k_ref, v_ref, g_ref, la_ref, wb_ref, h_ref):
    i = pl.program_id(0)

    @pl.when(i == 0)
    def _():
        wb_ref[...] = w_ref[...].astype(BF16)
        h_ref[0] = _rms(x_first_ref[...], gain_ref[...]).astype(BF16)

    n_main = 2 * GLA_DK + 2 * GLA_DV

    def step(cur, nxt):
        h = h_ref[cur]
        h_ref[nxt] = _rms(x_next_ref[...], gain_ref[...]).astype(BF16)
        a_lr = _dot(h, wb_ref[:, n_main:])
        a_hi, a_lo = _split(a_lr)
        w_hi, w_lo = _split(wg2_ref[...])
        z = _dot(jnp.concatenate([a_hi, a_lo, a_hi], axis=1), jnp.concatenate([w_hi, w_hi, w_lo], axis=0))
        la_ref[...] = _log_sigmoid(z + b_ref[...]) / GLA_NORMALIZER
        q_ref[...] = _dot(h, wb_ref[:, 0:GLA_DK]).astype(BF16)
        k_ref[...] = _dot(h, wb_ref[:, GLA_DK:2 * GLA_DK]).astype(BF16)
        v_ref[...] = _dot(h, wb_ref[:, 2 * GLA_DK:2 * GLA_DK + GLA_DV]).astype(BF16)
        g_ref[...] = _dot(h, wb_ref[:, 2 * GLA_DK + GLA_DV:n_main]).astype(BF16)

    @pl.when(i % 2 == 0)
    def _():
        step(0, 1)

    @pl.when(i % 2 == 1)
    def _():
        step(1, 0)


def _gla_inproj(x, gain, w_in, w_gate2, b_gate, ride_along, tm):
    t = x.shape[0]
    n_steps = t // tm
    row = lambda i: (i, 0)
    fixed = lambda i: (0, 0)
    cast_specs, cast_shapes = _ride_along_specs(ride_along, n_steps, lambda i: i)
    outs = pl.pallas_call(
        _with_ride_along(_gla_inproj_kernel, 6, 5, len(ride_along)),
        grid=(n_steps,),
        in_specs=[
            pl.BlockSpec((tm, D_MODEL), lambda i: (jnp.minimum(i + 1, n_steps - 1), 0)),
            pl.BlockSpec((tm, D_MODEL), fixed),
            pl.BlockSpec((1, D_MODEL), fixed),
            pl.BlockSpec((None,) + w_in.shape[1:], lambda i: (0, 0, 0), pipeline_mode=pl.Buffered(1)),
            pl.BlockSpec((GLA_RANK, GLA_DK), fixed),
            pl.BlockSpec((1, GLA_DK), fixed),
        ] + cast_specs,
        out_specs=[
            pl.BlockSpec((tm, GLA_DK), row),
            pl.BlockSpec((tm, GLA_DK), row),
            pl.BlockSpec((tm, GLA_DV), row),
            pl.BlockSpec((tm, GLA_DV), row),
            pl.BlockSpec((tm, GLA_DK), row),
        ] + cast_specs,
        out_shape=[
            jax.ShapeDtypeStruct((t, GLA_DK), BF16),
            jax.ShapeDtypeStruct((t, GLA_DK), BF16),
            jax.ShapeDtypeStruct((t, GLA_DV), BF16),
            jax.ShapeDtypeStruct((t, GLA_DV), BF16),
            jax.ShapeDtypeStruct((t, GLA_DK), F32),
        ] + cast_shapes,
        scratch_shapes=[pltpu.VMEM(w_in.shape[1:], BF16), pltpu.VMEM((2, tm, D_MODEL), BF16)],
        compiler_params=_params(("arbitrary",), 48 * 1024 * 1024),
        name="gla_inproj",
    )(x, x, gain, w_in, w_gate2, b_gate, *ride_along)
    return outs[:5], outs[5:]


def _gla_kernel(q_ref, k_ref, v_ref, g_ref, la_ref, og_ref, o_ref, st_ref, *, n_chunks):
    @pl.when(pl.program_id(1) == 0)
    def _():
        st_ref[...] = jnp.zeros_like(st_ref)

    n_rows = q_ref.shape[0]
    blk = n_chunks * GLA_CHUNK
    r = lax.broadcasted_iota(I32, (blk, blk), 0)
    c = lax.broadcasted_iota(I32, (blk, blk), 1)
    causal = (r >= c) & (r // GLA_CHUNK == c // GLA_CHUNK)
    tril = causal.astype(BF16)
    og = og_ref[...]
    chunk_rows = [slice(ci * GLA_CHUNK, (ci + 1) * GLA_CHUNK) for ci in range(n_chunks)]
    hk = [slice(h * GLA_HK, (h + 1) * GLA_HK) for h in range(GLA_HEADS)]
    hv = [slice(h * GLA_HV, (h + 1) * GLA_HV) for h in range(GLA_HEADS)]

    q_dec, k_inv, k_end, chunk_decay = [], [], [], []
    for b in range(n_rows):
        la = la_ref[b]
        la_hi, la_mid = _split(la)
        la_lo = (la - la_hi.astype(F32) - la_mid.astype(F32)).astype(BF16)
        cum = _dot(tril, la_hi) + _dot(tril, la_mid) + _dot(tril, la_lo)
        ends = [cum[rows.stop - 1:rows.stop, :] for rows in chunk_rows]
        last = jnp.concatenate([jnp.broadcast_to(e, (GLA_CHUNK, GLA_DK)) for e in ends], axis=0)
        q = q_ref[b].astype(F32) * (GLA_HK ** -0.5)
        k = k_ref[b].astype(F32)
        q_dec.append((q * jnp.exp(cum)).astype(BF16))
        k_inv.append((k * jnp.exp(-cum)).astype(BF16))
        k_end.append((k * jnp.exp(last - cum)).astype(BF16))
        chunk_decay.append([jnp.exp(e) for e in ends])

    streams = [(b, h) for b in range(n_rows) for h in range(GLA_HEADS)]
    scores = [jnp.where(causal, _dot_nt(q_dec[b][:, hk[h]], k_inv[b][:, hk[h]]), 0.0).astype(BF16)
              for b, h in streams]
    state_in = [[_dot_tn(v_ref[b, rows, hv[h]], k_end[b][rows, hk[h]]) for rows in chunk_rows]
                for b, h in streams]
    intra = [_dot(scores[n], v_ref[b, :, hv[h]]) for n, (b, h) in enumerate(streams)]
    for n, (b, h) in enumerate(streams):
        st = st_ref[b, h]
        inter = []
        for ci, rows in enumerate(chunk_rows):
            inter.append(_dot_nt(q_dec[b][rows, hk[h]], st.astype(BF16)))
            st = st * chunk_decay[b][ci][:, hk[h]] + state_in[n][ci]
        st_ref[b, h] = st
        o = intra[n] + jnp.concatenate(inter, axis=0)
        o = o * lax.rsqrt(jnp.mean(o * o, axis=-1, keepdims=True) + EPS) * og
        o = o * _silu(g_ref[b, :, hv[h]].astype(F32))
        o_ref[b, :, hv[h]] = o.astype(BF16)


def _gla(q, k, v, g, la, out_gain, ride_along, batch, seq, blk, rows_per_step):
    nb = seq // blk
    nbt = batch // rows_per_step
    per_row = lambda a: a.reshape(batch, seq, a.shape[-1])
    spec = lambda width: pl.BlockSpec((rows_per_step, blk, width), lambda b, i: (b, i, 0))
    cast_specs, cast_shapes = _ride_along_specs(ride_along, nbt * nb, lambda b, i: b * nb + i)
    outs = pl.pallas_call(
        _with_ride_along(functools.partial(_gla_kernel, n_chunks=blk // GLA_CHUNK), 6, 1, len(ride_along)),
        grid=(nbt, nb),
        in_specs=[spec(GLA_DK), spec(GLA_DK), spec(GLA_DV), spec(GLA_DV), spec(GLA_DK),
                  pl.BlockSpec((1, GLA_HV), lambda b, i: (0, 0))] + cast_specs,
        out_specs=[spec(GLA_DV)] + cast_specs,
        out_shape=[jax.ShapeDtypeStruct((batch, seq, GLA_DV), BF16)] + cast_shapes,
        scratch_shapes=[pltpu.VMEM((rows_per_step, GLA_HEADS, GLA_HV, GLA_HK), F32)],
        compiler_params=_params(("arbitrary", "arbitrary"), 48 * 1024 * 1024),
        name="gla_core",
    )(per_row(q), per_row(k), per_row(v), per_row(g), per_row(la), out_gain, *ride_along)
    return outs[0].reshape(batch * seq, GLA_DV), outs[1:]


def _ffn_kernel(x_ref, o_ref, wo_ref, gain_ref, wg_ref, wu_ref, wd_ref, y_ref, *, pieces):
    x1 = x_ref[...] + _dot(o_ref[...], wo_ref[...])
    h = _rms(x1, gain_ref[...]).astype(BF16)
    y_ref[...] = x1 + _swiglu_pieces(h, wg_ref, wu_ref, wd_ref, pieces)


def _ffn_residual(x, o, w_out, gain, w_gate, w_up, w_down, tm, pieces):
    t, d = x.shape
    row = lambda i: (i, 0)
    resident = lambda a: pl.BlockSpec(a.shape, lambda i: (0, 0), pipeline_mode=pl.Buffered(1))
    return pl.pallas_call(
        functools.partial(_ffn_kernel, pieces=pieces),
        grid=(t // tm,),
        in_specs=[pl.BlockSpec((tm, d), row), pl.BlockSpec((tm, d), row), resident(w_out), resident(gain),
                  resident(w_gate), resident(w_up), resident(w_down)],
        out_specs=pl.BlockSpec((tm, d), row),
        out_shape=jax.ShapeDtypeStruct((t, d), F32),
        compiler_params=_params(("parallel",), 56 * 1024 * 1024),
        name="ffn_swiglu",
    )(x, o, w_out, gain, w_gate, w_up, w_down)


def _moba_qkv_kernel(x_ref, gain_ref, w_ref, qg_ref, kg_ref, q_ref, k_ref, v_ref):
    h = _rms(x_ref[...], gain_ref[...]).astype(BF16)
    d = MOBA_HEADS * MOBA_HD
    qg = qg_ref[...]
    kg = kg_ref[...]
    q = _dot(h, w_ref[:, 0:d])
    k = _dot(h, w_ref[:, d:2 * d])
    for hd in range(MOBA_HEADS):
        cols = slice(hd * MOBA_HD, (hd + 1) * MOBA_HD)
        q_ref[:, cols] = (_rms(q[:, cols], qg) * MOBA_Q_SCALE).astype(BF16)
        k_ref[:, cols] = _rms(k[:, cols], kg).astype(BF16)
    v_ref[...] = _dot(h, w_ref[:, 2 * d:]).astype(BF16)


def _moba_qkv(x, gain, w, q_gain, k_gain, tm):
    t = x.shape[0]
    d = MOBA_HEADS * MOBA_HD
    row = lambda i: (i, 0)
    fixed = lambda i: (0, 0)
    return pl.pallas_call(
        _moba_qkv_kernel,
        grid=(t // tm,),
        in_specs=[
            pl.BlockSpec((tm, D_MODEL), row),
            pl.BlockSpec((1, D_MODEL), fixed),
            pl.BlockSpec((D_MODEL, 3 * d), fixed),
            pl.BlockSpec((1, MOBA_HD), fixed),
            pl.BlockSpec((1, MOBA_HD), fixed),
        ],
        out_specs=[pl.BlockSpec((tm, d), row)] * 3,
        out_shape=[jax.ShapeDtypeStruct((t, d), BF16)] * 3,
        compiler_params=_params(("parallel",), 48 * 1024 * 1024),
        name="moba_qkv",
    )(x, gain, w, q_gain, k_gain)


def _reduce_rows(x, op, stop=8):
    while x.shape[0] > stop:
        half = x.shape[0] // 2
        x = op(x[:half], x[half:])
    return x


def _moba_kernel(q_ref, k_ref, v_ref, o_ref, *, n_blocks):
    seq = n_blocks * MOBA_BLOCK
    n_pad = 16
    ones_rows = jnp.where(lax.broadcasted_iota(I32, (BF16_ROWS, seq), 0) == 0, 1.0, 0.0)
    v_t = jnp.concatenate([v_ref[...].astype(F32).T, ones_rows], axis=0).astype(BF16)
    blk_of_key = lax.broadcasted_iota(I32, (n_pad, seq), 1) // MOBA_BLOCK
    blk_row = lax.broadcasted_iota(I32, (n_pad, seq), 0)
    indicator = jnp.where(blk_of_key == blk_row, 1.0 / MOBA_BLOCK, 0.0).astype(BF16)
    k_mean = _dot(indicator, k_ref[...])
    km_hi, km_lo = _split(k_mean)

    blk_id = lax.broadcasted_iota(I32, (n_pad, MOBA_BLOCK), 0)
    key_pos = lax.broadcasted_iota(I32, (MOBA_BLOCK, MOBA_BLOCK), 0)
    query_pos = lax.broadcasted_iota(I32, (MOBA_BLOCK, MOBA_BLOCK), 1)
    causal = key_pos <= query_pos

    def masked_scores(i):
        rows = slice(i * MOBA_BLOCK, (i + 1) * MOBA_BLOCK)
        n_keys = (i + 1) * MOBA_BLOCK
        q = q_ref[rows, :]
        s = _dot_nt(k_ref[0:n_keys, :], q)
        selected = None
        if i > MOBA_TOPK:
            gate = _dot_nt(km_hi, q) + _dot_nt(km_lo, q)
            gate = jnp.where(blk_id < i, gate, NEG_INF)
            rank = jnp.zeros_like(gate)
            for jp in range(i):
                row = gate[jp:jp + 1, :]
                beats = (row > gate) | ((row == gate) & (jp < blk_id))
                rank = rank + beats.astype(F32)
            selected = jnp.where((rank < MOBA_TOPK) & (blk_id < i), 1.0, 0.0)
        pieces = []
        for j in range(i + 1):
            sj = s[j * MOBA_BLOCK:(j + 1) * MOBA_BLOCK, :]
            if j == i:
                sj = jnp.where(causal, sj, NEG_INF)
            elif selected is not None:
                sj = jnp.where(selected[j:j + 1, :] > 0.5, sj, NEG_INF)
            pieces.append(sj)
        return pieces

    def weighted_values(i, p):
        o_t = _dot(v_t[:, 0:(i + 1) * MOBA_BLOCK], p)
        rows = slice(i * MOBA_BLOCK, (i + 1) * MOBA_BLOCK)
        o_ref[rows, :] = (o_t[:MOBA_HD, :] / o_t[MOBA_HD:MOBA_HD + 1, :]).T.astype(BF16)

    upcoming = masked_scores(0)
    pending = None
    for i in range(n_blocks):
        pieces = upcoming
        if i + 1 < n_blocks:
            upcoming = masked_scores(i + 1)
        if pending is not None:
            weighted_values(i - 1, pending)
        m = _reduce_rows(functools.reduce(jnp.maximum, pieces), jnp.maximum).max(axis=0, keepdims=True)
        probs = [jnp.exp2(sj - m).astype(BF16) for sj in pieces]
        pending = probs[0] if len(probs) == 1 else jnp.concatenate(probs, axis=0)
    weighted_values(n_blocks - 1, pending)


def _moba(q, k, v, ride_along, batch, seq):
    t, d = q.shape
    spec = pl.BlockSpec((seq, MOBA_HD), lambda b, h: (b, h))
    cast_specs, cast_shapes = _ride_along_specs(ride_along, batch * MOBA_HEADS, lambda b, h: b * MOBA_HEADS + h)
    outs = pl.pallas_call(
        _with_ride_along(functools.partial(_moba_kernel, n_blocks=seq // MOBA_BLOCK), 3, 1, len(ride_along)),
        grid=(batch, MOBA_HEADS),
        in_specs=[spec, spec, spec] + cast_specs,
        out_specs=[spec] + cast_specs,
        out_shape=[jax.ShapeDtypeStruct((t, d), BF16)] + cast_shapes,
        compiler_params=_params(("arbitrary", "arbitrary"), 48 * 1024 * 1024),
        name="moba_attn",
    )(q, k, v, *ride_along)
    return outs[0], outs[1:]


def _router_kernel(x_ref, o_ref, wo_ref, gain_ref, w_ref, x2_ref, h_ref, info_ref, rows_ref, cnt_ref,
                   carry_ref, tri_ref,
                   *, region_rows):
    tr = x_ref.shape[0]

    @pl.when(pl.program_id(0) == 0)
    def _():
        carry_ref[...] = jnp.zeros_like(carry_ref)
        r = lax.broadcasted_iota(I32, (TOK_BLK, TOK_BLK), 0)
        c = lax.broadcasted_iota(I32, (TOK_BLK, TOK_BLK), 1)
        tri_ref[...] = (c < r).astype(BF16)

    x2 = x_ref[...] + _dot(o_ref[...], wo_ref[...])
    x2_ref[...] = x2
    h = _rms(x2, gain_ref[...])
    h_hi, h_lo = _split(h)
    h_ref[...] = h_hi
    both = _dot(h_hi, w_ref[...])
    logits = both[:, :LANES] + both[:, LANES:] + _dot(h_lo, w_ref[:, :LANES])
    e = lax.broadcasted_iota(I32, logits.shape, 1)
    logits = jnp.where(e < N_EXPERTS, logits, -jnp.inf)
    m1 = logits.max(axis=-1, keepdims=True)
    i1 = jnp.where(logits == m1, e, LANES).min(axis=-1, keepdims=True)
    first = e == i1
    rest = jnp.where(first, -jnp.inf, logits)
    m2 = rest.max(axis=-1, keepdims=True)
    i2 = jnp.where(rest == m2, e, LANES).min(axis=-1, keepdims=True)
    second = e == i2
    e2 = jnp.exp(m2 - m1)
    denom = 1.0 + e2
    sel = jnp.where(first | second, 1.0, 0.0).astype(BF16)
    nb = tr // TOK_BLK
    blk_row = lax.broadcasted_iota(I32, (nb, tr), 0)
    blk_tok = lax.broadcasted_iota(I32, (nb, tr), 1) // TOK_BLK
    cnt = _dot((blk_row == blk_tok).astype(BF16), sel)
    cnt_ref[...] = cnt
    offset = carry_ref[...]
    before = []
    for b in range(nb):
        before.append(_dot(tri_ref[...], sel[b * TOK_BLK:(b + 1) * TOK_BLK, :]) + offset)
        offset = offset + cnt[b:b + 1, :]
    carry_ref[...] = offset
    before = jnp.concatenate(before, axis=0)
    rank1 = jnp.where(first, before, 0.0).sum(axis=-1, keepdims=True)
    rank2 = jnp.where(second, before, 0.0).sum(axis=-1, keepdims=True)
    fields = (i1.astype(F32) * region_rows + rank1, i2.astype(F32) * region_rows + rank2,
              1.0 / denom, e2 / denom)
    info = jnp.zeros((tr, LANES), F32)
    for n, f in enumerate(fields):
        info = jnp.where(e == n, f, info)
    info_ref[...] = info[:, :len(fields)]
    info_t = info.T
    for n in range(len(fields)):
        for b in range(nb):
            rows_ref[n, b:b + 1, :] = info_t[n:n + 1, b * TOK_BLK:(b + 1) * TOK_BLK]


def _router(x, o, w_out, gain, w_router, tr):
    t = x.shape[0]
    nb = tr // TOK_BLK
    pad = ((0, 0), (0, LANES - N_EXPERTS))
    w_hi = w_router.astype(BF16)
    w_lo = (w_router - w_hi.astype(F32)).astype(BF16)
    w_both = jnp.concatenate([jnp.pad(w_hi, pad), jnp.pad(w_lo, pad)], axis=1)
    row = lambda i: (i, 0)
    fixed = lambda i: (0, 0)
    return pl.pallas_call(
        functools.partial(_router_kernel, region_rows=float(t)),
        grid=(t // tr,),
        in_specs=[
            pl.BlockSpec((tr, D_MODEL), row),
            pl.BlockSpec((tr, D_MODEL), row),
            pl.BlockSpec((D_MODEL, D_MODEL), fixed),
            pl.BlockSpec((1, D_MODEL), fixed),
            pl.BlockSpec((D_MODEL, 2 * LANES), fixed),
        ],
        out_specs=[
            pl.BlockSpec((tr, D_MODEL), row),
            pl.BlockSpec((tr, D_MODEL), row),
            pl.BlockSpec((tr, 4), row),
            pl.BlockSpec((4, nb, TOK_BLK), lambda i: (0, i, 0)),
            pl.BlockSpec((nb, LANES), row),
        ],
        out_shape=[
            jax.ShapeDtypeStruct((t, D_MODEL), F32),
            jax.ShapeDtypeStruct((t, D_MODEL), BF16),
            jax.ShapeDtypeStruct((t, 4), F32),
            jax.ShapeDtypeStruct((4, t // TOK_BLK, TOK_BLK), F32),
            jax.ShapeDtypeStruct((t // TOK_BLK, LANES), F32),
        ],
        scratch_shapes=[pltpu.VMEM((1, LANES), F32), pltpu.VMEM((TOK_BLK, TOK_BLK), BF16)],
        compiler_params=_params(("arbitrary",), 56 * 1024 * 1024),
        name="moe_router",
    )(x, o, w_out, gain, w_both)


def _moe_plan(cnt, t):
    e_ids = jnp.arange(N_EXPERTS, dtype=I32)
    by_expert = lambda table, idx: jnp.sum(jnp.where(idx[..., None] == e_ids, table, 0), axis=-1)
    cnt_blk = cnt[:, :N_EXPERTS].astype(I32)
    cum_blk = jnp.cumsum(cnt_blk, axis=0)
    counts = cum_blk[-1]
    tiles_e = (counts + MOE_TM - 1) // MOE_TM
    tile_end = jnp.cumsum(tiles_e)
    tile_off = tile_end - tiles_e
    nt = TOP_K * t // MOE_TM + N_EXPERTS
    r = jnp.arange(nt, dtype=I32)
    valid = r < tile_end[-1]
    te = jnp.minimum(jnp.sum((r[:, None] >= tile_end[None, :]).astype(I32), axis=1), N_EXPERTS - 1)
    local = r - by_expert(tile_off, te)
    last_e = jnp.max(jnp.where(tiles_e > 0, e_ids, 0))
    tile_expert = jnp.where(valid, te, last_e)
    shift = e_ids * t - tile_off * MOE_TM

    per = MOE_TM // MOE_TG
    rg = jnp.arange(nt * per, dtype=I32)
    eg = jnp.repeat(te, per)
    k0 = jnp.repeat(local, per) * MOE_TM + (rg % per) * MOE_TG
    k1 = jnp.minimum(k0 + MOE_TG, by_expert(counts, eg))
    has = jnp.repeat(valid, per) & (k1 > k0)
    cum_e = by_expert(cum_blk[None, :, :], eg[:, None])
    s_lo = jnp.sum((cum_e <= k0[:, None]).astype(I32), axis=1)
    s_hi = jnp.sum((cum_e <= (k1 - 1)[:, None]).astype(I32), axis=1)
    g_nwin = jnp.where(has, (s_hi - s_lo) // GATHER_WB + 1, 0)
    g_slo = jnp.where(has, s_lo, 0)
    g_code = eg * t + k0

    nts = t // COMB_TS
    cnt_ts = cnt_blk.reshape(nts, COMB_TS // TOK_BLK, N_EXPERTS).sum(axis=1)
    seg_lo = (tile_off * MOE_TM)[None, :] + jnp.cumsum(cnt_ts, axis=0) - cnt_ts
    seg_hi = seg_lo + cnt_ts
    ws0 = (seg_lo // BF16_ROWS) * BF16_ROWS
    nw = jnp.where(cnt_ts > 0, (seg_hi - ws0 + COMB_WC - 1) // COMB_WC, 0)
    nw_end = jnp.cumsum(nw, axis=1)
    k = jnp.arange(COMB_MAXW, dtype=I32)
    ek = jnp.minimum(jnp.sum((k[None, :, None] >= nw_end[:, None, :]).astype(I32), axis=2), N_EXPERTS - 1)
    pick = lambda a: by_expert(a[:, None, :], ek)
    live = k[None, :] < nw_end[:, -1:]
    nominal = pick(ws0) + (k[None, :] - pick(nw_end - nw)) * COMB_WC
    shift_k = by_expert(shift, ek)
    c_start = jnp.minimum(nominal, by_expert(tile_end * MOE_TM, ek) - COMB_WC)
    c_off = shift_k + c_start
    c_lo = jnp.maximum(pick(seg_lo), nominal) + shift_k
    c_hi = jnp.minimum(pick(seg_hi), nominal + COMB_WC) + shift_k
    flat = lambda a: jnp.where(live, a, 0).reshape(-1)
    return dict(
        tile_expert=tile_expert, tile_valid=valid.astype(I32),
        g_code=g_code, g_slo=g_slo, g_nwin=g_nwin,
        c_start=flat(c_start), c_off=flat(c_off), c_lo=flat(c_lo), c_hi=flat(c_hi), c_nwin=nw_end[:, -1])


def _moe_gather_kernel(code_ref, slo_ref, nwin_ref, rows_ref, h_ref, hs_ref, gs_ref):
    n_blk = rows_ref.shape[1]
    n_sub = hs_ref.shape[0] // MOE_TG
    sub_ids = [pl.program_id(0) * n_sub + sub for sub in range(n_sub)]
    out_rows = [slice(sub * MOE_TG, (sub + 1) * MOE_TG) for sub in range(n_sub)]

    def window(r, k):
        row_id = (code_ref[r] + lax.broadcasted_iota(I32, (MOE_TG, TOK_BLK), 0)).astype(F32)
        nominal = slo_ref[r] + k * GATHER_WB
        sb = jnp.minimum(nominal, n_blk - GATHER_WB)
        fresh = sb + lax.broadcasted_iota(I32, (GATHER_WB, TOK_BLK), 0) >= nominal
        p1 = jnp.where(fresh, rows_ref[0, pl.ds(sb, GATHER_WB), :], -1.0)
        p2 = jnp.where(fresh, rows_ref[1, pl.ds(sb, GATHER_WB), :], -1.0)
        w1 = rows_ref[2, pl.ds(sb, GATHER_WB), :]
        w2 = rows_ref[3, pl.ds(sb, GATHER_WB), :]
        pieces = []
        g = jnp.zeros((MOE_TG, TOK_BLK), F32)
        for a in range(GATHER_WB):
            m1 = p1[a:a + 1, :] == row_id
            m2 = p2[a:a + 1, :] == row_id
            pieces.append(jnp.where(m1 | m2, 1.0, 0.0).astype(BF16))
            g = g + jnp.where(m1, w1[a:a + 1, :], 0.0) + jnp.where(m2, w2[a:a + 1, :], 0.0)
        onehot = jnp.concatenate(pieces, axis=1)
        return onehot, g.sum(axis=-1, keepdims=True), pl.multiple_of(sb * TOK_BLK, TOK_BLK)

    gathered = lambda onehot, tok0: _dot(onehot, h_ref[pl.ds(tok0, GATHER_WB * TOK_BLK), :]).astype(BF16)

    firsts = [window(r, 0) for r in sub_ids]
    for rows, (onehot, weight, tok0) in zip(out_rows, firsts):
        hs_ref[rows, :] = gathered(onehot, tok0)
        gs_ref[rows, :] = weight

    for r, rows in zip(sub_ids, out_rows):
        def more(k, carry, r=r, rows=rows):
            onehot, weight, tok0 = window(r, k)
            hs_ref[rows, :] += gathered(onehot, tok0)
            gs_ref[rows, :] += weight
            return carry

        lax.fori_loop(1, nwin_ref[r], more, 0)


def _moe_gather(plan, h, rows):
    t = h.shape[0]
    n_rows = plan["g_code"].shape[0] * MOE_TG
    n_blk = t // TOK_BLK
    grid_spec = pltpu.PrefetchScalarGridSpec(
        num_scalar_prefetch=3,
        grid=(n_rows // GATHER_ROWS,),
        in_specs=[
            pl.BlockSpec((4, n_blk, TOK_BLK), lambda r, *_: (0, 0, 0), pipeline_mode=pl.Buffered(1)),
            pl.BlockSpec((t, D_MODEL), lambda r, *_: (0, 0), pipeline_mode=pl.Buffered(1)),
        ],
        out_specs=[
            pl.BlockSpec((GATHER_ROWS, D_MODEL), lambda r, *_: (r, 0)),
            pl.BlockSpec((GATHER_ROWS, 1), lambda r, *_: (r, 0)),
        ],
    )
    return pl.pallas_call(
        _moe_gather_kernel,
        grid_spec=grid_spec,
        out_shape=[
            jax.ShapeDtypeStruct((n_rows, D_MODEL), BF16),
            jax.ShapeDtypeStruct((n_rows, 1), F32),
        ],
        compiler_params=_params(("parallel",), 52 * 1024 * 1024),
        name="moe_gather",
    )(plan["g_code"], plan["g_slo"], plan["g_nwin"], rows, h)


def _moe_expert_kernel(te_ref, tv_ref, hs_ref, gs_ref, wg_ref, wu_ref, wd_ref, y_ref, acc_ref, *, nj, pieces):
    r = pl.program_id(0)
    j = pl.program_id(1)
    valid = tv_ref[r] > 0
    part = lambda: _swiglu_pieces(hs_ref[...], wg_ref, wu_ref, wd_ref, pieces)

    if nj == 1:
        @pl.when(valid)
        def _():
            y_ref[...] = (part() * gs_ref[...]).astype(BF16)
    else:
        @pl.when(valid & (j == 0))
        def _():
            acc_ref[...] = part()

        if nj > 2:
            @pl.when(valid & (j > 0) & (j < nj - 1))
            def _():
                acc_ref[...] += part()

        @pl.when(valid & (j == nj - 1))
        def _():
            y_ref[...] = ((acc_ref[...] + part()) * gs_ref[...]).astype(BF16)

    @pl.when(jnp.logical_not(valid) & (j == nj - 1))
    def _():
        y_ref[...] = jnp.zeros_like(y_ref)


def _moe_experts(plan, hs, gs, w_gate, w_up, w_down, pieces):
    tf = sum(pieces)
    n_rows, d = hs.shape
    ff = w_gate.shape[2]
    nj = ff // tf
    jj = lambda r, j, tv: jnp.where(tv[r] > 0, j, nj - 1)
    grid_spec = pltpu.PrefetchScalarGridSpec(
        num_scalar_prefetch=2,
        grid=(n_rows // MOE_TM, nj),
        in_specs=[
            pl.BlockSpec((MOE_TM, d), lambda r, j, te, tv: (r, 0)),
            pl.BlockSpec((MOE_TM, 1), lambda r, j, te, tv: (r, 0)),
            pl.BlockSpec((None, d, tf), lambda r, j, te, tv: (te[r], 0, jj(r, j, tv))),
            pl.BlockSpec((None, d, tf), lambda r, j, te, tv: (te[r], 0, jj(r, j, tv))),
            pl.BlockSpec((None, tf, d), lambda r, j, te, tv: (te[r], jj(r, j, tv), 0)),
        ],
        out_specs=pl.BlockSpec((MOE_TM, d), lambda r, j, te, tv: (r, 0)),
        scratch_shapes=[pltpu.VMEM((MOE_TM if nj > 1 else 8, d), F32)],
    )
    return pl.pallas_call(
        functools.partial(_moe_expert_kernel, nj=nj, pieces=pieces),
        grid_spec=grid_spec,
        out_shape=jax.ShapeDtypeStruct((n_rows, d), BF16),
        compiler_params=_params(("arbitrary", "arbitrary"), 56 * 1024 * 1024),
        name="moe_experts",
    )(plan["tile_expert"], plan["tile_valid"], hs, gs, w_gate, w_up, w_down)


def _moe_combine_kernel(start_ref, off_ref, lo_ref, hi_ref, nwin_ref, x_ref, info_ref, y_hbm, o_ref,
                        ybuf, ybuf_more, sem, sem_more):
    i = pl.program_id(0)
    n_tiles = pl.num_programs(0)

    def window_copy(idx, dst, dsem):
        start = pl.multiple_of(start_ref[idx], BF16_ROWS)
        return pltpu.make_async_copy(y_hbm.at[pl.ds(start, COMB_WC), :], dst, dsem)

    def first_group(tile, wait):
        buf = tile % COMB_AHEAD_BUFS
        for s in range(COMB_GROUP):
            @pl.when(s < nwin_ref[tile])
            def _():
                cp = window_copy(tile * COMB_MAXW + s, ybuf.at[buf, pl.ds(s * COMB_WC, COMB_WC), :],
                                 sem.at[buf, s])
                cp.wait() if wait else cp.start()

    @pl.when(i == 0)
    def _():
        ybuf[...] = jnp.zeros_like(ybuf)
        ybuf_more[...] = jnp.zeros_like(ybuf_more)
        for ahead in range(COMB_AHEAD_BUFS - 1):
            @pl.when(ahead < n_tiles)
            def _():
                first_group(ahead, wait=False)

    @pl.when(i + COMB_AHEAD_BUFS - 1 < n_tiles)
    def _():
        first_group(i + COMB_AHEAD_BUFS - 1, wait=False)

    first_group(i, wait=True)

    code1 = jnp.broadcast_to(info_ref[:, 0:1].astype(I32), (COMB_TS, COMB_WC))
    code2 = jnp.broadcast_to(info_ref[:, 1:2].astype(I32), (COMB_TS, COMB_WC))
    lane = lax.broadcasted_iota(I32, (COMB_TS, COMB_WC), 1)
    lane_row = lax.broadcasted_iota(I32, (1, COMB_WC), 1)

    def onehot_group(g):
        pieces = []
        for s in range(COMB_GROUP):
            idx = i * COMB_MAXW + g * COMB_GROUP + s
            off = off_ref[idx]
            in_span = (lane_row >= lo_ref[idx] - off) & (lane_row < hi_ref[idx] - off)
            target = lane + off
            hit = ((code1 == target) | (code2 == target)) & in_span
            pieces.append(jnp.where(hit, 1.0, 0.0).astype(BF16))
        return jnp.concatenate(pieces, axis=1)

    o_ref[...] = x_ref[...] + _dot(onehot_group(0), ybuf[i % COMB_AHEAD_BUFS])

    n = nwin_ref[i]

    def more(g, carry):
        for wait in (False, True):
            for s in range(COMB_GROUP):
                @pl.when(g * COMB_GROUP + s < n)
                def _():
                    cp = window_copy(i * COMB_MAXW + g * COMB_GROUP + s,
                                     ybuf_more.at[pl.ds(s * COMB_WC, COMB_WC), :], sem_more.at[s])
                    cp.wait() if wait else cp.start()
        o_ref[...] += _dot(onehot_group(g), ybuf_more[...])
        return carry

    lax.fori_loop(1, (n + COMB_GROUP - 1) // COMB_GROUP, more, 0)


def _moe_combine(plan, x, info, y):
    t, d = x.shape
    width = COMB_GROUP * COMB_WC
    grid_spec = pltpu.PrefetchScalarGridSpec(
        num_scalar_prefetch=5,
        grid=(t // COMB_TS,),
        in_specs=[
            pl.BlockSpec((COMB_TS, d), lambda i, *_: (i, 0)),
            pl.BlockSpec((COMB_TS, 4), lambda i, *_: (i, 0)),
            pl.BlockSpec(memory_space=pl.ANY),
        ],
        out_specs=pl.BlockSpec((COMB_TS, d), lambda i, *_: (i, 0)),
        scratch_shapes=[
            pltpu.VMEM((COMB_AHEAD_BUFS, width, d), BF16),
            pltpu.VMEM((width, d), BF16),
            pltpu.SemaphoreType.DMA((COMB_AHEAD_BUFS, COMB_GROUP)),
            pltpu.SemaphoreType.DMA((COMB_GROUP,)),
        ],
    )
    return pl.pallas_call(
        _moe_combine_kernel,
        grid_spec=grid_spec,
        out_shape=jax.ShapeDtypeStruct((t, d), F32),
        compiler_params=_params(("arbitrary",)),
        name="moe_combine",
    )(plan["c_start"], plan["c_off"], plan["c_lo"], plan["c_hi"], plan["c_nwin"], x, info, y)


def kernel(x, norm_mix, norm_ffn, gla_w_in, gla_w_gate2, gla_b_gate, gla_out_gain, gla_w_out,
           moba_w_qkv, moba_q_gain, moba_k_gain, moba_w_out, ffn_w_gate, ffn_w_up, ffn_w_down,
           moe_w_router, moe_w_gate, moe_w_up, moe_w_down):
    batch, seq, d = x.shape
    t = batch * seq
    xt = x.reshape(t, d)

    (q, k, v, g, la), (ffn_wg, ffn_wu, ffn_wd, gla_wo, moba_wqkv, moba_wo) = _gla_inproj(
        xt, norm_mix[0:1], gla_w_in, gla_w_gate2[0], gla_b_gate[0:1],
        ride_along=(ffn_w_gate, ffn_w_up, ffn_w_down, gla_w_out, moba_w_qkv, moba_w_out), tm=512)
    o, _ = _gla(q, k, v, g, la, gla_out_gain[0:1], (), batch, seq, blk=256, rows_per_step=2)
    xt = _ffn_residual(xt, o, gla_wo[0], norm_ffn[0:1], ffn_wg[0], ffn_wu[0], ffn_wd[0],
                       tm=512, pieces=MOE_PIECES + MOE_PIECES)

    q, k, v = _moba_qkv(xt, norm_mix[1:2], moba_wqkv[0], moba_q_gain[0:1], moba_k_gain[0:1], tm=512)
    o, (moe_wg, moe_wu, moe_wd) = _moba(q, k, v, (moe_w_gate[0], moe_w_up[0], moe_w_down[0]), batch, seq)
    xt, h, info, rows, cnt = _router(xt, o, moba_wo[0], norm_ffn[1:2], moe_w_router[0], tr=1024)
    plan = _moe_plan(cnt, t)
    hs, gs = _moe_gather(plan, h, rows)
    y = _moe_experts(plan, hs, gs, moe_wg, moe_wu, moe_wd, pieces=MOE_PIECES + MOE_PIECES)
    xt = _moe_combine(plan, xt, info, y)
    return xt.reshape(batch, seq, d)
```

```python
import functools

import jax
import jax.numpy as jnp
from jax import lax
from jax.experimental import pallas as pl
from jax.experimental.pallas import tpu as pltpu

F32 = jnp.float32
BF16 = jnp.bfloat16
I32 = jnp.int32

EPS = 1e-6
NEG_INF = -1e30

D_MODEL = 1024
GLA_HEADS = 4
GLA_DK = 512
GLA_DV = 1024
GLA_HK = 128
GLA_HV = 256
GLA_RANK = 16
GLA_NORMALIZER = 16.0
GLA_CHUNK = 64

MOBA_HEADS = 8
MOBA_HD = 128
MOBA_BLOCK = 256
MOBA_TOPK = 3
MOBA_Q_SCALE = MOBA_HD ** -0.5 * 1.4426950408889634

N_EXPERTS = 8
TOP_K = 2
ROUTER_ROWS = 16

LANES = 128
BF16_ROWS = 16

MOE_PIECES = (512, 512, 768)
MOE_TM = 256
MOE_TG = 128
GATHER_ROWS = 512
TOK_BLK = LANES
GATHER_WB = 6
COMB_TS = 256
COMB_WC = 128
COMB_GROUP = 8
COMB_AHEAD_BUFS = 3
COMB_MAXW = 16


def _params(semantics, vmem_bytes=None):
    return pltpu.CompilerParams(dimension_semantics=semantics, vmem_limit_bytes=vmem_bytes)


def _rms(x, gain):
    y = x * lax.rsqrt(jnp.mean(x * x, axis=-1, keepdims=True) + EPS)
    return y * gain


def _dot(a, b):
    return jnp.dot(a, b, preferred_element_type=F32)


def _dot_nt(a, b):
    return lax.dot_general(a, b, (((1,), (1,)), ((), ())), preferred_element_type=F32)


def _dot_tn(a, b):
    return lax.dot_general(a, b, (((0,), (0,)), ((), ())), preferred_element_type=F32)


def _split(a):
    hi = a.astype(BF16)
    lo = (a - hi.astype(F32)).astype(BF16)
    return hi, lo


def _log_sigmoid(z):
    return jnp.minimum(z, 0.0) - jnp.log(1.0 + jnp.exp(-jnp.abs(z)))


def _silu(z):
    return z / (1.0 + jnp.exp(-z))


def _swiglu_pieces(h, wg_ref, wu_ref, wd_ref, widths):
    bounds = [sum(widths[:n]) for n in range(len(widths) + 1)]
    assert bounds[-1] == wg_ref.shape[1], (widths, wg_ref.shape)
    gate_up = lambda c: (_dot(h, wg_ref[:, bounds[c]:bounds[c + 1]]), _dot(h, wu_ref[:, bounds[c]:bounds[c + 1]]))
    out = None
    upcoming = gate_up(0)
    for c in range(len(widths)):
        g, u = upcoming
        if c + 1 < len(widths):
            upcoming = gate_up(c + 1)
        d = _dot((_silu(g) * u).astype(BF16), wd_ref[bounds[c]:bounds[c + 1], :])
        out = d if out is None else out + d
    return out


def _ride_along_specs(arrays, n_steps, step):
    specs, shapes = [], []
    for w in arrays:
        n_e, rows, cols = w.shape
        per_e = n_steps // n_e
        if rows % (per_e * BF16_ROWS) == 0:
            blk = (None, rows // per_e, cols)
            imap = lambda *ids, per_e=per_e: (step(*ids) // per_e, step(*ids) % per_e, 0)
        else:
            assert cols % (per_e * LANES) == 0, (w.shape, n_steps)
            blk = (None, rows, cols // per_e)
            imap = lambda *ids, per_e=per_e: (step(*ids) // per_e, 0, step(*ids) % per_e)
        specs.append(pl.BlockSpec(blk, imap))
        shapes.append(jax.ShapeDtypeStruct(w.shape, BF16))
    return specs, shapes


def _with_ride_along(body, n_in, n_out, n_cast):
    def kernel(*refs):
        ins, refs = refs[:n_in], refs[n_in:]
        cast_in, refs = refs[:n_cast], refs[n_cast:]
        outs, refs = refs[:n_out], refs[n_out:]
        cast_out, scratch = refs[:n_cast], refs[n_cast:]
        for src, dst in zip(cast_in, cast_out):
            dst[...] = src[...].astype(BF16)
        body(*ins, *outs, *scratch)
    return kernel


def _gla_inproj_kernel(x_next_ref, x_first_ref, gain_ref, w_ref, wg2_ref, b_ref,
                       q_ref, k_ref, v_ref, g_ref, la_ref, wb_ref, h_ref):
    i = pl.program_id(0)

    @pl.when(i == 0)
    def _():
        wb_ref[...] = w_ref[...].astype(BF16)
        h_ref[0] = _rms(x_first_ref[...], gain_ref[...]).astype(BF16)

    n_main = 2 * GLA_DK + 2 * GLA_DV

    def step(cur, nxt):
        h = h_ref[cur]
        h_ref[nxt] = _rms(x_next_ref[...], gain_ref[...]).astype(BF16)
        a_lr = _dot(h, wb_ref[:, n_main:])
        a_hi, a_lo = _split(a_lr)
        w_hi, w_lo = _split(wg2_ref[...])
        z = _dot(jnp.concatenate([a_hi, a_lo, a_hi], axis=1), jnp.concatenate([w_hi, w_hi, w_lo], axis=0))
        la_ref[...] = _log_sigmoid(z + b_ref[...]) / GLA_NORMALIZER
        q_ref[...] = _dot(h, wb_ref[:, 0:GLA_DK]).astype(BF16)
        k_ref[...] = _dot(h, wb_ref[:, GLA_DK:2 * GLA_DK]).astype(BF16)
        v_ref[...] = _dot(h, wb_ref[:, 2 * GLA_DK:2 * GLA_DK + GLA_DV]).astype(BF16)
        g_ref[...] = _dot(h, wb_ref[:, 2 * GLA_DK + GLA_DV:n_main]).astype(BF16)

    @pl.when(i % 2 == 0)
    def _():
        step(0, 1)

    @pl.when(i % 2 == 1)
    def _():
        step(1, 0)


def _gla_inproj(x, gain, w_in, w_gate2, b_gate, ride_along, tm):
    t = x.shape[0]
    n_steps = t // tm
    row = lambda i: (i, 0)
    fixed = lambda i: (0, 0)
    cast_specs, cast_shapes = _ride_along_specs(ride_along, n_steps, lambda i: i)
    outs = pl.pallas_call(
        _with_ride_along(_gla_inproj_kernel, 6, 5, len(ride_along)),
        grid=(n_steps,),
        in_specs=[
            pl.BlockSpec((tm, D_MODEL), lambda i: (jnp.minimum(i + 1, n_steps - 1), 0)),
            pl.BlockSpec((tm, D_MODEL), fixed),
            pl.BlockSpec((1, D_MODEL), fixed),
            pl.BlockSpec((None,) + w_in.shape[1:], lambda i: (0, 0, 0), pipeline_mode=pl.Buffered(1)),
            pl.BlockSpec((GLA_RANK, GLA_DK), fixed),
            pl.BlockSpec((1, GLA_DK), fixed),
        ] + cast_specs,
        out_specs=[
            pl.BlockSpec((tm, GLA_DK), row),
            pl.BlockSpec((tm, GLA_DK), row),
            pl.BlockSpec((tm, GLA_DV), row),
            pl.BlockSpec((tm, GLA_DV), row),
            pl.BlockSpec((tm, GLA_DK), row),
        ] + cast_specs,
        out_shape=[
            jax.ShapeDtypeStruct((t, GLA_DK), BF16),
            jax.ShapeDtypeStruct((t, GLA_DK), BF16),
            jax.ShapeDtypeStruct((t, GLA_DV), BF16),
            jax.ShapeDtypeStruct((t, GLA_DV), BF16),
            jax.ShapeDtypeStruct((t, GLA_DK), F32),
        ] + cast_shapes,
        scratch_shapes=[pltpu.VMEM(w_in.shape[1:], BF16), pltpu.VMEM((2, tm, D_MODEL), BF16)],
        compiler_params=_params(("arbitrary",), 48 * 1024 * 1024),
        name="gla_inproj",
    )(x, x, gain, w_in, w_gate2, b_gate, *ride_along)
    return outs[:5], outs[5:]


def _gla_kernel(q_ref, k_ref, v_ref, g_ref, la_ref, og_ref, o_ref, st_ref, *, n_chunks):
    @pl.when(pl.program_id(1) == 0)
    def _():
        st_ref[...] = jnp.zeros_like(st_ref)

    n_rows = q_ref.shape[0]
    blk = n_chunks * GLA_CHUNK
    r = lax.broadcasted_iota(I32, (blk, blk), 0)
    c = lax.broadcasted_iota(I32, (blk, blk), 1)
    causal = (r >= c) & (r // GLA_CHUNK == c // GLA_CHUNK)
    tril = causal.astype(BF16)
    og = og_ref[...]
    chunk_rows = [slice(ci * GLA_CHUNK, (ci + 1) * GLA_CHUNK) for ci in range(n_chunks)]
    hk = [slice(h * GLA_HK, (h + 1) * GLA_HK) for h in range(GLA_HEADS)]
    hv = [slice(h * GLA_HV, (h + 1) * GLA_HV) for h in range(GLA_HEADS)]

    q_dec, k_inv, k_end, chunk_decay = [], [], [], []
    for b in range(n_rows):
        la = la_ref[b]
        la_hi, la_mid = _split(la)
        la_lo = (la - la_hi.astype(F32) - la_mid.astype(F32)).astype(BF16)
        cum = _dot(tril, la_hi) + _dot(tril, la_mid) + _dot(tril, la_lo)
        ends = [cum[rows.stop - 1:rows.stop, :] for rows in chunk_rows]
        last = jnp.concatenate([jnp.broadcast_to(e, (GLA_CHUNK, GLA_DK)) for e in ends], axis=0)
        q = q_ref[b].astype(F32) * (GLA_HK ** -0.5)
        k = k_ref[b].astype(F32)
        q_dec.append((q * jnp.exp(cum)).astype(BF16))
        k_inv.append((k * jnp.exp(-cum)).astype(BF16))
        k_end.append((k * jnp.exp(last - cum)).astype(BF16))
        chunk_decay.append([jnp.exp(e) for e in ends])

    streams = [(b, h) for b in range(n_rows) for h in range(GLA_HEADS)]
    scores = [jnp.where(causal, _dot_nt(q_dec[b][:, hk[h]], k_inv[b][:, hk[h]]), 0.0).astype(BF16)
              for b, h in streams]
    state_in = [[_dot_tn(v_ref[b, rows, hv[h]], k_end[b][rows, hk[h]]) for rows in chunk_rows]
                for b, h in streams]
    intra = [_dot(scores[n], v_ref[b, :, hv[h]]) for n, (b, h) in enumerate(streams)]
    for n, (b, h) in enumerate(streams):
        st = st_ref[b, h]
        inter = []
        for ci, rows in enumerate(chunk_rows):
            inter.append(_dot_nt(q_dec[b][rows, hk[h]], st.astype(BF16)))
            st = st * chunk_decay[b][ci][:, hk[h]] + state_in[n][ci]
        st_ref[b, h] = st
        o = intra[n] + jnp.concatenate(inter, axis=0)
        o = o * lax.rsqrt(jnp.mean(o * o, axis=-1, keepdims=True) + EPS) * og
        o = o * _silu(g_ref[b, :, hv[h]].astype(F32))
        o_ref[b, :, hv[h]] = o.astype(BF16)


def _gla(q, k, v, g, la, out_gain, ride_along, batch, seq, blk, rows_per_step):
    nb = seq // blk
    nbt = batch // rows_per_step
    per_row = lambda a: a.reshape(batch, seq, a.shape[-1])
    spec = lambda width: pl.BlockSpec((rows_per_step, blk, width), lambda b, i: (b, i, 0))
    cast_specs, cast_shapes = _ride_along_specs(ride_along, nbt * nb, lambda b, i: b * nb + i)
    outs = pl.pallas_call(
        _with_ride_along(functools.partial(_gla_kernel, n_chunks=blk // GLA_CHUNK), 6, 1, len(ride_along)),
        grid=(nbt, nb),
        in_specs=[spec(GLA_DK), spec(GLA_DK), spec(GLA_DV), spec(GLA_DV), spec(GLA_DK),
                  pl.BlockSpec((1, GLA_HV), lambda b, i: (0, 0))] + cast_specs,
        out_specs=[spec(GLA_DV)] + cast_specs,
        out_shape=[jax.ShapeDtypeStruct((batch, seq, GLA_DV), BF16)] + cast_shapes,
        scratch_shapes=[pltpu.VMEM((rows_per_step, GLA_HEADS, GLA_HV, GLA_HK), F32)],
        compiler_params=_params(("arbitrary", "arbitrary"), 48 * 1024 * 1024),
        name="gla_core",
    )(per_row(q), per_row(k), per_row(v), per_row(g), per_row(la), out_gain, *ride_along)
    return outs[0].reshape(batch * seq, GLA_DV), outs[1:]


def _ffn_kernel(x_ref, o_ref, wo_ref, gain_ref, wg_ref, wu_ref, wd_ref, y_ref, *, pieces):
    x1 = x_ref[...] + _dot(o_ref[...], wo_ref[...])
    h = _rms(x1, gain_ref[...]).astype(BF16)
    y_ref[...] = x1 + _swiglu_pieces(h, wg_ref, wu_ref, wd_ref, pieces)


def _ffn_residual(x, o, w_out, gain, w_gate, w_up, w_down, tm, pieces):
    t, d = x.shape
    row = lambda i: (i, 0)
    resident = lambda a: pl.BlockSpec(a.shape, lambda i: (0, 0), pipeline_mode=pl.Buffered(1))
    return pl.pallas_call(
        functools.partial(_ffn_kernel, pieces=pieces),
        grid=(t // tm,),
        in_specs=[pl.BlockSpec((tm, d), row), pl.BlockSpec((tm, d), row), resident(w_out), resident(gain),
                  resident(w_gate), resident(w_up), resident(w_down)],
        out_specs=pl.BlockSpec((tm, d), row),
        out_shape=jax.ShapeDtypeStruct((t, d), F32),
        compiler_params=_params(("parallel",), 56 * 1024 * 1024),
        name="ffn_swiglu",
    )(x, o, w_out, gain, w_gate, w_up, w_down)


def _moba_qkv_kernel(x_ref, gain_ref, w_ref, qg_ref, kg_ref, q_ref, k_ref, v_ref):
    h = _rms(x_ref[...], gain_ref[...]).astype(BF16)
    d = MOBA_HEADS * MOBA_HD
    qg = qg_ref[...]
    kg = kg_ref[...]
    q = _dot(h, w_ref[:, 0:d])
    k = _dot(h, w_ref[:, d:2 * d])
    for hd in range(MOBA_HEADS):
        cols = slice(hd * MOBA_HD, (hd + 1) * MOBA_HD)
        q_ref[:, cols] = (_rms(q[:, cols], qg) * MOBA_Q_SCALE).astype(BF16)
        k_ref[:, cols] = _rms(k[:, cols], kg).astype(BF16)
    v_ref[...] = _dot(h, w_ref[:, 2 * d:]).astype(BF16)


def _moba_qkv(x, gain, w, q_gain, k_gain, tm):
    t = x.shape[0]
    d = MOBA_HEADS * MOBA_HD
    row = lambda i: (i, 0)
    fixed = lambda i: (0, 0)
    return pl.pallas_call(
        _moba_qkv_kernel,
        grid=(t // tm,),
        in_specs=[
            pl.BlockSpec((tm, D_MODEL), row),
            pl.BlockSpec((1, D_MODEL), fixed),
            pl.BlockSpec((D_MODEL, 3 * d), fixed),
            pl.BlockSpec((1, MOBA_HD), fixed),
            pl.BlockSpec((1, MOBA_HD), fixed),
        ],
        out_specs=[pl.BlockSpec((tm, d), row)] * 3,
        out_shape=[jax.ShapeDtypeStruct((t, d), BF16)] * 3,
        compiler_params=_params(("parallel",), 48 * 1024 * 1024),
        name="moba_qkv",
    )(x, gain, w, q_gain, k_gain)


def _reduce_rows(x, op, stop=8):
    while x.shape[0] > stop:
        half = x.shape[0] // 2
        x = op(x[:half], x[half:])
    return x


def _moba_kernel(q_ref, k_ref, v_ref, o_ref, *, n_blocks):
    seq = n_blocks * MOBA_BLOCK
    heads = range(q_ref.shape[1] // MOBA_HD)
    cols = [slice(hd * MOBA_HD, (hd + 1) * MOBA_HD) for hd in heads]
    n_pad = 16
    ones_rows = jnp.where(lax.broadcasted_iota(I32, (BF16_ROWS, seq), 0) == 0, 1.0, 0.0)
    v_t = [jnp.concatenate([v_ref[:, cols[hd]].astype(F32).T, ones_rows], axis=0).astype(BF16) for hd in heads]
    blk_of_key = lax.broadcasted_iota(I32, (n_pad, seq), 1) // MOBA_BLOCK
    blk_row = lax.broadcasted_iota(I32, (n_pad, seq), 0)
    indicator = jnp.where(blk_of_key == blk_row, 1.0 / MOBA_BLOCK, 0.0).astype(BF16)
    k_mean = [_split(_dot(indicator, k_ref[:, cols[hd]])) for hd in heads]

    blk_id = lax.broadcasted_iota(I32, (n_pad, MOBA_BLOCK), 0)
    key_pos = lax.broadcasted_iota(I32, (MOBA_BLOCK, MOBA_BLOCK), 0)
    query_pos = lax.broadcasted_iota(I32, (MOBA_BLOCK, MOBA_BLOCK), 1)
    causal = key_pos <= query_pos

    def masked_scores(hd, i):
        rows = slice(i * MOBA_BLOCK, (i + 1) * MOBA_BLOCK)
        n_keys = (i + 1) * MOBA_BLOCK
        q = q_ref[rows, cols[hd]]
        s = _dot_nt(k_ref[0:n_keys, cols[hd]], q)
        selected = None
        if i > MOBA_TOPK:
            km_hi, km_lo = k_mean[hd]
            gate = _dot_nt(km_hi, q) + _dot_nt(km_lo, q)
            gate = jnp.where(blk_id < i, gate, NEG_INF)
            rank = jnp.zeros_like(gate)
            for jp in range(i):
                row = gate[jp:jp + 1, :]
                beats = (row > gate) | ((row == gate) & (jp < blk_id))
                rank = rank + beats.astype(F32)
            selected = jnp.where((rank < MOBA_TOPK) & (blk_id < i), 1.0, 0.0)
        pieces = []
        top = None
        for j in range(i + 1):
            sj = s[j * MOBA_BLOCK:(j + 1) * MOBA_BLOCK, :]
            if j == i:
                sj = jnp.where(causal, sj, NEG_INF)
            elif selected is not None:
                sj = jnp.where(selected[j:j + 1, :] > 0.5, sj, NEG_INF)
            pieces.append(sj)
            folded = _reduce_rows(sj, jnp.maximum)
            top = folded if top is None else jnp.maximum(top, folded)
        return pieces, top

    def weighted_values(hd, i, p):
        o_t = _dot(v_t[hd][:, 0:(i + 1) * MOBA_BLOCK], p)
        rows = slice(i * MOBA_BLOCK, (i + 1) * MOBA_BLOCK)
        o_ref[rows, cols[hd]] = (o_t[:MOBA_HD, :] / o_t[MOBA_HD:MOBA_HD + 1, :]).T.astype(BF16)

    def softmax_numerators(scored):
        pieces, top = scored
        m = top.max(axis=0, keepdims=True)
        probs = [jnp.exp2(sj - m).astype(BF16) for sj in pieces]
        return probs[0] if len(probs) == 1 else jnp.concatenate(probs, axis=0)

    upcoming = [masked_scores(hd, 0) for hd in heads]
    pending = None
    for i in range(n_blocks):
        pieces = upcoming
        if i + 1 < n_blocks:
            upcoming = [masked_scores(hd, i + 1) for hd in heads]
        if pending is not None:
            for hd in heads:
                weighted_values(hd, i - 1, pending[hd])
        pending = [softmax_numerators(pieces[hd]) for hd in heads]
    for hd in heads:
        weighted_values(hd, n_blocks - 1, pending[hd])


def _moba(q, k, v, ride_along, batch, seq, heads_per_step):
    t, d = q.shape
    n_groups = MOBA_HEADS // heads_per_step
    spec = pl.BlockSpec((seq, heads_per_step * MOBA_HD), lambda b, h: (b, h))
    cast_specs, cast_shapes = _ride_along_specs(ride_along, batch * n_groups, lambda b, h: b * n_groups + h)
    outs = pl.pallas_call(
        _with_ride_along(functools.partial(_moba_kernel, n_blocks=seq // MOBA_BLOCK), 3, 1, len(ride_along)),
        grid=(batch, n_groups),
        in_specs=[spec, spec, spec] + cast_specs,
        out_specs=[spec] + cast_specs,
        out_shape=[jax.ShapeDtypeStruct((t, d), BF16)] + cast_shapes,
        compiler_params=_params(("arbitrary", "arbitrary"), 48 * 1024 * 1024),
        name="moba_attn",
    )(q, k, v, *ride_along)
    return outs[0], outs[1:]


def _router_kernel(x_ref, o_ref, wo_ref, gain_ref, w_ref, x2_ref, h_ref, info_ref, rows_ref, cnt_ref,
                   carry_ref, tri_ref,
                   *, region_rows):
    tr = x_ref.shape[0]

    @pl.when(pl.program_id(0) == 0)
    def _():
        carry_ref[...] = jnp.zeros_like(carry_ref)
        r = lax.broadcasted_iota(I32, (TOK_BLK, TOK_BLK), 0)
        c = lax.broadcasted_iota(I32, (TOK_BLK, TOK_BLK), 1)
        tri_ref[...] = (r < c).astype(BF16)

    x2 = x_ref[...] + _dot(o_ref[...], wo_ref[...])
    x2_ref[...] = x2
    h = _rms(x2, gain_ref[...])
    h_hi, h_lo = _split(h)
    h_ref[...] = h_hi
    n_e = ROUTER_ROWS
    wt = w_ref[...]
    passes = _dot_nt(wt, h_hi)
    logits = passes[:n_e] + passes[n_e:] + _dot_nt(wt[:n_e], h_lo)
    e = lax.broadcasted_iota(I32, logits.shape, 0)
    logits = jnp.where(e < N_EXPERTS, logits, -jnp.inf)
    m1 = logits.max(axis=0, keepdims=True)
    i1 = jnp.where(logits == m1, e, n_e).min(axis=0, keepdims=True)
    first = e == i1
    rest = jnp.where(first, -jnp.inf, logits)
    m2 = rest.max(axis=0, keepdims=True)
    i2 = jnp.where(rest == m2, e, n_e).min(axis=0, keepdims=True)
    second = e == i2
    e2 = jnp.exp(m2 - m1)
    denom = 1.0 + e2
    sel = jnp.where(first | second, 1.0, 0.0).astype(BF16)
    nb = tr // TOK_BLK
    blk_row = lax.broadcasted_iota(I32, (nb, tr), 0)
    blk_tok = lax.broadcasted_iota(I32, (nb, tr), 1) // TOK_BLK
    cnt = _dot_nt((blk_row == blk_tok).astype(BF16), sel)
    cnt_ref[...] = jnp.concatenate([cnt, jnp.zeros((nb, LANES - n_e), F32)], axis=1)
    offset = carry_ref[...]
    before = []
    for b in range(nb):
        sel_b = sel[:, b * TOK_BLK:(b + 1) * TOK_BLK]
        before.append(_dot(sel_b, tri_ref[...]) + offset)
        offset = offset + jnp.broadcast_to(sel_b.astype(F32).sum(axis=1, keepdims=True), offset.shape)
    carry_ref[...] = offset
    before = jnp.concatenate(before, axis=1)
    rank1 = jnp.where(first, before, 0.0).sum(axis=0, keepdims=True)
    rank2 = jnp.where(second, before, 0.0).sum(axis=0, keepdims=True)
    fields = (i1.astype(F32) * region_rows + rank1, i2.astype(F32) * region_rows + rank2,
              1.0 / denom, e2 / denom)
    for n, f in enumerate(fields):
        for b in range(nb):
            rows_ref[n, b:b + 1, :] = f[:, b * TOK_BLK:(b + 1) * TOK_BLK]
    info_t = jnp.concatenate(list(fields) + [jnp.zeros((LANES - len(fields), tr), F32)], axis=0)
    info_ref[...] = info_t.T[:, :len(fields)]


def _router(x, o, w_out, gain, w_router, tr):
    t = x.shape[0]
    nb = tr // TOK_BLK
    pad = ((0, ROUTER_ROWS - N_EXPERTS), (0, 0))
    w_hi = w_router.T.astype(BF16)
    w_lo = (w_router.T - w_hi.astype(F32)).astype(BF16)
    w_both = jnp.concatenate([jnp.pad(w_hi, pad), jnp.pad(w_lo, pad)], axis=0)
    row = lambda i: (i, 0)
    fixed = lambda i: (0, 0)
    return pl.pallas_call(
        functools.partial(_router_kernel, region_rows=float(t)),
        grid=(t // tr,),
        in_specs=[
            pl.BlockSpec((tr, D_MODEL), row),
            pl.BlockSpec((tr, D_MODEL), row),
            pl.BlockSpec((D_MODEL, D_MODEL), fixed),
            pl.BlockSpec((1, D_MODEL), fixed),
            pl.BlockSpec((2 * ROUTER_ROWS, D_MODEL), fixed),
        ],
        out_specs=[
            pl.BlockSpec((tr, D_MODEL), row),
            pl.BlockSpec((tr, D_MODEL), row),
            pl.BlockSpec((tr, 4), row),
            pl.BlockSpec((4, nb, TOK_BLK), lambda i: (0, i, 0)),
            pl.BlockSpec((nb, LANES), row),
        ],
        out_shape=[
            jax.ShapeDtypeStruct((t, D_MODEL), F32),
            jax.ShapeDtypeStruct((t, D_MODEL), BF16),
            jax.ShapeDtypeStruct((t, 4), F32),
            jax.ShapeDtypeStruct((4, t // TOK_BLK, TOK_BLK), F32),
            jax.ShapeDtypeStruct((t // TOK_BLK, LANES), F32),
        ],
        scratch_shapes=[pltpu.VMEM((ROUTER_ROWS, LANES), F32), pltpu.VMEM((TOK_BLK, TOK_BLK), BF16)],
        compiler_params=_params(("arbitrary",), 56 * 1024 * 1024),
        name="moe_router",
    )(x, o, w_out, gain, w_both)


def _moe_plan(cnt, t):
    e_ids = jnp.arange(N_EXPERTS, dtype=I32)
    by_expert = lambda table, idx: jnp.sum(jnp.where(idx[..., None] == e_ids, table, 0), axis=-1)
    cnt_blk = cnt[:, :N_EXPERTS].astype(I32)
    cum_blk = jnp.cumsum(cnt_blk, axis=0)
    counts = cum_blk[-1]
    tiles_e = (counts + MOE_TM - 1) // MOE_TM
    tile_end = jnp.cumsum(tiles_e)
    tile_off = tile_end - tiles_e
    nt = TOP_K * t // MOE_TM + N_EXPERTS
    r = jnp.arange(nt, dtype=I32)
    valid = r < tile_end[-1]
    te = jnp.minimum(jnp.sum((r[:, None] >= tile_end[None, :]).astype(I32), axis=1), N_EXPERTS - 1)
    local = r - by_expert(tile_off, te)
    last_e = jnp.max(jnp.where(tiles_e > 0, e_ids, 0))
    tile_expert = jnp.where(valid, te, last_e)
    shift = e_ids * t - tile_off * MOE_TM

    per = MOE_TM // MOE_TG
    rg = jnp.arange(nt * per, dtype=I32)
    eg = jnp.repeat(te, per)
    k0 = jnp.repeat(local, per) * MOE_TM + (rg % per) * MOE_TG
    k1 = jnp.minimum(k0 + MOE_TG, by_expert(counts, eg))
    has = jnp.repeat(valid, per) & (k1 > k0)
    cum_e = by_expert(cum_blk[None, :, :], eg[:, None])
    s_lo = jnp.sum((cum_e <= k0[:, None]).astype(I32), axis=1)
    s_hi = jnp.sum((cum_e <= (k1 - 1)[:, None]).astype(I32), axis=1)
    g_nwin = jnp.where(has, (s_hi - s_lo) // GATHER_WB + 1, 0)
    g_slo = jnp.where(has, s_lo, 0)
    g_code = eg * t + k0

    nts = t // COMB_TS
    cnt_ts = cnt_blk.reshape(nts, COMB_TS // TOK_BLK, N_EXPERTS).sum(axis=1)
    seg_lo = (tile_off * MOE_TM)[None, :] + jnp.cumsum(cnt_ts, axis=0) - cnt_ts
    seg_hi = seg_lo + cnt_ts
    ws0 = (seg_lo // BF16_ROWS) * BF16_ROWS
    nw = jnp.where(cnt_ts > 0, (seg_hi - ws0 + COMB_WC - 1) // COMB_WC, 0)
    nw_end = jnp.cumsum(nw, axis=1)
    k = jnp.arange(COMB_MAXW, dtype=I32)
    ek = jnp.minimum(jnp.sum((k[None, :, None] >= nw_end[:, None, :]).astype(I32), axis=2), N_EXPERTS - 1)
    pick = lambda a: by_expert(a[:, None, :], ek)
    live = k[None, :] < nw_end[:, -1:]
    nominal = pick(ws0) + (k[None, :] - pick(nw_end - nw)) * COMB_WC
    shift_k = by_expert(shift, ek)
    c_start = jnp.minimum(nominal, by_expert(tile_end * MOE_TM, ek) - COMB_WC)
    c_off = shift_k + c_start
    c_lo = jnp.maximum(pick(seg_lo), nominal) + shift_k
    c_hi = jnp.minimum(pick(seg_hi), nominal + COMB_WC) + shift_k
    flat = lambda a: jnp.where(live, a, 0).reshape(-1)
    return dict(
        tile_expert=tile_expert, tile_valid=valid.astype(I32),
        g_code=g_code, g_slo=g_slo, g_nwin=g_nwin,
        c_start=flat(c_start), c_off=flat(c_off), c_lo=flat(c_lo), c_hi=flat(c_hi), c_nwin=nw_end[:, -1])


def _moe_gather_kernel(code_ref, slo_ref, nwin_ref, rows_ref, h_ref, hs_ref, gs_ref):
    n_blk = rows_ref.shape[1]
    n_sub = hs_ref.shape[0] // MOE_TG
    sub_ids = [pl.program_id(0) * n_sub + sub for sub in range(n_sub)]
    out_rows = [slice(sub * MOE_TG, (sub + 1) * MOE_TG) for sub in range(n_sub)]

    def window(r, k):
        row_id = (code_ref[r] + lax.broadcasted_iota(I32, (MOE_TG, TOK_BLK), 0)).astype(F32)
        nominal = slo_ref[r] + k * GATHER_WB
        sb = jnp.minimum(nominal, n_blk - GATHER_WB)
        fresh = sb + lax.broadcasted_iota(I32, (GATHER_WB, TOK_BLK), 0) >= nominal
        p1 = jnp.where(fresh, rows_ref[0, pl.ds(sb, GATHER_WB), :], -1.0)
        p2 = jnp.where(fresh, rows_ref[1, pl.ds(sb, GATHER_WB), :], -1.0)
        w1 = rows_ref[2, pl.ds(sb, GATHER_WB), :]
        w2 = rows_ref[3, pl.ds(sb, GATHER_WB), :]
        pieces = []
        g = jnp.zeros((MOE_TG, TOK_BLK), F32)
        for a in range(GATHER_WB):
            m1 = p1[a:a + 1, :] == row_id
            m2 = p2[a:a + 1, :] == row_id
            pieces.append(jnp.where(m1 | m2, 1.0, 0.0).astype(BF16))
            g = g + jnp.where(m1, w1[a:a + 1, :], 0.0) + jnp.where(m2, w2[a:a + 1, :], 0.0)
        onehot = jnp.concatenate(pieces, axis=1)
        return onehot, g.sum(axis=-1, keepdims=True), pl.multiple_of(sb * TOK_BLK, TOK_BLK)

    gathered = lambda onehot, tok0: _dot(onehot, h_ref[pl.ds(tok0, GATHER_WB * TOK_BLK), :]).astype(BF16)

    firsts = [window(r, 0) for r in sub_ids]
    for rows, (onehot, weight, tok0) in zip(out_rows, firsts):
        hs_ref[rows, :] = gathered(onehot, tok0)
        gs_ref[rows, :] = weight

    for r, rows in zip(sub_ids, out_rows):
        def more(k, carry, r=r, rows=rows):
            onehot, weight, tok0 = window(r, k)
            hs_ref[rows, :] += gathered(onehot, tok0)
            gs_ref[rows, :] += weight
            return carry

        lax.fori_loop(1, nwin_ref[r], more, 0)


def _moe_gather(plan, h, rows):
    t = h.shape[0]
    n_rows = plan["g_code"].shape[0] * MOE_TG
    n_blk = t // TOK_BLK
    grid_spec = pltpu.PrefetchScalarGridSpec(
        num_scalar_prefetch=3,
        grid=(n_rows // GATHER_ROWS,),
        in_specs=[
            pl.BlockSpec((4, n_blk, TOK_BLK), lambda r, *_: (0, 0, 0), pipeline_mode=pl.Buffered(1)),
            pl.BlockSpec((t, D_MODEL), lambda r, *_: (0, 0), pipeline_mode=pl.Buffered(1)),
        ],
        out_specs=[
            pl.BlockSpec((GATHER_ROWS, D_MODEL), lambda r, *_: (r, 0)),
            pl.BlockSpec((GATHER_ROWS, 1), lambda r, *_: (r, 0)),
        ],
    )
    return pl.pallas_call(
        _moe_gather_kernel,
        grid_spec=grid_spec,
        out_shape=[
            jax.ShapeDtypeStruct((n_rows, D_MODEL), BF16),
            jax.ShapeDtypeStruct((n_rows, 1), F32),
        ],
        compiler_params=_params(("parallel",), 52 * 1024 * 1024),
        name="moe_gather",
    )(plan["g_code"], plan["g_slo"], plan["g_nwin"], rows, h)


def _moe_expert_kernel(te_ref, tv_ref, hs_ref, gs_ref, wg_ref, wu_ref, wd_ref, y_ref, acc_ref, *, nj, pieces):
    r = pl.program_id(0)
    j = pl.program_id(1)
    valid = tv_ref[r] > 0
    part = lambda: _swiglu_pieces(hs_ref[...], wg_ref, wu_ref, wd_ref, pieces)

    if nj == 1:
        @pl.when(valid)
        def _():
            y_ref[...] = (part() * gs_ref[...]).astype(BF16)
    else:
        @pl.when(valid & (j == 0))
        def _():
            acc_ref[...] = part()

        if nj > 2:
            @pl.when(valid & (j > 0) & (j < nj - 1))
            def _():
                acc_ref[...] += part()

        @pl.when(valid & (j == nj - 1))
        def _():
            y_ref[...] = ((acc_ref[...] + part()) * gs_ref[...]).astype(BF16)

    @pl.when(jnp.logical_not(valid) & (j == nj - 1))
    def _():
        y_ref[...] = jnp.zeros_like(y_ref)


def _moe_experts(plan, hs, gs, w_gate, w_up, w_down, pieces):
    tf = sum(pieces)
    n_rows, d = hs.shape
    ff = w_gate.shape[2]
    nj = ff // tf
    jj = lambda r, j, tv: jnp.where(tv[r] > 0, j, nj - 1)
    grid_spec = pltpu.PrefetchScalarGridSpec(
        num_scalar_prefetch=2,
        grid=(n_rows // MOE_TM, nj),
        in_specs=[
            pl.BlockSpec((MOE_TM, d), lambda r, j, te, tv: (r, 0)),
            pl.BlockSpec((MOE_TM, 1), lambda r, j, te, tv: (r, 0)),
            pl.BlockSpec((None, d, tf), lambda r, j, te, tv: (te[r], 0, jj(r, j, tv))),
            pl.BlockSpec((None, d, tf), lambda r, j, te, tv: (te[r], 0, jj(r, j, tv))),
            pl.BlockSpec((None, tf, d), lambda r, j, te, tv: (te[r], jj(r, j, tv), 0)),
        ],
        out_specs=pl.BlockSpec((MOE_TM, d), lambda r, j, te, tv: (r, 0)),
        scratch_shapes=[pltpu.VMEM((MOE_TM if nj > 1 else 8, d), F32)],
    )
    return pl.pallas_call(
        functools.partial(_moe_expert_kernel, nj=nj, pieces=pieces),
        grid_spec=grid_spec,
        out_shape=jax.ShapeDtypeStruct((n_rows, d), BF16),
        compiler_params=_params(("arbitrary", "arbitrary"), 56 * 1024 * 1024),
        name="moe_experts",
    )(plan["tile_expert"], plan["tile_valid"], hs, gs, w_gate, w_up, w_down)


def _moe_combine_kernel(start_ref, off_ref, lo_ref, hi_ref, nwin_ref, x_ref, info_ref, y_hbm, o_ref,
                        ybuf, ybuf_more, sem, sem_more):
    i = pl.program_id(0)
    n_tiles = pl.num_programs(0)

    def window_copy(idx, dst, dsem):
        start = pl.multiple_of(start_ref[idx], BF16_ROWS)
        return pltpu.make_async_copy(y_hbm.at[pl.ds(start, COMB_WC), :], dst, dsem)

    def first_group(tile, wait):
        buf = tile % COMB_AHEAD_BUFS
        for s in range(COMB_GROUP):
            @pl.when(s < nwin_ref[tile])
            def _():
                cp = window_copy(tile * COMB_MAXW + s, ybuf.at[buf, pl.ds(s * COMB_WC, COMB_WC), :],
                                 sem.at[buf, s])
                cp.wait() if wait else cp.start()

    @pl.when(i == 0)
    def _():
        ybuf[...] = jnp.zeros_like(ybuf)
        ybuf_more[...] = jnp.zeros_like(ybuf_more)
        for ahead in range(COMB_AHEAD_BUFS - 1):
            @pl.when(ahead < n_tiles)
            def _():
                first_group(ahead, wait=False)

    @pl.when(i + COMB_AHEAD_BUFS - 1 < n_tiles)
    def _():
        first_group(i + COMB_AHEAD_BUFS - 1, wait=False)

    first_group(i, wait=True)

    code1 = jnp.broadcast_to(info_ref[:, 0:1].astype(I32), (COMB_TS, COMB_WC))
    code2 = jnp.broadcast_to(info_ref[:, 1:2].astype(I32), (COMB_TS, COMB_WC))
    lane = lax.broadcasted_iota(I32, (COMB_TS, COMB_WC), 1)
    lane_row = lax.broadcasted_iota(I32, (1, COMB_WC), 1)

    def onehot_group(g):
        pieces = []
        for s in range(COMB_GROUP):
            idx = i * COMB_MAXW + g * COMB_GROUP + s
            off = off_ref[idx]
            in_span = (lane_row >= lo_ref[idx] - off) & (lane_row < hi_ref[idx] - off)
            target = lane + off
            hit = ((code1 == target) | (code2 == target)) & in_span
            pieces.append(jnp.where(hit, 1.0, 0.0).astype(BF16))
        return jnp.concatenate(pieces, axis=1)

    o_ref[...] = x_ref[...] + _dot(onehot_group(0), ybuf[i % COMB_AHEAD_BUFS])

    n = nwin_ref[i]

    def more(g, carry):
        for wait in (False, True):
            for s in range(COMB_GROUP):
                @pl.when(g * COMB_GROUP + s < n)
                def _():
                    cp = window_copy(i * COMB_MAXW + g * COMB_GROUP + s,
                                     ybuf_more.at[pl.ds(s * COMB_WC, COMB_WC), :], sem_more.at[s])
                    cp.wait() if wait else cp.start()
        o_ref[...] += _dot(onehot_group(g), ybuf_more[...])
        return carry

    lax.fori_loop(1, (n + COMB_GROUP - 1) // COMB_GROUP, more, 0)


def _moe_combine(plan, x, info, y):
    t, d = x.shape
    width = COMB_GROUP * COMB_WC
    grid_spec = pltpu.PrefetchScalarGridSpec(
        num_scalar_prefetch=5,
        grid=(t // COMB_TS,),
        in_specs=[
            pl.BlockSpec((COMB_TS, d), lambda i, *_: (i, 0)),
            pl.BlockSpec((COMB_TS, 4), lambda i, *_: (i, 0)),
            pl.BlockSpec(memory_space=pl.ANY),
        ],
        out_specs=pl.BlockSpec((COMB_TS, d), lambda i, *_: (i, 0)),
        scratch_shapes=[
            pltpu.VMEM((COMB_AHEAD_BUFS, width, d), BF16),
            pltpu.VMEM((width, d), BF16),
            pltpu.SemaphoreType.DMA((COMB_AHEAD_BUFS, COMB_GROUP)),
            pltpu.SemaphoreType.DMA((COMB_GROUP,)),
        ],
    )
    return pl.pallas_call(
        _moe_combine_kernel,
        grid_spec=grid_spec,
        out_shape=jax.ShapeDtypeStruct((t, d), F32),
        compiler_params=_params(("arbitrary",)),
        name="moe_combine",
    )(plan["c_start"], plan["c_off"], plan["c_lo"], plan["c_hi"], plan["c_nwin"], x, info, y)


def kernel(x, norm_mix, norm_ffn, gla_w_in, gla_w_gate2, gla_b_gate, gla_out_gain, gla_w_out,
           moba_w_qkv, moba_q_gain, moba_k_gain, moba_w_out, ffn_w_gate, ffn_w_up, ffn_w_down,
           moe_w_router, moe_w_gate, moe_w_up, moe_w_down):
    batch, seq, d = x.shape
    t = batch * seq
    xt = x.reshape(t, d)

    (q, k, v, g, la), (ffn_wg, ffn_wu, ffn_wd, gla_wo, moba_wqkv, moba_wo) = _gla_inproj(
        xt, norm_mix[0:1], gla_w_in, gla_w_gate2[0], gla_b_gate[0:1],
        ride_along=(ffn_w_gate, ffn_w_up, ffn_w_down, gla_w_out, moba_w_qkv, moba_w_out), tm=512)
    o, _ = _gla(q, k, v, g, la, gla_out_gain[0:1], (), batch, seq, blk=256, rows_per_step=4)
    xt = _ffn_residual(xt, o, gla_wo[0], norm_ffn[0:1], ffn_wg[0], ffn_wu[0], ffn_wd[0],
                       tm=512, pieces=MOE_PIECES + MOE_PIECES)

    q, k, v = _moba_qkv(xt, norm_mix[1:2], moba_wqkv[0], moba_q_gain[0:1], moba_k_gain[0:1], tm=512)
    o, (moe_wg, moe_wu, moe_wd) = _moba(q, k, v, (moe_w_gate[0], moe_w_up[0], moe_w_down[0]), batch, seq,
                                        heads_per_step=1)
    xt, h, info, rows, cnt = _router(xt, o, moba_wo[0], norm_ffn[1:2], moe_w_router[0], tr=1024)
    plan = _moe_plan(cnt, t)
    hs, gs = _moe_gather(plan, h, rows)
    y = _moe_experts(plan, hs, gs, moe_wg, moe_wu, moe_wd, pieces=MOE_PIECES + MOE_PIECES)
    xt = _moe_combine(plan, xt, info, y)
    return xt.reshape(batch, seq, d)
```

```python
import functools

import jax
import jax.numpy as jnp
from jax import lax
from jax.experimental import pallas as pl
from jax.experimental.pallas import tpu as pltpu

F32 = jnp.float32
BF16 = jnp.bfloat16
I32 = jnp.int32

EPS = 1e-6
NEG_INF = -1e30

D_MODEL = 1024
GLA_HEADS = 4
GLA_DK = 512
GLA_DV = 1024
GLA_HK = 128
GLA_HV = 256
GLA_RANK = 16
GLA_NORMALIZER = 16.0
GLA_CHUNK = 64

MOBA_HEADS = 8
MOBA_HD = 128
MOBA_BLOCK = 256
MOBA_TOPK = 3
MOBA_Q_SCALE = MOBA_HD ** -0.5 * 1.4426950408889634

N_EXPERTS = 8
TOP_K = 2

LANES = 128
SUBLANES = 8
BF16_ROWS = 16
ROUTER_ROWS = BF16_ROWS
VMEM_LIMIT = 48 * 1024 * 1024
VMEM_LIMIT_RESIDENT = 56 * 1024 * 1024

PROJ_TM = 512
ROUTER_TR = 1024
GLA_BLK = 256
GLA_ROWS_PER_STEP = 4
MOBA_HEADS_PER_STEP = 1

MOE_PIECES = (512, 512, 768)
MOE_TM = 256
MOE_TG = 128
GATHER_ROWS = 512
TOK_BLK = LANES
GATHER_WB = 6
COMB_TS = 256
COMB_STEP_TILES = 2
COMB_WC = 128
COMB_GROUP = 8
COMB_AHEAD_BUFS = 3
COMB_MAXW = 16


def _params(semantics, vmem_bytes=None):
    return pltpu.CompilerParams(dimension_semantics=semantics, vmem_limit_bytes=vmem_bytes)


def _rms(x, gain):
    y = x * lax.rsqrt(jnp.mean(x * x, axis=-1, keepdims=True) + EPS)
    return y * gain


def _dot(a, b):
    return jnp.dot(a, b, preferred_element_type=F32)


def _dot_nt(a, b):
    return lax.dot_general(a, b, (((1,), (1,)), ((), ())), preferred_element_type=F32)


def _dot_tn(a, b):
    return lax.dot_general(a, b, (((0,), (0,)), ((), ())), preferred_element_type=F32)


def _split(a):
    hi = a.astype(BF16)
    lo = (a - hi.astype(F32)).astype(BF16)
    return hi, lo


def _log_sigmoid(z):
    return jnp.minimum(z, 0.0) - jnp.log(1.0 + jnp.exp(-jnp.abs(z)))


def _silu(z):
    return z / (1.0 + jnp.exp(-z))


def _swiglu_pieces(h, wg_ref, wu_ref, wd_ref, widths):
    bounds = [sum(widths[:n]) for n in range(len(widths) + 1)]
    assert bounds[-1] == wg_ref.shape[1], (widths, wg_ref.shape)
    gate_up = lambda c: (_dot(h, wg_ref[:, bounds[c]:bounds[c + 1]]), _dot(h, wu_ref[:, bounds[c]:bounds[c + 1]]))
    out = None
    upcoming = gate_up(0)
    for c in range(len(widths)):
        g, u = upcoming
        if c + 1 < len(widths):
            upcoming = gate_up(c + 1)
        d = _dot((_silu(g) * u).astype(BF16), wd_ref[bounds[c]:bounds[c + 1], :])
        out = d if out is None else out + d
    return out


def _ride_along_specs(arrays, n_steps, step):
    specs, shapes = [], []
    for w in arrays:
        n_e, rows, cols = w.shape
        per_e = n_steps // n_e
        if rows % (per_e * BF16_ROWS) == 0:
            blk = (None, rows // per_e, cols)
            imap = lambda *ids, per_e=per_e: (step(*ids) // per_e, step(*ids) % per_e, 0)
        else:
            assert cols % (per_e * LANES) == 0, (w.shape, n_steps)
            blk = (None, rows, cols // per_e)
            imap = lambda *ids, per_e=per_e: (step(*ids) // per_e, 0, step(*ids) % per_e)
        specs.append(pl.BlockSpec(blk, imap))
        shapes.append(jax.ShapeDtypeStruct(w.shape, BF16))
    return specs, shapes


def _with_ride_along(body, n_in, n_out, n_cast):
    def kernel(*refs):
        ins, refs = refs[:n_in], refs[n_in:]
        cast_in, refs = refs[:n_cast], refs[n_cast:]
        outs, refs = refs[:n_out], refs[n_out:]
        cast_out, scratch = refs[:n_cast], refs[n_cast:]
        for src, dst in zip(cast_in, cast_out):
            dst[...] = src[...].astype(BF16)
        body(*ins, *outs, *scratch)
    return kernel


def _gla_inproj_kernel(x_next_ref, x_first_ref, gain_ref, w_ref, wg2_ref, b_ref,
                       q_ref, k_ref, v_ref, g_ref, la_ref, wb_ref, h_ref):
    i = pl.program_id(0)

    @pl.when(i == 0)
    def _():
        wb_ref[...] = w_ref[...].astype(BF16)
        h_ref[0] = _rms(x_first_ref[...], gain_ref[...]).astype(BF16)

    n_main = 2 * GLA_DK + 2 * GLA_DV

    def step(cur, nxt):
        h = h_ref[cur]
        h_ref[nxt] = _rms(x_next_ref[...], gain_ref[...]).astype(BF16)
        a_lr = _dot(h, wb_ref[:, n_main:])
        a_hi, a_lo = _split(a_lr)
        w_hi, w_lo = _split(wg2_ref[...])
        z = _dot(jnp.concatenate([a_hi, a_lo, a_hi], axis=1), jnp.concatenate([w_hi, w_hi, w_lo], axis=0))
        la_ref[...] = _log_sigmoid(z + b_ref[...]) / GLA_NORMALIZER
        q_ref[...] = _dot(h, wb_ref[:, 0:GLA_DK]).astype(BF16)
        k_ref[...] = _dot(h, wb_ref[:, GLA_DK:2 * GLA_DK]).astype(BF16)
        v_ref[...] = _dot(h, wb_ref[:, 2 * GLA_DK:2 * GLA_DK + GLA_DV]).astype(BF16)
        g_ref[...] = _dot(h, wb_ref[:, 2 * GLA_DK + GLA_DV:n_main]).astype(BF16)

    @pl.when(i % 2 == 0)
    def _():
        step(0, 1)

    @pl.when(i % 2 == 1)
    def _():
        step(1, 0)


def _gla_inproj(x, gain, w_in, w_gate2, b_gate, ride_along, tm):
    t = x.shape[0]
    n_steps = t // tm
    row = lambda i: (i, 0)
    fixed = lambda i: (0, 0)
    cast_specs, cast_shapes = _ride_along_specs(ride_along, n_steps, lambda i: i)
    outs = pl.pallas_call(
        _with_ride_along(_gla_inproj_kernel, 6, 5, len(ride_along)),
        grid=(n_steps,),
        in_specs=[
            pl.BlockSpec((tm, D_MODEL), lambda i: (jnp.minimum(i + 1, n_steps - 1), 0)),
            pl.BlockSpec((tm, D_MODEL), fixed),
            pl.BlockSpec((1, D_MODEL), fixed),
            pl.BlockSpec((None,) + w_in.shape[1:], lambda i: (0, 0, 0), pipeline_mode=pl.Buffered(1)),
            pl.BlockSpec((GLA_RANK, GLA_DK), fixed),
            pl.BlockSpec((1, GLA_DK), fixed),
        ] + cast_specs,
        out_specs=[
            pl.BlockSpec((tm, GLA_DK), row),
            pl.BlockSpec((tm, GLA_DK), row),
            pl.BlockSpec((tm, GLA_DV), row),
            pl.BlockSpec((tm, GLA_DV), row),
            pl.BlockSpec((tm, GLA_DK), row),
        ] + cast_specs,
        out_shape=[
            jax.ShapeDtypeStruct((t, GLA_DK), BF16),
            jax.ShapeDtypeStruct((t, GLA_DK), BF16),
            jax.ShapeDtypeStruct((t, GLA_DV), BF16),
            jax.ShapeDtypeStruct((t, GLA_DV), BF16),
            jax.ShapeDtypeStruct((t, GLA_DK), F32),
        ] + cast_shapes,
        scratch_shapes=[pltpu.VMEM(w_in.shape[1:], BF16), pltpu.VMEM((2, tm, D_MODEL), BF16)],
        compiler_params=_params(("arbitrary",), VMEM_LIMIT),
        name="gla_inproj",
    )(x, x, gain, w_in, w_gate2, b_gate, *ride_along)
    return outs[:5], outs[5:]


def _gla_kernel(q_ref, k_ref, v_ref, g_ref, la_ref, og_ref, o_ref, st_ref, *, n_chunks):
    @pl.when(pl.program_id(1) == 0)
    def _():
        st_ref[...] = jnp.zeros_like(st_ref)

    n_rows = q_ref.shape[0]
    blk = n_chunks * GLA_CHUNK
    r = lax.broadcasted_iota(I32, (blk, blk), 0)
    c = lax.broadcasted_iota(I32, (blk, blk), 1)
    causal = (r >= c) & (r // GLA_CHUNK == c // GLA_CHUNK)
    tril = causal.astype(BF16)
    og = og_ref[...]
    chunk_rows = [slice(ci * GLA_CHUNK, (ci + 1) * GLA_CHUNK) for ci in range(n_chunks)]
    hk = [slice(h * GLA_HK, (h + 1) * GLA_HK) for h in range(GLA_HEADS)]
    hv = [slice(h * GLA_HV, (h + 1) * GLA_HV) for h in range(GLA_HEADS)]

    q_dec, k_inv, k_end, chunk_decay = [], [], [], []
    for b in range(n_rows):
        la = la_ref[b]
        la_hi, la_mid = _split(la)
        la_lo = (la - la_hi.astype(F32) - la_mid.astype(F32)).astype(BF16)
        cum = _dot(tril, la_hi) + _dot(tril, la_mid) + _dot(tril, la_lo)
        ends = [cum[rows.stop - 1:rows.stop, :] for rows in chunk_rows]
        last = jnp.concatenate([jnp.broadcast_to(e, (GLA_CHUNK, GLA_DK)) for e in ends], axis=0)
        q = q_ref[b].astype(F32) * (GLA_HK ** -0.5)
        k = k_ref[b].astype(F32)
        q_dec.append((q * jnp.exp(cum)).astype(BF16))
        k_inv.append((k * jnp.exp(-cum)).astype(BF16))
        k_end.append((k * jnp.exp(last - cum)).astype(BF16))
        chunk_decay.append([jnp.exp(e) for e in ends])

    streams = [(b, h) for b in range(n_rows) for h in range(GLA_HEADS)]
    scores = [jnp.where(causal, _dot_nt(q_dec[b][:, hk[h]], k_inv[b][:, hk[h]]), 0.0).astype(BF16)
              for b, h in streams]
    state_in = [[_dot_tn(v_ref[b, rows, hv[h]], k_end[b][rows, hk[h]]) for rows in chunk_rows]
                for b, h in streams]
    intra = [_dot(scores[n], v_ref[b, :, hv[h]]) for n, (b, h) in enumerate(streams)]
    for n, (b, h) in enumerate(streams):
        st = st_ref[b, h]
        inter = []
        for ci, rows in enumerate(chunk_rows):
            inter.append(_dot_nt(q_dec[b][rows, hk[h]], st.astype(BF16)))
            st = st * chunk_decay[b][ci][:, hk[h]] + state_in[n][ci]
        st_ref[b, h] = st
        o = intra[n] + jnp.concatenate(inter, axis=0)
        o = o * lax.rsqrt(jnp.mean(o * o, axis=-1, keepdims=True) + EPS) * og
        o = o * _silu(g_ref[b, :, hv[h]].astype(F32))
        o_ref[b, :, hv[h]] = o.astype(BF16)


def _gla(q, k, v, g, la, out_gain, ride_along, batch, seq, blk, rows_per_step):
    nb = seq // blk
    nbt = batch // rows_per_step
    per_row = lambda a: a.reshape(batch, seq, a.shape[-1])
    spec = lambda width: pl.BlockSpec((rows_per_step, blk, width), lambda b, i: (b, i, 0))
    cast_specs, cast_shapes = _ride_along_specs(ride_along, nbt * nb, lambda b, i: b * nb + i)
    outs = pl.pallas_call(
        _with_ride_along(functools.partial(_gla_kernel, n_chunks=blk // GLA_CHUNK), 6, 1, len(ride_along)),
        grid=(nbt, nb),
        in_specs=[spec(GLA_DK), spec(GLA_DK), spec(GLA_DV), spec(GLA_DV), spec(GLA_DK),
                  pl.BlockSpec((1, GLA_HV), lambda b, i: (0, 0))] + cast_specs,
        out_specs=[spec(GLA_DV)] + cast_specs,
        out_shape=[jax.ShapeDtypeStruct((batch, seq, GLA_DV), BF16)] + cast_shapes,
        scratch_shapes=[pltpu.VMEM((rows_per_step, GLA_HEADS, GLA_HV, GLA_HK), F32)],
        compiler_params=_params(("arbitrary", "arbitrary"), VMEM_LIMIT),
        name="gla_core",
    )(per_row(q), per_row(k), per_row(v), per_row(g), per_row(la), out_gain, *ride_along)
    return outs[0].reshape(batch * seq, GLA_DV), outs[1:]


def _ffn_kernel(x_ref, o_ref, wo_ref, gain_ref, wg_ref, wu_ref, wd_ref, y_ref, *, pieces):
    x1 = x_ref[...] + _dot(o_ref[...], wo_ref[...])
    h = _rms(x1, gain_ref[...]).astype(BF16)
    y_ref[...] = x1 + _swiglu_pieces(h, wg_ref, wu_ref, wd_ref, pieces)


def _ffn_residual(x, o, w_out, gain, w_gate, w_up, w_down, tm, pieces):
    t, d = x.shape
    row = lambda i: (i, 0)
    resident = lambda a: pl.BlockSpec(a.shape, lambda i: (0, 0), pipeline_mode=pl.Buffered(1))
    return pl.pallas_call(
        functools.partial(_ffn_kernel, pieces=pieces),
        grid=(t // tm,),
        in_specs=[pl.BlockSpec((tm, d), row), pl.BlockSpec((tm, d), row), resident(w_out), resident(gain),
                  resident(w_gate), resident(w_up), resident(w_down)],
        out_specs=pl.BlockSpec((tm, d), row),
        out_shape=jax.ShapeDtypeStruct((t, d), F32),
        compiler_params=_params(("parallel",), VMEM_LIMIT_RESIDENT),
        name="ffn_swiglu",
    )(x, o, w_out, gain, w_gate, w_up, w_down)


def _moba_qkv_kernel(x_ref, gain_ref, w_ref, qg_ref, kg_ref, q_ref, k_ref, v_ref):
    h = _rms(x_ref[...], gain_ref[...]).astype(BF16)
    d = MOBA_HEADS * MOBA_HD
    qg = qg_ref[...]
    kg = kg_ref[...]
    q = _dot(h, w_ref[:, 0:d])
    k = _dot(h, w_ref[:, d:2 * d])
    for hd in range(MOBA_HEADS):
        cols = slice(hd * MOBA_HD, (hd + 1) * MOBA_HD)
        q_ref[:, cols] = (_rms(q[:, cols], qg) * MOBA_Q_SCALE).astype(BF16)
        k_ref[:, cols] = _rms(k[:, cols], kg).astype(BF16)
    v_ref[...] = _dot(h, w_ref[:, 2 * d:]).astype(BF16)


def _moba_qkv(x, gain, w, q_gain, k_gain, tm):
    t = x.shape[0]
    d = MOBA_HEADS * MOBA_HD
    row = lambda i: (i, 0)
    fixed = lambda i: (0, 0)
    return pl.pallas_call(
        _moba_qkv_kernel,
        grid=(t // tm,),
        in_specs=[
            pl.BlockSpec((tm, D_MODEL), row),
            pl.BlockSpec((1, D_MODEL), fixed),
            pl.BlockSpec((D_MODEL, 3 * d), fixed),
            pl.BlockSpec((1, MOBA_HD), fixed),
            pl.BlockSpec((1, MOBA_HD), fixed),
        ],
        out_specs=[pl.BlockSpec((tm, d), row)] * 3,
        out_shape=[jax.ShapeDtypeStruct((t, d), BF16)] * 3,
        compiler_params=_params(("parallel",), VMEM_LIMIT),
        name="moba_qkv",
    )(x, gain, w, q_gain, k_gain)


def _reduce_rows(x, op, stop=SUBLANES):
    while x.shape[0] > stop:
        half = x.shape[0] // 2
        x = op(x[:half], x[half:])
    return x


def _moba_kernel(q_ref, k_ref, v_ref, o_ref, *, n_blocks):
    seq = n_blocks * MOBA_BLOCK
    heads = range(q_ref.shape[1] // MOBA_HD)
    cols = [slice(hd * MOBA_HD, (hd + 1) * MOBA_HD) for hd in heads]
    n_pad = BF16_ROWS
    ones_rows = jnp.where(lax.broadcasted_iota(I32, (BF16_ROWS, seq), 0) == 0, 1.0, 0.0)
    v_t = [jnp.concatenate([v_ref[:, cols[hd]].astype(F32).T, ones_rows], axis=0).astype(BF16) for hd in heads]
    blk_of_key = lax.broadcasted_iota(I32, (n_pad, seq), 1) // MOBA_BLOCK
    blk_row = lax.broadcasted_iota(I32, (n_pad, seq), 0)
    indicator = jnp.where(blk_of_key == blk_row, 1.0 / MOBA_BLOCK, 0.0).astype(BF16)
    k_mean = [_split(_dot(indicator, k_ref[:, cols[hd]])) for hd in heads]

    blk_id = lax.broadcasted_iota(I32, (n_pad, MOBA_BLOCK), 0)
    key_pos = lax.broadcasted_iota(I32, (MOBA_BLOCK, MOBA_BLOCK), 0)
    query_pos = lax.broadcasted_iota(I32, (MOBA_BLOCK, MOBA_BLOCK), 1)
    causal = key_pos <= query_pos

    def masked_scores(hd, i):
        rows = slice(i * MOBA_BLOCK, (i + 1) * MOBA_BLOCK)
        n_keys = (i + 1) * MOBA_BLOCK
        q = q_ref[rows, cols[hd]]
        s = _dot_nt(k_ref[0:n_keys, cols[hd]], q)
        selected = None
        if i > MOBA_TOPK:
            km_hi, km_lo = k_mean[hd]
            gate = _dot_nt(km_hi, q) + _dot_nt(km_lo, q)
            gate = jnp.where(blk_id < i, gate, NEG_INF)
            rank = jnp.zeros_like(gate)
            for jp in range(i):
                row = gate[jp:jp + 1, :]
                beats = (row > gate) | ((row == gate) & (jp < blk_id))
                rank = rank + beats.astype(F32)
            selected = jnp.where((rank < MOBA_TOPK) & (blk_id < i), 1.0, 0.0)
        pieces = []
        top = None
        for j in range(i + 1):
            sj = s[j * MOBA_BLOCK:(j + 1) * MOBA_BLOCK, :]
            if j == i:
                sj = jnp.where(causal, sj, NEG_INF)
            elif selected is not None:
                sj = jnp.where(selected[j:j + 1, :] > 0.5, sj, NEG_INF)
            pieces.append(sj)
            folded = _reduce_rows(sj, jnp.maximum)
            top = folded if top is None else jnp.maximum(top, folded)
        return pieces, top

    def weighted_values(hd, i, p):
        o_t = _dot(v_t[hd][:, 0:(i + 1) * MOBA_BLOCK], p)
        rows = slice(i * MOBA_BLOCK, (i + 1) * MOBA_BLOCK)
        o_ref[rows, cols[hd]] = (o_t[:MOBA_HD, :] / o_t[MOBA_HD:MOBA_HD + 1, :]).T.astype(BF16)

    def softmax_numerators(scored):
        pieces, top = scored
        m = top.max(axis=0, keepdims=True)
        probs = [jnp.exp2(sj - m).astype(BF16) for sj in pieces]
        return probs[0] if len(probs) == 1 else jnp.concatenate(probs, axis=0)

    upcoming = [masked_scores(hd, 0) for hd in heads]
    pending = None
    for i in range(n_blocks):
        pieces = upcoming
        if i + 1 < n_blocks:
            upcoming = [masked_scores(hd, i + 1) for hd in heads]
        if pending is not None:
            for hd in heads:
                weighted_values(hd, i - 1, pending[hd])
        pending = [softmax_numerators(pieces[hd]) for hd in heads]
    for hd in heads:
        weighted_values(hd, n_blocks - 1, pending[hd])


def _moba(q, k, v, ride_along, batch, seq, heads_per_step):
    t, d = q.shape
    n_groups = MOBA_HEADS // heads_per_step
    spec = pl.BlockSpec((seq, heads_per_step * MOBA_HD), lambda b, h: (b, h))
    cast_specs, cast_shapes = _ride_along_specs(ride_along, batch * n_groups, lambda b, h: b * n_groups + h)
    outs = pl.pallas_call(
        _with_ride_along(functools.partial(_moba_kernel, n_blocks=seq // MOBA_BLOCK), 3, 1, len(ride_along)),
        grid=(batch, n_groups),
        in_specs=[spec, spec, spec] + cast_specs,
        out_specs=[spec] + cast_specs,
        out_shape=[jax.ShapeDtypeStruct((t, d), BF16)] + cast_shapes,
        compiler_params=_params(("arbitrary", "arbitrary"), VMEM_LIMIT),
        name="moba_attn",
    )(q, k, v, *ride_along)
    return outs[0], outs[1:]


def _router_kernel(x_ref, o_ref, wo_ref, gain_ref, w_ref, x2_ref, h_ref, info_ref, rows_ref, cnt_ref,
                   carry_ref, tri_ref,
                   *, region_rows):
    tr = x_ref.shape[0]

    @pl.when(pl.program_id(0) == 0)
    def _():
        carry_ref[...] = jnp.zeros_like(carry_ref)
        r = lax.broadcasted_iota(I32, (TOK_BLK, TOK_BLK), 0)
        c = lax.broadcasted_iota(I32, (TOK_BLK, TOK_BLK), 1)
        tri_ref[...] = (r < c).astype(BF16)

    x2 = x_ref[...] + _dot(o_ref[...], wo_ref[...])
    x2_ref[...] = x2
    h = _rms(x2, gain_ref[...])
    h_hi, h_lo = _split(h)
    h_ref[...] = h_hi
    n_e = ROUTER_ROWS
    wt = w_ref[...]
    passes = _dot_nt(wt, h_hi)
    logits = passes[:n_e] + passes[n_e:] + _dot_nt(wt[:n_e], h_lo)
    e = lax.broadcasted_iota(I32, logits.shape, 0)
    logits = jnp.where(e < N_EXPERTS, logits, -jnp.inf)
    m1 = logits.max(axis=0, keepdims=True)
    i1 = jnp.where(logits == m1, e, n_e).min(axis=0, keepdims=True)
    first = e == i1
    rest = jnp.where(first, -jnp.inf, logits)
    m2 = rest.max(axis=0, keepdims=True)
    i2 = jnp.where(rest == m2, e, n_e).min(axis=0, keepdims=True)
    second = e == i2
    e2 = jnp.exp(m2 - m1)
    denom = 1.0 + e2
    sel = jnp.where(first | second, 1.0, 0.0).astype(BF16)
    nb = tr // TOK_BLK
    blk_row = lax.broadcasted_iota(I32, (nb, tr), 0)
    blk_tok = lax.broadcasted_iota(I32, (nb, tr), 1) // TOK_BLK
    cnt = _dot_nt((blk_row == blk_tok).astype(BF16), sel)
    cnt_ref[...] = jnp.concatenate([cnt, jnp.zeros((nb, LANES - n_e), F32)], axis=1)
    offset = carry_ref[...]
    before = []
    for b in range(nb):
        sel_b = sel[:, b * TOK_BLK:(b + 1) * TOK_BLK]
        before.append(_dot(sel_b, tri_ref[...]) + offset)
        offset = offset + jnp.broadcast_to(sel_b.astype(F32).sum(axis=1, keepdims=True), offset.shape)
    carry_ref[...] = offset
    before = jnp.concatenate(before, axis=1)
    rank1 = jnp.where(first, before, 0.0).sum(axis=0, keepdims=True)
    rank2 = jnp.where(second, before, 0.0).sum(axis=0, keepdims=True)
    fields = (i1.astype(F32) * region_rows + rank1, i2.astype(F32) * region_rows + rank2,
              1.0 / denom, e2 / denom)
    for n, f in enumerate(fields):
        for b in range(nb):
            rows_ref[n, b:b + 1, :] = f[:, b * TOK_BLK:(b + 1) * TOK_BLK]
    info_t = jnp.concatenate(list(fields) + [jnp.zeros((LANES - len(fields), tr), F32)], axis=0)
    info_ref[...] = info_t.T[:, :len(fields)]


def _router(x, o, w_out, gain, w_router, tr):
    t = x.shape[0]
    nb = tr // TOK_BLK
    pad = ((0, ROUTER_ROWS - N_EXPERTS), (0, 0))
    w_hi = w_router.T.astype(BF16)
    w_lo = (w_router.T - w_hi.astype(F32)).astype(BF16)
    w_both = jnp.concatenate([jnp.pad(w_hi, pad), jnp.pad(w_lo, pad)], axis=0)
    row = lambda i: (i, 0)
    fixed = lambda i: (0, 0)
    return pl.pallas_call(
        functools.partial(_router_kernel, region_rows=float(t)),
        grid=(t // tr,),
        in_specs=[
            pl.BlockSpec((tr, D_MODEL), row),
            pl.BlockSpec((tr, D_MODEL), row),
            pl.BlockSpec((D_MODEL, D_MODEL), fixed),
            pl.BlockSpec((1, D_MODEL), fixed),
            pl.BlockSpec((2 * ROUTER_ROWS, D_MODEL), fixed),
        ],
        out_specs=[
            pl.BlockSpec((tr, D_MODEL), row),
            pl.BlockSpec((tr, D_MODEL), row),
            pl.BlockSpec((tr, 4), row),
            pl.BlockSpec((4, nb, TOK_BLK), lambda i: (0, i, 0)),
            pl.BlockSpec((nb, LANES), row),
        ],
        out_shape=[
            jax.ShapeDtypeStruct((t, D_MODEL), F32),
            jax.ShapeDtypeStruct((t, D_MODEL), BF16),
            jax.ShapeDtypeStruct((t, 4), F32),
            jax.ShapeDtypeStruct((4, t // TOK_BLK, TOK_BLK), F32),
            jax.ShapeDtypeStruct((t // TOK_BLK, LANES), F32),
        ],
        scratch_shapes=[pltpu.VMEM((ROUTER_ROWS, LANES), F32), pltpu.VMEM((TOK_BLK, TOK_BLK), BF16)],
        compiler_params=_params(("arbitrary",), VMEM_LIMIT_RESIDENT),
        name="moe_router",
    )(x, o, w_out, gain, w_both)


def _moe_plan(cnt, t):
    e_ids = jnp.arange(N_EXPERTS, dtype=I32)
    by_expert = lambda table, idx: jnp.sum(jnp.where(idx[..., None] == e_ids, table, 0), axis=-1)
    cnt_blk = cnt[:, :N_EXPERTS].astype(I32)
    cum_blk = jnp.cumsum(cnt_blk, axis=0)
    counts = cum_blk[-1]
    tiles_e = (counts + MOE_TM - 1) // MOE_TM
    tile_end = jnp.cumsum(tiles_e)
    tile_off = tile_end - tiles_e
    nt = TOP_K * t // MOE_TM + N_EXPERTS
    r = jnp.arange(nt, dtype=I32)
    valid = r < tile_end[-1]
    te = jnp.minimum(jnp.sum((r[:, None] >= tile_end[None, :]).astype(I32), axis=1), N_EXPERTS - 1)
    local = r - by_expert(tile_off, te)
    last_e = jnp.max(jnp.where(tiles_e > 0, e_ids, 0))
    tile_expert = jnp.where(valid, te, last_e)
    shift = e_ids * t - tile_off * MOE_TM

    per = MOE_TM // MOE_TG
    rg = jnp.arange(nt * per, dtype=I32)
    eg = jnp.repeat(te, per)
    k0 = jnp.repeat(local, per) * MOE_TM + (rg % per) * MOE_TG
    k1 = jnp.minimum(k0 + MOE_TG, by_expert(counts, eg))
    has = jnp.repeat(valid, per) & (k1 > k0)
    cum_e = by_expert(cum_blk[None, :, :], eg[:, None])
    s_lo = jnp.sum((cum_e <= k0[:, None]).astype(I32), axis=1)
    s_hi = jnp.sum((cum_e <= (k1 - 1)[:, None]).astype(I32), axis=1)
    g_nwin = jnp.where(has, (s_hi - s_lo) // GATHER_WB + 1, 0)
    g_slo = jnp.where(has, s_lo, 0)
    g_code = eg * t + k0

    nts = t // COMB_TS
    cnt_ts = cnt_blk.reshape(nts, COMB_TS // TOK_BLK, N_EXPERTS).sum(axis=1)
    seg_lo = (tile_off * MOE_TM)[None, :] + jnp.cumsum(cnt_ts, axis=0) - cnt_ts
    seg_hi = seg_lo + cnt_ts
    ws0 = (seg_lo // BF16_ROWS) * BF16_ROWS
    nw = jnp.where(cnt_ts > 0, (seg_hi - ws0 + COMB_WC - 1) // COMB_WC, 0)
    nw_end = jnp.cumsum(nw, axis=1)
    k = jnp.arange(COMB_MAXW, dtype=I32)
    ek = jnp.minimum(jnp.sum((k[None, :, None] >= nw_end[:, None, :]).astype(I32), axis=2), N_EXPERTS - 1)
    pick = lambda a: by_expert(a[:, None, :], ek)
    live = k[None, :] < nw_end[:, -1:]
    nominal = pick(ws0) + (k[None, :] - pick(nw_end - nw)) * COMB_WC
    shift_k = by_expert(shift, ek)
    c_start = jnp.minimum(nominal, by_expert(tile_end * MOE_TM, ek) - COMB_WC)
    c_off = shift_k + c_start
    c_lo = jnp.maximum(pick(seg_lo), nominal) + shift_k
    c_hi = jnp.minimum(pick(seg_hi), nominal + COMB_WC) + shift_k
    flat = lambda a: jnp.where(live, a, 0).reshape(-1)
    return dict(
        tile_expert=tile_expert, tile_valid=valid.astype(I32),
        g_code=g_code, g_slo=g_slo, g_nwin=g_nwin,
        c_start=flat(c_start), c_off=flat(c_off), c_lo=flat(c_lo), c_hi=flat(c_hi), c_nwin=nw_end[:, -1])


def _moe_gather_kernel(code_ref, slo_ref, nwin_ref, rows_ref, h_ref, hs_ref, gs_ref):
    n_blk = rows_ref.shape[1]
    n_sub = hs_ref.shape[0] // MOE_TG
    sub_ids = [pl.program_id(0) * n_sub + sub for sub in range(n_sub)]
    out_rows = [slice(sub * MOE_TG, (sub + 1) * MOE_TG) for sub in range(n_sub)]

    def window(r, k):
        row_id = (code_ref[r] + lax.broadcasted_iota(I32, (MOE_TG, TOK_BLK), 0)).astype(F32)
        nominal = slo_ref[r] + k * GATHER_WB
        sb = jnp.minimum(nominal, n_blk - GATHER_WB)
        fresh = sb + lax.broadcasted_iota(I32, (GATHER_WB, TOK_BLK), 0) >= nominal
        p1 = jnp.where(fresh, rows_ref[0, pl.ds(sb, GATHER_WB), :], -1.0)
        p2 = jnp.where(fresh, rows_ref[1, pl.ds(sb, GATHER_WB), :], -1.0)
        w1 = rows_ref[2, pl.ds(sb, GATHER_WB), :]
        w2 = rows_ref[3, pl.ds(sb, GATHER_WB), :]
        pieces = []
        g = jnp.zeros((MOE_TG, TOK_BLK), F32)
        for a in range(GATHER_WB):
            m1 = p1[a:a + 1, :] == row_id
            m2 = p2[a:a + 1, :] == row_id
            pieces.append(jnp.where(m1 | m2, 1.0, 0.0).astype(BF16))
            g = g + jnp.where(m1, w1[a:a + 1, :], 0.0) + jnp.where(m2, w2[a:a + 1, :], 0.0)
        onehot = jnp.concatenate(pieces, axis=1)
        return onehot, g.sum(axis=-1, keepdims=True), pl.multiple_of(sb * TOK_BLK, TOK_BLK)

    gathered = lambda onehot, tok0: _dot(onehot, h_ref[pl.ds(tok0, GATHER_WB * TOK_BLK), :]).astype(BF16)

    firsts = [window(r, 0) for r in sub_ids]
    for rows, (onehot, weight, tok0) in zip(out_rows, firsts):
        hs_ref[rows, :] = gathered(onehot, tok0)
        gs_ref[rows, :] = weight

    for r, rows in zip(sub_ids, out_rows):
        def more(k, carry, r=r, rows=rows):
            onehot, weight, tok0 = window(r, k)
            hs_ref[rows, :] += gathered(onehot, tok0)
            gs_ref[rows, :] += weight
            return carry

        lax.fori_loop(1, nwin_ref[r], more, 0)


def _moe_gather(plan, h, rows):
    t = h.shape[0]
    n_rows = plan["g_code"].shape[0] * MOE_TG
    n_blk = t // TOK_BLK
    grid_spec = pltpu.PrefetchScalarGridSpec(
        num_scalar_prefetch=3,
        grid=(n_rows // GATHER_ROWS,),
        in_specs=[
            pl.BlockSpec((4, n_blk, TOK_BLK), lambda r, *_: (0, 0, 0), pipeline_mode=pl.Buffered(1)),
            pl.BlockSpec((t, D_MODEL), lambda r, *_: (0, 0), pipeline_mode=pl.Buffered(1)),
        ],
        out_specs=[
            pl.BlockSpec((GATHER_ROWS, D_MODEL), lambda r, *_: (r, 0)),
            pl.BlockSpec((GATHER_ROWS, 1), lambda r, *_: (r, 0)),
        ],
    )
    return pl.pallas_call(
        _moe_gather_kernel,
        grid_spec=grid_spec,
        out_shape=[
            jax.ShapeDtypeStruct((n_rows, D_MODEL), BF16),
            jax.ShapeDtypeStruct((n_rows, 1), F32),
        ],
        compiler_params=_params(("parallel",), VMEM_LIMIT_RESIDENT),
        name="moe_gather",
    )(plan["g_code"], plan["g_slo"], plan["g_nwin"], rows, h)


def _moe_expert_kernel(te_ref, tv_ref, hs_ref, gs_ref, wg_ref, wu_ref, wd_ref, y_ref, acc_ref, *, nj, pieces):
    r = pl.program_id(0)
    j = pl.program_id(1)
    valid = tv_ref[r] > 0
    part = lambda: _swiglu_pieces(hs_ref[...], wg_ref, wu_ref, wd_ref, pieces)

    if nj == 1:
        @pl.when(valid)
        def _():
            y_ref[...] = (part() * gs_ref[...]).astype(BF16)
    else:
        @pl.when(valid & (j == 0))
        def _():
            acc_ref[...] = part()

        if nj > 2:
            @pl.when(valid & (j > 0) & (j < nj - 1))
            def _():
                acc_ref[...] += part()

        @pl.when(valid & (j == nj - 1))
        def _():
            y_ref[...] = ((acc_ref[...] + part()) * gs_ref[...]).astype(BF16)

    @pl.when(jnp.logical_not(valid) & (j == nj - 1))
    def _():
        y_ref[...] = jnp.zeros_like(y_ref)


def _moe_experts(plan, hs, gs, w_gate, w_up, w_down, pieces):
    tf = sum(pieces)
    n_rows, d = hs.shape
    ff = w_gate.shape[2]
    nj = ff // tf
    jj = lambda r, j, tv: jnp.where(tv[r] > 0, j, nj - 1)
    grid_spec = pltpu.PrefetchScalarGridSpec(
        num_scalar_prefetch=2,
        grid=(n_rows // MOE_TM, nj),
        in_specs=[
            pl.BlockSpec((MOE_TM, d), lambda r, j, te, tv: (r, 0)),
            pl.BlockSpec((MOE_TM, 1), lambda r, j, te, tv: (r, 0)),
            pl.BlockSpec((None, d, tf), lambda r, j, te, tv: (te[r], 0, jj(r, j, tv))),
            pl.BlockSpec((None, d, tf), lambda r, j, te, tv: (te[r], 0, jj(r, j, tv))),
            pl.BlockSpec((None, tf, d), lambda r, j, te, tv: (te[r], jj(r, j, tv), 0)),
        ],
        out_specs=pl.BlockSpec((MOE_TM, d), lambda r, j, te, tv: (r, 0)),
        scratch_shapes=[pltpu.VMEM((MOE_TM if nj > 1 else SUBLANES, d), F32)],
    )
    return pl.pallas_call(
        functools.partial(_moe_expert_kernel, nj=nj, pieces=pieces),
        grid_spec=grid_spec,
        out_shape=jax.ShapeDtypeStruct((n_rows, d), BF16),
        compiler_params=_params(("arbitrary", "arbitrary"), VMEM_LIMIT_RESIDENT),
        name="moe_experts",
    )(plan["tile_expert"], plan["tile_valid"], hs, gs, w_gate, w_up, w_down)


def _moe_combine_kernel(start_ref, off_ref, lo_ref, hi_ref, nwin_ref, x_ref, info_ref, y_hbm, o_ref,
                        ybuf, ybuf_more, sem, sem_more):
    step = pl.program_id(0)
    n_steps = pl.num_programs(0)
    per = COMB_STEP_TILES
    ring = COMB_AHEAD_BUFS * per

    def window_copy(idx, dst, dsem):
        start = pl.multiple_of(start_ref[idx], BF16_ROWS)
        return pltpu.make_async_copy(y_hbm.at[pl.ds(start, COMB_WC), :], dst, dsem)

    def first_group(tile, wait):
        buf = tile % ring
        for s in range(COMB_GROUP):
            @pl.when(s < nwin_ref[tile])
            def _():
                cp = window_copy(tile * COMB_MAXW + s, ybuf.at[buf, pl.ds(s * COMB_WC, COMB_WC), :],
                                 sem.at[buf, s])
                cp.wait() if wait else cp.start()

    def request(grid_step):
        for sub in range(per):
            first_group(grid_step * per + sub, wait=False)

    @pl.when(step == 0)
    def _():
        ybuf[...] = jnp.zeros_like(ybuf)
        ybuf_more[...] = jnp.zeros_like(ybuf_more)
        for ahead in range(COMB_AHEAD_BUFS - 1):
            @pl.when(ahead < n_steps)
            def _():
                request(ahead)

    @pl.when(step + COMB_AHEAD_BUFS - 1 < n_steps)
    def _():
        request(step + COMB_AHEAD_BUFS - 1)

    tiles = [step * per + sub for sub in range(per)]
    rows = [slice(sub * COMB_TS, (sub + 1) * COMB_TS) for sub in range(per)]
    for tile in tiles:
        first_group(tile, wait=True)

    lane = lax.broadcasted_iota(I32, (COMB_TS, COMB_WC), 1)
    lane_row = lax.broadcasted_iota(I32, (1, COMB_WC), 1)
    codes = [[jnp.broadcast_to(info_ref[r, n:n + 1].astype(I32), (COMB_TS, COMB_WC)) for n in range(TOP_K)]
             for r in rows]

    def onehot_group(sub, g):
        pieces = []
        for s in range(COMB_GROUP):
            idx = tiles[sub] * COMB_MAXW + g * COMB_GROUP + s
            off = off_ref[idx]
            in_span = (lane_row >= lo_ref[idx] - off) & (lane_row < hi_ref[idx] - off)
            target = lane + off
            hit = ((codes[sub][0] == target) | (codes[sub][1] == target)) & in_span
            pieces.append(jnp.where(hit, 1.0, 0.0).astype(BF16))
        return jnp.concatenate(pieces, axis=1)

    onehots = [onehot_group(sub, 0) for sub in range(per)]
    for sub in range(per):
        o_ref[rows[sub], :] = x_ref[rows[sub], :] + _dot(onehots[sub], ybuf[tiles[sub] % ring])

    for sub in range(per):
        n = nwin_ref[tiles[sub]]

        def more(g, carry, sub=sub, n=n):
            for wait in (False, True):
                for s in range(COMB_GROUP):
                    @pl.when(g * COMB_GROUP + s < n)
                    def _():
                        cp = window_copy(tiles[sub] * COMB_MAXW + g * COMB_GROUP + s,
                                         ybuf_more.at[pl.ds(s * COMB_WC, COMB_WC), :], sem_more.at[s])
                        cp.wait() if wait else cp.start()
            o_ref[rows[sub], :] += _dot(onehot_group(sub, g), ybuf_more[...])
            return carry

        lax.fori_loop(1, (n + COMB_GROUP - 1) // COMB_GROUP, more, 0)


def _moe_combine(plan, x, info, y):
    t, d = x.shape
    width = COMB_GROUP * COMB_WC
    step_rows = COMB_STEP_TILES * COMB_TS
    grid_spec = pltpu.PrefetchScalarGridSpec(
        num_scalar_prefetch=5,
        grid=(t // step_rows,),
        in_specs=[
            pl.BlockSpec((step_rows, d), lambda i, *_: (i, 0)),
            pl.BlockSpec((step_rows, 4), lambda i, *_: (i, 0)),
            pl.BlockSpec(memory_space=pl.ANY),
        ],
        out_specs=pl.BlockSpec((step_rows, d), lambda i, *_: (i, 0)),
        scratch_shapes=[
            pltpu.VMEM((COMB_AHEAD_BUFS * COMB_STEP_TILES, width, d), BF16),
            pltpu.VMEM((width, d), BF16),
            pltpu.SemaphoreType.DMA((COMB_AHEAD_BUFS * COMB_STEP_TILES, COMB_GROUP)),
            pltpu.SemaphoreType.DMA((COMB_GROUP,)),
        ],
    )
    return pl.pallas_call(
        _moe_combine_kernel,
        grid_spec=grid_spec,
        out_shape=jax.ShapeDtypeStruct((t, d), F32),
        compiler_params=_params(("arbitrary",), VMEM_LIMIT),
        name="moe_combine",
    )(plan["c_start"], plan["c_off"], plan["c_lo"], plan["c_hi"], plan["c_nwin"], x, info, y)


def kernel(x, norm_mix, norm_ffn, gla_w_in, gla_w_gate2, gla_b_gate, gla_out_gain, gla_w_out,
           moba_w_qkv, moba_q_gain, moba_k_gain, moba_w_out, ffn_w_gate, ffn_w_up, ffn_w_down,
           moe_w_router, moe_w_gate, moe_w_up, moe_w_down):
    batch, seq, d = x.shape
    t = batch * seq
    xt = x.reshape(t, d)

    (q, k, v, g, la), (ffn_wg, ffn_wu, ffn_wd, gla_wo, moba_wqkv, moba_wo) = _gla_inproj(
        xt, norm_mix[0:1], gla_w_in, gla_w_gate2[0], gla_b_gate[0:1],
        ride_along=(ffn_w_gate, ffn_w_up, ffn_w_down, gla_w_out, moba_w_qkv, moba_w_out), tm=PROJ_TM)
    o, _ = _gla(q, k, v, g, la, gla_out_gain[0:1], (), batch, seq, blk=GLA_BLK,
                rows_per_step=GLA_ROWS_PER_STEP)
    xt = _ffn_residual(xt, o, gla_wo[0], norm_ffn[0:1], ffn_wg[0], ffn_wu[0], ffn_wd[0],
                       tm=PROJ_TM, pieces=MOE_PIECES + MOE_PIECES)

    q, k, v = _moba_qkv(xt, norm_mix[1:2], moba_wqkv[0], moba_q_gain[0:1], moba_k_gain[0:1], tm=PROJ_TM)
    o, (moe_wg, moe_wu, moe_wd) = _moba(q, k, v, (moe_w_gate[0], moe_w_up[0], moe_w_down[0]), batch, seq,
                                        heads_per_step=MOBA_HEADS_PER_STEP)
    xt, h, info, rows, cnt = _router(xt, o, moba_wo[0], norm_ffn[1:2], moe_w_router[0], tr=ROUTER_TR)
    plan = _moe_plan(cnt, t)
    hs, gs = _moe_gather(plan, h, rows)
    y = _moe_experts(plan, hs, gs, moe_wg, moe_wu, moe_wd, pieces=MOE_PIECES + MOE_PIECES)
    xt = _moe_combine(plan, xt, info, y)
    return xt.reshape(batch, seq, d)
```

```python
import functools

import jax
import jax.numpy as jnp
from jax import lax
from jax.experimental import pallas as pl
from jax.experimental.pallas import tpu as pltpu

F32 = jnp.float32
BF16 = jnp.bfloat16
I32 = jnp.int32

EPS = 1e-6
NEG_INF = -1e30

D_MODEL = 1024
GLA_HEADS = 4
GLA_DK = 512
GLA_DV = 1024
GLA_HK = 128
GLA_HV = 256
GLA_RANK = 16
GLA_NORMALIZER = 16.0
GLA_CHUNK = 64

MOBA_HEADS = 8
MOBA_HD = 128
MOBA_BLOCK = 256
MOBA_TOPK = 3
MOBA_Q_SCALE = MOBA_HD ** -0.5 * 1.4426950408889634

N_EXPERTS = 8
TOP_K = 2

LANES = 128
SUBLANES = 8
BF16_ROWS = 16
ROUTER_ROWS = BF16_ROWS
VMEM_LIMIT = 48 * 1024 * 1024
VMEM_LIMIT_RESIDENT = 56 * 1024 * 1024

PROJ_TM = 512
ROUTER_TR = 1024
GLA_BLK = 256
GLA_ROWS_PER_STEP = 4
MOBA_HEADS_PER_STEP = 1

MOE_PIECES = (512, 512, 768)
MOE_TM = 512
MOE_TG = 128
GATHER_ROWS = 1024
TOK_BLK = LANES
GATHER_WB = 6
COMB_TS = 256
COMB_STEP_TILES = 2
COMB_WC = 128
COMB_GROUP = 8
COMB_AHEAD_BUFS = 3
COMB_MAXW = 16


def _params(semantics, vmem_bytes=None):
    return pltpu.CompilerParams(dimension_semantics=semantics, vmem_limit_bytes=vmem_bytes)


def _rms(x, gain):
    y = x * lax.rsqrt(jnp.mean(x * x, axis=-1, keepdims=True) + EPS)
    return y * gain


def _dot(a, b):
    return jnp.dot(a, b, preferred_element_type=F32)


def _dot_nt(a, b):
    return lax.dot_general(a, b, (((1,), (1,)), ((), ())), preferred_element_type=F32)


def _dot_tn(a, b):
    return lax.dot_general(a, b, (((0,), (0,)), ((), ())), preferred_element_type=F32)


def _split(a):
    hi = a.astype(BF16)
    lo = (a - hi.astype(F32)).astype(BF16)
    return hi, lo


def _log_sigmoid(z):
    return jnp.minimum(z, 0.0) - jnp.log(1.0 + jnp.exp(-jnp.abs(z)))


def _silu(z):
    return z / (1.0 + jnp.exp(-z))


def _swiglu_pieces(h, wg_ref, wu_ref, wd_ref, widths):
    bounds = [sum(widths[:n]) for n in range(len(widths) + 1)]
    assert bounds[-1] == wg_ref.shape[1], (widths, wg_ref.shape)
    gate_up = lambda c: (_dot(h, wg_ref[:, bounds[c]:bounds[c + 1]]), _dot(h, wu_ref[:, bounds[c]:bounds[c + 1]]))
    out = None
    upcoming = gate_up(0)
    for c in range(len(widths)):
        g, u = upcoming
        if c + 1 < len(widths):
            upcoming = gate_up(c + 1)
        d = _dot((_silu(g) * u).astype(BF16), wd_ref[bounds[c]:bounds[c + 1], :])
        out = d if out is None else out + d
    return out


def _ride_along_specs(arrays, n_steps, step):
    specs, shapes = [], []
    for w in arrays:
        n_e, rows, cols = w.shape
        per_e = n_steps // n_e
        if rows % (per_e * BF16_ROWS) == 0:
            blk = (None, rows // per_e, cols)
            imap = lambda *ids, per_e=per_e: (step(*ids) // per_e, step(*ids) % per_e, 0)
        else:
            assert cols % (per_e * LANES) == 0, (w.shape, n_steps)
            blk = (None, rows, cols // per_e)
            imap = lambda *ids, per_e=per_e: (step(*ids) // per_e, 0, step(*ids) % per_e)
        specs.append(pl.BlockSpec(blk, imap))
        shapes.append(jax.ShapeDtypeStruct(w.shape, BF16))
    return specs, shapes


def _with_ride_along(body, n_in, n_out, n_cast):
    def kernel(*refs):
        ins, refs = refs[:n_in], refs[n_in:]
        cast_in, refs = refs[:n_cast], refs[n_cast:]
        outs, refs = refs[:n_out], refs[n_out:]
        cast_out, scratch = refs[:n_cast], refs[n_cast:]
        for src, dst in zip(cast_in, cast_out):
            dst[...] = src[...].astype(BF16)
        body(*ins, *outs, *scratch)
    return kernel


def _gla_inproj_kernel(x_next_ref, x_first_ref, gain_ref, w_ref, wg2_ref, b_ref,
                       q_ref, k_ref, v_ref, g_ref, la_ref, wb_ref, h_ref):
    i = pl.program_id(0)

    @pl.when(i == 0)
    def _():
        wb_ref[...] = w_ref[...].astype(BF16)
        h_ref[0] = _rms(x_first_ref[...], gain_ref[...]).astype(BF16)

    n_main = 2 * GLA_DK + 2 * GLA_DV

    def step(cur, nxt):
        h = h_ref[cur]
        h_ref[nxt] = _rms(x_next_ref[...], gain_ref[...]).astype(BF16)
        a_lr = _dot(h, wb_ref[:, n_main:])
        a_hi, a_lo = _split(a_lr)
        w_hi, w_lo = _split(wg2_ref[...])
        z = _dot(jnp.concatenate([a_hi, a_lo, a_hi], axis=1), jnp.concatenate([w_hi, w_hi, w_lo], axis=0))
        la_ref[...] = _log_sigmoid(z + b_ref[...]) / GLA_NORMALIZER
        q_ref[...] = _dot(h, wb_ref[:, 0:GLA_DK]).astype(BF16)
        k_ref[...] = _dot(h, wb_ref[:, GLA_DK:2 * GLA_DK]).astype(BF16)
        v_ref[...] = _dot(h, wb_ref[:, 2 * GLA_DK:2 * GLA_DK + GLA_DV]).astype(BF16)
        g_ref[...] = _dot(h, wb_ref[:, 2 * GLA_DK + GLA_DV:n_main]).astype(BF16)

    @pl.when(i % 2 == 0)
    def _():
        step(0, 1)

    @pl.when(i % 2 == 1)
    def _():
        step(1, 0)


def _gla_inproj(x, gain, w_in, w_gate2, b_gate, ride_along, tm):
    t = x.shape[0]
    n_steps = t // tm
    row = lambda i: (i, 0)
    fixed = lambda i: (0, 0)
    cast_specs, cast_shapes = _ride_along_specs(ride_along, n_steps, lambda i: i)
    outs = pl.pallas_call(
        _with_ride_along(_gla_inproj_kernel, 6, 5, len(ride_along)),
        grid=(n_steps,),
        in_specs=[
            pl.BlockSpec((tm, D_MODEL), lambda i: (jnp.minimum(i + 1, n_steps - 1), 0)),
            pl.BlockSpec((tm, D_MODEL), fixed),
            pl.BlockSpec((1, D_MODEL), fixed),
            pl.BlockSpec((None,) + w_in.shape[1:], lambda i: (0, 0, 0), pipeline_mode=pl.Buffered(1)),
            pl.BlockSpec((GLA_RANK, GLA_DK), fixed),
            pl.BlockSpec((1, GLA_DK), fixed),
        ] + cast_specs,
        out_specs=[
            pl.BlockSpec((tm, GLA_DK), row),
            pl.BlockSpec((tm, GLA_DK), row),
            pl.BlockSpec((tm, GLA_DV), row),
            pl.BlockSpec((tm, GLA_DV), row),
            pl.BlockSpec((tm, GLA_DK), row),
        ] + cast_specs,
        out_shape=[
            jax.ShapeDtypeStruct((t, GLA_DK), BF16),
            jax.ShapeDtypeStruct((t, GLA_DK), BF16),
            jax.ShapeDtypeStruct((t, GLA_DV), BF16),
            jax.ShapeDtypeStruct((t, GLA_DV), BF16),
            jax.ShapeDtypeStruct((t, GLA_DK), F32),
        ] + cast_shapes,
        scratch_shapes=[pltpu.VMEM(w_in.shape[1:], BF16), pltpu.VMEM((2, tm, D_MODEL), BF16)],
        compiler_params=_params(("arbitrary",), VMEM_LIMIT),
        name="gla_inproj",
    )(x, x, gain, w_in, w_gate2, b_gate, *ride_along)
    return outs[:5], outs[5:]


def _gla_kernel(q_ref, k_ref, v_ref, g_ref, la_ref, og_ref, o_ref, st_ref, *, n_chunks):
    @pl.when(pl.program_id(1) == 0)
    def _():
        st_ref[...] = jnp.zeros_like(st_ref)

    n_rows = q_ref.shape[0]
    blk = n_chunks * GLA_CHUNK
    r = lax.broadcasted_iota(I32, (blk, blk), 0)
    c = lax.broadcasted_iota(I32, (blk, blk), 1)
    causal = (r >= c) & (r // GLA_CHUNK == c // GLA_CHUNK)
    tril = causal.astype(BF16)
    og = og_ref[...]
    chunk_rows = [slice(ci * GLA_CHUNK, (ci + 1) * GLA_CHUNK) for ci in range(n_chunks)]
    hk = [slice(h * GLA_HK, (h + 1) * GLA_HK) for h in range(GLA_HEADS)]
    hv = [slice(h * GLA_HV, (h + 1) * GLA_HV) for h in range(GLA_HEADS)]

    q_dec, k_inv, k_end, chunk_decay = [], [], [], []
    for b in range(n_rows):
        la = la_ref[b]
        la_hi, la_mid = _split(la)
        la_lo = (la - la_hi.astype(F32) - la_mid.astype(F32)).astype(BF16)
        cum = _dot(tril, la_hi) + _dot(tril, la_mid) + _dot(tril, la_lo)
        ends = [cum[rows.stop - 1:rows.stop, :] for rows in chunk_rows]
        last = jnp.concatenate([jnp.broadcast_to(e, (GLA_CHUNK, GLA_DK)) for e in ends], axis=0)
        q = q_ref[b].astype(F32) * (GLA_HK ** -0.5)
        k = k_ref[b].astype(F32)
        q_dec.append((q * jnp.exp(cum)).astype(BF16))
        k_inv.append((k * jnp.exp(-cum)).astype(BF16))
        k_end.append((k * jnp.exp(last - cum)).astype(BF16))
        chunk_decay.append([jnp.exp(e) for e in ends])

    streams = [(b, h) for b in range(n_rows) for h in range(GLA_HEADS)]
    scores = [jnp.where(causal, _dot_nt(q_dec[b][:, hk[h]], k_inv[b][:, hk[h]]), 0.0).astype(BF16)
              for b, h in streams]
    state_in = [[_dot_tn(v_ref[b, rows, hv[h]], k_end[b][rows, hk[h]]) for rows in chunk_rows]
                for b, h in streams]
    intra = [_dot(scores[n], v_ref[b, :, hv[h]]) for n, (b, h) in enumerate(streams)]
    for n, (b, h) in enumerate(streams):
        st = st_ref[b, h]
        inter = []
        for ci, rows in enumerate(chunk_rows):
            inter.append(_dot_nt(q_dec[b][rows, hk[h]], st.astype(BF16)))
            st = st * chunk_decay[b][ci][:, hk[h]] + state_in[n][ci]
        st_ref[b, h] = st
        o = intra[n] + jnp.concatenate(inter, axis=0)
        o = o * lax.rsqrt(jnp.mean(o * o, axis=-1, keepdims=True) + EPS) * og
        o = o * _silu(g_ref[b, :, hv[h]].astype(F32))
        o_ref[b, :, hv[h]] = o.astype(BF16)


def _gla(q, k, v, g, la, out_gain, ride_along, batch, seq, blk, rows_per_step):
    nb = seq // blk
    nbt = batch // rows_per_step
    per_row = lambda a: a.reshape(batch, seq, a.shape[-1])
    spec = lambda width: pl.BlockSpec((rows_per_step, blk, width), lambda b, i: (b, i, 0))
    cast_specs, cast_shapes = _ride_along_specs(ride_along, nbt * nb, lambda b, i: b * nb + i)
    outs = pl.pallas_call(
        _with_ride_along(functools.partial(_gla_kernel, n_chunks=blk // GLA_CHUNK), 6, 1, len(ride_along)),
        grid=(nbt, nb),
        in_specs=[spec(GLA_DK), spec(GLA_DK), spec(GLA_DV), spec(GLA_DV), spec(GLA_DK),
                  pl.BlockSpec((1, GLA_HV), lambda b, i: (0, 0))] + cast_specs,
        out_specs=[spec(GLA_DV)] + cast_specs,
        out_shape=[jax.ShapeDtypeStruct((batch, seq, GLA_DV), BF16)] + cast_shapes,
        scratch_shapes=[pltpu.VMEM((rows_per_step, GLA_HEADS, GLA_HV, GLA_HK), F32)],
        compiler_params=_params(("arbitrary", "arbitrary"), VMEM_LIMIT),
        name="gla_core",
    )(per_row(q), per_row(k), per_row(v), per_row(g), per_row(la), out_gain, *ride_along)
    return outs[0].reshape(batch * seq, GLA_DV), outs[1:]


def _ffn_kernel(x_ref, o_ref, wo_ref, gain_ref, wg_ref, wu_ref, wd_ref, y_ref, *, pieces):
    x1 = x_ref[...] + _dot(o_ref[...], wo_ref[...])
    h = _rms(x1, gain_ref[...]).astype(BF16)
    y_ref[...] = x1 + _swiglu_pieces(h, wg_ref, wu_ref, wd_ref, pieces)


def _ffn_residual(x, o, w_out, gain, w_gate, w_up, w_down, tm, pieces):
    t, d = x.shape
    row = lambda i: (i, 0)
    resident = lambda a: pl.BlockSpec(a.shape, lambda i: (0, 0), pipeline_mode=pl.Buffered(1))
    return pl.pallas_call(
        functools.partial(_ffn_kernel, pieces=pieces),
        grid=(t // tm,),
        in_specs=[pl.BlockSpec((tm, d), row), pl.BlockSpec((tm, d), row), resident(w_out), resident(gain),
                  resident(w_gate), resident(w_up), resident(w_down)],
        out_specs=pl.BlockSpec((tm, d), row),
        out_shape=jax.ShapeDtypeStruct((t, d), F32),
        compiler_params=_params(("parallel",), VMEM_LIMIT_RESIDENT),
        name="ffn_swiglu",
    )(x, o, w_out, gain, w_gate, w_up, w_down)


def _moba_qkv_kernel(x_ref, gain_ref, w_ref, qg_ref, kg_ref, q_ref, k_ref, v_ref):
    h = _rms(x_ref[...], gain_ref[...]).astype(BF16)
    d = MOBA_HEADS * MOBA_HD
    qg = qg_ref[...]
    kg = kg_ref[...]
    q = _dot(h, w_ref[:, 0:d])
    k = _dot(h, w_ref[:, d:2 * d])
    for hd in range(MOBA_HEADS):
        cols = slice(hd * MOBA_HD, (hd + 1) * MOBA_HD)
        q_ref[:, cols] = (_rms(q[:, cols], qg) * MOBA_Q_SCALE).astype(BF16)
        k_ref[:, cols] = _rms(k[:, cols], kg).astype(BF16)
    v_ref[...] = _dot(h, w_ref[:, 2 * d:]).astype(BF16)


def _moba_qkv(x, gain, w, q_gain, k_gain, tm):
    t = x.shape[0]
    d = MOBA_HEADS * MOBA_HD
    row = lambda i: (i, 0)
    fixed = lambda i: (0, 0)
    return pl.pallas_call(
        _moba_qkv_kernel,
        grid=(t // tm,),
        in_specs=[
            pl.BlockSpec((tm, D_MODEL), row),
            pl.BlockSpec((1, D_MODEL), fixed),
            pl.BlockSpec((D_MODEL, 3 * d), fixed),
            pl.BlockSpec((1, MOBA_HD), fixed),
            pl.BlockSpec((1, MOBA_HD), fixed),
        ],
        out_specs=[pl.BlockSpec((tm, d), row)] * 3,
        out_shape=[jax.ShapeDtypeStruct((t, d), BF16)] * 3,
        compiler_params=_params(("parallel",), VMEM_LIMIT),
        name="moba_qkv",
    )(x, gain, w, q_gain, k_gain)


def _reduce_rows(x, op, stop=SUBLANES):
    while x.shape[0] > stop:
        half = x.shape[0] // 2
        x = op(x[:half], x[half:])
    return x


def _moba_kernel(q_ref, k_ref, v_ref, o_ref, *, n_blocks):
    seq = n_blocks * MOBA_BLOCK
    heads = range(q_ref.shape[1] // MOBA_HD)
    cols = [slice(hd * MOBA_HD, (hd + 1) * MOBA_HD) for hd in heads]
    n_pad = BF16_ROWS
    ones_rows = jnp.where(lax.broadcasted_iota(I32, (BF16_ROWS, seq), 0) == 0, 1.0, 0.0)
    v_t = [jnp.concatenate([v_ref[:, cols[hd]].astype(F32).T, ones_rows], axis=0).astype(BF16) for hd in heads]
    blk_of_key = lax.broadcasted_iota(I32, (n_pad, seq), 1) // MOBA_BLOCK
    blk_row = lax.broadcasted_iota(I32, (n_pad, seq), 0)
    indicator = jnp.where(blk_of_key == blk_row, 1.0 / MOBA_BLOCK, 0.0).astype(BF16)
    k_mean = [_split(_dot(indicator, k_ref[:, cols[hd]])) for hd in heads]

    blk_id = lax.broadcasted_iota(I32, (n_pad, MOBA_BLOCK), 0)
    key_pos = lax.broadcasted_iota(I32, (MOBA_BLOCK, MOBA_BLOCK), 0)
    query_pos = lax.broadcasted_iota(I32, (MOBA_BLOCK, MOBA_BLOCK), 1)
    causal = key_pos <= query_pos

    def masked_scores(hd, i):
        rows = slice(i * MOBA_BLOCK, (i + 1) * MOBA_BLOCK)
        n_keys = (i + 1) * MOBA_BLOCK
        q = q_ref[rows, cols[hd]]
        s = _dot_nt(k_ref[0:n_keys, cols[hd]], q)
        selected = None
        if i > MOBA_TOPK:
            km_hi, km_lo = k_mean[hd]
            gate = _dot_nt(km_hi, q) + _dot_nt(km_lo, q)
            gate = jnp.where(blk_id < i, gate, NEG_INF)
            rank = jnp.zeros_like(gate)
            for jp in range(i):
                row = gate[jp:jp + 1, :]
                beats = (row > gate) | ((row == gate) & (jp < blk_id))
                rank = rank + beats.astype(F32)
            selected = jnp.where((rank < MOBA_TOPK) & (blk_id < i), 1.0, 0.0)
        pieces = []
        top = None
        for j in range(i + 1):
            sj = s[j * MOBA_BLOCK:(j + 1) * MOBA_BLOCK, :]
            if j == i:
                sj = jnp.where(causal, sj, NEG_INF)
            elif selected is not None:
                sj = jnp.where(selected[j:j + 1, :] > 0.5, sj, NEG_INF)
            pieces.append(sj)
            folded = _reduce_rows(sj, jnp.maximum)
            top = folded if top is None else jnp.maximum(top, folded)
        return pieces, top

    def weighted_values(hd, i, p):
        o_t = _dot(v_t[hd][:, 0:(i + 1) * MOBA_BLOCK], p)
        rows = slice(i * MOBA_BLOCK, (i + 1) * MOBA_BLOCK)
        o_ref[rows, cols[hd]] = (o_t[:MOBA_HD, :] / o_t[MOBA_HD:MOBA_HD + 1, :]).T.astype(BF16)

    def softmax_numerators(scored):
        pieces, top = scored
        m = top.max(axis=0, keepdims=True)
        probs = [jnp.exp2(sj - m).astype(BF16) for sj in pieces]
        return probs[0] if len(probs) == 1 else jnp.concatenate(probs, axis=0)

    upcoming = [masked_scores(hd, 0) for hd in heads]
    pending = None
    for i in range(n_blocks):
        pieces = upcoming
        if i + 1 < n_blocks:
            upcoming = [masked_scores(hd, i + 1) for hd in heads]
        if pending is not None:
            for hd in heads:
                weighted_values(hd, i - 1, pending[hd])
        pending = [softmax_numerators(pieces[hd]) for hd in heads]
    for hd in heads:
        weighted_values(hd, n_blocks - 1, pending[hd])


def _moba(q, k, v, ride_along, batch, seq, heads_per_step):
    t, d = q.shape
    n_groups = MOBA_HEADS // heads_per_step
    spec = pl.BlockSpec((seq, heads_per_step * MOBA_HD), lambda b, h: (b, h))
    cast_specs, cast_shapes = _ride_along_specs(ride_along, batch * n_groups, lambda b, h: b * n_groups + h)
    outs = pl.pallas_call(
        _with_ride_along(functools.partial(_moba_kernel, n_blocks=seq // MOBA_BLOCK), 3, 1, len(ride_along)),
        grid=(batch, n_groups),
        in_specs=[spec, spec, spec] + cast_specs,
        out_specs=[spec] + cast_specs,
        out_shape=[jax.ShapeDtypeStruct((t, d), BF16)] + cast_shapes,
        compiler_params=_params(("arbitrary", "arbitrary"), VMEM_LIMIT),
        name="moba_attn",
    )(q, k, v, *ride_along)
    return outs[0], outs[1:]


def _router_kernel(x_ref, o_ref, wo_ref, gain_ref, w_ref, x2_ref, h_ref, info_ref, rows_ref, cnt_ref,
                   carry_ref, tri_ref,
                   *, region_rows):
    tr = x_ref.shape[0]

    @pl.when(pl.program_id(0) == 0)
    def _():
        carry_ref[...] = jnp.zeros_like(carry_ref)
        r = lax.broadcasted_iota(I32, (TOK_BLK, TOK_BLK), 0)
        c = lax.broadcasted_iota(I32, (TOK_BLK, TOK_BLK), 1)
        tri_ref[...] = (r < c).astype(BF16)

    x2 = x_ref[...] + _dot(o_ref[...], wo_ref[...])
    x2_ref[...] = x2
    h = _rms(x2, gain_ref[...])
    h_hi, h_lo = _split(h)
    h_ref[...] = h_hi
    n_e = ROUTER_ROWS
    wt = w_ref[...]
    passes = _dot_nt(wt, h_hi)
    logits = passes[:n_e] + passes[n_e:] + _dot_nt(wt[:n_e], h_lo)
    e = lax.broadcasted_iota(I32, logits.shape, 0)
    logits = jnp.where(e < N_EXPERTS, logits, -jnp.inf)
    m1 = logits.max(axis=0, keepdims=True)
    i1 = jnp.where(logits == m1, e, n_e).min(axis=0, keepdims=True)
    first = e == i1
    rest = jnp.where(first, -jnp.inf, logits)
    m2 = rest.max(axis=0, keepdims=True)
    i2 = jnp.where(rest == m2, e, n_e).min(axis=0, keepdims=True)
    second = e == i2
    e2 = jnp.exp(m2 - m1)
    denom = 1.0 + e2
    sel = jnp.where(first | second, 1.0, 0.0).astype(BF16)
    nb = tr // TOK_BLK
    blk_row = lax.broadcasted_iota(I32, (nb, tr), 0)
    blk_tok = lax.broadcasted_iota(I32, (nb, tr), 1) // TOK_BLK
    cnt = _dot_nt((blk_row == blk_tok).astype(BF16), sel)
    cnt_ref[...] = jnp.concatenate([cnt, jnp.zeros((nb, LANES - n_e), F32)], axis=1)
    offset = carry_ref[...]
    before = []
    for b in range(nb):
        sel_b = sel[:, b * TOK_BLK:(b + 1) * TOK_BLK]
        before.append(_dot(sel_b, tri_ref[...]) + offset)
        offset = offset + jnp.broadcast_to(sel_b.astype(F32).sum(axis=1, keepdims=True), offset.shape)
    carry_ref[...] = offset
    before = jnp.concatenate(before, axis=1)
    rank1 = jnp.where(first, before, 0.0).sum(axis=0, keepdims=True)
    rank2 = jnp.where(second, before, 0.0).sum(axis=0, keepdims=True)
    fields = (i1.astype(F32) * region_rows + rank1, i2.astype(F32) * region_rows + rank2,
              1.0 / denom, e2 / denom)
    for n, f in enumerate(fields):
        for b in range(nb):
            rows_ref[n, b:b + 1, :] = f[:, b * TOK_BLK:(b + 1) * TOK_BLK]
    info_t = jnp.concatenate(list(fields) + [jnp.zeros((LANES - len(fields), tr), F32)], axis=0)
    info_ref[...] = info_t.T[:, :len(fields)]


def _router(x, o, w_out, gain, w_router, tr):
    t = x.shape[0]
    nb = tr // TOK_BLK
    pad = ((0, ROUTER_ROWS - N_EXPERTS), (0, 0))
    w_hi = w_router.T.astype(BF16)
    w_lo = (w_router.T - w_hi.astype(F32)).astype(BF16)
    w_both = jnp.concatenate([jnp.pad(w_hi, pad), jnp.pad(w_lo, pad)], axis=0)
    row = lambda i: (i, 0)
    fixed = lambda i: (0, 0)
    return pl.pallas_call(
        functools.partial(_router_kernel, region_rows=float(t)),
        grid=(t // tr,),
        in_specs=[
            pl.BlockSpec((tr, D_MODEL), row),
            pl.BlockSpec((tr, D_MODEL), row),
            pl.BlockSpec((D_MODEL, D_MODEL), fixed),
            pl.BlockSpec((1, D_MODEL), fixed),
            pl.BlockSpec((2 * ROUTER_ROWS, D_MODEL), fixed),
        ],
        out_specs=[
            pl.BlockSpec((tr, D_MODEL), row),
            pl.BlockSpec((tr, D_MODEL), row),
            pl.BlockSpec((tr, 4), row),
            pl.BlockSpec((4, nb, TOK_BLK), lambda i: (0, i, 0)),
            pl.BlockSpec((nb, LANES), row),
        ],
        out_shape=[
            jax.ShapeDtypeStruct((t, D_MODEL), F32),
            jax.ShapeDtypeStruct((t, D_MODEL), BF16),
            jax.ShapeDtypeStruct((t, 4), F32),
            jax.ShapeDtypeStruct((4, t // TOK_BLK, TOK_BLK), F32),
            jax.ShapeDtypeStruct((t // TOK_BLK, LANES), F32),
        ],
        scratch_shapes=[pltpu.VMEM((ROUTER_ROWS, LANES), F32), pltpu.VMEM((TOK_BLK, TOK_BLK), BF16)],
        compiler_params=_params(("arbitrary",), VMEM_LIMIT_RESIDENT),
        name="moe_router",
    )(x, o, w_out, gain, w_both)


def _moe_plan(cnt, t):
    e_ids = jnp.arange(N_EXPERTS, dtype=I32)
    by_expert = lambda table, idx: jnp.sum(jnp.where(idx[..., None] == e_ids, table, 0), axis=-1)
    cnt_blk = cnt[:, :N_EXPERTS].astype(I32)
    cum_blk = jnp.cumsum(cnt_blk, axis=0)
    counts = cum_blk[-1]
    tiles_e = (counts + MOE_TM - 1) // MOE_TM
    tile_end = jnp.cumsum(tiles_e)
    tile_off = tile_end - tiles_e
    nt = TOP_K * t // MOE_TM + N_EXPERTS
    r = jnp.arange(nt, dtype=I32)
    valid = r < tile_end[-1]
    te = jnp.minimum(jnp.sum((r[:, None] >= tile_end[None, :]).astype(I32), axis=1), N_EXPERTS - 1)
    local = r - by_expert(tile_off, te)
    last_e = jnp.max(jnp.where(tiles_e > 0, e_ids, 0))
    tile_expert = jnp.where(valid, te, last_e)
    shift = e_ids * t - tile_off * MOE_TM

    per = MOE_TM // MOE_TG
    rg = jnp.arange(nt * per, dtype=I32)
    eg = jnp.repeat(te, per)
    k0 = jnp.repeat(local, per) * MOE_TM + (rg % per) * MOE_TG
    k1 = jnp.minimum(k0 + MOE_TG, by_expert(counts, eg))
    has = jnp.repeat(valid, per) & (k1 > k0)
    cum_e = by_expert(cum_blk[None, :, :], eg[:, None])
    s_lo = jnp.sum((cum_e <= k0[:, None]).astype(I32), axis=1)
    s_hi = jnp.sum((cum_e <= (k1 - 1)[:, None]).astype(I32), axis=1)
    g_nwin = jnp.where(has, (s_hi - s_lo) // GATHER_WB + 1, 0)
    g_slo = jnp.where(has, s_lo, 0)
    g_code = eg * t + k0

    nts = t // COMB_TS
    cnt_ts = cnt_blk.reshape(nts, COMB_TS // TOK_BLK, N_EXPERTS).sum(axis=1)
    seg_lo = (tile_off * MOE_TM)[None, :] + jnp.cumsum(cnt_ts, axis=0) - cnt_ts
    seg_hi = seg_lo + cnt_ts
    ws0 = (seg_lo // BF16_ROWS) * BF16_ROWS
    nw = jnp.where(cnt_ts > 0, (seg_hi - ws0 + COMB_WC - 1) // COMB_WC, 0)
    nw_end = jnp.cumsum(nw, axis=1)
    k = jnp.arange(COMB_MAXW, dtype=I32)
    ek = jnp.minimum(jnp.sum((k[None, :, None] >= nw_end[:, None, :]).astype(I32), axis=2), N_EXPERTS - 1)
    pick = lambda a: by_expert(a[:, None, :], ek)
    live = k[None, :] < nw_end[:, -1:]
    nominal = pick(ws0) + (k[None, :] - pick(nw_end - nw)) * COMB_WC
    shift_k = by_expert(shift, ek)
    c_start = jnp.minimum(nominal, by_expert(tile_end * MOE_TM, ek) - COMB_WC)
    c_off = shift_k + c_start
    c_lo = jnp.maximum(pick(seg_lo), nominal) + shift_k
    c_hi = jnp.minimum(pick(seg_hi), nominal + COMB_WC) + shift_k
    flat = lambda a: jnp.where(live, a, 0).reshape(-1)
    return dict(
        tile_expert=tile_expert, tile_valid=valid.astype(I32),
        g_code=g_code, g_slo=g_slo, g_nwin=g_nwin,
        c_start=flat(c_start), c_off=flat(c_off), c_lo=flat(c_lo), c_hi=flat(c_hi), c_nwin=nw_end[:, -1])


def _moe_gather_kernel(code_ref, slo_ref, nwin_ref, rows_ref, h_ref, hs_ref, gs_ref):
    n_blk = rows_ref.shape[1]
    n_sub = hs_ref.shape[0] // MOE_TG
    sub_ids = [pl.program_id(0) * n_sub + sub for sub in range(n_sub)]
    out_rows = [slice(sub * MOE_TG, (sub + 1) * MOE_TG) for sub in range(n_sub)]

    def window(r, k):
        row_id = (code_ref[r] + lax.broadcasted_iota(I32, (MOE_TG, TOK_BLK), 0)).astype(F32)
        nominal = slo_ref[r] + k * GATHER_WB
        sb = jnp.minimum(nominal, n_blk - GATHER_WB)
        fresh = sb + lax.broadcasted_iota(I32, (GATHER_WB, TOK_BLK), 0) >= nominal
        p1 = jnp.where(fresh, rows_ref[0, pl.ds(sb, GATHER_WB), :], -1.0)
        p2 = jnp.where(fresh, rows_ref[1, pl.ds(sb, GATHER_WB), :], -1.0)
        w1 = rows_ref[2, pl.ds(sb, GATHER_WB), :]
        w2 = rows_ref[3, pl.ds(sb, GATHER_WB), :]
        pieces = []
        g = jnp.zeros((MOE_TG, TOK_BLK), F32)
        for a in range(GATHER_WB):
            m1 = p1[a:a + 1, :] == row_id
            m2 = p2[a:a + 1, :] == row_id
            pieces.append(jnp.where(m1 | m2, 1.0, 0.0).astype(BF16))
            g = g + jnp.where(m1, w1[a:a + 1, :], 0.0) + jnp.where(m2, w2[a:a + 1, :], 0.0)
        onehot = jnp.concatenate(pieces, axis=1)
        return onehot, g.sum(axis=-1, keepdims=True), pl.multiple_of(sb * TOK_BLK, TOK_BLK)

    gathered = lambda onehot, tok0: _dot(onehot, h_ref[pl.ds(tok0, GATHER_WB * TOK_BLK), :]).astype(BF16)

    firsts = [window(r, 0) for r in sub_ids]
    for rows, (onehot, weight, tok0) in zip(out_rows, firsts):
        hs_ref[rows, :] = gathered(onehot, tok0)
        gs_ref[rows, :] = weight

    for r, rows in zip(sub_ids, out_rows):
        def more(k, carry, r=r, rows=rows):
            onehot, weight, tok0 = window(r, k)
            hs_ref[rows, :] += gathered(onehot, tok0)
            gs_ref[rows, :] += weight
            return carry

        lax.fori_loop(1, nwin_ref[r], more, 0)


def _moe_gather(plan, h, rows):
    t = h.shape[0]
    n_rows = plan["g_code"].shape[0] * MOE_TG
    n_blk = t // TOK_BLK
    grid_spec = pltpu.PrefetchScalarGridSpec(
        num_scalar_prefetch=3,
        grid=(n_rows // GATHER_ROWS,),
        in_specs=[
            pl.BlockSpec((4, n_blk, TOK_BLK), lambda r, *_: (0, 0, 0), pipeline_mode=pl.Buffered(1)),
            pl.BlockSpec((t, D_MODEL), lambda r, *_: (0, 0), pipeline_mode=pl.Buffered(1)),
        ],
        out_specs=[
            pl.BlockSpec((GATHER_ROWS, D_MODEL), lambda r, *_: (r, 0)),
            pl.BlockSpec((GATHER_ROWS, 1), lambda r, *_: (r, 0)),
        ],
    )
    return pl.pallas_call(
        _moe_gather_kernel,
        grid_spec=grid_spec,
        out_shape=[
            jax.ShapeDtypeStruct((n_rows, D_MODEL), BF16),
            jax.ShapeDtypeStruct((n_rows, 1), F32),
        ],
        compiler_params=_params(("parallel",), VMEM_LIMIT_RESIDENT),
        name="moe_gather",
    )(plan["g_code"], plan["g_slo"], plan["g_nwin"], rows, h)


def _moe_expert_kernel(te_ref, tv_ref, hs_ref, gs_ref, wg_ref, wu_ref, wd_ref, y_ref, acc_ref, *, nj, pieces):
    r = pl.program_id(0)
    j = pl.program_id(1)
    valid = tv_ref[r] > 0
    part = lambda: _swiglu_pieces(hs_ref[...], wg_ref, wu_ref, wd_ref, pieces)

    if nj == 1:
        @pl.when(valid)
        def _():
            y_ref[...] = (part() * gs_ref[...]).astype(BF16)
    else:
        @pl.when(valid & (j == 0))
        def _():
            acc_ref[...] = part()

        if nj > 2:
            @pl.when(valid & (j > 0) & (j < nj - 1))
            def _():
                acc_ref[...] += part()

        @pl.when(valid & (j == nj - 1))
        def _():
            y_ref[...] = ((acc_ref[...] + part()) * gs_ref[...]).astype(BF16)

    @pl.when(jnp.logical_not(valid) & (j == nj - 1))
    def _():
        y_ref[...] = jnp.zeros_like(y_ref)


def _moe_experts(plan, hs, gs, w_gate, w_up, w_down, pieces):
    tf = sum(pieces)
    n_rows, d = hs.shape
    ff = w_gate.shape[2]
    nj = ff // tf
    jj = lambda r, j, tv: jnp.where(tv[r] > 0, j, nj - 1)
    grid_spec = pltpu.PrefetchScalarGridSpec(
        num_scalar_prefetch=2,
        grid=(n_rows // MOE_TM, nj),
        in_specs=[
            pl.BlockSpec((MOE_TM, d), lambda r, j, te, tv: (r, 0)),
            pl.BlockSpec((MOE_TM, 1), lambda r, j, te, tv: (r, 0)),
            pl.BlockSpec((None, d, tf), lambda r, j, te, tv: (te[r], 0, jj(r, j, tv))),
            pl.BlockSpec((None, d, tf), lambda r, j, te, tv: (te[r], 0, jj(r, j, tv))),
            pl.BlockSpec((None, tf, d), lambda r, j, te, tv: (te[r], jj(r, j, tv), 0)),
        ],
        out_specs=pl.BlockSpec((MOE_TM, d), lambda r, j, te, tv: (r, 0)),
        scratch_shapes=[pltpu.VMEM((MOE_TM if nj > 1 else SUBLANES, d), F32)],
    )
    return pl.pallas_call(
        functools.partial(_moe_expert_kernel, nj=nj, pieces=pieces),
        grid_spec=grid_spec,
        out_shape=jax.ShapeDtypeStruct((n_rows, d), BF16),
        compiler_params=_params(("arbitrary", "arbitrary"), VMEM_LIMIT_RESIDENT),
        name="moe_experts",
    )(plan["tile_expert"], plan["tile_valid"], hs, gs, w_gate, w_up, w_down)


def _moe_combine_kernel(start_ref, off_ref, lo_ref, hi_ref, nwin_ref, x_ref, info_ref, y_hbm, o_ref,
                        ybuf, ybuf_more, sem, sem_more):
    step = pl.program_id(0)
    n_steps = pl.num_programs(0)
    per = COMB_STEP_TILES
    ring = COMB_AHEAD_BUFS * per

    def window_copy(idx, dst, dsem):
        start = pl.multiple_of(start_ref[idx], BF16_ROWS)
        return pltpu.make_async_copy(y_hbm.at[pl.ds(start, COMB_WC), :], dst, dsem)

    def first_group(tile, wait):
        buf = tile % ring
        for s in range(COMB_GROUP):
            @pl.when(s < nwin_ref[tile])
            def _():
                cp = window_copy(tile * COMB_MAXW + s, ybuf.at[buf, pl.ds(s * COMB_WC, COMB_WC), :],
                                 sem.at[buf, s])
                cp.wait() if wait else cp.start()

    def request(grid_step):
        for sub in range(per):
            first_group(grid_step * per + sub, wait=False)

    @pl.when(step == 0)
    def _():
        ybuf[...] = jnp.zeros_like(ybuf)
        ybuf_more[...] = jnp.zeros_like(ybuf_more)
        for ahead in range(COMB_AHEAD_BUFS - 1):
            @pl.when(ahead < n_steps)
            def _():
                request(ahead)

    @pl.when(step + COMB_AHEAD_BUFS - 1 < n_steps)
    def _():
        request(step + COMB_AHEAD_BUFS - 1)

    tiles = [step * per + sub for sub in range(per)]
    rows = [slice(sub * COMB_TS, (sub + 1) * COMB_TS) for sub in range(per)]
    for tile in tiles:
        first_group(tile, wait=True)

    lane = lax.broadcasted_iota(I32, (COMB_TS, COMB_WC), 1)
    lane_row = lax.broadcasted_iota(I32, (1, COMB_WC), 1)
    codes = [[jnp.broadcast_to(info_ref[r, n:n + 1].astype(I32), (COMB_TS, COMB_WC)) for n in range(TOP_K)]
             for r in rows]

    def onehot_group(sub, g):
        pieces = []
        for s in range(COMB_GROUP):
            idx = tiles[sub] * COMB_MAXW + g * COMB_GROUP + s
            off = off_ref[idx]
            in_span = (lane_row >= lo_ref[idx] - off) & (lane_row < hi_ref[idx] - off)
            target = lane + off
            hit = ((codes[sub][0] == target) | (codes[sub][1] == target)) & in_span
            pieces.append(jnp.where(hit, 1.0, 0.0).astype(BF16))
        return jnp.concatenate(pieces, axis=1)

    onehots = [onehot_group(sub, 0) for sub in range(per)]
    for sub in range(per):
        o_ref[rows[sub], :] = x_ref[rows[sub], :] + _dot(onehots[sub], ybuf[tiles[sub] % ring])

    for sub in range(per):
        n = nwin_ref[tiles[sub]]

        def more(g, carry, sub=sub, n=n):
            for wait in (False, True):
                for s in range(COMB_GROUP):
                    @pl.when(g * COMB_GROUP + s < n)
                    def _():
                        cp = window_copy(tiles[sub] * COMB_MAXW + g * COMB_GROUP + s,
                                         ybuf_more.at[pl.ds(s * COMB_WC, COMB_WC), :], sem_more.at[s])
                        cp.wait() if wait else cp.start()
            o_ref[rows[sub], :] += _dot(onehot_group(sub, g), ybuf_more[...])
            return carry

        lax.fori_loop(1, (n + COMB_GROUP - 1) // COMB_GROUP, more, 0)


def _moe_combine(plan, x, info, y):
    t, d = x.shape
    width = COMB_GROUP * COMB_WC
    step_rows = COMB_STEP_TILES * COMB_TS
    grid_spec = pltpu.PrefetchScalarGridSpec(
        num_scalar_prefetch=5,
        grid=(t // step_rows,),
        in_specs=[
            pl.BlockSpec((step_rows, d), lambda i, *_: (i, 0)),
            pl.BlockSpec((step_rows, 4), lambda i, *_: (i, 0)),
            pl.BlockSpec(memory_space=pl.ANY),
        ],
        out_specs=pl.BlockSpec((step_rows, d), lambda i, *_: (i, 0)),
        scratch_shapes=[
            pltpu.VMEM((COMB_AHEAD_BUFS * COMB_STEP_TILES, width, d), BF16),
            pltpu.VMEM((width, d), BF16),
            pltpu.SemaphoreType.DMA((COMB_AHEAD_BUFS * COMB_STEP_TILES, COMB_GROUP)),
            pltpu.SemaphoreType.DMA((COMB_GROUP,)),
        ],
    )
    return pl.pallas_call(
        _moe_combine_kernel,
        grid_spec=grid_spec,
        out_shape=jax.ShapeDtypeStruct((t, d), F32),
        compiler_params=_params(("arbitrary",), VMEM_LIMIT),
        name="moe_combine",
    )(plan["c_start"], plan["c_off"], plan["c_lo"], plan["c_hi"], plan["c_nwin"], x, info, y)


def kernel(x, norm_mix, norm_ffn, gla_w_in, gla_w_gate2, gla_b_gate, gla_out_gain, gla_w_out,
           moba_w_qkv, moba_q_gain, moba_k_gain, moba_w_out, ffn_w_gate, ffn_w_up, ffn_w_down,
           moe_w_router, moe_w_gate, moe_w_up, moe_w_down):
    batch, seq, d = x.shape
    t = batch * seq
    xt = x.reshape(t, d)

    (q, k, v, g, la), (ffn_wg, ffn_wu, ffn_wd, gla_wo, moba_wqkv, moba_wo) = _gla_inproj(
        xt, norm_mix[0:1], gla_w_in, gla_w_gate2[0], gla_b_gate[0:1],
        ride_along=(ffn_w_gate, ffn_w_up, ffn_w_down, gla_w_out, moba_w_qkv, moba_w_out), tm=PROJ_TM)
    o, _ = _gla(q, k, v, g, la, gla_out_gain[0:1], (), batch, seq, blk=GLA_BLK,
                rows_per_step=GLA_ROWS_PER_STEP)
    xt = _ffn_residual(xt, o, gla_wo[0], norm_ffn[0:1], ffn_wg[0], ffn_wu[0], ffn_wd[0],
                       tm=PROJ_TM, pieces=MOE_PIECES + MOE_PIECES)

    q, k, v = _moba_qkv(xt, norm_mix[1:2], moba_wqkv[0], moba_q_gain[0:1], moba_k_gain[0:1], tm=PROJ_TM)
    o, (moe_wg, moe_wu, moe_wd) = _moba(q, k, v, (moe_w_gate[0], moe_w_up[0], moe_w_down[0]), batch, seq,
                                        heads_per_step=MOBA_HEADS_PER_STEP)
    xt, h, info, rows, cnt = _router(xt, o, moba_wo[0], norm_ffn[1:2], moe_w_router[0], tr=ROUTER_TR)
    plan = _moe_plan(cnt, t)
    hs, gs = _moe_gather(plan, h, rows)
    y = _moe_experts(plan, hs, gs, moe_wg, moe_wu, moe_wd, pieces=MOE_PIECES + MOE_PIECES)
    xt = _moe_combine(plan, xt, info, y)
    return xt.reshape(batch, seq, d)
```

```python
import functools

import jax
import jax.numpy as jnp
from jax import lax
from jax.experimental import pallas as pl
from jax.experimental.pallas import tpu as pltpu

F32 = jnp.float32
BF16 = jnp.bfloat16
I32 = jnp.int32

EPS = 1e-6
NEG_INF = -1e30

D_MODEL = 1024
GLA_HEADS = 4
GLA_DK = 512
GLA_DV = 1024
GLA_HK = 128
GLA_HV = 256
GLA_RANK = 16
GLA_NORMALIZER = 16.0
GLA_CHUNK = 64

MOBA_HEADS = 8
MOBA_HD = 128
MOBA_BLOCK = 256
MOBA_TOPK = 3
MOBA_Q_SCALE = MOBA_HD ** -0.5 * 1.4426950408889634

N_EXPERTS = 8
TOP_K = 2

LANES = 128
SUBLANES = 8
BF16_ROWS = 16
ROUTER_ROWS = BF16_ROWS
VMEM_LIMIT = 48 * 1024 * 1024
VMEM_LIMIT_RESIDENT = 56 * 1024 * 1024

PROJ_TM = 512
ROUTER_TR = 1024
GLA_BLK = 256
GLA_ROWS_PER_STEP = 4
MOBA_HEADS_PER_STEP = 1

MOE_PIECES = (512, 512, 768)
MOE_TM = 512
MOE_TG = 128
GATHER_ROWS = 1024
TOK_BLK = LANES
GATHER_WB = 6
COMB_TS = 256
COMB_STEP_TILES = 2
COMB_WC = 128
COMB_GROUP = 8
COMB_AHEAD_BUFS = 3
COMB_MAXW = 16


def _params(semantics, vmem_bytes=None):
    return pltpu.CompilerParams(dimension_semantics=semantics, vmem_limit_bytes=vmem_bytes)


def _rms(x, gain):
    y = x * lax.rsqrt(jnp.mean(x * x, axis=-1, keepdims=True) + EPS)
    return y * gain


def _dot(a, b):
    return jnp.dot(a, b, preferred_element_type=F32)


def _dot_nt(a, b):
    return lax.dot_general(a, b, (((1,), (1,)), ((), ())), preferred_element_type=F32)


def _dot_tn(a, b):
    return lax.dot_general(a, b, (((0,), (0,)), ((), ())), preferred_element_type=F32)


def _split(a):
    hi = a.astype(BF16)
    lo = (a - hi.astype(F32)).astype(BF16)
    return hi, lo


def _log_sigmoid(z):
    return jnp.minimum(z, 0.0) - jnp.log(1.0 + jnp.exp(-jnp.abs(z)))


def _silu(z):
    return z / (1.0 + jnp.exp(-z))


def _swiglu_pieces(h, wg_ref, wu_ref, wd_ref, widths):
    bounds = [sum(widths[:n]) for n in range(len(widths) + 1)]
    assert bounds[-1] == wg_ref.shape[1], (widths, wg_ref.shape)
    gate_up = lambda c: (_dot(h, wg_ref[:, bounds[c]:bounds[c + 1]]), _dot(h, wu_ref[:, bounds[c]:bounds[c + 1]]))
    out = None
    upcoming = gate_up(0)
    for c in range(len(widths)):
        g, u = upcoming
        if c + 1 < len(widths):
            upcoming = gate_up(c + 1)
        d = _dot((_silu(g) * u).astype(BF16), wd_ref[bounds[c]:bounds[c + 1], :])
        out = d if out is None else out + d
    return out


def _ride_along_specs(arrays, n_steps, step):
    specs, shapes = [], []
    for w in arrays:
        n_e, rows, cols = w.shape
        per_e = n_steps // n_e
        if rows % (per_e * BF16_ROWS) == 0:
            blk = (None, rows // per_e, cols)
            imap = lambda *ids, per_e=per_e: (step(*ids) // per_e, step(*ids) % per_e, 0)
        else:
            assert cols % (per_e * LANES) == 0, (w.shape, n_steps)
            blk = (None, rows, cols // per_e)
            imap = lambda *ids, per_e=per_e: (step(*ids) // per_e, 0, step(*ids) % per_e)
        specs.append(pl.BlockSpec(blk, imap))
        shapes.append(jax.ShapeDtypeStruct(w.shape, BF16))
    return specs, shapes


def _with_ride_along(body, n_in, n_out, n_cast):
    def kernel(*refs):
        ins, refs = refs[:n_in], refs[n_in:]
        cast_in, refs = refs[:n_cast], refs[n_cast:]
        outs, refs = refs[:n_out], refs[n_out:]
        cast_out, scratch = refs[:n_cast], refs[n_cast:]
        for src, dst in zip(cast_in, cast_out):
            dst[...] = src[...].astype(BF16)
        body(*ins, *outs, *scratch)
    return kernel


def _gla_inproj_kernel(x_next_ref, x_first_ref, gain_ref, w_ref, wg2_ref, b_ref,
                       q_ref, k_ref, v_ref, g_ref, la_ref, wb_ref, h_ref):
    i = pl.program_id(0)

    @pl.when(i == 0)
    def _():
        wb_ref[...] = w_ref[...].astype(BF16)
        h_ref[0] = _rms(x_first_ref[...], gain_ref[...]).astype(BF16)

    n_main = 2 * GLA_DK + 2 * GLA_DV

    def step(cur, nxt):
        h = h_ref[cur]
        h_ref[nxt] = _rms(x_next_ref[...], gain_ref[...]).astype(BF16)
        a_lr = _dot(h, wb_ref[:, n_main:])
        a_hi, a_lo = _split(a_lr)
        w_hi, w_lo = _split(wg2_ref[...])
        z = _dot(jnp.concatenate([a_hi, a_lo, a_hi], axis=1), jnp.concatenate([w_hi, w_hi, w_lo], axis=0))
        la_ref[...] = _log_sigmoid(z + b_ref[...]) / GLA_NORMALIZER
        q_ref[...] = _dot(h, wb_ref[:, 0:GLA_DK]).astype(BF16)
        k_ref[...] = _dot(h, wb_ref[:, GLA_DK:2 * GLA_DK]).astype(BF16)
        v_ref[...] = _dot(h, wb_ref[:, 2 * GLA_DK:2 * GLA_DK + GLA_DV]).astype(BF16)
        g_ref[...] = _dot(h, wb_ref[:, 2 * GLA_DK + GLA_DV:n_main]).astype(BF16)

    @pl.when(i % 2 == 0)
    def _():
        step(0, 1)

    @pl.when(i % 2 == 1)
    def _():
        step(1, 0)


def _gla_inproj(x, gain, w_in, w_gate2, b_gate, ride_along, tm):
    t = x.shape[0]
    n_steps = t // tm
    row = lambda i: (i, 0)
    fixed = lambda i: (0, 0)
    cast_specs, cast_shapes = _ride_along_specs(ride_along, n_steps, lambda i: i)
    outs = pl.pallas_call(
        _with_ride_along(_gla_inproj_kernel, 6, 5, len(ride_along)),
        grid=(n_steps,),
        in_specs=[
            pl.BlockSpec((tm, D_MODEL), lambda i: (jnp.minimum(i + 1, n_steps - 1), 0)),
            pl.BlockSpec((tm, D_MODEL), fixed),
            pl.BlockSpec((1, D_MODEL), fixed),
            pl.BlockSpec((None,) + w_in.shape[1:], lambda i: (0, 0, 0), pipeline_mode=pl.Buffered(1)),
            pl.BlockSpec((GLA_RANK, GLA_DK), fixed),
            pl.BlockSpec((1, GLA_DK), fixed),
        ] + cast_specs,
        out_specs=[
            pl.BlockSpec((tm, GLA_DK), row),
            pl.BlockSpec((tm, GLA_DK), row),
            pl.BlockSpec((tm, GLA_DV), row),
            pl.BlockSpec((tm, GLA_DV), row),
            pl.BlockSpec((tm, GLA_DK), row),
        ] + cast_specs,
        out_shape=[
            jax.ShapeDtypeStruct((t, GLA_DK), BF16),
            jax.ShapeDtypeStruct((t, GLA_DK), BF16),
            jax.ShapeDtypeStruct((t, GLA_DV), BF16),
            jax.ShapeDtypeStruct((t, GLA_DV), BF16),
            jax.ShapeDtypeStruct((t, GLA_DK), F32),
        ] + cast_shapes,
        scratch_shapes=[pltpu.VMEM(w_in.shape[1:], BF16), pltpu.VMEM((2, tm, D_MODEL), BF16)],
        compiler_params=_params(("arbitrary",), VMEM_LIMIT),
        name="gla_inproj",
    )(x, x, gain, w_in, w_gate2, b_gate, *ride_along)
    return outs[:5], outs[5:]


def _gla_kernel(q_ref, k_ref, v_ref, g_ref, la_ref, og_ref, o_ref, st_ref, *, n_chunks):
    @pl.when(pl.program_id(1) == 0)
    def _():
        st_ref[...] = jnp.zeros_like(st_ref)

    n_rows = q_ref.shape[0]
    blk = n_chunks * GLA_CHUNK
    r = lax.broadcasted_iota(I32, (blk, blk), 0)
    c = lax.broadcasted_iota(I32, (blk, blk), 1)
    causal = (r >= c) & (r // GLA_CHUNK == c // GLA_CHUNK)
    tril = causal.astype(BF16)
    og = og_ref[...]
    chunk_rows = [slice(ci * GLA_CHUNK, (ci + 1) * GLA_CHUNK) for ci in range(n_chunks)]
    hk = [slice(h * GLA_HK, (h + 1) * GLA_HK) for h in range(GLA_HEADS)]
    hv = [slice(h * GLA_HV, (h + 1) * GLA_HV) for h in range(GLA_HEADS)]

    q_dec, k_inv, k_end, chunk_decay = [], [], [], []
    for b in range(n_rows):
        la = la_ref[b]
        la_hi, la_mid = _split(la)
        la_lo = (la - la_hi.astype(F32) - la_mid.astype(F32)).astype(BF16)
        cum = _dot(tril, la_hi) + _dot(tril, la_mid) + _dot(tril, la_lo)
        ends = [cum[rows.stop - 1:rows.stop, :] for rows in chunk_rows]
        last = jnp.concatenate([jnp.broadcast_to(e, (GLA_CHUNK, GLA_DK)) for e in ends], axis=0)
        q = q_ref[b].astype(F32) * (GLA_HK ** -0.5)
        k = k_ref[b].astype(F32)
        q_dec.append((q * jnp.exp(cum)).astype(BF16))
        k_inv.append((k * jnp.exp(-cum)).astype(BF16))
        k_end.append((k * jnp.exp(last - cum)).astype(BF16))
        chunk_decay.append([jnp.exp(e) for e in ends])

    streams = [(b, h) for b in range(n_rows) for h in range(GLA_HEADS)]
    scores = [jnp.where(causal, _dot_nt(q_dec[b][:, hk[h]], k_inv[b][:, hk[h]]), 0.0).astype(BF16)
              for b, h in streams]
    state_in = [[_dot_tn(v_ref[b, rows, hv[h]], k_end[b][rows, hk[h]]) for rows in chunk_rows]
                for b, h in streams]
    intra = [_dot(scores[n], v_ref[b, :, hv[h]]) for n, (b, h) in enumerate(streams)]
    for n, (b, h) in enumerate(streams):
        st = st_ref[b, h]
        inter = []
        for ci, rows in enumerate(chunk_rows):
            inter.append(_dot_nt(q_dec[b][rows, hk[h]], st.astype(BF16)))
            st = st * chunk_decay[b][ci][:, hk[h]] + state_in[n][ci]
        st_ref[b, h] = st
        o = intra[n] + jnp.concatenate(inter, axis=0)
        o = o * lax.rsqrt(jnp.mean(o * o, axis=-1, keepdims=True) + EPS) * og
        o = o * _silu(g_ref[b, :, hv[h]].astype(F32))
        o_ref[b, :, hv[h]] = o.astype(BF16)


def _gla(q, k, v, g, la, out_gain, ride_along, batch, seq, blk, rows_per_step):
    nb = seq // blk
    nbt = batch // rows_per_step
    per_row = lambda a: a.reshape(batch, seq, a.shape[-1])
    spec = lambda width: pl.BlockSpec((rows_per_step, blk, width), lambda b, i: (b, i, 0))
    cast_specs, cast_shapes = _ride_along_specs(ride_along, nbt * nb, lambda b, i: b * nb + i)
    outs = pl.pallas_call(
        _with_ride_along(functools.partial(_gla_kernel, n_chunks=blk // GLA_CHUNK), 6, 1, len(ride_along)),
        grid=(nbt, nb),
        in_specs=[spec(GLA_DK), spec(GLA_DK), spec(GLA_DV), spec(GLA_DV), spec(GLA_DK),
                  pl.BlockSpec((1, GLA_HV), lambda b, i: (0, 0))] + cast_specs,
        out_specs=[spec(GLA_DV)] + cast_specs,
        out_shape=[jax.ShapeDtypeStruct((batch, seq, GLA_DV), BF16)] + cast_shapes,
        scratch_shapes=[pltpu.VMEM((rows_per_step, GLA_HEADS, GLA_HV, GLA_HK), F32)],
        compiler_params=_params(("arbitrary", "arbitrary"), VMEM_LIMIT),
        name="gla_core",
    )(per_row(q), per_row(k), per_row(v), per_row(g), per_row(la), out_gain, *ride_along)
    return outs[0].reshape(batch * seq, GLA_DV), outs[1:]


def _ffn_kernel(x_ref, o_ref, wo_ref, gain_ref, wg_ref, wu_ref, wd_ref, y_ref, *, pieces):
    x1 = x_ref[...] + _dot(o_ref[...], wo_ref[...])
    h = _rms(x1, gain_ref[...]).astype(BF16)
    y_ref[...] = x1 + _swiglu_pieces(h, wg_ref, wu_ref, wd_ref, pieces)


def _ffn_residual(x, o, w_out, gain, w_gate, w_up, w_down, tm, pieces):
    t, d = x.shape
    row = lambda i: (i, 0)
    resident = lambda a: pl.BlockSpec(a.shape, lambda i: (0, 0), pipeline_mode=pl.Buffered(1))
    return pl.pallas_call(
        functools.partial(_ffn_kernel, pieces=pieces),
        grid=(t // tm,),
        in_specs=[pl.BlockSpec((tm, d), row), pl.BlockSpec((tm, d), row), resident(w_out), resident(gain),
                  resident(w_gate), resident(w_up), resident(w_down)],
        out_specs=pl.BlockSpec((tm, d), row),
        out_shape=jax.ShapeDtypeStruct((t, d), F32),
        compiler_params=_params(("parallel",), VMEM_LIMIT_RESIDENT),
        name="ffn_swiglu",
    )(x, o, w_out, gain, w_gate, w_up, w_down)


def _moba_qkv_kernel(x_ref, gain_ref, w_ref, qg_ref, kg_ref, q_ref, k_ref, v_ref):
    h = _rms(x_ref[...], gain_ref[...]).astype(BF16)
    d = MOBA_HEADS * MOBA_HD
    qg = qg_ref[...]
    kg = kg_ref[...]
    q = _dot(h, w_ref[:, 0:d])
    k = _dot(h, w_ref[:, d:2 * d])
    for hd in range(MOBA_HEADS):
        cols = slice(hd * MOBA_HD, (hd + 1) * MOBA_HD)
        q_ref[:, cols] = (_rms(q[:, cols], qg) * MOBA_Q_SCALE).astype(BF16)
        k_ref[:, cols] = _rms(k[:, cols], kg).astype(BF16)
    v_ref[...] = _dot(h, w_ref[:, 2 * d:]).astype(BF16)


def _moba_qkv(x, gain, w, q_gain, k_gain, tm):
    t = x.shape[0]
    d = MOBA_HEADS * MOBA_HD
    row = lambda i: (i, 0)
    fixed = lambda i: (0, 0)
    return pl.pallas_call(
        _moba_qkv_kernel,
        grid=(t // tm,),
        in_specs=[
            pl.BlockSpec((tm, D_MODEL), row),
            pl.BlockSpec((1, D_MODEL), fixed),
            pl.BlockSpec((D_MODEL, 3 * d), fixed),
            pl.BlockSpec((1, MOBA_HD), fixed),
            pl.BlockSpec((1, MOBA_HD), fixed),
        ],
        out_specs=[pl.BlockSpec((tm, d), row)] * 3,
        out_shape=[jax.ShapeDtypeStruct((t, d), BF16)] * 3,
        compiler_params=_params(("parallel",), VMEM_LIMIT),
        name="moba_qkv",
    )(x, gain, w, q_gain, k_gain)


def _reduce_rows(x, op, stop=SUBLANES):
    while x.shape[0] > stop:
        half = x.shape[0] // 2
        x = op(x[:half], x[half:])
    return x


def _moba_kernel(q_ref, k_ref, v_ref, o_ref, *, n_blocks):
    seq = n_blocks * MOBA_BLOCK
    heads = range(q_ref.shape[1] // MOBA_HD)
    cols = [slice(hd * MOBA_HD, (hd + 1) * MOBA_HD) for hd in heads]
    n_pad = BF16_ROWS
    ones_rows = jnp.where(lax.broadcasted_iota(I32, (BF16_ROWS, seq), 0) == 0, 1.0, 0.0)
    v_t = [jnp.concatenate([v_ref[:, cols[hd]].astype(F32).T, ones_rows], axis=0).astype(BF16) for hd in heads]
    blk_of_key = lax.broadcasted_iota(I32, (n_pad, seq), 1) // MOBA_BLOCK
    blk_row = lax.broadcasted_iota(I32, (n_pad, seq), 0)
    indicator = jnp.where(blk_of_key == blk_row, 1.0 / MOBA_BLOCK, 0.0).astype(BF16)
    k_mean = [_split(_dot(indicator, k_ref[:, cols[hd]])) for hd in heads]

    blk_id = lax.broadcasted_iota(I32, (n_pad, MOBA_BLOCK), 0)
    key_pos = lax.broadcasted_iota(I32, (MOBA_BLOCK, MOBA_BLOCK), 0)
    query_pos = lax.broadcasted_iota(I32, (MOBA_BLOCK, MOBA_BLOCK), 1)
    causal = key_pos <= query_pos

    def masked_scores(hd, i):
        rows = slice(i * MOBA_BLOCK, (i + 1) * MOBA_BLOCK)
        n_keys = (i + 1) * MOBA_BLOCK
        q = q_ref[rows, cols[hd]]
        s = _dot_nt(k_ref[0:n_keys, cols[hd]], q).astype(BF16)
        selected = None
        if i > MOBA_TOPK:
            km_hi, km_lo = k_mean[hd]
            gate = _dot_nt(km_hi, q) + _dot_nt(km_lo, q)
            gate = jnp.where(blk_id < i, gate, NEG_INF)
            rank = jnp.zeros_like(gate)
            for jp in range(i):
                row = gate[jp:jp + 1, :]
                beats = (row > gate) | ((row == gate) & (jp < blk_id))
                rank = rank + beats.astype(F32)
            selected = jnp.where((rank < MOBA_TOPK) & (blk_id < i), 1.0, 0.0)
        pieces = []
        top = None
        for j in range(i + 1):
            sj = s[j * MOBA_BLOCK:(j + 1) * MOBA_BLOCK, :]
            if j == i:
                sj = jnp.where(causal, sj, NEG_INF)
            elif selected is not None:
                sj = jnp.where(selected[j:j + 1, :] > 0.5, sj, NEG_INF)
            pieces.append(sj)
            folded = _reduce_rows(sj, jnp.maximum, stop=BF16_ROWS)
            top = folded if top is None else jnp.maximum(top, folded)
        return pieces, top

    def weighted_values(hd, i, p):
        o_t = _dot(v_t[hd][:, 0:(i + 1) * MOBA_BLOCK], p)
        rows = slice(i * MOBA_BLOCK, (i + 1) * MOBA_BLOCK)
        o_ref[rows, cols[hd]] = (o_t[:MOBA_HD, :] / o_t[MOBA_HD:MOBA_HD + 1, :]).T.astype(BF16)

    def softmax_numerators(scored):
        pieces, top = scored
        m = top.max(axis=0, keepdims=True)
        probs = [jnp.exp2(sj - m) for sj in pieces]
        return probs[0] if len(probs) == 1 else jnp.concatenate(probs, axis=0)

    upcoming = [masked_scores(hd, 0) for hd in heads]
    pending = None
    for i in range(n_blocks):
        pieces = upcoming
        if i + 1 < n_blocks:
            upcoming = [masked_scores(hd, i + 1) for hd in heads]
        if pending is not None:
            for hd in heads:
                weighted_values(hd, i - 1, pending[hd])
        pending = [softmax_numerators(pieces[hd]) for hd in heads]
    for hd in heads:
        weighted_values(hd, n_blocks - 1, pending[hd])


def _moba(q, k, v, ride_along, batch, seq, heads_per_step):
    t, d = q.shape
    n_groups = MOBA_HEADS // heads_per_step
    spec = pl.BlockSpec((seq, heads_per_step * MOBA_HD), lambda b, h: (b, h))
    cast_specs, cast_shapes = _ride_along_specs(ride_along, batch * n_groups, lambda b, h: b * n_groups + h)
    outs = pl.pallas_call(
        _with_ride_along(functools.partial(_moba_kernel, n_blocks=seq // MOBA_BLOCK), 3, 1, len(ride_along)),
        grid=(batch, n_groups),
        in_specs=[spec, spec, spec] + cast_specs,
        out_specs=[spec] + cast_specs,
        out_shape=[jax.ShapeDtypeStruct((t, d), BF16)] + cast_shapes,
        compiler_params=_params(("arbitrary", "arbitrary"), VMEM_LIMIT),
        name="moba_attn",
    )(q, k, v, *ride_along)
    return outs[0], outs[1:]


def _router_kernel(x_ref, o_ref, wo_ref, gain_ref, w_ref, x2_ref, h_ref, info_ref, rows_ref, cnt_ref,
                   carry_ref, tri_ref,
                   *, region_rows):
    tr = x_ref.shape[0]

    @pl.when(pl.program_id(0) == 0)
    def _():
        carry_ref[...] = jnp.zeros_like(carry_ref)
        r = lax.broadcasted_iota(I32, (TOK_BLK, TOK_BLK), 0)
        c = lax.broadcasted_iota(I32, (TOK_BLK, TOK_BLK), 1)
        tri_ref[...] = (r < c).astype(BF16)

    x2 = x_ref[...] + _dot(o_ref[...], wo_ref[...])
    x2_ref[...] = x2
    h = _rms(x2, gain_ref[...])
    h_hi, h_lo = _split(h)
    h_ref[...] = h_hi
    n_e = ROUTER_ROWS
    wt = w_ref[...]
    passes = _dot_nt(wt, h_hi)
    logits = passes[:n_e] + passes[n_e:] + _dot_nt(wt[:n_e], h_lo)
    e = lax.broadcasted_iota(I32, logits.shape, 0)
    logits = jnp.where(e < N_EXPERTS, logits, -jnp.inf)
    m1 = logits.max(axis=0, keepdims=True)
    i1 = jnp.where(logits == m1, e, n_e).min(axis=0, keepdims=True)
    first = e == i1
    rest = jnp.where(first, -jnp.inf, logits)
    m2 = rest.max(axis=0, keepdims=True)
    i2 = jnp.where(rest == m2, e, n_e).min(axis=0, keepdims=True)
    second = e == i2
    e2 = jnp.exp(m2 - m1)
    denom = 1.0 + e2
    sel = jnp.where(first | second, 1.0, 0.0).astype(BF16)
    nb = tr // TOK_BLK
    blk_row = lax.broadcasted_iota(I32, (nb, tr), 0)
    blk_tok = lax.broadcasted_iota(I32, (nb, tr), 1) // TOK_BLK
    cnt = _dot_nt((blk_row == blk_tok).astype(BF16), sel)
    cnt_ref[...] = jnp.concatenate([cnt, jnp.zeros((nb, LANES - n_e), F32)], axis=1)
    offset = carry_ref[...]
    before = []
    for b in range(nb):
        sel_b = sel[:, b * TOK_BLK:(b + 1) * TOK_BLK]
        before.append(_dot(sel_b, tri_ref[...]) + offset)
        offset = offset + jnp.broadcast_to(sel_b.astype(F32).sum(axis=1, keepdims=True), offset.shape)
    carry_ref[...] = offset
    before = jnp.concatenate(before, axis=1)
    rank1 = jnp.where(first, before, 0.0).sum(axis=0, keepdims=True)
    rank2 = jnp.where(second, before, 0.0).sum(axis=0, keepdims=True)
    fields = (i1.astype(F32) * region_rows + rank1, i2.astype(F32) * region_rows + rank2,
              1.0 / denom, e2 / denom)
    for n, f in enumerate(fields):
        for b in range(nb):
            rows_ref[n, b:b + 1, :] = f[:, b * TOK_BLK:(b + 1) * TOK_BLK]
    info_t = jnp.concatenate(list(fields) + [jnp.zeros((LANES - len(fields), tr), F32)], axis=0)
    info_ref[...] = info_t.T[:, :len(fields)]


def _router(x, o, w_out, gain, w_router, tr):
    t = x.shape[0]
    nb = tr // TOK_BLK
    pad = ((0, ROUTER_ROWS - N_EXPERTS), (0, 0))
    w_hi = w_router.T.astype(BF16)
    w_lo = (w_router.T - w_hi.astype(F32)).astype(BF16)
    w_both = jnp.concatenate([jnp.pad(w_hi, pad), jnp.pad(w_lo, pad)], axis=0)
    row = lambda i: (i, 0)
    fixed = lambda i: (0, 0)
    return pl.pallas_call(
        functools.partial(_router_kernel, region_rows=float(t)),
        grid=(t // tr,),
        in_specs=[
            pl.BlockSpec((tr, D_MODEL), row),
            pl.BlockSpec((tr, D_MODEL), row),
            pl.BlockSpec((D_MODEL, D_MODEL), fixed),
            pl.BlockSpec((1, D_MODEL), fixed),
            pl.BlockSpec((2 * ROUTER_ROWS, D_MODEL), fixed),
        ],
        out_specs=[
            pl.BlockSpec((tr, D_MODEL), row),
            pl.BlockSpec((tr, D_MODEL), row),
            pl.BlockSpec((tr, 4), row),
            pl.BlockSpec((4, nb, TOK_BLK), lambda i: (0, i, 0)),
            pl.BlockSpec((nb, LANES), row),
        ],
        out_shape=[
            jax.ShapeDtypeStruct((t, D_MODEL), F32),
            jax.ShapeDtypeStruct((t, D_MODEL), BF16),
            jax.ShapeDtypeStruct((t, 4), F32),
            jax.ShapeDtypeStruct((4, t // TOK_BLK, TOK_BLK), F32),
            jax.ShapeDtypeStruct((t // TOK_BLK, LANES), F32),
        ],
        scratch_shapes=[pltpu.VMEM((ROUTER_ROWS, LANES), F32), pltpu.VMEM((TOK_BLK, TOK_BLK), BF16)],
        compiler_params=_params(("arbitrary",), VMEM_LIMIT_RESIDENT),
        name="moe_router",
    )(x, o, w_out, gain, w_both)


def _moe_plan(cnt, t):
    e_ids = jnp.arange(N_EXPERTS, dtype=I32)
    by_expert = lambda table, idx: jnp.sum(jnp.where(idx[..., None] == e_ids, table, 0), axis=-1)
    cnt_blk = cnt[:, :N_EXPERTS].astype(I32)
    cum_blk = jnp.cumsum(cnt_blk, axis=0)
    counts = cum_blk[-1]
    tiles_e = (counts + MOE_TM - 1) // MOE_TM
    tile_end = jnp.cumsum(tiles_e)
    tile_off = tile_end - tiles_e
    nt = TOP_K * t // MOE_TM + N_EXPERTS
    r = jnp.arange(nt, dtype=I32)
    valid = r < tile_end[-1]
    te = jnp.minimum(jnp.sum((r[:, None] >= tile_end[None, :]).astype(I32), axis=1), N_EXPERTS - 1)
    local = r - by_expert(tile_off, te)
    last_e = jnp.max(jnp.where(tiles_e > 0, e_ids, 0))
    tile_expert = jnp.where(valid, te, last_e)
    shift = e_ids * t - tile_off * MOE_TM

    per = MOE_TM // MOE_TG
    rg = jnp.arange(nt * per, dtype=I32)
    eg = jnp.repeat(te, per)
    k0 = jnp.repeat(local, per) * MOE_TM + (rg % per) * MOE_TG
    k1 = jnp.minimum(k0 + MOE_TG, by_expert(counts, eg))
    has = jnp.repeat(valid, per) & (k1 > k0)
    cum_e = by_expert(cum_blk[None, :, :], eg[:, None])
    s_lo = jnp.sum((cum_e <= k0[:, None]).astype(I32), axis=1)
    s_hi = jnp.sum((cum_e <= (k1 - 1)[:, None]).astype(I32), axis=1)
    g_nwin = jnp.where(has, (s_hi - s_lo) // GATHER_WB + 1, 0)
    g_slo = jnp.where(has, s_lo, 0)
    g_code = eg * t + k0

    nts = t // COMB_TS
    cnt_ts = cnt_blk.reshape(nts, COMB_TS // TOK_BLK, N_EXPERTS).sum(axis=1)
    seg_lo = (tile_off * MOE_TM)[None, :] + jnp.cumsum(cnt_ts, axis=0) - cnt_ts
    seg_hi = seg_lo + cnt_ts
    ws0 = (seg_lo // BF16_ROWS) * BF16_ROWS
    nw = jnp.where(cnt_ts > 0, (seg_hi - ws0 + COMB_WC - 1) // COMB_WC, 0)
    nw_end = jnp.cumsum(nw, axis=1)
    k = jnp.arange(COMB_MAXW, dtype=I32)
    ek = jnp.minimum(jnp.sum((k[None, :, None] >= nw_end[:, None, :]).astype(I32), axis=2), N_EXPERTS - 1)
    pick = lambda a: by_expert(a[:, None, :], ek)
    live = k[None, :] < nw_end[:, -1:]
    nominal = pick(ws0) + (k[None, :] - pick(nw_end - nw)) * COMB_WC
    shift_k = by_expert(shift, ek)
    c_start = jnp.minimum(nominal, by_expert(tile_end * MOE_TM, ek) - COMB_WC)
    c_off = shift_k + c_start
    c_lo = jnp.maximum(pick(seg_lo), nominal) + shift_k
    c_hi = jnp.minimum(pick(seg_hi), nominal + COMB_WC) + shift_k
    flat = lambda a: jnp.where(live, a, 0).reshape(-1)
    return dict(
        tile_expert=tile_expert, tile_valid=valid.astype(I32),
        g_code=g_code, g_slo=g_slo, g_nwin=g_nwin,
        c_start=flat(c_start), c_off=flat(c_off), c_lo=flat(c_lo), c_hi=flat(c_hi), c_nwin=nw_end[:, -1])


def _moe_gather_kernel(code_ref, slo_ref, nwin_ref, rows_ref, h_ref, hs_ref, gs_ref):
    n_blk = rows_ref.shape[1]
    n_sub = hs_ref.shape[0] // MOE_TG
    sub_ids = [pl.program_id(0) * n_sub + sub for sub in range(n_sub)]
    out_rows = [slice(sub * MOE_TG, (sub + 1) * MOE_TG) for sub in range(n_sub)]

    def window(r, k):
        row_id = (code_ref[r] + lax.broadcasted_iota(I32, (MOE_TG, TOK_BLK), 0)).astype(F32)
        nominal = slo_ref[r] + k * GATHER_WB
        sb = jnp.minimum(nominal, n_blk - GATHER_WB)
        fresh = sb + lax.broadcasted_iota(I32, (GATHER_WB, TOK_BLK), 0) >= nominal
        p1 = jnp.where(fresh, rows_ref[0, pl.ds(sb, GATHER_WB), :], -1.0)
        p2 = jnp.where(fresh, rows_ref[1, pl.ds(sb, GATHER_WB), :], -1.0)
        w1 = rows_ref[2, pl.ds(sb, GATHER_WB), :]
        w2 = rows_ref[3, pl.ds(sb, GATHER_WB), :]
        pieces = []
        g = jnp.zeros((MOE_TG, TOK_BLK), F32)
        for a in range(GATHER_WB):
            m1 = p1[a:a + 1, :] == row_id
            m2 = p2[a:a + 1, :] == row_id
            pieces.append(jnp.where(m1 | m2, 1.0, 0.0).astype(BF16))
            g = g + jnp.where(m1, w1[a:a + 1, :], 0.0) + jnp.where(m2, w2[a:a + 1, :], 0.0)
        onehot = jnp.concatenate(pieces, axis=1)
        return onehot, g.sum(axis=-1, keepdims=True), pl.multiple_of(sb * TOK_BLK, TOK_BLK)

    gathered = lambda onehot, tok0: _dot(onehot, h_ref[pl.ds(tok0, GATHER_WB * TOK_BLK), :]).astype(BF16)

    firsts = [window(r, 0) for r in sub_ids]
    for rows, (onehot, weight, tok0) in zip(out_rows, firsts):
        hs_ref[rows, :] = gathered(onehot, tok0)
        gs_ref[rows, :] = weight

    for r, rows in zip(sub_ids, out_rows):
        def more(k, carry, r=r, rows=rows):
            onehot, weight, tok0 = window(r, k)
            hs_ref[rows, :] += gathered(onehot, tok0)
            gs_ref[rows, :] += weight
            return carry

        lax.fori_loop(1, nwin_ref[r], more, 0)


def _moe_gather(plan, h, rows):
    t = h.shape[0]
    n_rows = plan["g_code"].shape[0] * MOE_TG
    n_blk = t // TOK_BLK
    grid_spec = pltpu.PrefetchScalarGridSpec(
        num_scalar_prefetch=3,
        grid=(n_rows // GATHER_ROWS,),
        in_specs=[
            pl.BlockSpec((4, n_blk, TOK_BLK), lambda r, *_: (0, 0, 0), pipeline_mode=pl.Buffered(1)),
            pl.BlockSpec((t, D_MODEL), lambda r, *_: (0, 0), pipeline_mode=pl.Buffered(1)),
        ],
        out_specs=[
            pl.BlockSpec((GATHER_ROWS, D_MODEL), lambda r, *_: (r, 0)),
            pl.BlockSpec((GATHER_ROWS, 1), lambda r, *_: (r, 0)),
        ],
    )
    return pl.pallas_call(
        _moe_gather_kernel,
        grid_spec=grid_spec,
        out_shape=[
            jax.ShapeDtypeStruct((n_rows, D_MODEL), BF16),
            jax.ShapeDtypeStruct((n_rows, 1), F32),
        ],
        compiler_params=_params(("parallel",), VMEM_LIMIT_RESIDENT),
        name="moe_gather",
    )(plan["g_code"], plan["g_slo"], plan["g_nwin"], rows, h)


def _moe_expert_kernel(te_ref, tv_ref, hs_ref, gs_ref, wg_ref, wu_ref, wd_ref, y_ref, acc_ref, *, nj, pieces):
    r = pl.program_id(0)
    j = pl.program_id(1)
    valid = tv_ref[r] > 0
    part = lambda: _swiglu_pieces(hs_ref[...], wg_ref, wu_ref, wd_ref, pieces)

    if nj == 1:
        @pl.when(valid)
        def _():
            y_ref[...] = (part() * gs_ref[...]).astype(BF16)
    else:
        @pl.when(valid & (j == 0))
        def _():
            acc_ref[...] = part()

        if nj > 2:
            @pl.when(valid & (j > 0) & (j < nj - 1))
            def _():
                acc_ref[...] += part()

        @pl.when(valid & (j == nj - 1))
        def _():
            y_ref[...] = ((acc_ref[...] + part()) * gs_ref[...]).astype(BF16)

    @pl.when(jnp.logical_not(valid) & (j == nj - 1))
    def _():
        y_ref[...] = jnp.zeros_like(y_ref)


def _moe_experts(plan, hs, gs, w_gate, w_up, w_down, pieces):
    tf = sum(pieces)
    n_rows, d = hs.shape
    ff = w_gate.shape[2]
    nj = ff // tf
    jj = lambda r, j, tv: jnp.where(tv[r] > 0, j, nj - 1)
    grid_spec = pltpu.PrefetchScalarGridSpec(
        num_scalar_prefetch=2,
        grid=(n_rows // MOE_TM, nj),
        in_specs=[
            pl.BlockSpec((MOE_TM, d), lambda r, j, te, tv: (r, 0)),
            pl.BlockSpec((MOE_TM, 1), lambda r, j, te, tv: (r, 0)),
            pl.BlockSpec((None, d, tf), lambda r, j, te, tv: (te[r], 0, jj(r, j, tv))),
            pl.BlockSpec((None, d, tf), lambda r, j, te, tv: (te[r], 0, jj(r, j, tv))),
            pl.BlockSpec((None, tf, d), lambda r, j, te, tv: (te[r], jj(r, j, tv), 0)),
        ],
        out_specs=pl.BlockSpec((MOE_TM, d), lambda r, j, te, tv: (r, 0)),
        scratch_shapes=[pltpu.VMEM((MOE_TM if nj > 1 else SUBLANES, d), F32)],
    )
    return pl.pallas_call(
        functools.partial(_moe_expert_kernel, nj=nj, pieces=pieces),
        grid_spec=grid_spec,
        out_shape=jax.ShapeDtypeStruct((n_rows, d), BF16),
        compiler_params=_params(("arbitrary", "arbitrary"), VMEM_LIMIT_RESIDENT),
        name="moe_experts",
    )(plan["tile_expert"], plan["tile_valid"], hs, gs, w_gate, w_up, w_down)


def _moe_combine_kernel(start_ref, off_ref, lo_ref, hi_ref, nwin_ref, x_ref, info_ref, y_hbm, o_ref,
                        ybuf, ybuf_more, sem, sem_more):
    step = pl.program_id(0)
    n_steps = pl.num_programs(0)
    per = COMB_STEP_TILES
    ring = COMB_AHEAD_BUFS * per

    def window_copy(idx, dst, dsem):
        start = pl.multiple_of(start_ref[idx], BF16_ROWS)
        return pltpu.make_async_copy(y_hbm.at[pl.ds(start, COMB_WC), :], dst, dsem)

    def first_group(tile, wait):
        buf = tile % ring
        for s in range(COMB_GROUP):
            @pl.when(s < nwin_ref[tile])
            def _():
                cp = window_copy(tile * COMB_MAXW + s, ybuf.at[buf, pl.ds(s * COMB_WC, COMB_WC), :],
                                 sem.at[buf, s])
                cp.wait() if wait else cp.start()

    def request(grid_step):
        for sub in range(per):
            first_group(grid_step * per + sub, wait=False)

    @pl.when(step == 0)
    def _():
        ybuf[...] = jnp.zeros_like(ybuf)
        ybuf_more[...] = jnp.zeros_like(ybuf_more)
        for ahead in range(COMB_AHEAD_BUFS - 1):
            @pl.when(ahead < n_steps)
            def _():
                request(ahead)

    @pl.when(step + COMB_AHEAD_BUFS - 1 < n_steps)
    def _():
        request(step + COMB_AHEAD_BUFS - 1)

    tiles = [step * per + sub for sub in range(per)]
    rows = [slice(sub * COMB_TS, (sub + 1) * COMB_TS) for sub in range(per)]
    for tile in tiles:
        first_group(tile, wait=True)

    lane = lax.broadcasted_iota(I32, (COMB_TS, COMB_WC), 1)
    lane_row = lax.broadcasted_iota(I32, (1, COMB_WC), 1)
    codes = [[jnp.broadcast_to(info_ref[r, n:n + 1].astype(I32), (COMB_TS, COMB_WC)) for n in range(TOP_K)]
             for r in rows]

    def onehot_group(sub, g):
        pieces = []
        for s in range(COMB_GROUP):
            idx = tiles[sub] * COMB_MAXW + g * COMB_GROUP + s
            off = off_ref[idx]
            in_span = (lane_row >= lo_ref[idx] - off) & (lane_row < hi_ref[idx] - off)
            target = lane + off
            hit = ((codes[sub][0] == target) | (codes[sub][1] == target)) & in_span
            pieces.append(jnp.where(hit, 1.0, 0.0).astype(BF16))
        return jnp.concatenate(pieces, axis=1)

    onehots = [onehot_group(sub, 0) for sub in range(per)]
    for sub in range(per):
        o_ref[rows[sub], :] = x_ref[rows[sub], :] + _dot(onehots[sub], ybuf[tiles[sub] % ring])

    for sub in range(per):
        n = nwin_ref[tiles[sub]]

        def more(g, carry, sub=sub, n=n):
            for wait in (False, True):
                for s in range(COMB_GROUP):
                    @pl.when(g * COMB_GROUP + s < n)
                    def _():
                        cp = window_copy(tiles[sub] * COMB_MAXW + g * COMB_GROUP + s,
                                         ybuf_more.at[pl.ds(s * COMB_WC, COMB_WC), :], sem_more.at[s])
                        cp.wait() if wait else cp.start()
            o_ref[rows[sub], :] += _dot(onehot_group(sub, g), ybuf_more[...])
            return carry

        lax.fori_loop(1, (n + COMB_GROUP - 1) // COMB_GROUP, more, 0)


def _moe_combine(plan, x, info, y):
    t, d = x.shape
    width = COMB_GROUP * COMB_WC
    step_rows = COMB_STEP_TILES * COMB_TS
    grid_spec = pltpu.PrefetchScalarGridSpec(
        num_scalar_prefetch=5,
        grid=(t // step_rows,),
        in_specs=[
            pl.BlockSpec((step_rows, d), lambda i, *_: (i, 0)),
            pl.BlockSpec((step_rows, 4), lambda i, *_: (i, 0)),
            pl.BlockSpec(memory_space=pl.ANY),
        ],
        out_specs=pl.BlockSpec((step_rows, d), lambda i, *_: (i, 0)),
        scratch_shapes=[
            pltpu.VMEM((COMB_AHEAD_BUFS * COMB_STEP_TILES, width, d), BF16),
            pltpu.VMEM((width, d), BF16),
            pltpu.SemaphoreType.DMA((COMB_AHEAD_BUFS * COMB_STEP_TILES, COMB_GROUP)),
            pltpu.SemaphoreType.DMA((COMB_GROUP,)),
        ],
    )
    return pl.pallas_call(
        _moe_combine_kernel,
        grid_spec=grid_spec,
        out_shape=jax.ShapeDtypeStruct((t, d), F32),
        compiler_params=_params(("arbitrary",), VMEM_LIMIT),
        name="moe_combine",
    )(plan["c_start"], plan["c_off"], plan["c_lo"], plan["c_hi"], plan["c_nwin"], x, info, y)


def kernel(x, norm_mix, norm_ffn, gla_w_in, gla_w_gate2, gla_b_gate, gla_out_gain, gla_w_out,
           moba_w_qkv, moba_q_gain, moba_k_gain, moba_w_out, ffn_w_gate, ffn_w_up, ffn_w_down,
           moe_w_router, moe_w_gate, moe_w_up, moe_w_down):
    batch, seq, d = x.shape
    t = batch * seq
    xt = x.reshape(t, d)

    (q, k, v, g, la), (ffn_wg, ffn_wu, ffn_wd, gla_wo, moba_wqkv, moba_wo) = _gla_inproj(
        xt, norm_mix[0:1], gla_w_in, gla_w_gate2[0], gla_b_gate[0:1],
        ride_along=(ffn_w_gate, ffn_w_up, ffn_w_down, gla_w_out, moba_w_qkv, moba_w_out), tm=PROJ_TM)
    o, _ = _gla(q, k, v, g, la, gla_out_gain[0:1], (), batch, seq, blk=GLA_BLK,
                rows_per_step=GLA_ROWS_PER_STEP)
    xt = _ffn_residual(xt, o, gla_wo[0], norm_ffn[0:1], ffn_wg[0], ffn_wu[0], ffn_wd[0],
                       tm=PROJ_TM, pieces=MOE_PIECES + MOE_PIECES)

    q, k, v = _moba_qkv(xt, norm_mix[1:2], moba_wqkv[0], moba_q_gain[0:1], moba_k_gain[0:1], tm=PROJ_TM)
    o, (moe_wg, moe_wu, moe_wd) = _moba(q, k, v, (moe_w_gate[0], moe_w_up[0], moe_w_down[0]), batch, seq,
                                        heads_per_step=MOBA_HEADS_PER_STEP)
    xt, h, info, rows, cnt = _router(xt, o, moba_wo[0], norm_ffn[1:2], moe_w_router[0], tr=ROUTER_TR)
    plan = _moe_plan(cnt, t)
    hs, gs = _moe_gather(plan, h, rows)
    y = _moe_experts(plan, hs, gs, moe_wg, moe_wu, moe_wd, pieces=MOE_PIECES + MOE_PIECES)
    xt = _moe_combine(plan, xt, info, y)
    return xt.reshape(batch, seq, d)
```

```python
import functools

import jax
import jax.numpy as jnp
from jax import lax
from jax.experimental import pallas as pl
from jax.experimental.pallas import tpu as pltpu

F32 = jnp.float32
BF16 = jnp.bfloat16
I32 = jnp.int32

EPS = 1e-6
NEG_INF = -1e30

D_MODEL = 1024
GLA_HEADS = 4
GLA_DK = 512
GLA_DV = 1024
GLA_HK = 128
GLA_HV = 256
GLA_RANK = 16
GLA_NORMALIZER = 16.0
GLA_CHUNK = 64

MOBA_HEADS = 8
MOBA_HD = 128
MOBA_BLOCK = 256
MOBA_TOPK = 3
MOBA_Q_SCALE = MOBA_HD ** -0.5 * 1.4426950408889634

N_EXPERTS = 8
TOP_K = 2

LANES = 128
SUBLANES = 8
BF16_ROWS = 16
ROUTER_ROWS = BF16_ROWS
VMEM_LIMIT = 48 * 1024 * 1024
VMEM_LIMIT_RESIDENT = 56 * 1024 * 1024

PROJ_TM = 512
ROUTER_TR = 1024
GLA_BLK = 256
GLA_ROWS_PER_STEP = 4
MOBA_HEADS_PER_STEP = 1

MOE_PIECES = (512, 512, 768)
MOE_TM = 256
MOE_TG = 128
GATHER_ROWS = 512
TOK_BLK = LANES
GATHER_WB = 6
COMB_TS = 256
COMB_STEP_TILES = 2
COMB_WC = 128
COMB_GROUP = 8
COMB_AHEAD_BUFS = 3
COMB_MAXW = 16


def _params(semantics, vmem_bytes=None):
    return pltpu.CompilerParams(dimension_semantics=semantics, vmem_limit_bytes=vmem_bytes)


def _rms(x, gain):
    y = x * lax.rsqrt(jnp.mean(x * x, axis=-1, keepdims=True) + EPS)
    return y * gain


def _dot(a, b):
    return jnp.dot(a, b, preferred_element_type=F32)


def _dot_nt(a, b):
    return lax.dot_general(a, b, (((1,), (1,)), ((), ())), preferred_element_type=F32)


def _dot_tn(a, b):
    return lax.dot_general(a, b, (((0,), (0,)), ((), ())), preferred_element_type=F32)


def _split(a):
    hi = a.astype(BF16)
    lo = (a - hi.astype(F32)).astype(BF16)
    return hi, lo


def _log_sigmoid(z):
    return jnp.minimum(z, 0.0) - jnp.log(1.0 + jnp.exp(-jnp.abs(z)))


def _silu(z):
    return z / (1.0 + jnp.exp(-z))


def _swiglu_pieces(h, wg_ref, wu_ref, wd_ref, widths):
    bounds = [sum(widths[:n]) for n in range(len(widths) + 1)]
    assert bounds[-1] == wg_ref.shape[1], (widths, wg_ref.shape)
    gate_up = lambda c: (_dot(h, wg_ref[:, bounds[c]:bounds[c + 1]]), _dot(h, wu_ref[:, bounds[c]:bounds[c + 1]]))
    out = None
    upcoming = gate_up(0)
    for c in range(len(widths)):
        g, u = upcoming
        if c + 1 < len(widths):
            upcoming = gate_up(c + 1)
        d = _dot((_silu(g) * u).astype(BF16), wd_ref[bounds[c]:bounds[c + 1], :])
        out = d if out is None else out + d
    return out


def _ride_along_specs(arrays, n_steps, step):
    specs, shapes = [], []
    for w in arrays:
        n_e, rows, cols = w.shape
        per_e = n_steps // n_e
        if rows % (per_e * BF16_ROWS) == 0:
            blk = (None, rows // per_e, cols)
            imap = lambda *ids, per_e=per_e: (step(*ids) // per_e, step(*ids) % per_e, 0)
        else:
            assert cols % (per_e * LANES) == 0, (w.shape, n_steps)
            blk = (None, rows, cols // per_e)
            imap = lambda *ids, per_e=per_e: (step(*ids) // per_e, 0, step(*ids) % per_e)
        specs.append(pl.BlockSpec(blk, imap))
        shapes.append(jax.ShapeDtypeStruct(w.shape, BF16))
    return specs, shapes


def _with_ride_along(body, n_in, n_out, n_cast):
    def kernel(*refs):
        ins, refs = refs[:n_in], refs[n_in:]
        cast_in, refs = refs[:n_cast], refs[n_cast:]
        outs, refs = refs[:n_out], refs[n_out:]
        cast_out, scratch = refs[:n_cast], refs[n_cast:]
        for src, dst in zip(cast_in, cast_out):
            dst[...] = src[...].astype(BF16)
        body(*ins, *outs, *scratch)
    return kernel


def _gla_inproj_kernel(x_next_ref, x_first_ref, gain_ref, w_ref, wg2_ref, b_ref,
                       q_ref, k_ref, v_ref, g_ref, la_ref, wb_ref, h_ref):
    i = pl.program_id(0)

    @pl.when(i == 0)
    def _():
        wb_ref[...] = w_ref[...].astype(BF16)
        h_ref[0] = _rms(x_first_ref[...], gain_ref[...]).astype(BF16)

    n_main = 2 * GLA_DK + 2 * GLA_DV

    def step(cur, nxt):
        h = h_ref[cur]
        h_ref[nxt] = _rms(x_next_ref[...], gain_ref[...]).astype(BF16)
        a_lr = _dot(h, wb_ref[:, n_main:])
        a_hi, a_lo = _split(a_lr)
        w_hi, w_lo = _split(wg2_ref[...])
        z = _dot(jnp.concatenate([a_hi, a_lo, a_hi], axis=1), jnp.concatenate([w_hi, w_hi, w_lo], axis=0))
        la_ref[...] = _log_sigmoid(z + b_ref[...]) / GLA_NORMALIZER
        q_ref[...] = _dot(h, wb_ref[:, 0:GLA_DK]).astype(BF16)
        k_ref[...] = _dot(h, wb_ref[:, GLA_DK:2 * GLA_DK]).astype(BF16)
        v_ref[...] = _dot(h, wb_ref[:, 2 * GLA_DK:2 * GLA_DK + GLA_DV]).astype(BF16)
        g_ref[...] = _dot(h, wb_ref[:, 2 * GLA_DK + GLA_DV:n_main]).astype(BF16)

    @pl.when(i % 2 == 0)
    def _():
        step(0, 1)

    @pl.when(i % 2 == 1)
    def _():
        step(1, 0)


def _gla_inproj(x, gain, w_in, w_gate2, b_gate, ride_along, tm):
    t = x.shape[0]
    n_steps = t // tm
    row = lambda i: (i, 0)
    fixed = lambda i: (0, 0)
    cast_specs, cast_shapes = _ride_along_specs(ride_along, n_steps, lambda i: i)
    outs = pl.pallas_call(
        _with_ride_along(_gla_inproj_kernel, 6, 5, len(ride_along)),
        grid=(n_steps,),
        in_specs=[
            pl.BlockSpec((tm, D_MODEL), lambda i: (jnp.minimum(i + 1, n_steps - 1), 0)),
            pl.BlockSpec((tm, D_MODEL), fixed),
            pl.BlockSpec((1, D_MODEL), fixed),
            pl.BlockSpec((None,) + w_in.shape[1:], lambda i: (0, 0, 0), pipeline_mode=pl.Buffered(1)),
            pl.BlockSpec((GLA_RANK, GLA_DK), fixed),
            pl.BlockSpec((1, GLA_DK), fixed),
        ] + cast_specs,
        out_specs=[
            pl.BlockSpec((tm, GLA_DK), row),
            pl.BlockSpec((tm, GLA_DK), row),
            pl.BlockSpec((tm, GLA_DV), row),
            pl.BlockSpec((tm, GLA_DV), row),
            pl.BlockSpec((tm, GLA_DK), row),
        ] + cast_specs,
        out_shape=[
            jax.ShapeDtypeStruct((t, GLA_DK), BF16),
            jax.ShapeDtypeStruct((t, GLA_DK), BF16),
            jax.ShapeDtypeStruct((t, GLA_DV), BF16),
            jax.ShapeDtypeStruct((t, GLA_DV), BF16),
            jax.ShapeDtypeStruct((t, GLA_DK), F32),
        ] + cast_shapes,
        scratch_shapes=[pltpu.VMEM(w_in.shape[1:], BF16), pltpu.VMEM((2, tm, D_MODEL), BF16)],
        compiler_params=_params(("arbitrary",), VMEM_LIMIT),
        name="gla_inproj",
    )(x, x, gain, w_in, w_gate2, b_gate, *ride_along)
    return outs[:5], outs[5:]


def _gla_kernel(q_ref, k_ref, v_ref, g_ref, la_ref, og_ref, o_ref, st_ref, *, n_chunks):
    @pl.when(pl.program_id(1) == 0)
    def _():
        st_ref[...] = jnp.zeros_like(st_ref)

    n_rows = q_ref.shape[0]
    blk = n_chunks * GLA_CHUNK
    r = lax.broadcasted_iota(I32, (blk, blk), 0)
    c = lax.broadcasted_iota(I32, (blk, blk), 1)
    causal = (r >= c) & (r // GLA_CHUNK == c // GLA_CHUNK)
    tril = causal.astype(BF16)
    og = og_ref[...]
    chunk_rows = [slice(ci * GLA_CHUNK, (ci + 1) * GLA_CHUNK) for ci in range(n_chunks)]
    hk = [slice(h * GLA_HK, (h + 1) * GLA_HK) for h in range(GLA_HEADS)]
    hv = [slice(h * GLA_HV, (h + 1) * GLA_HV) for h in range(GLA_HEADS)]

    q_dec, k_inv, k_end, chunk_decay = [], [], [], []
    for b in range(n_rows):
        la = la_ref[b]
        la_hi, la_mid = _split(la)
        la_lo = (la - la_hi.astype(F32) - la_mid.astype(F32)).astype(BF16)
        cum = _dot(tril, la_hi) + _dot(tril, la_mid) + _dot(tril, la_lo)
        ends = [cum[rows.stop - 1:rows.stop, :] for rows in chunk_rows]
        last = jnp.concatenate([jnp.broadcast_to(e, (GLA_CHUNK, GLA_DK)) for e in ends], axis=0)
        q = q_ref[b].astype(F32) * (GLA_HK ** -0.5)
        k = k_ref[b].astype(F32)
        q_dec.append((q * jnp.exp(cum)).astype(BF16))
        k_inv.append((k * jnp.exp(-cum)).astype(BF16))
        k_end.append((k * jnp.exp(last - cum)).astype(BF16))
        chunk_decay.append([jnp.exp(e) for e in ends])

    streams = [(b, h) for b in range(n_rows) for h in range(GLA_HEADS)]
    scores = [jnp.where(causal, _dot_nt(q_dec[b][:, hk[h]], k_inv[b][:, hk[h]]), 0.0).astype(BF16)
              for b, h in streams]
    state_in = [[_dot_tn(v_ref[b, rows, hv[h]], k_end[b][rows, hk[h]]) for rows in chunk_rows]
                for b, h in streams]
    intra = [_dot(scores[n], v_ref[b, :, hv[h]]) for n, (b, h) in enumerate(streams)]
    for n, (b, h) in enumerate(streams):
        st = st_ref[b, h]
        inter = []
        for ci, rows in enumerate(chunk_rows):
            inter.append(_dot_nt(q_dec[b][rows, hk[h]], st.astype(BF16)))
            st = st * chunk_decay[b][ci][:, hk[h]] + state_in[n][ci]
        st_ref[b, h] = st
        o = intra[n] + jnp.concatenate(inter, axis=0)
        o = o * lax.rsqrt(jnp.mean(o * o, axis=-1, keepdims=True) + EPS) * og
        o = o * _silu(g_ref[b, :, hv[h]].astype(F32))
        o_ref[b, :, hv[h]] = o.astype(BF16)


def _gla(q, k, v, g, la, out_gain, ride_along, batch, seq, blk, rows_per_step):
    nb = seq // blk
    nbt = batch // rows_per_step
    per_row = lambda a: a.reshape(batch, seq, a.shape[-1])
    spec = lambda width: pl.BlockSpec((rows_per_step, blk, width), lambda b, i: (b, i, 0))
    cast_specs, cast_shapes = _ride_along_specs(ride_along, nbt * nb, lambda b, i: b * nb + i)
    outs = pl.pallas_call(
        _with_ride_along(functools.partial(_gla_kernel, n_chunks=blk // GLA_CHUNK), 6, 1, len(ride_along)),
        grid=(nbt, nb),
        in_specs=[spec(GLA_DK), spec(GLA_DK), spec(GLA_DV), spec(GLA_DV), spec(GLA_DK),
                  pl.BlockSpec((1, GLA_HV), lambda b, i: (0, 0))] + cast_specs,
        out_specs=[spec(GLA_DV)] + cast_specs,
        out_shape=[jax.ShapeDtypeStruct((batch, seq, GLA_DV), BF16)] + cast_shapes,
        scratch_shapes=[pltpu.VMEM((rows_per_step, GLA_HEADS, GLA_HV, GLA_HK), F32)],
        compiler_params=_params(("arbitrary", "arbitrary"), VMEM_LIMIT),
        name="gla_core",
    )(per_row(q), per_row(k), per_row(v), per_row(g), per_row(la), out_gain, *ride_along)
    return outs[0].reshape(batch * seq, GLA_DV), outs[1:]


def _ffn_kernel(x_ref, o_ref, wo_ref, gain_ref, wg_ref, wu_ref, wd_ref, y_ref, *, pieces):
    x1 = x_ref[...] + _dot(o_ref[...], wo_ref[...])
    h = _rms(x1, gain_ref[...]).astype(BF16)
    y_ref[...] = x1 + _swiglu_pieces(h, wg_ref, wu_ref, wd_ref, pieces)


def _ffn_residual(x, o, w_out, gain, w_gate, w_up, w_down, tm, pieces):
    t, d = x.shape
    row = lambda i: (i, 0)
    resident = lambda a: pl.BlockSpec(a.shape, lambda i: (0, 0), pipeline_mode=pl.Buffered(1))
    return pl.pallas_call(
        functools.partial(_ffn_kernel, pieces=pieces),
        grid=(t // tm,),
        in_specs=[pl.BlockSpec((tm, d), row), pl.BlockSpec((tm, d), row), resident(w_out), resident(gain),
                  resident(w_gate), resident(w_up), resident(w_down)],
        out_specs=pl.BlockSpec((tm, d), row),
        out_shape=jax.ShapeDtypeStruct((t, d), F32),
        compiler_params=_params(("parallel",), VMEM_LIMIT_RESIDENT),
        name="ffn_swiglu",
    )(x, o, w_out, gain, w_gate, w_up, w_down)


def _moba_qkv_kernel(x_ref, gain_ref, w_ref, qg_ref, kg_ref, q_ref, k_ref, v_ref):
    h = _rms(x_ref[...], gain_ref[...]).astype(BF16)
    d = MOBA_HEADS * MOBA_HD
    qg = qg_ref[...]
    kg = kg_ref[...]
    q = _dot(h, w_ref[:, 0:d])
    k = _dot(h, w_ref[:, d:2 * d])
    for hd in range(MOBA_HEADS):
        cols = slice(hd * MOBA_HD, (hd + 1) * MOBA_HD)
        q_ref[:, cols] = (_rms(q[:, cols], qg) * MOBA_Q_SCALE).astype(BF16)
        k_ref[:, cols] = _rms(k[:, cols], kg).astype(BF16)
    v_ref[...] = _dot(h, w_ref[:, 2 * d:]).astype(BF16)


def _moba_qkv(x, gain, w, q_gain, k_gain, tm):
    t = x.shape[0]
    d = MOBA_HEADS * MOBA_HD
    row = lambda i: (i, 0)
    fixed = lambda i: (0, 0)
    return pl.pallas_call(
        _moba_qkv_kernel,
        grid=(t // tm,),
        in_specs=[
            pl.BlockSpec((tm, D_MODEL), row),
            pl.BlockSpec((1, D_MODEL), fixed),
            pl.BlockSpec((D_MODEL, 3 * d), fixed),
            pl.BlockSpec((1, MOBA_HD), fixed),
            pl.BlockSpec((1, MOBA_HD), fixed),
        ],
        out_specs=[pl.BlockSpec((tm, d), row)] * 3,
        out_shape=[jax.ShapeDtypeStruct((t, d), BF16)] * 3,
        compiler_params=_params(("parallel",), VMEM_LIMIT),
        name="moba_qkv",
    )(x, gain, w, q_gain, k_gain)


def _reduce_rows(x, op, stop=SUBLANES):
    while x.shape[0] > stop:
        half = x.shape[0] // 2
        x = op(x[:half], x[half:])
    return x


def _moba_kernel(q_ref, k_ref, v_ref, o_ref, *, n_blocks):
    seq = n_blocks * MOBA_BLOCK
    heads = range(q_ref.shape[1] // MOBA_HD)
    cols = [slice(hd * MOBA_HD, (hd + 1) * MOBA_HD) for hd in heads]
    n_pad = BF16_ROWS
    ones_rows = jnp.where(lax.broadcasted_iota(I32, (BF16_ROWS, seq), 0) == 0, 1.0, 0.0)
    v_t = [jnp.concatenate([v_ref[:, cols[hd]].astype(F32).T, ones_rows], axis=0).astype(BF16) for hd in heads]
    blk_of_key = lax.broadcasted_iota(I32, (n_pad, seq), 1) // MOBA_BLOCK
    blk_row = lax.broadcasted_iota(I32, (n_pad, seq), 0)
    indicator = jnp.where(blk_of_key == blk_row, 1.0 / MOBA_BLOCK, 0.0).astype(BF16)
    k_mean = [_split(_dot(indicator, k_ref[:, cols[hd]])) for hd in heads]

    blk_id = lax.broadcasted_iota(I32, (n_pad, MOBA_BLOCK), 0)
    key_pos = lax.broadcasted_iota(I32, (MOBA_BLOCK, MOBA_BLOCK), 0)
    query_pos = lax.broadcasted_iota(I32, (MOBA_BLOCK, MOBA_BLOCK), 1)
    causal = key_pos <= query_pos

    def masked_scores(hd, i):
        rows = slice(i * MOBA_BLOCK, (i + 1) * MOBA_BLOCK)
        n_keys = (i + 1) * MOBA_BLOCK
        q = q_ref[rows, cols[hd]]
        s = _dot_nt(k_ref[0:n_keys, cols[hd]], q).astype(BF16)
        selected = None
        if i > MOBA_TOPK:
            km_hi, km_lo = k_mean[hd]
            gate = _dot_nt(km_hi, q) + _dot_nt(km_lo, q)
            gate = jnp.where(blk_id < i, gate, NEG_INF)
            rank = jnp.zeros_like(gate)
            for jp in range(i):
                row = gate[jp:jp + 1, :]
                beats = (row > gate) | ((row == gate) & (jp < blk_id))
                rank = rank + beats.astype(F32)
            selected = jnp.where((rank < MOBA_TOPK) & (blk_id < i), 1.0, 0.0)
        pieces = []
        top = None
        for j in range(i + 1):
            sj = s[j * MOBA_BLOCK:(j + 1) * MOBA_BLOCK, :]
            if j == i:
                sj = jnp.where(causal, sj, NEG_INF)
            elif selected is not None:
                sj = jnp.where(selected[j:j + 1, :] > 0.5, sj, NEG_INF)
            pieces.append(sj)
            folded = _reduce_rows(sj, jnp.maximum, stop=BF16_ROWS)
            top = folded if top is None else jnp.maximum(top, folded)
        return pieces, top

    def weighted_values(hd, i, p):
        o_t = _dot(v_t[hd][:, 0:(i + 1) * MOBA_BLOCK], p)
        rows = slice(i * MOBA_BLOCK, (i + 1) * MOBA_BLOCK)
        o_ref[rows, cols[hd]] = (o_t[:MOBA_HD, :] / o_t[MOBA_HD:MOBA_HD + 1, :]).T.astype(BF16)

    def softmax_numerators(scored):
        pieces, top = scored
        m = top.max(axis=0, keepdims=True)
        probs = [jnp.exp2(sj - m) for sj in pieces]
        return probs[0] if len(probs) == 1 else jnp.concatenate(probs, axis=0)

    upcoming = [masked_scores(hd, 0) for hd in heads]
    pending = None
    for i in range(n_blocks):
        pieces = upcoming
        if i + 1 < n_blocks:
            upcoming = [masked_scores(hd, i + 1) for hd in heads]
        if pending is not None:
            for hd in heads:
                weighted_values(hd, i - 1, pending[hd])
        pending = [softmax_numerators(pieces[hd]) for hd in heads]
    for hd in heads:
        weighted_values(hd, n_blocks - 1, pending[hd])


def _moba(q, k, v, ride_along, batch, seq, heads_per_step):
    t, d = q.shape
    n_groups = MOBA_HEADS // heads_per_step
    spec = pl.BlockSpec((seq, heads_per_step * MOBA_HD), lambda b, h: (b, h))
    cast_specs, cast_shapes = _ride_along_specs(ride_along, batch * n_groups, lambda b, h: b * n_groups + h)
    outs = pl.pallas_call(
        _with_ride_along(functools.partial(_moba_kernel, n_blocks=seq // MOBA_BLOCK), 3, 1, len(ride_along)),
        grid=(batch, n_groups),
        in_specs=[spec, spec, spec] + cast_specs,
        out_specs=[spec] + cast_specs,
        out_shape=[jax.ShapeDtypeStruct((t, d), BF16)] + cast_shapes,
        compiler_params=_params(("arbitrary", "arbitrary"), VMEM_LIMIT),
        name="moba_attn",
    )(q, k, v, *ride_along)
    return outs[0], outs[1:]


def _router_kernel(x_ref, o_ref, wo_ref, gain_ref, w_ref, x2_ref, h_ref, info_ref, rows_ref, cnt_ref,
                   carry_ref, tri_ref,
                   *, region_rows):
    tr = x_ref.shape[0]

    @pl.when(pl.program_id(0) == 0)
    def _():
        carry_ref[...] = jnp.zeros_like(carry_ref)
        r = lax.broadcasted_iota(I32, (TOK_BLK, TOK_BLK), 0)
        c = lax.broadcasted_iota(I32, (TOK_BLK, TOK_BLK), 1)
        tri_ref[...] = (r < c).astype(BF16)

    x2 = x_ref[...] + _dot(o_ref[...], wo_ref[...])
    x2_ref[...] = x2
    h = _rms(x2, gain_ref[...])
    h_hi, h_lo = _split(h)
    h_ref[...] = h_hi
    n_e = ROUTER_ROWS
    wt = w_ref[...]
    passes = _dot_nt(wt, h_hi)
    logits = passes[:n_e] + passes[n_e:] + _dot_nt(wt[:n_e], h_lo)
    e = lax.broadcasted_iota(I32, logits.shape, 0)
    logits = jnp.where(e < N_EXPERTS, logits, -jnp.inf)
    m1 = logits.max(axis=0, keepdims=True)
    i1 = jnp.where(logits == m1, e, n_e).min(axis=0, keepdims=True)
    first = e == i1
    rest = jnp.where(first, -jnp.inf, logits)
    m2 = rest.max(axis=0, keepdims=True)
    i2 = jnp.where(rest == m2, e, n_e).min(axis=0, keepdims=True)
    second = e == i2
    e2 = jnp.exp(m2 - m1)
    denom = 1.0 + e2
    sel = jnp.where(first | second, 1.0, 0.0).astype(BF16)
    nb = tr // TOK_BLK
    blk_row = lax.broadcasted_iota(I32, (nb, tr), 0)
    blk_tok = lax.broadcasted_iota(I32, (nb, tr), 1) // TOK_BLK
    cnt = _dot_nt((blk_row == blk_tok).astype(BF16), sel)
    cnt_ref[...] = jnp.concatenate([cnt, jnp.zeros((nb, LANES - n_e), F32)], axis=1)
    offset = carry_ref[...]
    before = []
    for b in range(nb):
        sel_b = sel[:, b * TOK_BLK:(b + 1) * TOK_BLK]
        before.append(_dot(sel_b, tri_ref[...]) + offset)
        offset = offset + jnp.broadcast_to(sel_b.astype(F32).sum(axis=1, keepdims=True), offset.shape)
    carry_ref[...] = offset
    before = jnp.concatenate(before, axis=1)
    rank1 = jnp.where(first, before, 0.0).sum(axis=0, keepdims=True)
    rank2 = jnp.where(second, before, 0.0).sum(axis=0, keepdims=True)
    fields = (i1.astype(F32) * region_rows + rank1, i2.astype(F32) * region_rows + rank2,
              1.0 / denom, e2 / denom)
    for n, f in enumerate(fields):
        for b in range(nb):
            rows_ref[n, b:b + 1, :] = f[:, b * TOK_BLK:(b + 1) * TOK_BLK]
    info_t = jnp.concatenate(list(fields) + [jnp.zeros((LANES - len(fields), tr), F32)], axis=0)
    info_ref[...] = info_t.T[:, :len(fields)]


def _router(x, o, w_out, gain, w_router, tr):
    t = x.shape[0]
    nb = tr // TOK_BLK
    pad = ((0, ROUTER_ROWS - N_EXPERTS), (0, 0))
    w_hi = w_router.T.astype(BF16)
    w_lo = (w_router.T - w_hi.astype(F32)).astype(BF16)
    w_both = jnp.concatenate([jnp.pad(w_hi, pad), jnp.pad(w_lo, pad)], axis=0)
    row = lambda i: (i, 0)
    fixed = lambda i: (0, 0)
    return pl.pallas_call(
        functools.partial(_router_kernel, region_rows=float(t)),
        grid=(t // tr,),
        in_specs=[
            pl.BlockSpec((tr, D_MODEL), row),
            pl.BlockSpec((tr, D_MODEL), row),
            pl.BlockSpec((D_MODEL, D_MODEL), fixed),
            pl.BlockSpec((1, D_MODEL), fixed),
            pl.BlockSpec((2 * ROUTER_ROWS, D_MODEL), fixed),
        ],
        out_specs=[
            pl.BlockSpec((tr, D_MODEL), row),
            pl.BlockSpec((tr, D_MODEL), row),
            pl.BlockSpec((tr, 4), row),
            pl.BlockSpec((4, nb, TOK_BLK), lambda i: (0, i, 0)),
            pl.BlockSpec((nb, LANES), row),
        ],
        out_shape=[
            jax.ShapeDtypeStruct((t, D_MODEL), F32),
            jax.ShapeDtypeStruct((t, D_MODEL), BF16),
            jax.ShapeDtypeStruct((t, 4), F32),
            jax.ShapeDtypeStruct((4, t // TOK_BLK, TOK_BLK), F32),
            jax.ShapeDtypeStruct((t // TOK_BLK, LANES), F32),
        ],
        scratch_shapes=[pltpu.VMEM((ROUTER_ROWS, LANES), F32), pltpu.VMEM((TOK_BLK, TOK_BLK), BF16)],
        compiler_params=_params(("arbitrary",), VMEM_LIMIT_RESIDENT),
        name="moe_router",
    )(x, o, w_out, gain, w_both)


def _moe_plan(cnt, t):
    e_ids = jnp.arange(N_EXPERTS, dtype=I32)
    by_expert = lambda table, idx: jnp.sum(jnp.where(idx[..., None] == e_ids, table, 0), axis=-1)
    cnt_blk = cnt[:, :N_EXPERTS].astype(I32)
    cum_blk = jnp.cumsum(cnt_blk, axis=0)
    counts = cum_blk[-1]
    tiles_e = (counts + MOE_TM - 1) // MOE_TM
    tile_end = jnp.cumsum(tiles_e)
    tile_off = tile_end - tiles_e
    nt = TOP_K * t // MOE_TM + N_EXPERTS
    r = jnp.arange(nt, dtype=I32)
    valid = r < tile_end[-1]
    te = jnp.minimum(jnp.sum((r[:, None] >= tile_end[None, :]).astype(I32), axis=1), N_EXPERTS - 1)
    local = r - by_expert(tile_off, te)
    last_e = jnp.max(jnp.where(tiles_e > 0, e_ids, 0))
    tile_expert = jnp.where(valid, te, last_e)
    shift = e_ids * t - tile_off * MOE_TM

    per = MOE_TM // MOE_TG
    rg = jnp.arange(nt * per, dtype=I32)
    eg = jnp.repeat(te, per)
    k0 = jnp.repeat(local, per) * MOE_TM + (rg % per) * MOE_TG
    k1 = jnp.minimum(k0 + MOE_TG, by_expert(counts, eg))
    has = jnp.repeat(valid, per) & (k1 > k0)
    cum_e = by_expert(cum_blk[None, :, :], eg[:, None])
    s_lo = jnp.sum((cum_e <= k0[:, None]).astype(I32), axis=1)
    s_hi = jnp.sum((cum_e <= (k1 - 1)[:, None]).astype(I32), axis=1)
    g_nwin = jnp.where(has, (s_hi - s_lo) // GATHER_WB + 1, 0)
    g_slo = jnp.where(has, s_lo, 0)
    g_code = eg * t + k0

    nts = t // COMB_TS
    cnt_ts = cnt_blk.reshape(nts, COMB_TS // TOK_BLK, N_EXPERTS).sum(axis=1)
    seg_lo = (tile_off * MOE_TM)[None, :] + jnp.cumsum(cnt_ts, axis=0) - cnt_ts
    seg_hi = seg_lo + cnt_ts
    ws0 = (seg_lo // BF16_ROWS) * BF16_ROWS
    nw = jnp.where(cnt_ts > 0, (seg_hi - ws0 + COMB_WC - 1) // COMB_WC, 0)
    nw_end = jnp.cumsum(nw, axis=1)
    k = jnp.arange(COMB_MAXW, dtype=I32)
    ek = jnp.minimum(jnp.sum((k[None, :, None] >= nw_end[:, None, :]).astype(I32), axis=2), N_EXPERTS - 1)
    pick = lambda a: by_expert(a[:, None, :], ek)
    live = k[None, :] < nw_end[:, -1:]
    nominal = pick(ws0) + (k[None, :] - pick(nw_end - nw)) * COMB_WC
    shift_k = by_expert(shift, ek)
    c_start = jnp.minimum(nominal, by_expert(tile_end * MOE_TM, ek) - COMB_WC)
    c_off = shift_k + c_start
    c_lo = jnp.maximum(pick(seg_lo), nominal) + shift_k
    c_hi = jnp.minimum(pick(seg_hi), nominal + COMB_WC) + shift_k
    flat = lambda a: jnp.where(live, a, 0).reshape(-1)
    return dict(
        tile_expert=tile_expert, tile_valid=valid.astype(I32),
        g_code=g_code, g_slo=g_slo, g_nwin=g_nwin,
        c_start=flat(c_start), c_off=flat(c_off), c_lo=flat(c_lo), c_hi=flat(c_hi), c_nwin=nw_end[:, -1])


def _moe_gather_kernel(code_ref, slo_ref, nwin_ref, rows_ref, h_ref, hs_ref, gs_ref):
    n_blk = rows_ref.shape[1]
    n_sub = hs_ref.shape[0] // MOE_TG
    sub_ids = [pl.program_id(0) * n_sub + sub for sub in range(n_sub)]
    out_rows = [slice(sub * MOE_TG, (sub + 1) * MOE_TG) for sub in range(n_sub)]

    def window(r, k):
        row_id = (code_ref[r] + lax.broadcasted_iota(I32, (MOE_TG, TOK_BLK), 0)).astype(F32)
        nominal = slo_ref[r] + k * GATHER_WB
        sb = jnp.minimum(nominal, n_blk - GATHER_WB)
        fresh = sb + lax.broadcasted_iota(I32, (GATHER_WB, TOK_BLK), 0) >= nominal
        p1 = jnp.where(fresh, rows_ref[0, pl.ds(sb, GATHER_WB), :], -1.0)
        p2 = jnp.where(fresh, rows_ref[1, pl.ds(sb, GATHER_WB), :], -1.0)
        w1 = rows_ref[2, pl.ds(sb, GATHER_WB), :]
        w2 = rows_ref[3, pl.ds(sb, GATHER_WB), :]
        pieces = []
        g = jnp.zeros((MOE_TG, TOK_BLK), F32)
        for a in range(GATHER_WB):
            m1 = p1[a:a + 1, :] == row_id
            m2 = p2[a:a + 1, :] == row_id
            pieces.append(jnp.where(m1 | m2, 1.0, 0.0).astype(BF16))
            g = g + jnp.where(m1, w1[a:a + 1, :], 0.0) + jnp.where(m2, w2[a:a + 1, :], 0.0)
        onehot = jnp.concatenate(pieces, axis=1)
        return onehot, g.sum(axis=-1, keepdims=True), pl.multiple_of(sb * TOK_BLK, TOK_BLK)

    gathered = lambda onehot, tok0: _dot(onehot, h_ref[pl.ds(tok0, GATHER_WB * TOK_BLK), :]).astype(BF16)

    firsts = [window(r, 0) for r in sub_ids]
    for rows, (onehot, weight, tok0) in zip(out_rows, firsts):
        hs_ref[rows, :] = gathered(onehot, tok0)
        gs_ref[rows, :] = weight

    for r, rows in zip(sub_ids, out_rows):
        def more(k, carry, r=r, rows=rows):
            onehot, weight, tok0 = window(r, k)
            hs_ref[rows, :] += gathered(onehot, tok0)
            gs_ref[rows, :] += weight
            return carry

        lax.fori_loop(1, nwin_ref[r], more, 0)


def _moe_gather(plan, h, rows):
    t = h.shape[0]
    n_rows = plan["g_code"].shape[0] * MOE_TG
    n_blk = t // TOK_BLK
    grid_spec = pltpu.PrefetchScalarGridSpec(
        num_scalar_prefetch=3,
        grid=(n_rows // GATHER_ROWS,),
        in_specs=[
            pl.BlockSpec((4, n_blk, TOK_BLK), lambda r, *_: (0, 0, 0), pipeline_mode=pl.Buffered(1)),
            pl.BlockSpec((t, D_MODEL), lambda r, *_: (0, 0), pipeline_mode=pl.Buffered(1)),
        ],
        out_specs=[
            pl.BlockSpec((GATHER_ROWS, D_MODEL), lambda r, *_: (r, 0)),
            pl.BlockSpec((GATHER_ROWS, 1), lambda r, *_: (r, 0)),
        ],
    )
    return pl.pallas_call(
        _moe_gather_kernel,
        grid_spec=grid_spec,
        out_shape=[
            jax.ShapeDtypeStruct((n_rows, D_MODEL), BF16),
            jax.ShapeDtypeStruct((n_rows, 1), F32),
        ],
        compiler_params=_params(("parallel",), VMEM_LIMIT_RESIDENT),
        name="moe_gather",
    )(plan["g_code"], plan["g_slo"], plan["g_nwin"], rows, h)


def _moe_expert_kernel(te_ref, tv_ref, hs_ref, gs_ref, wg_ref, wu_ref, wd_ref, y_ref, acc_ref, *, nj, pieces):
    r = pl.program_id(0)
    j = pl.program_id(1)
    valid = tv_ref[r] > 0
    part = lambda: _swiglu_pieces(hs_ref[...], wg_ref, wu_ref, wd_ref, pieces)

    if nj == 1:
        @pl.when(valid)
        def _():
            y_ref[...] = (part() * gs_ref[...]).astype(BF16)
    else:
        @pl.when(valid & (j == 0))
        def _():
            acc_ref[...] = part()

        if nj > 2:
            @pl.when(valid & (j > 0) & (j < nj - 1))
            def _():
                acc_ref[...] += part()

        @pl.when(valid & (j == nj - 1))
        def _():
            y_ref[...] = ((acc_ref[...] + part()) * gs_ref[...]).astype(BF16)

    @pl.when(jnp.logical_not(valid) & (j == nj - 1))
    def _():
        y_ref[...] = jnp.zeros_like(y_ref)


def _moe_experts(plan, hs, gs, w_gate, w_up, w_down, pieces):
    tf = sum(pieces)
    n_rows, d = hs.shape
    ff = w_gate.shape[2]
    nj = ff // tf
    jj = lambda r, j, tv: jnp.where(tv[r] > 0, j, nj - 1)
    grid_spec = pltpu.PrefetchScalarGridSpec(
        num_scalar_prefetch=2,
        grid=(n_rows // MOE_TM, nj),
        in_specs=[
            pl.BlockSpec((MOE_TM, d), lambda r, j, te, tv: (r, 0)),
            pl.BlockSpec((MOE_TM, 1), lambda r, j, te, tv: (r, 0)),
            pl.BlockSpec((None, d, tf), lambda r, j, te, tv: (te[r], 0, jj(r, j, tv))),
            pl.BlockSpec((None, d, tf), lambda r, j, te, tv: (te[r], 0, jj(r, j, tv))),
            pl.BlockSpec((None, tf, d), lambda r, j, te, tv: (te[r], jj(r, j, tv), 0)),
        ],
        out_specs=pl.BlockSpec((MOE_TM, d), lambda r, j, te, tv: (r, 0)),
        scratch_shapes=[pltpu.VMEM((MOE_TM if nj > 1 else SUBLANES, d), F32)],
    )
    return pl.pallas_call(
        functools.partial(_moe_expert_kernel, nj=nj, pieces=pieces),
        grid_spec=grid_spec,
        out_shape=jax.ShapeDtypeStruct((n_rows, d), BF16),
        compiler_params=_params(("arbitrary", "arbitrary"), VMEM_LIMIT_RESIDENT),
        name="moe_experts",
    )(plan["tile_expert"], plan["tile_valid"], hs, gs, w_gate, w_up, w_down)


def _moe_combine_kernel(start_ref, off_ref, lo_ref, hi_ref, nwin_ref, x_ref, info_ref, y_hbm, o_ref,
                        ybuf, ybuf_more, sem, sem_more):
    step = pl.program_id(0)
    n_steps = pl.num_programs(0)
    per = COMB_STEP_TILES
    ring = COMB_AHEAD_BUFS * per

    def window_copy(idx, dst, dsem):
        start = pl.multiple_of(start_ref[idx], BF16_ROWS)
        return pltpu.make_async_copy(y_hbm.at[pl.ds(start, COMB_WC), :], dst, dsem)

    def first_group(tile, wait):
        buf = tile % ring
        for s in range(COMB_GROUP):
            @pl.when(s < nwin_ref[tile])
            def _():
                cp = window_copy(tile * COMB_MAXW + s, ybuf.at[buf, pl.ds(s * COMB_WC, COMB_WC), :],
                                 sem.at[buf, s])
                cp.wait() if wait else cp.start()

    def request(grid_step):
        for sub in range(per):
            first_group(grid_step * per + sub, wait=False)

    @pl.when(step == 0)
    def _():
        ybuf[...] = jnp.zeros_like(ybuf)
        ybuf_more[...] = jnp.zeros_like(ybuf_more)
        for ahead in range(COMB_AHEAD_BUFS - 1):
            @pl.when(ahead < n_steps)
            def _():
                request(ahead)

    @pl.when(step + COMB_AHEAD_BUFS - 1 < n_steps)
    def _():
        request(step + COMB_AHEAD_BUFS - 1)

    tiles = [step * per + sub for sub in range(per)]
    rows = [slice(sub * COMB_TS, (sub + 1) * COMB_TS) for sub in range(per)]
    for tile in tiles:
        first_group(tile, wait=True)

    lane = lax.broadcasted_iota(I32, (COMB_TS, COMB_WC), 1)
    lane_row = lax.broadcasted_iota(I32, (1, COMB_WC), 1)
    codes = [[jnp.broadcast_to(info_ref[r, n:n + 1].astype(I32), (COMB_TS, COMB_WC)) for n in range(TOP_K)]
             for r in rows]

    def onehot_group(sub, g):
        pieces = []
        for s in range(COMB_GROUP):
            idx = tiles[sub] * COMB_MAXW + g * COMB_GROUP + s
            off = off_ref[idx]
            in_span = (lane_row >= lo_ref[idx] - off) & (lane_row < hi_ref[idx] - off)
            target = lane + off
            hit = ((codes[sub][0] == target) | (codes[sub][1] == target)) & in_span
            pieces.append(jnp.where(hit, 1.0, 0.0).astype(BF16))
        return jnp.concatenate(pieces, axis=1)

    onehots = [onehot_group(sub, 0) for sub in range(per)]
    for sub in range(per):
        o_ref[rows[sub], :] = x_ref[rows[sub], :] + _dot(onehots[sub], ybuf[tiles[sub] % ring])

    for sub in range(per):
        n = nwin_ref[tiles[sub]]

        def more(g, carry, sub=sub, n=n):
            for wait in (False, True):
                for s in range(COMB_GROUP):
                    @pl.when(g * COMB_GROUP + s < n)
                    def _():
                        cp = window_copy(tiles[sub] * COMB_MAXW + g * COMB_GROUP + s,
                                         ybuf_more.at[pl.ds(s * COMB_WC, COMB_WC), :], sem_more.at[s])
                        cp.wait() if wait else cp.start()
            o_ref[rows[sub], :] += _dot(onehot_group(sub, g), ybuf_more[...])
            return carry

        lax.fori_loop(1, (n + COMB_GROUP - 1) // COMB_GROUP, more, 0)


def _moe_combine(plan, x, info, y):
    t, d = x.shape
    width = COMB_GROUP * COMB_WC
    step_rows = COMB_STEP_TILES * COMB_TS
    grid_spec = pltpu.PrefetchScalarGridSpec(
        num_scalar_prefetch=5,
        grid=(t // step_rows,),
        in_specs=[
            pl.BlockSpec((step_rows, d), lambda i, *_: (i, 0)),
            pl.BlockSpec((step_rows, 4), lambda i, *_: (i, 0)),
            pl.BlockSpec(memory_space=pl.ANY),
        ],
        out_specs=pl.BlockSpec((step_rows, d), lambda i, *_: (i, 0)),
        scratch_shapes=[
            pltpu.VMEM((COMB_AHEAD_BUFS * COMB_STEP_TILES, width, d), BF16),
            pltpu.VMEM((width, d), BF16),
            pltpu.SemaphoreType.DMA((COMB_AHEAD_BUFS * COMB_STEP_TILES, COMB_GROUP)),
            pltpu.SemaphoreType.DMA((COMB_GROUP,)),
        ],
    )
    return pl.pallas_call(
        _moe_combine_kernel,
        grid_spec=grid_spec,
        out_shape=jax.ShapeDtypeStruct((t, d), F32),
        compiler_params=_params(("arbitrary",), VMEM_LIMIT),
        name="moe_combine",
    )(plan["c_start"], plan["c_off"], plan["c_lo"], plan["c_hi"], plan["c_nwin"], x, info, y)


def kernel(x, norm_mix, norm_ffn, gla_w_in, gla_w_gate2, gla_b_gate, gla_out_gain, gla_w_out,
           moba_w_qkv, moba_q_gain, moba_k_gain, moba_w_out, ffn_w_gate, ffn_w_up, ffn_w_down,
           moe_w_router, moe_w_gate, moe_w_up, moe_w_down):
    batch, seq, d = x.shape
    t = batch * seq
    xt = x.reshape(t, d)

    (q, k, v, g, la), (ffn_wg, ffn_wu, ffn_wd, gla_wo, moba_wqkv, moba_wo) = _gla_inproj(
        xt, norm_mix[0:1], gla_w_in, gla_w_gate2[0], gla_b_gate[0:1],
        ride_along=(ffn_w_gate, ffn_w_up, ffn_w_down, gla_w_out, moba_w_qkv, moba_w_out), tm=PROJ_TM)
    o, _ = _gla(q, k, v, g, la, gla_out_gain[0:1], (), batch, seq, blk=GLA_BLK,
                rows_per_step=GLA_ROWS_PER_STEP)
    xt = _ffn_residual(xt, o, gla_wo[0], norm_ffn[0:1], ffn_wg[0], ffn_wu[0], ffn_wd[0],
                       tm=PROJ_TM, pieces=MOE_PIECES + MOE_PIECES)

    q, k, v = _moba_qkv(xt, norm_mix[1:2], moba_wqkv[0], moba_q_gain[0:1], moba_k_gain[0:1], tm=PROJ_TM)
    o, (moe_wg, moe_wu, moe_wd) = _moba(q, k, v, (moe_w_gate[0], moe_w_up[0], moe_w_down[0]), batch, seq,
                                        heads_per_step=MOBA_HEADS_PER_STEP)
    xt, h, info, rows, cnt = _router(xt, o, moba_wo[0], norm_ffn[1:2], moe_w_router[0], tr=ROUTER_TR)
    plan = _moe_plan(cnt, t)
    hs, gs = _moe_gather(plan, h, rows)
    y = _moe_experts(plan, hs, gs, moe_wg, moe_wu, moe_wd, pieces=MOE_PIECES + MOE_PIECES)
    xt = _moe_combine(plan, xt, info, y)
    return xt.reshape(batch, seq, d)
```

```python
import functools

import jax
import jax.numpy as jnp
from jax import lax
from jax.experimental import pallas as pl
from jax.experimental.pallas import tpu as pltpu

F32 = jnp.float32
BF16 = jnp.bfloat16
I32 = jnp.int32

EPS = 1e-6
NEG_INF = -1e30

D_MODEL = 1024
GLA_HEADS = 4
GLA_DK = 512
GLA_DV = 1024
GLA_HK = 128
GLA_HV = 256
GLA_RANK = 16
GLA_NORMALIZER = 16.0
GLA_CHUNK = 64

MOBA_HEADS = 8
MOBA_HD = 128
MOBA_BLOCK = 256
MOBA_TOPK = 3
MOBA_Q_SCALE = MOBA_HD ** -0.5 * 1.4426950408889634

N_EXPERTS = 8
TOP_K = 2

LANES = 128
SUBLANES = 8
BF16_ROWS = 16
ROUTER_ROWS = BF16_ROWS
VMEM_LIMIT = 48 * 1024 * 1024
VMEM_LIMIT_RESIDENT = 56 * 1024 * 1024

PROJ_TM = 512
ROUTER_TR = 1024
GLA_BLK = 256
GLA_ROWS_PER_STEP = 4
MOBA_HEADS_PER_STEP = 2

MOE_PIECES = (512, 512, 768)
MOE_TM = 256
MOE_TG = 128
GATHER_ROWS = 512
TOK_BLK = LANES
GATHER_WB = 6
COMB_TS = 256
COMB_STEP_TILES = 2
COMB_WC = 128
COMB_GROUP = 8
COMB_AHEAD_BUFS = 3
COMB_MAXW = 16


def _params(semantics, vmem_bytes=None):
    return pltpu.CompilerParams(dimension_semantics=semantics, vmem_limit_bytes=vmem_bytes)


def _rms(x, gain):
    y = x * lax.rsqrt(jnp.mean(x * x, axis=-1, keepdims=True) + EPS)
    return y * gain


def _dot(a, b):
    return jnp.dot(a, b, preferred_element_type=F32)


def _dot_nt(a, b):
    return lax.dot_general(a, b, (((1,), (1,)), ((), ())), preferred_element_type=F32)


def _dot_tn(a, b):
    return lax.dot_general(a, b, (((0,), (0,)), ((), ())), preferred_element_type=F32)


def _split(a):
    hi = a.astype(BF16)
    lo = (a - hi.astype(F32)).astype(BF16)
    return hi, lo


def _log_sigmoid(z):
    return jnp.minimum(z, 0.0) - jnp.log(1.0 + jnp.exp(-jnp.abs(z)))


def _silu(z):
    return z / (1.0 + jnp.exp(-z))


def _swiglu_pieces(h, wg_ref, wu_ref, wd_ref, widths):
    bounds = [sum(widths[:n]) for n in range(len(widths) + 1)]
    assert bounds[-1] == wg_ref.shape[1], (widths, wg_ref.shape)
    gate_up = lambda c: (_dot(h, wg_ref[:, bounds[c]:bounds[c + 1]]), _dot(h, wu_ref[:, bounds[c]:bounds[c + 1]]))
    out = None
    upcoming = gate_up(0)
    for c in range(len(widths)):
        g, u = upcoming
        if c + 1 < len(widths):
            upcoming = gate_up(c + 1)
        d = _dot((_silu(g) * u).astype(BF16), wd_ref[bounds[c]:bounds[c + 1], :])
        out = d if out is None else out + d
    return out


def _ride_along_specs(arrays, n_steps, step):
    specs, shapes = [], []
    for w in arrays:
        n_e, rows, cols = w.shape
        per_e = n_steps // n_e
        if rows % (per_e * BF16_ROWS) == 0:
            blk = (None, rows // per_e, cols)
            imap = lambda *ids, per_e=per_e: (step(*ids) // per_e, step(*ids) % per_e, 0)
        else:
            assert cols % (per_e * LANES) == 0, (w.shape, n_steps)
            blk = (None, rows, cols // per_e)
            imap = lambda *ids, per_e=per_e: (step(*ids) // per_e, 0, step(*ids) % per_e)
        specs.append(pl.BlockSpec(blk, imap))
        shapes.append(jax.ShapeDtypeStruct(w.shape, BF16))
    return specs, shapes


def _with_ride_along(body, n_in, n_out, n_cast):
    def kernel(*refs):
        ins, refs = refs[:n_in], refs[n_in:]
        cast_in, refs = refs[:n_cast], refs[n_cast:]
        outs, refs = refs[:n_out], refs[n_out:]
        cast_out, scratch = refs[:n_cast], refs[n_cast:]
        for src, dst in zip(cast_in, cast_out):
            dst[...] = src[...].astype(BF16)
        body(*ins, *outs, *scratch)
    return kernel


def _gla_inproj_kernel(x_next_ref, x_first_ref, gain_ref, w_ref, wg2_ref, b_ref,
                       q_ref, k_ref, v_ref, g_ref, la_ref, wb_ref, h_ref):
    i = pl.program_id(0)

    @pl.when(i == 0)
    def _():
        wb_ref[...] = w_ref[...].astype(BF16)
        h_ref[0] = _rms(x_first_ref[...], gain_ref[...]).astype(BF16)

    n_main = 2 * GLA_DK + 2 * GLA_DV

    def step(cur, nxt):
        h = h_ref[cur]
        h_ref[nxt] = _rms(x_next_ref[...], gain_ref[...]).astype(BF16)
        a_lr = _dot(h, wb_ref[:, n_main:])
        a_hi, a_lo = _split(a_lr)
        w_hi, w_lo = _split(wg2_ref[...])
        z = _dot(jnp.concatenate([a_hi, a_lo, a_hi], axis=1), jnp.concatenate([w_hi, w_hi, w_lo], axis=0))
        la_ref[...] = _log_sigmoid(z + b_ref[...]) / GLA_NORMALIZER
        q_ref[...] = _dot(h, wb_ref[:, 0:GLA_DK]).astype(BF16)
        k_ref[...] = _dot(h, wb_ref[:, GLA_DK:2 * GLA_DK]).astype(BF16)
        v_ref[...] = _dot(h, wb_ref[:, 2 * GLA_DK:2 * GLA_DK + GLA_DV]).astype(BF16)
        g_ref[...] = _dot(h, wb_ref[:, 2 * GLA_DK + GLA_DV:n_main]).astype(BF16)

    @pl.when(i % 2 == 0)
    def _():
        step(0, 1)

    @pl.when(i % 2 == 1)
    def _():
        step(1, 0)


def _gla_inproj(x, gain, w_in, w_gate2, b_gate, ride_along, tm):
    t = x.shape[0]
    n_steps = t // tm
    row = lambda i: (i, 0)
    fixed = lambda i: (0, 0)
    cast_specs, cast_shapes = _ride_along_specs(ride_along, n_steps, lambda i: i)
    outs = pl.pallas_call(
        _with_ride_along(_gla_inproj_kernel, 6, 5, len(ride_along)),
        grid=(n_steps,),
        in_specs=[
            pl.BlockSpec((tm, D_MODEL), lambda i: (jnp.minimum(i + 1, n_steps - 1), 0)),
            pl.BlockSpec((tm, D_MODEL), fixed),
            pl.BlockSpec((1, D_MODEL), fixed),
            pl.BlockSpec((None,) + w_in.shape[1:], lambda i: (0, 0, 0), pipeline_mode=pl.Buffered(1)),
            pl.BlockSpec((GLA_RANK, GLA_DK), fixed),
            pl.BlockSpec((1, GLA_DK), fixed),
        ] + cast_specs,
        out_specs=[
            pl.BlockSpec((tm, GLA_DK), row),
            pl.BlockSpec((tm, GLA_DK), row),
            pl.BlockSpec((tm, GLA_DV), row),
            pl.BlockSpec((tm, GLA_DV), row),
            pl.BlockSpec((tm, GLA_DK), row),
        ] + cast_specs,
        out_shape=[
            jax.ShapeDtypeStruct((t, GLA_DK), BF16),
            jax.ShapeDtypeStruct((t, GLA_DK), BF16),
            jax.ShapeDtypeStruct((t, GLA_DV), BF16),
            jax.ShapeDtypeStruct((t, GLA_DV), BF16),
            jax.ShapeDtypeStruct((t, GLA_DK), F32),
        ] + cast_shapes,
        scratch_shapes=[pltpu.VMEM(w_in.shape[1:], BF16), pltpu.VMEM((2, tm, D_MODEL), BF16)],
        compiler_params=_params(("arbitrary",), VMEM_LIMIT),
        name="gla_inproj",
    )(x, x, gain, w_in, w_gate2, b_gate, *ride_along)
    return outs[:5], outs[5:]


def _gla_kernel(q_ref, k_ref, v_ref, g_ref, la_ref, og_ref, o_ref, st_ref, *, n_chunks):
    @pl.when(pl.program_id(1) == 0)
    def _():
        st_ref[...] = jnp.zeros_like(st_ref)

    n_rows = q_ref.shape[0]
    blk = n_chunks * GLA_CHUNK
    r = lax.broadcasted_iota(I32, (blk, blk), 0)
    c = lax.broadcasted_iota(I32, (blk, blk), 1)
    causal = (r >= c) & (r // GLA_CHUNK == c // GLA_CHUNK)
    tril = causal.astype(BF16)
    og = og_ref[...]
    chunk_rows = [slice(ci * GLA_CHUNK, (ci + 1) * GLA_CHUNK) for ci in range(n_chunks)]
    hk = [slice(h * GLA_HK, (h + 1) * GLA_HK) for h in range(GLA_HEADS)]
    hv = [slice(h * GLA_HV, (h + 1) * GLA_HV) for h in range(GLA_HEADS)]

    q_dec, k_inv, k_end, chunk_decay = [], [], [], []
    for b in range(n_rows):
        la = la_ref[b]
        la_hi, la_mid = _split(la)
        la_lo = (la - la_hi.astype(F32) - la_mid.astype(F32)).astype(BF16)
        cum = _dot(tril, la_hi) + _dot(tril, la_mid) + _dot(tril, la_lo)
        ends = [cum[rows.stop - 1:rows.stop, :] for rows in chunk_rows]
        last = jnp.concatenate([jnp.broadcast_to(e, (GLA_CHUNK, GLA_DK)) for e in ends], axis=0)
        q = q_ref[b].astype(F32) * (GLA_HK ** -0.5)
        k = k_ref[b].astype(F32)
        q_dec.append((q * jnp.exp(cum)).astype(BF16))
        k_inv.append((k * jnp.exp(-cum)).astype(BF16))
        k_end.append((k * jnp.exp(last - cum)).astype(BF16))
        chunk_decay.append([jnp.exp(e) for e in ends])

    streams = [(b, h) for b in range(n_rows) for h in range(GLA_HEADS)]
    scores = [jnp.where(causal, _dot_nt(q_dec[b][:, hk[h]], k_inv[b][:, hk[h]]), 0.0).astype(BF16)
              for b, h in streams]
    state_in = [[_dot_tn(v_ref[b, rows, hv[h]], k_end[b][rows, hk[h]]) for rows in chunk_rows]
                for b, h in streams]
    intra = [_dot(scores[n], v_ref[b, :, hv[h]]) for n, (b, h) in enumerate(streams)]
    for n, (b, h) in enumerate(streams):
        st = st_ref[b, h]
        inter = []
        for ci, rows in enumerate(chunk_rows):
            inter.append(_dot_nt(q_dec[b][rows, hk[h]], st.astype(BF16)))
            st = st * chunk_decay[b][ci][:, hk[h]] + state_in[n][ci]
        st_ref[b, h] = st
        o = intra[n] + jnp.concatenate(inter, axis=0)
        o = o * lax.rsqrt(jnp.mean(o * o, axis=-1, keepdims=True) + EPS) * og
        o = o * _silu(g_ref[b, :, hv[h]].astype(F32))
        o_ref[b, :, hv[h]] = o.astype(BF16)


def _gla(q, k, v, g, la, out_gain, ride_along, batch, seq, blk, rows_per_step):
    nb = seq // blk
    nbt = batch // rows_per_step
    per_row = lambda a: a.reshape(batch, seq, a.shape[-1])
    spec = lambda width: pl.BlockSpec((rows_per_step, blk, width), lambda b, i: (b, i, 0))
    cast_specs, cast_shapes = _ride_along_specs(ride_along, nbt * nb, lambda b, i: b * nb + i)
    outs = pl.pallas_call(
        _with_ride_along(functools.partial(_gla_kernel, n_chunks=blk // GLA_CHUNK), 6, 1, len(ride_along)),
        grid=(nbt, nb),
        in_specs=[spec(GLA_DK), spec(GLA_DK), spec(GLA_DV), spec(GLA_DV), spec(GLA_DK),
                  pl.BlockSpec((1, GLA_HV), lambda b, i: (0, 0))] + cast_specs,
        out_specs=[spec(GLA_DV)] + cast_specs,
        out_shape=[jax.ShapeDtypeStruct((batch, seq, GLA_DV), BF16)] + cast_shapes,
        scratch_shapes=[pltpu.VMEM((rows_per_step, GLA_HEADS, GLA_HV, GLA_HK), F32)],
        compiler_params=_params(("arbitrary", "arbitrary"), VMEM_LIMIT),
        name="gla_core",
    )(per_row(q), per_row(k), per_row(v), per_row(g), per_row(la), out_gain, *ride_along)
    return outs[0].reshape(batch * seq, GLA_DV), outs[1:]


def _ffn_kernel(x_ref, o_ref, wo_ref, gain_ref, wg_ref, wu_ref, wd_ref, y_ref, *, pieces):
    x1 = x_ref[...] + _dot(o_ref[...], wo_ref[...])
    h = _rms(x1, gain_ref[...]).astype(BF16)
    y_ref[...] = x1 + _swiglu_pieces(h, wg_ref, wu_ref, wd_ref, pieces)


def _ffn_residual(x, o, w_out, gain, w_gate, w_up, w_down, tm, pieces):
    t, d = x.shape
    row = lambda i: (i, 0)
    resident = lambda a: pl.BlockSpec(a.shape, lambda i: (0, 0), pipeline_mode=pl.Buffered(1))
    return pl.pallas_call(
        functools.partial(_ffn_kernel, pieces=pieces),
        grid=(t // tm,),
        in_specs=[pl.BlockSpec((tm, d), row), pl.BlockSpec((tm, d), row), resident(w_out), resident(gain),
                  resident(w_gate), resident(w_up), resident(w_down)],
        out_specs=pl.BlockSpec((tm, d), row),
        out_shape=jax.ShapeDtypeStruct((t, d), F32),
        compiler_params=_params(("parallel",), VMEM_LIMIT_RESIDENT),
        name="ffn_swiglu",
    )(x, o, w_out, gain, w_gate, w_up, w_down)


def _moba_qkv_kernel(x_ref, gain_ref, w_ref, qg_ref, kg_ref, q_ref, k_ref, v_ref):
    h = _rms(x_ref[...], gain_ref[...]).astype(BF16)
    d = MOBA_HEADS * MOBA_HD
    qg = qg_ref[...]
    kg = kg_ref[...]
    q = _dot(h, w_ref[:, 0:d])
    k = _dot(h, w_ref[:, d:2 * d])
    for hd in range(MOBA_HEADS):
        cols = slice(hd * MOBA_HD, (hd + 1) * MOBA_HD)
        q_ref[:, cols] = (_rms(q[:, cols], qg) * MOBA_Q_SCALE).astype(BF16)
        k_ref[:, cols] = _rms(k[:, cols], kg).astype(BF16)
    v_ref[...] = _dot(h, w_ref[:, 2 * d:]).astype(BF16)


def _moba_qkv(x, gain, w, q_gain, k_gain, tm):
    t = x.shape[0]
    d = MOBA_HEADS * MOBA_HD
    row = lambda i: (i, 0)
    fixed = lambda i: (0, 0)
    return pl.pallas_call(
        _moba_qkv_kernel,
        grid=(t // tm,),
        in_specs=[
            pl.BlockSpec((tm, D_MODEL), row),
            pl.BlockSpec((1, D_MODEL), fixed),
            pl.BlockSpec((D_MODEL, 3 * d), fixed),
            pl.BlockSpec((1, MOBA_HD), fixed),
            pl.BlockSpec((1, MOBA_HD), fixed),
        ],
        out_specs=[pl.BlockSpec((tm, d), row)] * 3,
        out_shape=[jax.ShapeDtypeStruct((t, d), BF16)] * 3,
        compiler_params=_params(("parallel",), VMEM_LIMIT),
        name="moba_qkv",
    )(x, gain, w, q_gain, k_gain)


def _reduce_rows(x, op, stop=SUBLANES):
    while x.shape[0] > stop:
        half = x.shape[0] // 2
        x = op(x[:half], x[half:])
    return x


def _moba_kernel(q_ref, k_ref, v_ref, o_ref, *, n_blocks):
    seq = n_blocks * MOBA_BLOCK
    heads = range(q_ref.shape[1] // MOBA_HD)
    cols = [slice(hd * MOBA_HD, (hd + 1) * MOBA_HD) for hd in heads]
    n_pad = BF16_ROWS
    ones_rows = jnp.where(lax.broadcasted_iota(I32, (BF16_ROWS, seq), 0) == 0, 1.0, 0.0)
    v_t = [jnp.concatenate([v_ref[:, cols[hd]].astype(F32).T, ones_rows], axis=0).astype(BF16) for hd in heads]
    blk_of_key = lax.broadcasted_iota(I32, (n_pad, seq), 1) // MOBA_BLOCK
    blk_row = lax.broadcasted_iota(I32, (n_pad, seq), 0)
    indicator = jnp.where(blk_of_key == blk_row, 1.0 / MOBA_BLOCK, 0.0).astype(BF16)
    k_mean = [_split(_dot(indicator, k_ref[:, cols[hd]])) for hd in heads]

    blk_id = lax.broadcasted_iota(I32, (n_pad, MOBA_BLOCK), 0)
    key_pos = lax.broadcasted_iota(I32, (MOBA_BLOCK, MOBA_BLOCK), 0)
    query_pos = lax.broadcasted_iota(I32, (MOBA_BLOCK, MOBA_BLOCK), 1)
    causal = key_pos <= query_pos

    def masked_scores(hd, i):
        rows = slice(i * MOBA_BLOCK, (i + 1) * MOBA_BLOCK)
        n_keys = (i + 1) * MOBA_BLOCK
        q = q_ref[rows, cols[hd]]
        s = _dot_nt(k_ref[0:n_keys, cols[hd]], q).astype(BF16)
        selected = None
        if i > MOBA_TOPK:
            km_hi, km_lo = k_mean[hd]
            gate = _dot_nt(km_hi, q) + _dot_nt(km_lo, q)
            gate = jnp.where(blk_id < i, gate, NEG_INF)
            rank = jnp.zeros_like(gate)
            for jp in range(i):
                row = gate[jp:jp + 1, :]
                beats = (row > gate) | ((row == gate) & (jp < blk_id))
                rank = rank + beats.astype(F32)
            selected = jnp.where((rank < MOBA_TOPK) & (blk_id < i), 1.0, 0.0)
        pieces = []
        top = None
        for j in range(i + 1):
            sj = s[j * MOBA_BLOCK:(j + 1) * MOBA_BLOCK, :]
            if j == i:
                sj = jnp.where(causal, sj, NEG_INF)
            elif selected is not None:
                sj = jnp.where(selected[j:j + 1, :] > 0.5, sj, NEG_INF)
            pieces.append(sj)
            folded = _reduce_rows(sj, jnp.maximum, stop=BF16_ROWS)
            top = folded if top is None else jnp.maximum(top, folded)
        return pieces, top

    def weighted_values(hd, i, p):
        o_t = _dot(v_t[hd][:, 0:(i + 1) * MOBA_BLOCK], p)
        rows = slice(i * MOBA_BLOCK, (i + 1) * MOBA_BLOCK)
        o_ref[rows, cols[hd]] = (o_t[:MOBA_HD, :] / o_t[MOBA_HD:MOBA_HD + 1, :]).T.astype(BF16)

    def softmax_numerators(scored):
        pieces, top = scored
        m = top.max(axis=0, keepdims=True)
        probs = [jnp.exp2(sj - m) for sj in pieces]
        return probs[0] if len(probs) == 1 else jnp.concatenate(probs, axis=0)

    upcoming = [masked_scores(hd, 0) for hd in heads]
    pending = None
    for i in range(n_blocks):
        pieces = upcoming
        if i + 1 < n_blocks:
            upcoming = [masked_scores(hd, i + 1) for hd in heads]
        if pending is not None:
            for hd in heads:
                weighted_values(hd, i - 1, pending[hd])
        pending = [softmax_numerators(pieces[hd]) for hd in heads]
    for hd in heads:
        weighted_values(hd, n_blocks - 1, pending[hd])


def _moba(q, k, v, ride_along, batch, seq, heads_per_step):
    t, d = q.shape
    n_groups = MOBA_HEADS // heads_per_step
    spec = pl.BlockSpec((seq, heads_per_step * MOBA_HD), lambda b, h: (b, h))
    cast_specs, cast_shapes = _ride_along_specs(ride_along, batch * n_groups, lambda b, h: b * n_groups + h)
    outs = pl.pallas_call(
        _with_ride_along(functools.partial(_moba_kernel, n_blocks=seq // MOBA_BLOCK), 3, 1, len(ride_along)),
        grid=(batch, n_groups),
        in_specs=[spec, spec, spec] + cast_specs,
        out_specs=[spec] + cast_specs,
        out_shape=[jax.ShapeDtypeStruct((t, d), BF16)] + cast_shapes,
        compiler_params=_params(("arbitrary", "arbitrary"), VMEM_LIMIT_RESIDENT),
        name="moba_attn",
    )(q, k, v, *ride_along)
    return outs[0], outs[1:]


def _router_kernel(x_ref, o_ref, wo_ref, gain_ref, w_ref, x2_ref, h_ref, info_ref, rows_ref, cnt_ref,
                   carry_ref, tri_ref,
                   *, region_rows):
    tr = x_ref.shape[0]

    @pl.when(pl.program_id(0) == 0)
    def _():
        carry_ref[...] = jnp.zeros_like(carry_ref)
        r = lax.broadcasted_iota(I32, (TOK_BLK, TOK_BLK), 0)
        c = lax.broadcasted_iota(I32, (TOK_BLK, TOK_BLK), 1)
        tri_ref[...] = (r < c).astype(BF16)

    x2 = x_ref[...] + _dot(o_ref[...], wo_ref[...])
    x2_ref[...] = x2
    h = _rms(x2, gain_ref[...])
    h_hi, h_lo = _split(h)
    h_ref[...] = h_hi
    n_e = ROUTER_ROWS
    wt = w_ref[...]
    passes = _dot_nt(wt, h_hi)
    logits = passes[:n_e] + passes[n_e:] + _dot_nt(wt[:n_e], h_lo)
    e = lax.broadcasted_iota(I32, logits.shape, 0)
    logits = jnp.where(e < N_EXPERTS, logits, -jnp.inf)
    m1 = logits.max(axis=0, keepdims=True)
    i1 = jnp.where(logits == m1, e, n_e).min(axis=0, keepdims=True)
    first = e == i1
    rest = jnp.where(first, -jnp.inf, logits)
    m2 = rest.max(axis=0, keepdims=True)
    i2 = jnp.where(rest == m2, e, n_e).min(axis=0, keepdims=True)
    second = e == i2
    e2 = jnp.exp(m2 - m1)
    denom = 1.0 + e2
    sel = jnp.where(first | second, 1.0, 0.0).astype(BF16)
    nb = tr // TOK_BLK
    blk_row = lax.broadcasted_iota(I32, (nb, tr), 0)
    blk_tok = lax.broadcasted_iota(I32, (nb, tr), 1) // TOK_BLK
    cnt = _dot_nt((blk_row == blk_tok).astype(BF16), sel)
    cnt_ref[...] = jnp.concatenate([cnt, jnp.zeros((nb, LANES - n_e), F32)], axis=1)
    offset = carry_ref[...]
    before = []
    for b in range(nb):
        sel_b = sel[:, b * TOK_BLK:(b + 1) * TOK_BLK]
        before.append(_dot(sel_b, tri_ref[...]) + offset)
        offset = offset + jnp.broadcast_to(sel_b.astype(F32).sum(axis=1, keepdims=True), offset.shape)
    carry_ref[...] = offset
    before = jnp.concatenate(before, axis=1)
    rank1 = jnp.where(first, before, 0.0).sum(axis=0, keepdims=True)
    rank2 = jnp.where(second, before, 0.0).sum(axis=0, keepdims=True)
    fields = (i1.astype(F32) * region_rows + rank1, i2.astype(F32) * region_rows + rank2,
              1.0 / denom, e2 / denom)
    for n, f in enumerate(fields):
        for b in range(nb):
            rows_ref[n, b:b + 1, :] = f[:, b * TOK_BLK:(b + 1) * TOK_BLK]
    info_t = jnp.concatenate(list(fields) + [jnp.zeros((LANES - len(fields), tr), F32)], axis=0)
    info_ref[...] = info_t.T[:, :len(fields)]


def _router(x, o, w_out, gain, w_router, tr):
    t = x.shape[0]
    nb = tr // TOK_BLK
    pad = ((0, ROUTER_ROWS - N_EXPERTS), (0, 0))
    w_hi = w_router.T.astype(BF16)
    w_lo = (w_router.T - w_hi.astype(F32)).astype(BF16)
    w_both = jnp.concatenate([jnp.pad(w_hi, pad), jnp.pad(w_lo, pad)], axis=0)
    row = lambda i: (i, 0)
    fixed = lambda i: (0, 0)
    return pl.pallas_call(
        functools.partial(_router_kernel, region_rows=float(t)),
        grid=(t // tr,),
        in_specs=[
            pl.BlockSpec((tr, D_MODEL), row),
            pl.BlockSpec((tr, D_MODEL), row),
            pl.BlockSpec((D_MODEL, D_MODEL), fixed),
            pl.BlockSpec((1, D_MODEL), fixed),
            pl.BlockSpec((2 * ROUTER_ROWS, D_MODEL), fixed),
        ],
        out_specs=[
            pl.BlockSpec((tr, D_MODEL), row),
            pl.BlockSpec((tr, D_MODEL), row),
            pl.BlockSpec((tr, 4), row),
            pl.BlockSpec((4, nb, TOK_BLK), lambda i: (0, i, 0)),
            pl.BlockSpec((nb, LANES), row),
        ],
        out_shape=[
            jax.ShapeDtypeStruct((t, D_MODEL), F32),
            jax.ShapeDtypeStruct((t, D_MODEL), BF16),
            jax.ShapeDtypeStruct((t, 4), F32),
            jax.ShapeDtypeStruct((4, t // TOK_BLK, TOK_BLK), F32),
            jax.ShapeDtypeStruct((t // TOK_BLK, LANES), F32),
        ],
        scratch_shapes=[pltpu.VMEM((ROUTER_ROWS, LANES), F32), pltpu.VMEM((TOK_BLK, TOK_BLK), BF16)],
        compiler_params=_params(("arbitrary",), VMEM_LIMIT_RESIDENT),
        name="moe_router",
    )(x, o, w_out, gain, w_both)


def _moe_plan(cnt, t):
    e_ids = jnp.arange(N_EXPERTS, dtype=I32)
    by_expert = lambda table, idx: jnp.sum(jnp.where(idx[..., None] == e_ids, table, 0), axis=-1)
    cnt_blk = cnt[:, :N_EXPERTS].astype(I32)
    cum_blk = jnp.cumsum(cnt_blk, axis=0)
    counts = cum_blk[-1]
    tiles_e = (counts + MOE_TM - 1) // MOE_TM
    tile_end = jnp.cumsum(tiles_e)
    tile_off = tile_end - tiles_e
    nt = TOP_K * t // MOE_TM + N_EXPERTS
    r = jnp.arange(nt, dtype=I32)
    valid = r < tile_end[-1]
    te = jnp.minimum(jnp.sum((r[:, None] >= tile_end[None, :]).astype(I32), axis=1), N_EXPERTS - 1)
    local = r - by_expert(tile_off, te)
    last_e = jnp.max(jnp.where(tiles_e > 0, e_ids, 0))
    tile_expert = jnp.where(valid, te, last_e)
    shift = e_ids * t - tile_off * MOE_TM

    per = MOE_TM // MOE_TG
    rg = jnp.arange(nt * per, dtype=I32)
    eg = jnp.repeat(te, per)
    k0 = jnp.repeat(local, per) * MOE_TM + (rg % per) * MOE_TG
    k1 = jnp.minimum(k0 + MOE_TG, by_expert(counts, eg))
    has = jnp.repeat(valid, per) & (k1 > k0)
    cum_e = by_expert(cum_blk[None, :, :], eg[:, None])
    s_lo = jnp.sum((cum_e <= k0[:, None]).astype(I32), axis=1)
    s_hi = jnp.sum((cum_e <= (k1 - 1)[:, None]).astype(I32), axis=1)
    g_nwin = jnp.where(has, (s_hi - s_lo) // GATHER_WB + 1, 0)
    g_slo = jnp.where(has, s_lo, 0)
    g_code = eg * t + k0

    nts = t // COMB_TS
    cnt_ts = cnt_blk.reshape(nts, COMB_TS // TOK_BLK, N_EXPERTS).sum(axis=1)
    seg_lo = (tile_off * MOE_TM)[None, :] + jnp.cumsum(cnt_ts, axis=0) - cnt_ts
    seg_hi = seg_lo + cnt_ts
    ws0 = (seg_lo // BF16_ROWS) * BF16_ROWS
    nw = jnp.where(cnt_ts > 0, (seg_hi - ws0 + COMB_WC - 1) // COMB_WC, 0)
    nw_end = jnp.cumsum(nw, axis=1)
    k = jnp.arange(COMB_MAXW, dtype=I32)
    ek = jnp.minimum(jnp.sum((k[None, :, None] >= nw_end[:, None, :]).astype(I32), axis=2), N_EXPERTS - 1)
    pick = lambda a: by_expert(a[:, None, :], ek)
    live = k[None, :] < nw_end[:, -1:]
    nominal = pick(ws0) + (k[None, :] - pick(nw_end - nw)) * COMB_WC
    shift_k = by_expert(shift, ek)
    c_start = jnp.minimum(nominal, by_expert(tile_end * MOE_TM, ek) - COMB_WC)
    c_off = shift_k + c_start
    c_lo = jnp.maximum(pick(seg_lo), nominal) + shift_k
    c_hi = jnp.minimum(pick(seg_hi), nominal + COMB_WC) + shift_k
    flat = lambda a: jnp.where(live, a, 0).reshape(-1)
    return dict(
        tile_expert=tile_expert, tile_valid=valid.astype(I32),
        g_code=g_code, g_slo=g_slo, g_nwin=g_nwin,
        c_start=flat(c_start), c_off=flat(c_off), c_lo=flat(c_lo), c_hi=flat(c_hi), c_nwin=nw_end[:, -1])


def _moe_gather_kernel(code_ref, slo_ref, nwin_ref, rows_ref, h_ref, hs_ref, gs_ref):
    n_blk = rows_ref.shape[1]
    n_sub = hs_ref.shape[0] // MOE_TG
    sub_ids = [pl.program_id(0) * n_sub + sub for sub in range(n_sub)]
    out_rows = [slice(sub * MOE_TG, (sub + 1) * MOE_TG) for sub in range(n_sub)]

    def window(r, k):
        row_id = (code_ref[r] + lax.broadcasted_iota(I32, (MOE_TG, TOK_BLK), 0)).astype(F32)
        nominal = slo_ref[r] + k * GATHER_WB
        sb = jnp.minimum(nominal, n_blk - GATHER_WB)
        fresh = sb + lax.broadcasted_iota(I32, (GATHER_WB, TOK_BLK), 0) >= nominal
        p1 = jnp.where(fresh, rows_ref[0, pl.ds(sb, GATHER_WB), :], -1.0)
        p2 = jnp.where(fresh, rows_ref[1, pl.ds(sb, GATHER_WB), :], -1.0)
        w1 = rows_ref[2, pl.ds(sb, GATHER_WB), :]
        w2 = rows_ref[3, pl.ds(sb, GATHER_WB), :]
        pieces = []
        g = jnp.zeros((MOE_TG, TOK_BLK), F32)
        for a in range(GATHER_WB):
            m1 = p1[a:a + 1, :] == row_id
            m2 = p2[a:a + 1, :] == row_id
            pieces.append(jnp.where(m1 | m2, 1.0, 0.0).astype(BF16))
            g = g + jnp.where(m1, w1[a:a + 1, :], 0.0) + jnp.where(m2, w2[a:a + 1, :], 0.0)
        onehot = jnp.concatenate(pieces, axis=1)
        return onehot, g.sum(axis=-1, keepdims=True), pl.multiple_of(sb * TOK_BLK, TOK_BLK)

    gathered = lambda onehot, tok0: _dot(onehot, h_ref[pl.ds(tok0, GATHER_WB * TOK_BLK), :]).astype(BF16)

    firsts = [window(r, 0) for r in sub_ids]
    for rows, (onehot, weight, tok0) in zip(out_rows, firsts):
        hs_ref[rows, :] = gathered(onehot, tok0)
        gs_ref[rows, :] = weight

    for r, rows in zip(sub_ids, out_rows):
        def more(k, carry, r=r, rows=rows):
            onehot, weight, tok0 = window(r, k)
            hs_ref[rows, :] += gathered(onehot, tok0)
            gs_ref[rows, :] += weight
            return carry

        lax.fori_loop(1, nwin_ref[r], more, 0)


def _moe_gather(plan, h, rows):
    t = h.shape[0]
    n_rows = plan["g_code"].shape[0] * MOE_TG
    n_blk = t // TOK_BLK
    grid_spec = pltpu.PrefetchScalarGridSpec(
        num_scalar_prefetch=3,
        grid=(n_rows // GATHER_ROWS,),
        in_specs=[
            pl.BlockSpec((4, n_blk, TOK_BLK), lambda r, *_: (0, 0, 0), pipeline_mode=pl.Buffered(1)),
            pl.BlockSpec((t, D_MODEL), lambda r, *_: (0, 0), pipeline_mode=pl.Buffered(1)),
        ],
        out_specs=[
            pl.BlockSpec((GATHER_ROWS, D_MODEL), lambda r, *_: (r, 0)),
            pl.BlockSpec((GATHER_ROWS, 1), lambda r, *_: (r, 0)),
        ],
    )
    return pl.pallas_call(
        _moe_gather_kernel,
        grid_spec=grid_spec,
        out_shape=[
            jax.ShapeDtypeStruct((n_rows, D_MODEL), BF16),
            jax.ShapeDtypeStruct((n_rows, 1), F32),
        ],
        compiler_params=_params(("parallel",), VMEM_LIMIT_RESIDENT),
        name="moe_gather",
    )(plan["g_code"], plan["g_slo"], plan["g_nwin"], rows, h)


def _moe_expert_kernel(te_ref, tv_ref, hs_ref, gs_ref, wg_ref, wu_ref, wd_ref, y_ref, acc_ref, *, nj, pieces):
    r = pl.program_id(0)
    j = pl.program_id(1)
    valid = tv_ref[r] > 0
    part = lambda: _swiglu_pieces(hs_ref[...], wg_ref, wu_ref, wd_ref, pieces)

    if nj == 1:
        @pl.when(valid)
        def _():
            y_ref[...] = (part() * gs_ref[...]).astype(BF16)
    else:
        @pl.when(valid & (j == 0))
        def _():
            acc_ref[...] = part()

        if nj > 2:
            @pl.when(valid & (j > 0) & (j < nj - 1))
            def _():
                acc_ref[...] += part()

        @pl.when(valid & (j == nj - 1))
        def _():
            y_ref[...] = ((acc_ref[...] + part()) * gs_ref[...]).astype(BF16)

    @pl.when(jnp.logical_not(valid) & (j == nj - 1))
    def _():
        y_ref[...] = jnp.zeros_like(y_ref)


def _moe_experts(plan, hs, gs, w_gate, w_up, w_down, pieces):
    tf = sum(pieces)
    n_rows, d = hs.shape
    ff = w_gate.shape[2]
    nj = ff // tf
    jj = lambda r, j, tv: jnp.where(tv[r] > 0, j, nj - 1)
    grid_spec = pltpu.PrefetchScalarGridSpec(
        num_scalar_prefetch=2,
        grid=(n_rows // MOE_TM, nj),
        in_specs=[
            pl.BlockSpec((MOE_TM, d), lambda r, j, te, tv: (r, 0)),
            pl.BlockSpec((MOE_TM, 1), lambda r, j, te, tv: (r, 0)),
            pl.BlockSpec((None, d, tf), lambda r, j, te, tv: (te[r], 0, jj(r, j, tv))),
            pl.BlockSpec((None, d, tf), lambda r, j, te, tv: (te[r], 0, jj(r, j, tv))),
            pl.BlockSpec((None, tf, d), lambda r, j, te, tv: (te[r], jj(r, j, tv), 0)),
        ],
        out_specs=pl.BlockSpec((MOE_TM, d), lambda r, j, te, tv: (r, 0)),
        scratch_shapes=[pltpu.VMEM((MOE_TM if nj > 1 else SUBLANES, d), F32)],
    )
    return pl.pallas_call(
        functools.partial(_moe_expert_kernel, nj=nj, pieces=pieces),
        grid_spec=grid_spec,
        out_shape=jax.ShapeDtypeStruct((n_rows, d), BF16),
        compiler_params=_params(("arbitrary", "arbitrary"), VMEM_LIMIT_RESIDENT),
        name="moe_experts",
    )(plan["tile_expert"], plan["tile_valid"], hs, gs, w_gate, w_up, w_down)


def _moe_combine_kernel(start_ref, off_ref, lo_ref, hi_ref, nwin_ref, x_ref, info_ref, y_hbm, o_ref,
                        ybuf, ybuf_more, sem, sem_more):
    step = pl.program_id(0)
    n_steps = pl.num_programs(0)
    per = COMB_STEP_TILES
    ring = COMB_AHEAD_BUFS * per

    def window_copy(idx, dst, dsem):
        start = pl.multiple_of(start_ref[idx], BF16_ROWS)
        return pltpu.make_async_copy(y_hbm.at[pl.ds(start, COMB_WC), :], dst, dsem)

    def first_group(tile, wait):
        buf = tile % ring
        for s in range(COMB_GROUP):
            @pl.when(s < nwin_ref[tile])
            def _():
                cp = window_copy(tile * COMB_MAXW + s, ybuf.at[buf, pl.ds(s * COMB_WC, COMB_WC), :],
                                 sem.at[buf, s])
                cp.wait() if wait else cp.start()

    def request(grid_step):
        for sub in range(per):
            first_group(grid_step * per + sub, wait=False)

    @pl.when(step == 0)
    def _():
        ybuf[...] = jnp.zeros_like(ybuf)
        ybuf_more[...] = jnp.zeros_like(ybuf_more)
        for ahead in range(COMB_AHEAD_BUFS - 1):
            @pl.when(ahead < n_steps)
            def _():
                request(ahead)

    @pl.when(step + COMB_AHEAD_BUFS - 1 < n_steps)
    def _():
        request(step + COMB_AHEAD_BUFS - 1)

    tiles = [step * per + sub for sub in range(per)]
    rows = [slice(sub * COMB_TS, (sub + 1) * COMB_TS) for sub in range(per)]
    for tile in tiles:
        first_group(tile, wait=True)

    lane = lax.broadcasted_iota(I32, (COMB_TS, COMB_WC), 1)
    lane_row = lax.broadcasted_iota(I32, (1, COMB_WC), 1)
    codes = [[jnp.broadcast_to(info_ref[r, n:n + 1].astype(I32), (COMB_TS, COMB_WC)) for n in range(TOP_K)]
             for r in rows]

    def onehot_group(sub, g):
        pieces = []
        for s in range(COMB_GROUP):
            idx = tiles[sub] * COMB_MAXW + g * COMB_GROUP + s
            off = off_ref[idx]
            in_span = (lane_row >= lo_ref[idx] - off) & (lane_row < hi_ref[idx] - off)
            target = lane + off
            hit = ((codes[sub][0] == target) | (codes[sub][1] == target)) & in_span
            pieces.append(jnp.where(hit, 1.0, 0.0).astype(BF16))
        return jnp.concatenate(pieces, axis=1)

    onehots = [onehot_group(sub, 0) for sub in range(per)]
    for sub in range(per):
        o_ref[rows[sub], :] = x_ref[rows[sub], :] + _dot(onehots[sub], ybuf[tiles[sub] % ring])

    for sub in range(per):
        n = nwin_ref[tiles[sub]]

        def more(g, carry, sub=sub, n=n):
            for wait in (False, True):
                for s in range(COMB_GROUP):
                    @pl.when(g * COMB_GROUP + s < n)
                    def _():
                        cp = window_copy(tiles[sub] * COMB_MAXW + g * COMB_GROUP + s,
                                         ybuf_more.at[pl.ds(s * COMB_WC, COMB_WC), :], sem_more.at[s])
                        cp.wait() if wait else cp.start()
            o_ref[rows[sub], :] += _dot(onehot_group(sub, g), ybuf_more[...])
            return carry

        lax.fori_loop(1, (n + COMB_GROUP - 1) // COMB_GROUP, more, 0)


def _moe_combine(plan, x, info, y):
    t, d = x.shape
    width = COMB_GROUP * COMB_WC
    step_rows = COMB_STEP_TILES * COMB_TS
    grid_spec = pltpu.PrefetchScalarGridSpec(
        num_scalar_prefetch=5,
        grid=(t // step_rows,),
        in_specs=[
            pl.BlockSpec((step_rows, d), lambda i, *_: (i, 0)),
            pl.BlockSpec((step_rows, 4), lambda i, *_: (i, 0)),
            pl.BlockSpec(memory_space=pl.ANY),
        ],
        out_specs=pl.BlockSpec((step_rows, d), lambda i, *_: (i, 0)),
        scratch_shapes=[
            pltpu.VMEM((COMB_AHEAD_BUFS * COMB_STEP_TILES, width, d), BF16),
            pltpu.VMEM((width, d), BF16),
            pltpu.SemaphoreType.DMA((COMB_AHEAD_BUFS * COMB_STEP_TILES, COMB_GROUP)),
            pltpu.SemaphoreType.DMA((COMB_GROUP,)),
        ],
    )
    return pl.pallas_call(
        _moe_combine_kernel,
        grid_spec=grid_spec,
        out_shape=jax.ShapeDtypeStruct((t, d), F32),
        compiler_params=_params(("arbitrary",), VMEM_LIMIT),
        name="moe_combine",
    )(plan["c_start"], plan["c_off"], plan["c_lo"], plan["c_hi"], plan["c_nwin"], x, info, y)


def kernel(x, norm_mix, norm_ffn, gla_w_in, gla_w_gate2, gla_b_gate, gla_out_gain, gla_w_out,
           moba_w_qkv, moba_q_gain, moba_k_gain, moba_w_out, ffn_w_gate, ffn_w_up, ffn_w_down,
           moe_w_router, moe_w_gate, moe_w_up, moe_w_down):
    batch, seq, d = x.shape
    t = batch * seq
    xt = x.reshape(t, d)

    (q, k, v, g, la), (ffn_wg, ffn_wu, ffn_wd, gla_wo, moba_wqkv, moba_wo) = _gla_inproj(
        xt, norm_mix[0:1], gla_w_in, gla_w_gate2[0], gla_b_gate[0:1],
        ride_along=(ffn_w_gate, ffn_w_up, ffn_w_down, gla_w_out, moba_w_qkv, moba_w_out), tm=PROJ_TM)
    o, _ = _gla(q, k, v, g, la, gla_out_gain[0:1], (), batch, seq, blk=GLA_BLK,
                rows_per_step=GLA_ROWS_PER_STEP)
    xt = _ffn_residual(xt, o, gla_wo[0], norm_ffn[0:1], ffn_wg[0], ffn_wu[0], ffn_wd[0],
                       tm=PROJ_TM, pieces=MOE_PIECES + MOE_PIECES)

    q, k, v = _moba_qkv(xt, norm_mix[1:2], moba_wqkv[0], moba_q_gain[0:1], moba_k_gain[0:1], tm=PROJ_TM)
    o, (moe_wg, moe_wu, moe_wd) = _moba(q, k, v, (moe_w_gate[0], moe_w_up[0], moe_w_down[0]), batch, seq,
                                        heads_per_step=MOBA_HEADS_PER_STEP)
    xt, h, info, rows, cnt = _router(xt, o, moba_wo[0], norm_ffn[1:2], moe_w_router[0], tr=ROUTER_TR)
    plan = _moe_plan(cnt, t)
    hs, gs = _moe_gather(plan, h, rows)
    y = _moe_experts(plan, hs, gs, moe_wg, moe_wu, moe_wd, pieces=MOE_PIECES + MOE_PIECES)
    xt = _moe_combine(plan, xt, info, y)
    return xt.reshape(batch, seq, d)
```

```python
import functools

import jax
import jax.numpy as jnp
from jax import lax
from jax.experimental import pallas as pl
from jax.experimental.pallas import tpu as pltpu

F32 = jnp.float32
BF16 = jnp.bfloat16
I32 = jnp.int32

EPS = 1e-6
NEG_INF = -1e30

D_MODEL = 1024
GLA_HEADS = 4
GLA_DK = 512
GLA_DV = 1024
GLA_HK = 128
GLA_HV = 256
GLA_RANK = 16
GLA_NORMALIZER = 16.0
GLA_CHUNK = 64

MOBA_HEADS = 8
MOBA_HD = 128
MOBA_BLOCK = 256
MOBA_TOPK = 3
MOBA_Q_SCALE = MOBA_HD ** -0.5 * 1.4426950408889634

N_EXPERTS = 8
TOP_K = 2

LANES = 128
SUBLANES = 8
BF16_ROWS = 16
ROUTER_ROWS = BF16_ROWS
VMEM_LIMIT = 48 * 1024 * 1024
VMEM_LIMIT_RESIDENT = 56 * 1024 * 1024

PROJ_TM = 512
ROUTER_TR = 1024
GLA_BLK = 256
GLA_ROWS_PER_STEP = 4
MOBA_HEADS_PER_STEP = 2

MOE_PIECES = (512, 512, 768)
MOE_TM = 256
MOE_TG = 128
GATHER_ROWS = 512
TOK_BLK = LANES
GATHER_WB = 6
COMB_TS = 256
COMB_STEP_TILES = 4
COMB_WC = 128
COMB_GROUP = 8
COMB_AHEAD_BUFS = 3
COMB_MAXW = 16


def _params(semantics, vmem_bytes=None):
    return pltpu.CompilerParams(dimension_semantics=semantics, vmem_limit_bytes=vmem_bytes)


def _rms(x, gain):
    y = x * lax.rsqrt(jnp.mean(x * x, axis=-1, keepdims=True) + EPS)
    return y * gain


def _dot(a, b):
    return jnp.dot(a, b, preferred_element_type=F32)


def _dot_nt(a, b):
    return lax.dot_general(a, b, (((1,), (1,)), ((), ())), preferred_element_type=F32)


def _dot_tn(a, b):
    return lax.dot_general(a, b, (((0,), (0,)), ((), ())), preferred_element_type=F32)


def _split(a):
    hi = a.astype(BF16)
    lo = (a - hi.astype(F32)).astype(BF16)
    return hi, lo


def _log_sigmoid(z):
    return jnp.minimum(z, 0.0) - jnp.log(1.0 + jnp.exp(-jnp.abs(z)))


def _silu(z):
    return z / (1.0 + jnp.exp(-z))


def _swiglu_pieces(h, wg_ref, wu_ref, wd_ref, widths):
    bounds = [sum(widths[:n]) for n in range(len(widths) + 1)]
    assert bounds[-1] == wg_ref.shape[1], (widths, wg_ref.shape)
    gate_up = lambda c: (_dot(h, wg_ref[:, bounds[c]:bounds[c + 1]]), _dot(h, wu_ref[:, bounds[c]:bounds[c + 1]]))
    out = None
    upcoming = gate_up(0)
    for c in range(len(widths)):
        g, u = upcoming
        if c + 1 < len(widths):
            upcoming = gate_up(c + 1)
        d = _dot((_silu(g) * u).astype(BF16), wd_ref[bounds[c]:bounds[c + 1], :])
        out = d if out is None else out + d
    return out


def _ride_along_specs(arrays, n_steps, step):
    specs, shapes = [], []
    for w in arrays:
        n_e, rows, cols = w.shape
        per_e = n_steps // n_e
        if rows % (per_e * BF16_ROWS) == 0:
            blk = (None, rows // per_e, cols)
            imap = lambda *ids, per_e=per_e: (step(*ids) // per_e, step(*ids) % per_e, 0)
        else:
            assert cols % (per_e * LANES) == 0, (w.shape, n_steps)
            blk = (None, rows, cols // per_e)
            imap = lambda *ids, per_e=per_e: (step(*ids) // per_e, 0, step(*ids) % per_e)
        specs.append(pl.BlockSpec(blk, imap))
        shapes.append(jax.ShapeDtypeStruct(w.shape, BF16))
    return specs, shapes


def _with_ride_along(body, n_in, n_out, n_cast):
    def kernel(*refs):
        ins, refs = refs[:n_in], refs[n_in:]
        cast_in, refs = refs[:n_cast], refs[n_cast:]
        outs, refs = refs[:n_out], refs[n_out:]
        cast_out, scratch = refs[:n_cast], refs[n_cast:]
        for src, dst in zip(cast_in, cast_out):
            dst[...] = src[...].astype(BF16)
        body(*ins, *outs, *scratch)
    return kernel


def _gla_inproj_kernel(x_next_ref, x_first_ref, gain_ref, w_ref, wg2_ref, b_ref,
                       q_ref, k_ref, v_ref, g_ref, la_ref, wb_ref, h_ref):
    i = pl.program_id(0)

    @pl.when(i == 0)
    def _():
        wb_ref[...] = w_ref[...].astype(BF16)
        h_ref[0] = _rms(x_first_ref[...], gain_ref[...]).astype(BF16)

    n_main = 2 * GLA_DK + 2 * GLA_DV

    def step(cur, nxt):
        h = h_ref[cur]
        h_ref[nxt] = _rms(x_next_ref[...], gain_ref[...]).astype(BF16)
        a_lr = _dot(h, wb_ref[:, n_main:])
        a_hi, a_lo = _split(a_lr)
        w_hi, w_lo = _split(wg2_ref[...])
        z = _dot(jnp.concatenate([a_hi, a_lo, a_hi], axis=1), jnp.concatenate([w_hi, w_hi, w_lo], axis=0))
        la_ref[...] = _log_sigmoid(z + b_ref[...]) / GLA_NORMALIZER
        q_ref[...] = _dot(h, wb_ref[:, 0:GLA_DK]).astype(BF16)
        k_ref[...] = _dot(h, wb_ref[:, GLA_DK:2 * GLA_DK]).astype(BF16)
        v_ref[...] = _dot(h, wb_ref[:, 2 * GLA_DK:2 * GLA_DK + GLA_DV]).astype(BF16)
        g_ref[...] = _dot(h, wb_ref[:, 2 * GLA_DK + GLA_DV:n_main]).astype(BF16)

    @pl.when(i % 2 == 0)
    def _():
        step(0, 1)

    @pl.when(i % 2 == 1)
    def _():
        step(1, 0)


def _gla_inproj(x, gain, w_in, w_gate2, b_gate, ride_along, tm):
    t = x.shape[0]
    n_steps = t // tm
    row = lambda i: (i, 0)
    fixed = lambda i: (0, 0)
    cast_specs, cast_shapes = _ride_along_specs(ride_along, n_steps, lambda i: i)
    outs = pl.pallas_call(
        _with_ride_along(_gla_inproj_kernel, 6, 5, len(ride_along)),
        grid=(n_steps,),
        in_specs=[
            pl.BlockSpec((tm, D_MODEL), lambda i: (jnp.minimum(i + 1, n_steps - 1), 0)),
            pl.BlockSpec((tm, D_MODEL), fixed),
            pl.BlockSpec((1, D_MODEL), fixed),
            pl.BlockSpec((None,) + w_in.shape[1:], lambda i: (0, 0, 0), pipeline_mode=pl.Buffered(1)),
            pl.BlockSpec((GLA_RANK, GLA_DK), fixed),
            pl.BlockSpec((1, GLA_DK), fixed),
        ] + cast_specs,
        out_specs=[
            pl.BlockSpec((tm, GLA_DK), row),
            pl.BlockSpec((tm, GLA_DK), row),
            pl.BlockSpec((tm, GLA_DV), row),
            pl.BlockSpec((tm, GLA_DV), row),
            pl.BlockSpec((tm, GLA_DK), row),
        ] + cast_specs,
        out_shape=[
            jax.ShapeDtypeStruct((t, GLA_DK), BF16),
            jax.ShapeDtypeStruct((t, GLA_DK), BF16),
            jax.ShapeDtypeStruct((t, GLA_DV), BF16),
            jax.ShapeDtypeStruct((t, GLA_DV), BF16),
            jax.ShapeDtypeStruct((t, GLA_DK), F32),
        ] + cast_shapes,
        scratch_shapes=[pltpu.VMEM(w_in.shape[1:], BF16), pltpu.VMEM((2, tm, D_MODEL), BF16)],
        compiler_params=_params(("arbitrary",), VMEM_LIMIT),
        name="gla_inproj",
    )(x, x, gain, w_in, w_gate2, b_gate, *ride_along)
    return outs[:5], outs[5:]


def _gla_kernel(q_ref, k_ref, v_ref, g_ref, la_ref, og_ref, o_ref, st_ref, *, n_chunks):
    @pl.when(pl.program_id(1) == 0)
    def _():
        st_ref[...] = jnp.zeros_like(st_ref)

    n_rows = q_ref.shape[0]
    blk = n_chunks * GLA_CHUNK
    r = lax.broadcasted_iota(I32, (blk, blk), 0)
    c = lax.broadcasted_iota(I32, (blk, blk), 1)
    causal = (r >= c) & (r // GLA_CHUNK == c // GLA_CHUNK)
    tril = causal.astype(BF16)
    og = og_ref[...]
    chunk_rows = [slice(ci * GLA_CHUNK, (ci + 1) * GLA_CHUNK) for ci in range(n_chunks)]
    hk = [slice(h * GLA_HK, (h + 1) * GLA_HK) for h in range(GLA_HEADS)]
    hv = [slice(h * GLA_HV, (h + 1) * GLA_HV) for h in range(GLA_HEADS)]

    q_dec, k_inv, k_end, chunk_decay = [], [], [], []
    for b in range(n_rows):
        la = la_ref[b]
        la_hi, la_mid = _split(la)
        la_lo = (la - la_hi.astype(F32) - la_mid.astype(F32)).astype(BF16)
        cum = _dot(tril, la_hi) + _dot(tril, la_mid) + _dot(tril, la_lo)
        ends = [cum[rows.stop - 1:rows.stop, :] for rows in chunk_rows]
        last = jnp.concatenate([jnp.broadcast_to(e, (GLA_CHUNK, GLA_DK)) for e in ends], axis=0)
        q = q_ref[b].astype(F32) * (GLA_HK ** -0.5)
        k = k_ref[b].astype(F32)
        q_dec.append((q * jnp.exp(cum)).astype(BF16))
        k_inv.append((k * jnp.exp(-cum)).astype(BF16))
        k_end.append((k * jnp.exp(last - cum)).astype(BF16))
        chunk_decay.append([jnp.exp(e) for e in ends])

    streams = [(b, h) for b in range(n_rows) for h in range(GLA_HEADS)]
    scores = [jnp.where(causal, _dot_nt(q_dec[b][:, hk[h]], k_inv[b][:, hk[h]]), 0.0).astype(BF16)
              for b, h in streams]
    state_in = [[_dot_tn(v_ref[b, rows, hv[h]], k_end[b][rows, hk[h]]) for rows in chunk_rows]
                for b, h in streams]
    intra = [_dot(scores[n], v_ref[b, :, hv[h]]) for n, (b, h) in enumerate(streams)]
    for n, (b, h) in enumerate(streams):
        st = st_ref[b, h]
        inter = []
        for ci, rows in enumerate(chunk_rows):
            inter.append(_dot_nt(q_dec[b][rows, hk[h]], st.astype(BF16)))
            st = st * chunk_decay[b][ci][:, hk[h]] + state_in[n][ci]
        st_ref[b, h] = st
        o = intra[n] + jnp.concatenate(inter, axis=0)
        o = o * lax.rsqrt(jnp.mean(o * o, axis=-1, keepdims=True) + EPS) * og
        o = o * _silu(g_ref[b, :, hv[h]].astype(F32))
        o_ref[b, :, hv[h]] = o.astype(BF16)


def _gla(q, k, v, g, la, out_gain, ride_along, batch, seq, blk, rows_per_step):
    nb = seq // blk
    nbt = batch // rows_per_step
    per_row = lambda a: a.reshape(batch, seq, a.shape[-1])
    spec = lambda width: pl.BlockSpec((rows_per_step, blk, width), lambda b, i: (b, i, 0))
    cast_specs, cast_shapes = _ride_along_specs(ride_along, nbt * nb, lambda b, i: b * nb + i)
    outs = pl.pallas_call(
        _with_ride_along(functools.partial(_gla_kernel, n_chunks=blk // GLA_CHUNK), 6, 1, len(ride_along)),
        grid=(nbt, nb),
        in_specs=[spec(GLA_DK), spec(GLA_DK), spec(GLA_DV), spec(GLA_DV), spec(GLA_DK),
                  pl.BlockSpec((1, GLA_HV), lambda b, i: (0, 0))] + cast_specs,
        out_specs=[spec(GLA_DV)] + cast_specs,
        out_shape=[jax.ShapeDtypeStruct((batch, seq, GLA_DV), BF16)] + cast_shapes,
        scratch_shapes=[pltpu.VMEM((rows_per_step, GLA_HEADS, GLA_HV, GLA_HK), F32)],
        compiler_params=_params(("arbitrary", "arbitrary"), VMEM_LIMIT),
        name="gla_core",
    )(per_row(q), per_row(k), per_row(v), per_row(g), per_row(la), out_gain, *ride_along)
    return outs[0].reshape(batch * seq, GLA_DV), outs[1:]


def _ffn_kernel(x_ref, o_ref, wo_ref, gain_ref, wg_ref, wu_ref, wd_ref, y_ref, *, pieces):
    x1 = x_ref[...] + _dot(o_ref[...], wo_ref[...])
    h = _rms(x1, gain_ref[...]).astype(BF16)
    y_ref[...] = x1 + _swiglu_pieces(h, wg_ref, wu_ref, wd_ref, pieces)


def _ffn_residual(x, o, w_out, gain, w_gate, w_up, w_down, tm, pieces):
    t, d = x.shape
    row = lambda i: (i, 0)
    resident = lambda a: pl.BlockSpec(a.shape, lambda i: (0, 0), pipeline_mode=pl.Buffered(1))
    return pl.pallas_call(
        functools.partial(_ffn_kernel, pieces=pieces),
        grid=(t // tm,),
        in_specs=[pl.BlockSpec((tm, d), row), pl.BlockSpec((tm, d), row), resident(w_out), resident(gain),
                  resident(w_gate), resident(w_up), resident(w_down)],
        out_specs=pl.BlockSpec((tm, d), row),
        out_shape=jax.ShapeDtypeStruct((t, d), F32),
        compiler_params=_params(("parallel",), VMEM_LIMIT_RESIDENT),
        name="ffn_swiglu",
    )(x, o, w_out, gain, w_gate, w_up, w_down)


def _moba_qkv_kernel(x_ref, gain_ref, w_ref, qg_ref, kg_ref, q_ref, k_ref, v_ref):
    h = _rms(x_ref[...], gain_ref[...]).astype(BF16)
    d = MOBA_HEADS * MOBA_HD
    qg = qg_ref[...]
    kg = kg_ref[...]
    q = _dot(h, w_ref[:, 0:d])
    k = _dot(h, w_ref[:, d:2 * d])
    for hd in range(MOBA_HEADS):
        cols = slice(hd * MOBA_HD, (hd + 1) * MOBA_HD)
        q_ref[:, cols] = (_rms(q[:, cols], qg) * MOBA_Q_SCALE).astype(BF16)
        k_ref[:, cols] = _rms(k[:, cols], kg).astype(BF16)
    v_ref[...] = _dot(h, w_ref[:, 2 * d:]).astype(BF16)


def _moba_qkv(x, gain, w, q_gain, k_gain, tm):
    t = x.shape[0]
    d = MOBA_HEADS * MOBA_HD
    row = lambda i: (i, 0)
    fixed = lambda i: (0, 0)
    return pl.pallas_call(
        _moba_qkv_kernel,
        grid=(t // tm,),
        in_specs=[
            pl.BlockSpec((tm, D_MODEL), row),
            pl.BlockSpec((1, D_MODEL), fixed),
            pl.BlockSpec((D_MODEL, 3 * d), fixed),
            pl.BlockSpec((1, MOBA_HD), fixed),
            pl.BlockSpec((1, MOBA_HD), fixed),
        ],
        out_specs=[pl.BlockSpec((tm, d), row)] * 3,
        out_shape=[jax.ShapeDtypeStruct((t, d), BF16)] * 3,
        compiler_params=_params(("parallel",), VMEM_LIMIT),
        name="moba_qkv",
    )(x, gain, w, q_gain, k_gain)


def _reduce_rows(x, op, stop=SUBLANES):
    while x.shape[0] > stop:
        half = x.shape[0] // 2
        x = op(x[:half], x[half:])
    return x


def _moba_kernel(q_ref, k_ref, v_ref, o_ref, *, n_blocks):
    seq = n_blocks * MOBA_BLOCK
    heads = range(q_ref.shape[1] // MOBA_HD)
    cols = [slice(hd * MOBA_HD, (hd + 1) * MOBA_HD) for hd in heads]
    n_pad = BF16_ROWS
    ones_rows = jnp.where(lax.broadcasted_iota(I32, (BF16_ROWS, seq), 0) == 0, 1.0, 0.0)
    v_t = [jnp.concatenate([v_ref[:, cols[hd]].astype(F32).T, ones_rows], axis=0).astype(BF16) for hd in heads]
    blk_of_key = lax.broadcasted_iota(I32, (n_pad, seq), 1) // MOBA_BLOCK
    blk_row = lax.broadcasted_iota(I32, (n_pad, seq), 0)
    indicator = jnp.where(blk_of_key == blk_row, 1.0 / MOBA_BLOCK, 0.0).astype(BF16)
    k_mean = [_split(_dot(indicator, k_ref[:, cols[hd]])) for hd in heads]

    blk_id = lax.broadcasted_iota(I32, (n_pad, MOBA_BLOCK), 0)
    key_pos = lax.broadcasted_iota(I32, (MOBA_BLOCK, MOBA_BLOCK), 0)
    query_pos = lax.broadcasted_iota(I32, (MOBA_BLOCK, MOBA_BLOCK), 1)
    causal = key_pos <= query_pos

    def masked_scores(hd, i):
        rows = slice(i * MOBA_BLOCK, (i + 1) * MOBA_BLOCK)
        n_keys = (i + 1) * MOBA_BLOCK
        q = q_ref[rows, cols[hd]]
        s = _dot_nt(k_ref[0:n_keys, cols[hd]], q).astype(BF16)
        selected = None
        if i > MOBA_TOPK:
            km_hi, km_lo = k_mean[hd]
            gate = _dot_nt(km_hi, q) + _dot_nt(km_lo, q)
            gate = jnp.where(blk_id < i, gate, NEG_INF)
            rank = jnp.zeros_like(gate)
            for jp in range(i):
                row = gate[jp:jp + 1, :]
                beats = (row > gate) | ((row == gate) & (jp < blk_id))
                rank = rank + beats.astype(F32)
            selected = jnp.where((rank < MOBA_TOPK) & (blk_id < i), 1.0, 0.0)
        pieces = []
        top = None
        for j in range(i + 1):
            sj = s[j * MOBA_BLOCK:(j + 1) * MOBA_BLOCK, :]
            if j == i:
                sj = jnp.where(causal, sj, NEG_INF)
            elif selected is not None:
                sj = jnp.where(selected[j:j + 1, :] > 0.5, sj, NEG_INF)
            pieces.append(sj)
            folded = _reduce_rows(sj, jnp.maximum, stop=BF16_ROWS)
            top = folded if top is None else jnp.maximum(top, folded)
        return pieces, top

    def weighted_values(hd, i, p):
        o_t = _dot(v_t[hd][:, 0:(i + 1) * MOBA_BLOCK], p)
        rows = slice(i * MOBA_BLOCK, (i + 1) * MOBA_BLOCK)
        o_ref[rows, cols[hd]] = (o_t[:MOBA_HD, :] / o_t[MOBA_HD:MOBA_HD + 1, :]).T.astype(BF16)

    def softmax_numerators(scored):
        pieces, top = scored
        m = top.max(axis=0, keepdims=True)
        probs = [jnp.exp2(sj - m) for sj in pieces]
        return probs[0] if len(probs) == 1 else jnp.concatenate(probs, axis=0)

    upcoming = [masked_scores(hd, 0) for hd in heads]
    pending = None
    for i in range(n_blocks):
        pieces = upcoming
        if i + 1 < n_blocks:
            upcoming = [masked_scores(hd, i + 1) for hd in heads]
        if pending is not None:
            for hd in heads:
                weighted_values(hd, i - 1, pending[hd])
        pending = [softmax_numerators(pieces[hd]) for hd in heads]
    for hd in heads:
        weighted_values(hd, n_blocks - 1, pending[hd])


def _moba(q, k, v, ride_along, batch, seq, heads_per_step):
    t, d = q.shape
    n_groups = MOBA_HEADS // heads_per_step
    spec = pl.BlockSpec((seq, heads_per_step * MOBA_HD), lambda b, h: (b, h))
    cast_specs, cast_shapes = _ride_along_specs(ride_along, batch * n_groups, lambda b, h: b * n_groups + h)
    outs = pl.pallas_call(
        _with_ride_along(functools.partial(_moba_kernel, n_blocks=seq // MOBA_BLOCK), 3, 1, len(ride_along)),
        grid=(batch, n_groups),
        in_specs=[spec, spec, spec] + cast_specs,
        out_specs=[spec] + cast_specs,
        out_shape=[jax.ShapeDtypeStruct((t, d), BF16)] + cast_shapes,
        compiler_params=_params(("arbitrary", "arbitrary"), VMEM_LIMIT_RESIDENT),
        name="moba_attn",
    )(q, k, v, *ride_along)
    return outs[0], outs[1:]


def _router_kernel(x_ref, o_ref, wo_ref, gain_ref, w_ref, x2_ref, h_ref, info_ref, rows_ref, cnt_ref,
                   carry_ref, tri_ref,
                   *, region_rows):
    tr = x_ref.shape[0]

    @pl.when(pl.program_id(0) == 0)
    def _():
        carry_ref[...] = jnp.zeros_like(carry_ref)
        r = lax.broadcasted_iota(I32, (TOK_BLK, TOK_BLK), 0)
        c = lax.broadcasted_iota(I32, (TOK_BLK, TOK_BLK), 1)
        tri_ref[...] = (r < c).astype(BF16)

    x2 = x_ref[...] + _dot(o_ref[...], wo_ref[...])
    x2_ref[...] = x2
    h = _rms(x2, gain_ref[...])
    h_hi, h_lo = _split(h)
    h_ref[...] = h_hi
    n_e = ROUTER_ROWS
    wt = w_ref[...]
    passes = _dot_nt(wt, h_hi)
    logits = passes[:n_e] + passes[n_e:] + _dot_nt(wt[:n_e], h_lo)
    e = lax.broadcasted_iota(I32, logits.shape, 0)
    logits = jnp.where(e < N_EXPERTS, logits, -jnp.inf)
    m1 = logits.max(axis=0, keepdims=True)
    i1 = jnp.where(logits == m1, e, n_e).min(axis=0, keepdims=True)
    first = e == i1
    rest = jnp.where(first, -jnp.inf, logits)
    m2 = rest.max(axis=0, keepdims=True)
    i2 = jnp.where(rest == m2, e, n_e).min(axis=0, keepdims=True)
    second = e == i2
    e2 = jnp.exp(m2 - m1)
    denom = 1.0 + e2
    sel = jnp.where(first | second, 1.0, 0.0).astype(BF16)
    nb = tr // TOK_BLK
    blk_row = lax.broadcasted_iota(I32, (nb, tr), 0)
    blk_tok = lax.broadcasted_iota(I32, (nb, tr), 1) // TOK_BLK
    cnt = _dot_nt((blk_row == blk_tok).astype(BF16), sel)
    cnt_ref[...] = jnp.concatenate([cnt, jnp.zeros((nb, LANES - n_e), F32)], axis=1)
    offset = carry_ref[...]
    before = []
    for b in range(nb):
        sel_b = sel[:, b * TOK_BLK:(b + 1) * TOK_BLK]
        before.append(_dot(sel_b, tri_ref[...]) + offset)
        offset = offset + jnp.broadcast_to(sel_b.astype(F32).sum(axis=1, keepdims=True), offset.shape)
    carry_ref[...] = offset
    before = jnp.concatenate(before, axis=1)
    rank1 = jnp.where(first, before, 0.0).sum(axis=0, keepdims=True)
    rank2 = jnp.where(second, before, 0.0).sum(axis=0, keepdims=True)
    fields = (i1.astype(F32) * region_rows + rank1, i2.astype(F32) * region_rows + rank2,
              1.0 / denom, e2 / denom)
    for n, f in enumerate(fields):
        for b in range(nb):
            rows_ref[n, b:b + 1, :] = f[:, b * TOK_BLK:(b + 1) * TOK_BLK]
    info_t = jnp.concatenate(list(fields) + [jnp.zeros((LANES - len(fields), tr), F32)], axis=0)
    info_ref[...] = info_t.T[:, :len(fields)]


def _router(x, o, w_out, gain, w_router, tr):
    t = x.shape[0]
    nb = tr // TOK_BLK
    pad = ((0, ROUTER_ROWS - N_EXPERTS), (0, 0))
    w_hi = w_router.T.astype(BF16)
    w_lo = (w_router.T - w_hi.astype(F32)).astype(BF16)
    w_both = jnp.concatenate([jnp.pad(w_hi, pad), jnp.pad(w_lo, pad)], axis=0)
    row = lambda i: (i, 0)
    fixed = lambda i: (0, 0)
    return pl.pallas_call(
        functools.partial(_router_kernel, region_rows=float(t)),
        grid=(t // tr,),
        in_specs=[
            pl.BlockSpec((tr, D_MODEL), row),
            pl.BlockSpec((tr, D_MODEL), row),
            pl.BlockSpec((D_MODEL, D_MODEL), fixed),
            pl.BlockSpec((1, D_MODEL), fixed),
            pl.BlockSpec((2 * ROUTER_ROWS, D_MODEL), fixed),
        ],
        out_specs=[
            pl.BlockSpec((tr, D_MODEL), row),
            pl.BlockSpec((tr, D_MODEL), row),
            pl.BlockSpec((tr, 4), row),
            pl.BlockSpec((4, nb, TOK_BLK), lambda i: (0, i, 0)),
            pl.BlockSpec((nb, LANES), row),
        ],
        out_shape=[
            jax.ShapeDtypeStruct((t, D_MODEL), F32),
            jax.ShapeDtypeStruct((t, D_MODEL), BF16),
            jax.ShapeDtypeStruct((t, 4), F32),
            jax.ShapeDtypeStruct((4, t // TOK_BLK, TOK_BLK), F32),
            jax.ShapeDtypeStruct((t // TOK_BLK, LANES), F32),
        ],
        scratch_shapes=[pltpu.VMEM((ROUTER_ROWS, LANES), F32), pltpu.VMEM((TOK_BLK, TOK_BLK), BF16)],
        compiler_params=_params(("arbitrary",), VMEM_LIMIT_RESIDENT),
        name="moe_router",
    )(x, o, w_out, gain, w_both)


def _moe_plan(cnt, t):
    e_ids = jnp.arange(N_EXPERTS, dtype=I32)
    by_expert = lambda table, idx: jnp.sum(jnp.where(idx[..., None] == e_ids, table, 0), axis=-1)
    cnt_blk = cnt[:, :N_EXPERTS].astype(I32)
    cum_blk = jnp.cumsum(cnt_blk, axis=0)
    counts = cum_blk[-1]
    tiles_e = (counts + MOE_TM - 1) // MOE_TM
    tile_end = jnp.cumsum(tiles_e)
    tile_off = tile_end - tiles_e
    nt = TOP_K * t // MOE_TM + N_EXPERTS
    r = jnp.arange(nt, dtype=I32)
    valid = r < tile_end[-1]
    te = jnp.minimum(jnp.sum((r[:, None] >= tile_end[None, :]).astype(I32), axis=1), N_EXPERTS - 1)
    local = r - by_expert(tile_off, te)
    last_e = jnp.max(jnp.where(tiles_e > 0, e_ids, 0))
    tile_expert = jnp.where(valid, te, last_e)
    shift = e_ids * t - tile_off * MOE_TM

    per = MOE_TM // MOE_TG
    rg = jnp.arange(nt * per, dtype=I32)
    eg = jnp.repeat(te, per)
    k0 = jnp.repeat(local, per) * MOE_TM + (rg % per) * MOE_TG
    k1 = jnp.minimum(k0 + MOE_TG, by_expert(counts, eg))
    has = jnp.repeat(valid, per) & (k1 > k0)
    cum_e = by_expert(cum_blk[None, :, :], eg[:, None])
    s_lo = jnp.sum((cum_e <= k0[:, None]).astype(I32), axis=1)
    s_hi = jnp.sum((cum_e <= (k1 - 1)[:, None]).astype(I32), axis=1)
    g_nwin = jnp.where(has, (s_hi - s_lo) // GATHER_WB + 1, 0)
    g_slo = jnp.where(has, s_lo, 0)
    g_code = eg * t + k0

    nts = t // COMB_TS
    cnt_ts = cnt_blk.reshape(nts, COMB_TS // TOK_BLK, N_EXPERTS).sum(axis=1)
    seg_lo = (tile_off * MOE_TM)[None, :] + jnp.cumsum(cnt_ts, axis=0) - cnt_ts
    seg_hi = seg_lo + cnt_ts
    ws0 = (seg_lo // BF16_ROWS) * BF16_ROWS
    nw = jnp.where(cnt_ts > 0, (seg_hi - ws0 + COMB_WC - 1) // COMB_WC, 0)
    nw_end = jnp.cumsum(nw, axis=1)
    k = jnp.arange(COMB_MAXW, dtype=I32)
    ek = jnp.minimum(jnp.sum((k[None, :, None] >= nw_end[:, None, :]).astype(I32), axis=2), N_EXPERTS - 1)
    pick = lambda a: by_expert(a[:, None, :], ek)
    live = k[None, :] < nw_end[:, -1:]
    nominal = pick(ws0) + (k[None, :] - pick(nw_end - nw)) * COMB_WC
    shift_k = by_expert(shift, ek)
    c_start = jnp.minimum(nominal, by_expert(tile_end * MOE_TM, ek) - COMB_WC)
    c_off = shift_k + c_start
    c_lo = jnp.maximum(pick(seg_lo), nominal) + shift_k
    c_hi = jnp.minimum(pick(seg_hi), nominal + COMB_WC) + shift_k
    flat = lambda a: jnp.where(live, a, 0).reshape(-1)
    return dict(
        tile_expert=tile_expert, tile_valid=valid.astype(I32),
        g_code=g_code, g_slo=g_slo, g_nwin=g_nwin,
        c_start=flat(c_start), c_off=flat(c_off), c_lo=flat(c_lo), c_hi=flat(c_hi), c_nwin=nw_end[:, -1])


def _moe_gather_kernel(code_ref, slo_ref, nwin_ref, rows_ref, h_ref, hs_ref, gs_ref):
    n_blk = rows_ref.shape[1]
    n_sub = hs_ref.shape[0] // MOE_TG
    sub_ids = [pl.program_id(0) * n_sub + sub for sub in range(n_sub)]
    out_rows = [slice(sub * MOE_TG, (sub + 1) * MOE_TG) for sub in range(n_sub)]

    def window(r, k):
        row_id = (code_ref[r] + lax.broadcasted_iota(I32, (MOE_TG, TOK_BLK), 0)).astype(F32)
        nominal = slo_ref[r] + k * GATHER_WB
        sb = jnp.minimum(nominal, n_blk - GATHER_WB)
        fresh = sb + lax.broadcasted_iota(I32, (GATHER_WB, TOK_BLK), 0) >= nominal
        p1 = jnp.where(fresh, rows_ref[0, pl.ds(sb, GATHER_WB), :], -1.0)
        p2 = jnp.where(fresh, rows_ref[1, pl.ds(sb, GATHER_WB), :], -1.0)
        w1 = rows_ref[2, pl.ds(sb, GATHER_WB), :]
        w2 = rows_ref[3, pl.ds(sb, GATHER_WB), :]
        pieces = []
        g = jnp.zeros((MOE_TG, TOK_BLK), F32)
        for a in range(GATHER_WB):
            m1 = p1[a:a + 1, :] == row_id
            m2 = p2[a:a + 1, :] == row_id
            pieces.append(jnp.where(m1 | m2, 1.0, 0.0).astype(BF16))
            g = g + jnp.where(m1, w1[a:a + 1, :], 0.0) + jnp.where(m2, w2[a:a + 1, :], 0.0)
        onehot = jnp.concatenate(pieces, axis=1)
        return onehot, g.sum(axis=-1, keepdims=True), pl.multiple_of(sb * TOK_BLK, TOK_BLK)

    gathered = lambda onehot, tok0: _dot(onehot, h_ref[pl.ds(tok0, GATHER_WB * TOK_BLK), :]).astype(BF16)

    firsts = [window(r, 0) for r in sub_ids]
    for rows, (onehot, weight, tok0) in zip(out_rows, firsts):
        hs_ref[rows, :] = gathered(onehot, tok0)
        gs_ref[rows, :] = weight

    for r, rows in zip(sub_ids, out_rows):
        def more(k, carry, r=r, rows=rows):
            onehot, weight, tok0 = window(r, k)
            hs_ref[rows, :] += gathered(onehot, tok0)
            gs_ref[rows, :] += weight
            return carry

        lax.fori_loop(1, nwin_ref[r], more, 0)


def _moe_gather(plan, h, rows):
    t = h.shape[0]
    n_rows = plan["g_code"].shape[0] * MOE_TG
    n_blk = t // TOK_BLK
    grid_spec = pltpu.PrefetchScalarGridSpec(
        num_scalar_prefetch=3,
        grid=(n_rows // GATHER_ROWS,),
        in_specs=[
            pl.BlockSpec((4, n_blk, TOK_BLK), lambda r, *_: (0, 0, 0), pipeline_mode=pl.Buffered(1)),
            pl.BlockSpec((t, D_MODEL), lambda r, *_: (0, 0), pipeline_mode=pl.Buffered(1)),
        ],
        out_specs=[
            pl.BlockSpec((GATHER_ROWS, D_MODEL), lambda r, *_: (r, 0)),
            pl.BlockSpec((GATHER_ROWS, 1), lambda r, *_: (r, 0)),
        ],
    )
    return pl.pallas_call(
        _moe_gather_kernel,
        grid_spec=grid_spec,
        out_shape=[
            jax.ShapeDtypeStruct((n_rows, D_MODEL), BF16),
            jax.ShapeDtypeStruct((n_rows, 1), F32),
        ],
        compiler_params=_params(("parallel",), VMEM_LIMIT_RESIDENT),
        name="moe_gather",
    )(plan["g_code"], plan["g_slo"], plan["g_nwin"], rows, h)


def _moe_expert_kernel(te_ref, tv_ref, hs_ref, gs_ref, wg_ref, wu_ref, wd_ref, y_ref, acc_ref, *, nj, pieces):
    r = pl.program_id(0)
    j = pl.program_id(1)
    valid = tv_ref[r] > 0
    part = lambda: _swiglu_pieces(hs_ref[...], wg_ref, wu_ref, wd_ref, pieces)

    if nj == 1:
        @pl.when(valid)
        def _():
            y_ref[...] = (part() * gs_ref[...]).astype(BF16)
    else:
        @pl.when(valid & (j == 0))
        def _():
            acc_ref[...] = part()

        if nj > 2:
            @pl.when(valid & (j > 0) & (j < nj - 1))
            def _():
                acc_ref[...] += part()

        @pl.when(valid & (j == nj - 1))
        def _():
            y_ref[...] = ((acc_ref[...] + part()) * gs_ref[...]).astype(BF16)

    @pl.when(jnp.logical_not(valid) & (j == nj - 1))
    def _():
        y_ref[...] = jnp.zeros_like(y_ref)


def _moe_experts(plan, hs, gs, w_gate, w_up, w_down, pieces):
    tf = sum(pieces)
    n_rows, d = hs.shape
    ff = w_gate.shape[2]
    nj = ff // tf
    jj = lambda r, j, tv: jnp.where(tv[r] > 0, j, nj - 1)
    grid_spec = pltpu.PrefetchScalarGridSpec(
        num_scalar_prefetch=2,
        grid=(n_rows // MOE_TM, nj),
        in_specs=[
            pl.BlockSpec((MOE_TM, d), lambda r, j, te, tv: (r, 0)),
            pl.BlockSpec((MOE_TM, 1), lambda r, j, te, tv: (r, 0)),
            pl.BlockSpec((None, d, tf), lambda r, j, te, tv: (te[r], 0, jj(r, j, tv))),
            pl.BlockSpec((None, d, tf), lambda r, j, te, tv: (te[r], 0, jj(r, j, tv))),
            pl.BlockSpec((None, tf, d), lambda r, j, te, tv: (te[r], jj(r, j, tv), 0)),
        ],
        out_specs=pl.BlockSpec((MOE_TM, d), lambda r, j, te, tv: (r, 0)),
        scratch_shapes=[pltpu.VMEM((MOE_TM if nj > 1 else SUBLANES, d), F32)],
    )
    return pl.pallas_call(
        functools.partial(_moe_expert_kernel, nj=nj, pieces=pieces),
        grid_spec=grid_spec,
        out_shape=jax.ShapeDtypeStruct((n_rows, d), BF16),
        compiler_params=_params(("arbitrary", "arbitrary"), VMEM_LIMIT_RESIDENT),
        name="moe_experts",
    )(plan["tile_expert"], plan["tile_valid"], hs, gs, w_gate, w_up, w_down)


def _moe_combine_kernel(start_ref, off_ref, lo_ref, hi_ref, nwin_ref, x_ref, info_ref, y_hbm, o_ref,
                        ybuf, ybuf_more, sem, sem_more):
    step = pl.program_id(0)
    n_steps = pl.num_programs(0)
    per = COMB_STEP_TILES
    ring = COMB_AHEAD_BUFS * per

    def window_copy(idx, dst, dsem):
        start = pl.multiple_of(start_ref[idx], BF16_ROWS)
        return pltpu.make_async_copy(y_hbm.at[pl.ds(start, COMB_WC), :], dst, dsem)

    def first_group(tile, wait):
        buf = tile % ring
        for s in range(COMB_GROUP):
            @pl.when(s < nwin_ref[tile])
            def _():
                cp = window_copy(tile * COMB_MAXW + s, ybuf.at[buf, pl.ds(s * COMB_WC, COMB_WC), :],
                                 sem.at[buf, s])
                cp.wait() if wait else cp.start()

    def request(grid_step):
        for sub in range(per):
            first_group(grid_step * per + sub, wait=False)

    @pl.when(step == 0)
    def _():
        ybuf[...] = jnp.zeros_like(ybuf)
        ybuf_more[...] = jnp.zeros_like(ybuf_more)
        for ahead in range(COMB_AHEAD_BUFS - 1):
            @pl.when(ahead < n_steps)
            def _():
                request(ahead)

    @pl.when(step + COMB_AHEAD_BUFS - 1 < n_steps)
    def _():
        request(step + COMB_AHEAD_BUFS - 1)

    tiles = [step * per + sub for sub in range(per)]
    rows = [slice(sub * COMB_TS, (sub + 1) * COMB_TS) for sub in range(per)]
    for tile in tiles:
        first_group(tile, wait=True)

    lane = lax.broadcasted_iota(I32, (COMB_TS, COMB_WC), 1)
    lane_row = lax.broadcasted_iota(I32, (1, COMB_WC), 1)
    codes = [[jnp.broadcast_to(info_ref[r, n:n + 1].astype(I32), (COMB_TS, COMB_WC)) for n in range(TOP_K)]
             for r in rows]

    def onehot_group(sub, g):
        pieces = []
        for s in range(COMB_GROUP):
            idx = tiles[sub] * COMB_MAXW + g * COMB_GROUP + s
            off = off_ref[idx]
            in_span = (lane_row >= lo_ref[idx] - off) & (lane_row < hi_ref[idx] - off)
            target = lane + off
            hit = ((codes[sub][0] == target) | (codes[sub][1] == target)) & in_span
            pieces.append(jnp.where(hit, 1.0, 0.0).astype(BF16))
        return jnp.concatenate(pieces, axis=1)

    onehots = [onehot_group(sub, 0) for sub in range(per)]
    for sub in range(per):
        o_ref[rows[sub], :] = x_ref[rows[sub], :] + _dot(onehots[sub], ybuf[tiles[sub] % ring])

    for sub in range(per):
        n = nwin_ref[tiles[sub]]

        def more(g, carry, sub=sub, n=n):
            for wait in (False, True):
                for s in range(COMB_GROUP):
                    @pl.when(g * COMB_GROUP + s < n)
                    def _():
                        cp = window_copy(tiles[sub] * COMB_MAXW + g * COMB_GROUP + s,
                                         ybuf_more.at[pl.ds(s * COMB_WC, COMB_WC), :], sem_more.at[s])
                        cp.wait() if wait else cp.start()
            o_ref[rows[sub], :] += _dot(onehot_group(sub, g), ybuf_more[...])
            return carry

        lax.fori_loop(1, (n + COMB_GROUP - 1) // COMB_GROUP, more, 0)


def _moe_combine(plan, x, info, y):
    t, d = x.shape
    width = COMB_GROUP * COMB_WC
    step_rows = COMB_STEP_TILES * COMB_TS
    grid_spec = pltpu.PrefetchScalarGridSpec(
        num_scalar_prefetch=5,
        grid=(t // step_rows,),
        in_specs=[
            pl.BlockSpec((step_rows, d), lambda i, *_: (i, 0)),
            pl.BlockSpec((step_rows, 4), lambda i, *_: (i, 0)),
            pl.BlockSpec(memory_space=pl.ANY),
        ],
        out_specs=pl.BlockSpec((step_rows, d), lambda i, *_: (i, 0)),
        scratch_shapes=[
            pltpu.VMEM((COMB_AHEAD_BUFS * COMB_STEP_TILES, width, d), BF16),
            pltpu.VMEM((width, d), BF16),
            pltpu.SemaphoreType.DMA((COMB_AHEAD_BUFS * COMB_STEP_TILES, COMB_GROUP)),
            pltpu.SemaphoreType.DMA((COMB_GROUP,)),
        ],
    )
    return pl.pallas_call(
        _moe_combine_kernel,
        grid_spec=grid_spec,
        out_shape=jax.ShapeDtypeStruct((t, d), F32),
        compiler_params=_params(("arbitrary",), VMEM_LIMIT),
        name="moe_combine",
    )(plan["c_start"], plan["c_off"], plan["c_lo"], plan["c_hi"], plan["c_nwin"], x, info, y)


def kernel(x, norm_mix, norm_ffn, gla_w_in, gla_w_gate2, gla_b_gate, gla_out_gain, gla_w_out,
           moba_w_qkv, moba_q_gain, moba_k_gain, moba_w_out, ffn_w_gate, ffn_w_up, ffn_w_down,
           moe_w_router, moe_w_gate, moe_w_up, moe_w_down):
    batch, seq, d = x.shape
    t = batch * seq
    xt = x.reshape(t, d)

    (q, k, v, g, la), (ffn_wg, ffn_wu, ffn_wd, gla_wo, moba_wqkv, moba_wo) = _gla_inproj(
        xt, norm_mix[0:1], gla_w_in, gla_w_gate2[0], gla_b_gate[0:1],
        ride_along=(ffn_w_gate, ffn_w_up, ffn_w_down, gla_w_out, moba_w_qkv, moba_w_out), tm=PROJ_TM)
    o, _ = _gla(q, k, v, g, la, gla_out_gain[0:1], (), batch, seq, blk=GLA_BLK,
                rows_per_step=GLA_ROWS_PER_STEP)
    xt = _ffn_residual(xt, o, gla_wo[0], norm_ffn[0:1], ffn_wg[0], ffn_wu[0], ffn_wd[0],
                       tm=PROJ_TM, pieces=MOE_PIECES + MOE_PIECES)

    q, k, v = _moba_qkv(xt, norm_mix[1:2], moba_wqkv[0], moba_q_gain[0:1], moba_k_gain[0:1], tm=PROJ_TM)
    o, (moe_wg, moe_wu, moe_wd) = _moba(q, k, v, (moe_w_gate[0], moe_w_up[0], moe_w_down[0]), batch, seq,
                                        heads_per_step=MOBA_HEADS_PER_STEP)
    xt, h, info, rows, cnt = _router(xt, o, moba_wo[0], norm_ffn[1:2], moe_w_router[0], tr=ROUTER_TR)
    plan = _moe_plan(cnt, t)
    hs, gs = _moe_gather(plan, h, rows)
    y = _moe_experts(plan, hs, gs, moe_wg, moe_wu, moe_wd, pieces=MOE_PIECES + MOE_PIECES)
    xt = _moe_combine(plan, xt, info, y)
    return xt.reshape(batch, seq, d)
```

```python
import functools

import jax
import jax.numpy as jnp
from jax import lax
from jax.experimental import pallas as pl
from jax.experimental.pallas import tpu as pltpu

F32 = jnp.float32
BF16 = jnp.bfloat16
I32 = jnp.int32

EPS = 1e-6
NEG_INF = -1e30

D_MODEL = 1024
GLA_HEADS = 4
GLA_DK = 512
GLA_DV = 1024
GLA_HK = 128
GLA_HV = 256
GLA_RANK = 16
GLA_NORMALIZER = 16.0
GLA_CHUNK = 64

MOBA_HEADS = 8
MOBA_HD = 128
MOBA_BLOCK = 256
MOBA_TOPK = 3
MOBA_Q_SCALE = MOBA_HD ** -0.5 * 1.4426950408889634

N_EXPERTS = 8
TOP_K = 2

LANES = 128
SUBLANES = 8
BF16_ROWS = 16
ROUTER_ROWS = BF16_ROWS
VMEM_LIMIT = 48 * 1024 * 1024
VMEM_LIMIT_RESIDENT = 56 * 1024 * 1024

STREAM_BUFS = 3
PROJ_TM = 512
ROUTER_TR = 1024
GLA_BLK = 256
GLA_ROWS_PER_STEP = 4
MOBA_HEADS_PER_STEP = 2

MOE_PIECES = (512, 512, 768)
MOE_TM = 256
MOE_TG = 128
GATHER_ROWS = 512
TOK_BLK = LANES
GATHER_WB = 6
COMB_TS = 256
COMB_STEP_TILES = 2
COMB_WC = 128
COMB_GROUP = 8
COMB_AHEAD_BUFS = 3
COMB_MAXW = 16


def _params(semantics, vmem_bytes=None):
    return pltpu.CompilerParams(dimension_semantics=semantics, vmem_limit_bytes=vmem_bytes)


def _rms(x, gain):
    y = x * lax.rsqrt(jnp.mean(x * x, axis=-1, keepdims=True) + EPS)
    return y * gain


def _dot(a, b):
    return jnp.dot(a, b, preferred_element_type=F32)


def _dot_nt(a, b):
    return lax.dot_general(a, b, (((1,), (1,)), ((), ())), preferred_element_type=F32)


def _dot_tn(a, b):
    return lax.dot_general(a, b, (((0,), (0,)), ((), ())), preferred_element_type=F32)


def _split(a):
    hi = a.astype(BF16)
    lo = (a - hi.astype(F32)).astype(BF16)
    return hi, lo


def _log_sigmoid(z):
    return jnp.minimum(z, 0.0) - jnp.log(1.0 + jnp.exp(-jnp.abs(z)))


def _silu(z):
    return z / (1.0 + jnp.exp(-z))


def _swiglu_pieces(h, wg_ref, wu_ref, wd_ref, widths):
    bounds = [sum(widths[:n]) for n in range(len(widths) + 1)]
    assert bounds[-1] == wg_ref.shape[1], (widths, wg_ref.shape)
    gate_up = lambda c: (_dot(h, wg_ref[:, bounds[c]:bounds[c + 1]]), _dot(h, wu_ref[:, bounds[c]:bounds[c + 1]]))
    out = None
    upcoming = gate_up(0)
    for c in range(len(widths)):
        g, u = upcoming
        if c + 1 < len(widths):
            upcoming = gate_up(c + 1)
        d = _dot((_silu(g) * u).astype(BF16), wd_ref[bounds[c]:bounds[c + 1], :])
        out = d if out is None else out + d
    return out


def _ride_along_specs(arrays, n_steps, step):
    specs, shapes = [], []
    for w in arrays:
        n_e, rows, cols = w.shape
        per_e = n_steps // n_e
        if rows % (per_e * BF16_ROWS) == 0:
            blk = (None, rows // per_e, cols)
            imap = lambda *ids, per_e=per_e: (step(*ids) // per_e, step(*ids) % per_e, 0)
        else:
            assert cols % (per_e * LANES) == 0, (w.shape, n_steps)
            blk = (None, rows, cols // per_e)
            imap = lambda *ids, per_e=per_e: (step(*ids) // per_e, 0, step(*ids) % per_e)
        specs.append(pl.BlockSpec(blk, imap))
        shapes.append(jax.ShapeDtypeStruct(w.shape, BF16))
    return specs, shapes


def _with_ride_along(body, n_in, n_out, n_cast):
    def kernel(*refs):
        ins, refs = refs[:n_in], refs[n_in:]
        cast_in, refs = refs[:n_cast], refs[n_cast:]
        outs, refs = refs[:n_out], refs[n_out:]
        cast_out, scratch = refs[:n_cast], refs[n_cast:]
        for src, dst in zip(cast_in, cast_out):
            dst[...] = src[...].astype(BF16)
        body(*ins, *outs, *scratch)
    return kernel


def _gla_inproj_kernel(x_next_ref, x_first_ref, gain_ref, w_ref, wg2_ref, b_ref,
                       q_ref, k_ref, v_ref, g_ref, la_ref, wb_ref, h_ref):
    i = pl.program_id(0)

    @pl.when(i == 0)
    def _():
        wb_ref[...] = w_ref[...].astype(BF16)
        h_ref[0] = _rms(x_first_ref[...], gain_ref[...]).astype(BF16)

    n_main = 2 * GLA_DK + 2 * GLA_DV

    def step(cur, nxt):
        h = h_ref[cur]
        h_ref[nxt] = _rms(x_next_ref[...], gain_ref[...]).astype(BF16)
        a_lr = _dot(h, wb_ref[:, n_main:])
        a_hi, a_lo = _split(a_lr)
        w_hi, w_lo = _split(wg2_ref[...])
        z = _dot(jnp.concatenate([a_hi, a_lo, a_hi], axis=1), jnp.concatenate([w_hi, w_hi, w_lo], axis=0))
        la_ref[...] = _log_sigmoid(z + b_ref[...]) / GLA_NORMALIZER
        q_ref[...] = _dot(h, wb_ref[:, 0:GLA_DK]).astype(BF16)
        k_ref[...] = _dot(h, wb_ref[:, GLA_DK:2 * GLA_DK]).astype(BF16)
        v_ref[...] = _dot(h, wb_ref[:, 2 * GLA_DK:2 * GLA_DK + GLA_DV]).astype(BF16)
        g_ref[...] = _dot(h, wb_ref[:, 2 * GLA_DK + GLA_DV:n_main]).astype(BF16)

    @pl.when(i % 2 == 0)
    def _():
        step(0, 1)

    @pl.when(i % 2 == 1)
    def _():
        step(1, 0)


def _gla_inproj(x, gain, w_in, w_gate2, b_gate, ride_along, tm):
    t = x.shape[0]
    n_steps = t // tm
    row = lambda i: (i, 0)
    fixed = lambda i: (0, 0)
    cast_specs, cast_shapes = _ride_along_specs(ride_along, n_steps, lambda i: i)
    outs = pl.pallas_call(
        _with_ride_along(_gla_inproj_kernel, 6, 5, len(ride_along)),
        grid=(n_steps,),
        in_specs=[
            pl.BlockSpec((tm, D_MODEL), lambda i: (jnp.minimum(i + 1, n_steps - 1), 0)),
            pl.BlockSpec((tm, D_MODEL), fixed),
            pl.BlockSpec((1, D_MODEL), fixed),
            pl.BlockSpec((None,) + w_in.shape[1:], lambda i: (0, 0, 0), pipeline_mode=pl.Buffered(1)),
            pl.BlockSpec((GLA_RANK, GLA_DK), fixed),
            pl.BlockSpec((1, GLA_DK), fixed),
        ] + cast_specs,
        out_specs=[
            pl.BlockSpec((tm, GLA_DK), row),
            pl.BlockSpec((tm, GLA_DK), row),
            pl.BlockSpec((tm, GLA_DV), row),
            pl.BlockSpec((tm, GLA_DV), row),
            pl.BlockSpec((tm, GLA_DK), row),
        ] + cast_specs,
        out_shape=[
            jax.ShapeDtypeStruct((t, GLA_DK), BF16),
            jax.ShapeDtypeStruct((t, GLA_DK), BF16),
            jax.ShapeDtypeStruct((t, GLA_DV), BF16),
            jax.ShapeDtypeStruct((t, GLA_DV), BF16),
            jax.ShapeDtypeStruct((t, GLA_DK), F32),
        ] + cast_shapes,
        scratch_shapes=[pltpu.VMEM(w_in.shape[1:], BF16), pltpu.VMEM((2, tm, D_MODEL), BF16)],
        compiler_params=_params(("arbitrary",), VMEM_LIMIT),
        name="gla_inproj",
    )(x, x, gain, w_in, w_gate2, b_gate, *ride_along)
    return outs[:5], outs[5:]


def _gla_kernel(q_ref, k_ref, v_ref, g_ref, la_ref, og_ref, o_ref, st_ref, *, n_chunks):
    @pl.when(pl.program_id(1) == 0)
    def _():
        st_ref[...] = jnp.zeros_like(st_ref)

    n_rows = q_ref.shape[0]
    blk = n_chunks * GLA_CHUNK
    r = lax.broadcasted_iota(I32, (blk, blk), 0)
    c = lax.broadcasted_iota(I32, (blk, blk), 1)
    causal = (r >= c) & (r // GLA_CHUNK == c // GLA_CHUNK)
    tril = causal.astype(BF16)
    og = og_ref[...]
    chunk_rows = [slice(ci * GLA_CHUNK, (ci + 1) * GLA_CHUNK) for ci in range(n_chunks)]
    hk = [slice(h * GLA_HK, (h + 1) * GLA_HK) for h in range(GLA_HEADS)]
    hv = [slice(h * GLA_HV, (h + 1) * GLA_HV) for h in range(GLA_HEADS)]

    q_dec, k_inv, k_end, chunk_decay = [], [], [], []
    for b in range(n_rows):
        la = la_ref[b]
        la_hi, la_mid = _split(la)
        la_lo = (la - la_hi.astype(F32) - la_mid.astype(F32)).astype(BF16)
        cum = _dot(tril, la_hi) + _dot(tril, la_mid) + _dot(tril, la_lo)
        ends = [cum[rows.stop - 1:rows.stop, :] for rows in chunk_rows]
        last = jnp.concatenate([jnp.broadcast_to(e, (GLA_CHUNK, GLA_DK)) for e in ends], axis=0)
        q = q_ref[b].astype(F32) * (GLA_HK ** -0.5)
        k = k_ref[b].astype(F32)
        q_dec.append((q * jnp.exp(cum)).astype(BF16))
        k_inv.append((k * jnp.exp(-cum)).astype(BF16))
        k_end.append((k * jnp.exp(last - cum)).astype(BF16))
        chunk_decay.append([jnp.exp(e) for e in ends])

    streams = [(b, h) for b in range(n_rows) for h in range(GLA_HEADS)]
    scores = [jnp.where(causal, _dot_nt(q_dec[b][:, hk[h]], k_inv[b][:, hk[h]]), 0.0).astype(BF16)
              for b, h in streams]
    state_in = [[_dot_tn(v_ref[b, rows, hv[h]], k_end[b][rows, hk[h]]) for rows in chunk_rows]
                for b, h in streams]
    intra = [_dot(scores[n], v_ref[b, :, hv[h]]) for n, (b, h) in enumerate(streams)]
    for n, (b, h) in enumerate(streams):
        st = st_ref[b, h]
        inter = []
        for ci, rows in enumerate(chunk_rows):
            inter.append(_dot_nt(q_dec[b][rows, hk[h]], st.astype(BF16)))
            st = st * chunk_decay[b][ci][:, hk[h]] + state_in[n][ci]
        st_ref[b, h] = st
        o = intra[n] + jnp.concatenate(inter, axis=0)
        o = o * lax.rsqrt(jnp.mean(o * o, axis=-1, keepdims=True) + EPS) * og
        o = o * _silu(g_ref[b, :, hv[h]].astype(F32))
        o_ref[b, :, hv[h]] = o.astype(BF16)


def _gla(q, k, v, g, la, out_gain, ride_along, batch, seq, blk, rows_per_step):
    nb = seq // blk
    nbt = batch // rows_per_step
    per_row = lambda a: a.reshape(batch, seq, a.shape[-1])
    spec = lambda width: pl.BlockSpec((rows_per_step, blk, width), lambda b, i: (b, i, 0))
    cast_specs, cast_shapes = _ride_along_specs(ride_along, nbt * nb, lambda b, i: b * nb + i)
    outs = pl.pallas_call(
        _with_ride_along(functools.partial(_gla_kernel, n_chunks=blk // GLA_CHUNK), 6, 1, len(ride_along)),
        grid=(nbt, nb),
        in_specs=[spec(GLA_DK), spec(GLA_DK), spec(GLA_DV), spec(GLA_DV), spec(GLA_DK),
                  pl.BlockSpec((1, GLA_HV), lambda b, i: (0, 0))] + cast_specs,
        out_specs=[spec(GLA_DV)] + cast_specs,
        out_shape=[jax.ShapeDtypeStruct((batch, seq, GLA_DV), BF16)] + cast_shapes,
        scratch_shapes=[pltpu.VMEM((rows_per_step, GLA_HEADS, GLA_HV, GLA_HK), F32)],
        compiler_params=_params(("arbitrary", "arbitrary"), VMEM_LIMIT),
        name="gla_core",
    )(per_row(q), per_row(k), per_row(v), per_row(g), per_row(la), out_gain, *ride_along)
    return outs[0].reshape(batch * seq, GLA_DV), outs[1:]


def _ffn_kernel(x_ref, o_ref, wo_ref, gain_ref, wg_ref, wu_ref, wd_ref, y_ref, *, pieces):
    x1 = x_ref[...] + _dot(o_ref[...], wo_ref[...])
    h = _rms(x1, gain_ref[...]).astype(BF16)
    y_ref[...] = x1 + _swiglu_pieces(h, wg_ref, wu_ref, wd_ref, pieces)


def _ffn_residual(x, o, w_out, gain, w_gate, w_up, w_down, tm, pieces):
    t, d = x.shape
    row = lambda i: (i, 0)
    resident = lambda a: pl.BlockSpec(a.shape, lambda i: (0, 0), pipeline_mode=pl.Buffered(1))
    return pl.pallas_call(
        functools.partial(_ffn_kernel, pieces=pieces),
        grid=(t // tm,),
        in_specs=[pl.BlockSpec((tm, d), row), pl.BlockSpec((tm, d), row), resident(w_out), resident(gain),
                  resident(w_gate), resident(w_up), resident(w_down)],
        out_specs=pl.BlockSpec((tm, d), row),
        out_shape=jax.ShapeDtypeStruct((t, d), F32),
        compiler_params=_params(("parallel",), VMEM_LIMIT_RESIDENT),
        name="ffn_swiglu",
    )(x, o, w_out, gain, w_gate, w_up, w_down)


def _moba_qkv_kernel(x_ref, gain_ref, w_ref, qg_ref, kg_ref, q_ref, k_ref, v_ref):
    h = _rms(x_ref[...], gain_ref[...]).astype(BF16)
    d = MOBA_HEADS * MOBA_HD
    qg = qg_ref[...]
    kg = kg_ref[...]
    q = _dot(h, w_ref[:, 0:d])
    k = _dot(h, w_ref[:, d:2 * d])
    for hd in range(MOBA_HEADS):
        cols = slice(hd * MOBA_HD, (hd + 1) * MOBA_HD)
        q_ref[:, cols] = (_rms(q[:, cols], qg) * MOBA_Q_SCALE).astype(BF16)
        k_ref[:, cols] = _rms(k[:, cols], kg).astype(BF16)
    v_ref[...] = _dot(h, w_ref[:, 2 * d:]).astype(BF16)


def _moba_qkv(x, gain, w, q_gain, k_gain, tm):
    t = x.shape[0]
    d = MOBA_HEADS * MOBA_HD
    row = lambda i: (i, 0)
    fixed = lambda i: (0, 0)
    return pl.pallas_call(
        _moba_qkv_kernel,
        grid=(t // tm,),
        in_specs=[
            pl.BlockSpec((tm, D_MODEL), row),
            pl.BlockSpec((1, D_MODEL), fixed),
            pl.BlockSpec((D_MODEL, 3 * d), fixed),
            pl.BlockSpec((1, MOBA_HD), fixed),
            pl.BlockSpec((1, MOBA_HD), fixed),
        ],
        out_specs=[pl.BlockSpec((tm, d), row)] * 3,
        out_shape=[jax.ShapeDtypeStruct((t, d), BF16)] * 3,
        compiler_params=_params(("parallel",), VMEM_LIMIT),
        name="moba_qkv",
    )(x, gain, w, q_gain, k_gain)


def _reduce_rows(x, op, stop=SUBLANES):
    while x.shape[0] > stop:
        half = x.shape[0] // 2
        x = op(x[:half], x[half:])
    return x


def _moba_kernel(q_ref, k_ref, v_ref, o_ref, *, n_blocks):
    seq = n_blocks * MOBA_BLOCK
    heads = range(q_ref.shape[1] // MOBA_HD)
    cols = [slice(hd * MOBA_HD, (hd + 1) * MOBA_HD) for hd in heads]
    n_pad = BF16_ROWS
    ones_rows = jnp.where(lax.broadcasted_iota(I32, (BF16_ROWS, seq), 0) == 0, 1.0, 0.0)
    v_t = [jnp.concatenate([v_ref[:, cols[hd]].astype(F32).T, ones_rows], axis=0).astype(BF16) for hd in heads]
    blk_of_key = lax.broadcasted_iota(I32, (n_pad, seq), 1) // MOBA_BLOCK
    blk_row = lax.broadcasted_iota(I32, (n_pad, seq), 0)
    indicator = jnp.where(blk_of_key == blk_row, 1.0 / MOBA_BLOCK, 0.0).astype(BF16)
    k_mean = [_split(_dot(indicator, k_ref[:, cols[hd]])) for hd in heads]

    blk_id = lax.broadcasted_iota(I32, (n_pad, MOBA_BLOCK), 0)
    key_pos = lax.broadcasted_iota(I32, (MOBA_BLOCK, MOBA_BLOCK), 0)
    query_pos = lax.broadcasted_iota(I32, (MOBA_BLOCK, MOBA_BLOCK), 1)
    causal = key_pos <= query_pos

    def masked_scores(hd, i):
        rows = slice(i * MOBA_BLOCK, (i + 1) * MOBA_BLOCK)
        n_keys = (i + 1) * MOBA_BLOCK
        q = q_ref[rows, cols[hd]]
        s = _dot_nt(k_ref[0:n_keys, cols[hd]], q).astype(BF16)
        selected = None
        if i > MOBA_TOPK:
            km_hi, km_lo = k_mean[hd]
            gate = _dot_nt(km_hi, q) + _dot_nt(km_lo, q)
            gate = jnp.where(blk_id < i, gate, NEG_INF)
            rank = jnp.zeros_like(gate)
            for jp in range(i):
                row = gate[jp:jp + 1, :]
                beats = (row > gate) | ((row == gate) & (jp < blk_id))
                rank = rank + beats.astype(F32)
            selected = jnp.where((rank < MOBA_TOPK) & (blk_id < i), 1.0, 0.0)
        pieces = []
        top = None
        for j in range(i + 1):
            sj = s[j * MOBA_BLOCK:(j + 1) * MOBA_BLOCK, :]
            if j == i:
                sj = jnp.where(causal, sj, NEG_INF)
            elif selected is not None:
                sj = jnp.where(selected[j:j + 1, :] > 0.5, sj, NEG_INF)
            pieces.append(sj)
            folded = _reduce_rows(sj, jnp.maximum, stop=BF16_ROWS)
            top = folded if top is None else jnp.maximum(top, folded)
        return pieces, top

    def weighted_values(hd, i, p):
        o_t = _dot(v_t[hd][:, 0:(i + 1) * MOBA_BLOCK], p)
        rows = slice(i * MOBA_BLOCK, (i + 1) * MOBA_BLOCK)
        o_ref[rows, cols[hd]] = (o_t[:MOBA_HD, :] / o_t[MOBA_HD:MOBA_HD + 1, :]).T.astype(BF16)

    def softmax_numerators(scored):
        pieces, top = scored
        m = top.max(axis=0, keepdims=True)
        probs = [jnp.exp2(sj - m) for sj in pieces]
        return probs[0] if len(probs) == 1 else jnp.concatenate(probs, axis=0)

    upcoming = [masked_scores(hd, 0) for hd in heads]
    pending = None
    for i in range(n_blocks):
        pieces = upcoming
        if i + 1 < n_blocks:
            upcoming = [masked_scores(hd, i + 1) for hd in heads]
        if pending is not None:
            for hd in heads:
                weighted_values(hd, i - 1, pending[hd])
        pending = [softmax_numerators(pieces[hd]) for hd in heads]
    for hd in heads:
        weighted_values(hd, n_blocks - 1, pending[hd])


def _moba(q, k, v, ride_along, batch, seq, heads_per_step):
    t, d = q.shape
    n_groups = MOBA_HEADS // heads_per_step
    spec = pl.BlockSpec((seq, heads_per_step * MOBA_HD), lambda b, h: (b, h))
    cast_specs, cast_shapes = _ride_along_specs(ride_along, batch * n_groups, lambda b, h: b * n_groups + h)
    outs = pl.pallas_call(
        _with_ride_along(functools.partial(_moba_kernel, n_blocks=seq // MOBA_BLOCK), 3, 1, len(ride_along)),
        grid=(batch, n_groups),
        in_specs=[spec, spec, spec] + cast_specs,
        out_specs=[spec] + cast_specs,
        out_shape=[jax.ShapeDtypeStruct((t, d), BF16)] + cast_shapes,
        compiler_params=_params(("arbitrary", "arbitrary"), VMEM_LIMIT_RESIDENT),
        name="moba_attn",
    )(q, k, v, *ride_along)
    return outs[0], outs[1:]


def _router_kernel(x_ref, o_ref, wo_ref, gain_ref, w_ref, x2_ref, h_ref, info_ref, rows_ref, cnt_ref,
                   carry_ref, tri_ref,
                   *, region_rows):
    tr = x_ref.shape[0]

    @pl.when(pl.program_id(0) == 0)
    def _():
        carry_ref[...] = jnp.zeros_like(carry_ref)
        r = lax.broadcasted_iota(I32, (TOK_BLK, TOK_BLK), 0)
        c = lax.broadcasted_iota(I32, (TOK_BLK, TOK_BLK), 1)
        tri_ref[...] = (r < c).astype(BF16)

    x2 = x_ref[...] + _dot(o_ref[...], wo_ref[...])
    x2_ref[...] = x2
    h = _rms(x2, gain_ref[...])
    h_hi, h_lo = _split(h)
    h_ref[...] = h_hi
    n_e = ROUTER_ROWS
    wt = w_ref[...]
    passes = _dot_nt(wt, h_hi)
    logits = passes[:n_e] + passes[n_e:] + _dot_nt(wt[:n_e], h_lo)
    e = lax.broadcasted_iota(I32, logits.shape, 0)
    logits = jnp.where(e < N_EXPERTS, logits, -jnp.inf)
    m1 = logits.max(axis=0, keepdims=True)
    i1 = jnp.where(logits == m1, e, n_e).min(axis=0, keepdims=True)
    first = e == i1
    rest = jnp.where(first, -jnp.inf, logits)
    m2 = rest.max(axis=0, keepdims=True)
    i2 = jnp.where(rest == m2, e, n_e).min(axis=0, keepdims=True)
    second = e == i2
    e2 = jnp.exp(m2 - m1)
    denom = 1.0 + e2
    sel = jnp.where(first | second, 1.0, 0.0).astype(BF16)
    nb = tr // TOK_BLK
    blk_row = lax.broadcasted_iota(I32, (nb, tr), 0)
    blk_tok = lax.broadcasted_iota(I32, (nb, tr), 1) // TOK_BLK
    cnt = _dot_nt((blk_row == blk_tok).astype(BF16), sel)
    cnt_ref[...] = jnp.concatenate([cnt, jnp.zeros((nb, LANES - n_e), F32)], axis=1)
    offset = carry_ref[...]
    before = []
    for b in range(nb):
        sel_b = sel[:, b * TOK_BLK:(b + 1) * TOK_BLK]
        before.append(_dot(sel_b, tri_ref[...]) + offset)
        offset = offset + jnp.broadcast_to(sel_b.astype(F32).sum(axis=1, keepdims=True), offset.shape)
    carry_ref[...] = offset
    before = jnp.concatenate(before, axis=1)
    rank1 = jnp.where(first, before, 0.0).sum(axis=0, keepdims=True)
    rank2 = jnp.where(second, before, 0.0).sum(axis=0, keepdims=True)
    fields = (i1.astype(F32) * region_rows + rank1, i2.astype(F32) * region_rows + rank2,
              1.0 / denom, e2 / denom)
    for n, f in enumerate(fields):
        for b in range(nb):
            rows_ref[n, b:b + 1, :] = f[:, b * TOK_BLK:(b + 1) * TOK_BLK]
    info_t = jnp.concatenate(list(fields) + [jnp.zeros((LANES - len(fields), tr), F32)], axis=0)
    info_ref[...] = info_t.T[:, :len(fields)]


def _router(x, o, w_out, gain, w_router, tr):
    t = x.shape[0]
    nb = tr // TOK_BLK
    pad = ((0, ROUTER_ROWS - N_EXPERTS), (0, 0))
    w_hi = w_router.T.astype(BF16)
    w_lo = (w_router.T - w_hi.astype(F32)).astype(BF16)
    w_both = jnp.concatenate([jnp.pad(w_hi, pad), jnp.pad(w_lo, pad)], axis=0)
    row = lambda i: (i, 0)
    fixed = lambda i: (0, 0)
    return pl.pallas_call(
        functools.partial(_router_kernel, region_rows=float(t)),
        grid=(t // tr,),
        in_specs=[
            pl.BlockSpec((tr, D_MODEL), row),
            pl.BlockSpec((tr, D_MODEL), row),
            pl.BlockSpec((D_MODEL, D_MODEL), fixed),
            pl.BlockSpec((1, D_MODEL), fixed),
            pl.BlockSpec((2 * ROUTER_ROWS, D_MODEL), fixed),
        ],
        out_specs=[
            pl.BlockSpec((tr, D_MODEL), row),
            pl.BlockSpec((tr, D_MODEL), row),
            pl.BlockSpec((tr, 4), row),
            pl.BlockSpec((4, nb, TOK_BLK), lambda i: (0, i, 0)),
            pl.BlockSpec((nb, LANES), row),
        ],
        out_shape=[
            jax.ShapeDtypeStruct((t, D_MODEL), F32),
            jax.ShapeDtypeStruct((t, D_MODEL), BF16),
            jax.ShapeDtypeStruct((t, 4), F32),
            jax.ShapeDtypeStruct((4, t // TOK_BLK, TOK_BLK), F32),
            jax.ShapeDtypeStruct((t // TOK_BLK, LANES), F32),
        ],
        scratch_shapes=[pltpu.VMEM((ROUTER_ROWS, LANES), F32), pltpu.VMEM((TOK_BLK, TOK_BLK), BF16)],
        compiler_params=_params(("arbitrary",), VMEM_LIMIT_RESIDENT),
        name="moe_router",
    )(x, o, w_out, gain, w_both)


def _moe_plan(cnt, t):
    e_ids = jnp.arange(N_EXPERTS, dtype=I32)
    by_expert = lambda table, idx: jnp.sum(jnp.where(idx[..., None] == e_ids, table, 0), axis=-1)
    cnt_blk = cnt[:, :N_EXPERTS].astype(I32)
    cum_blk = jnp.cumsum(cnt_blk, axis=0)
    counts = cum_blk[-1]
    tiles_e = (counts + MOE_TM - 1) // MOE_TM
    tile_end = jnp.cumsum(tiles_e)
    tile_off = tile_end - tiles_e
    nt = TOP_K * t // MOE_TM + N_EXPERTS
    r = jnp.arange(nt, dtype=I32)
    valid = r < tile_end[-1]
    te = jnp.minimum(jnp.sum((r[:, None] >= tile_end[None, :]).astype(I32), axis=1), N_EXPERTS - 1)
    local = r - by_expert(tile_off, te)
    last_e = jnp.max(jnp.where(tiles_e > 0, e_ids, 0))
    tile_expert = jnp.where(valid, te, last_e)
    shift = e_ids * t - tile_off * MOE_TM

    per = MOE_TM // MOE_TG
    rg = jnp.arange(nt * per, dtype=I32)
    eg = jnp.repeat(te, per)
    k0 = jnp.repeat(local, per) * MOE_TM + (rg % per) * MOE_TG
    k1 = jnp.minimum(k0 + MOE_TG, by_expert(counts, eg))
    has = jnp.repeat(valid, per) & (k1 > k0)
    cum_e = by_expert(cum_blk[None, :, :], eg[:, None])
    s_lo = jnp.sum((cum_e <= k0[:, None]).astype(I32), axis=1)
    s_hi = jnp.sum((cum_e <= (k1 - 1)[:, None]).astype(I32), axis=1)
    g_nwin = jnp.where(has, (s_hi - s_lo) // GATHER_WB + 1, 0)
    g_slo = jnp.where(has, s_lo, 0)
    g_code = eg * t + k0

    nts = t // COMB_TS
    cnt_ts = cnt_blk.reshape(nts, COMB_TS // TOK_BLK, N_EXPERTS).sum(axis=1)
    seg_lo = (tile_off * MOE_TM)[None, :] + jnp.cumsum(cnt_ts, axis=0) - cnt_ts
    seg_hi = seg_lo + cnt_ts
    ws0 = (seg_lo // BF16_ROWS) * BF16_ROWS
    nw = jnp.where(cnt_ts > 0, (seg_hi - ws0 + COMB_WC - 1) // COMB_WC, 0)
    nw_end = jnp.cumsum(nw, axis=1)
    k = jnp.arange(COMB_MAXW, dtype=I32)
    ek = jnp.minimum(jnp.sum((k[None, :, None] >= nw_end[:, None, :]).astype(I32), axis=2), N_EXPERTS - 1)
    pick = lambda a: by_expert(a[:, None, :], ek)
    live = k[None, :] < nw_end[:, -1:]
    nominal = pick(ws0) + (k[None, :] - pick(nw_end - nw)) * COMB_WC
    shift_k = by_expert(shift, ek)
    c_start = jnp.minimum(nominal, by_expert(tile_end * MOE_TM, ek) - COMB_WC)
    c_off = shift_k + c_start
    c_lo = jnp.maximum(pick(seg_lo), nominal) + shift_k
    c_hi = jnp.minimum(pick(seg_hi), nominal + COMB_WC) + shift_k
    flat = lambda a: jnp.where(live, a, 0).reshape(-1)
    return dict(
        tile_expert=tile_expert, tile_valid=valid.astype(I32),
        g_code=g_code, g_slo=g_slo, g_nwin=g_nwin,
        c_start=flat(c_start), c_off=flat(c_off), c_lo=flat(c_lo), c_hi=flat(c_hi), c_nwin=nw_end[:, -1])


def _moe_gather_kernel(code_ref, slo_ref, nwin_ref, rows_ref, h_ref, hs_ref, gs_ref):
    n_blk = rows_ref.shape[1]
    n_sub = hs_ref.shape[0] // MOE_TG
    sub_ids = [pl.program_id(0) * n_sub + sub for sub in range(n_sub)]
    out_rows = [slice(sub * MOE_TG, (sub + 1) * MOE_TG) for sub in range(n_sub)]

    def window(r, k):
        row_id = (code_ref[r] + lax.broadcasted_iota(I32, (MOE_TG, TOK_BLK), 0)).astype(F32)
        nominal = slo_ref[r] + k * GATHER_WB
        sb = jnp.minimum(nominal, n_blk - GATHER_WB)
        fresh = sb + lax.broadcasted_iota(I32, (GATHER_WB, TOK_BLK), 0) >= nominal
        p1 = jnp.where(fresh, rows_ref[0, pl.ds(sb, GATHER_WB), :], -1.0)
        p2 = jnp.where(fresh, rows_ref[1, pl.ds(sb, GATHER_WB), :], -1.0)
        w1 = rows_ref[2, pl.ds(sb, GATHER_WB), :]
        w2 = rows_ref[3, pl.ds(sb, GATHER_WB), :]
        pieces = []
        g = jnp.zeros((MOE_TG, TOK_BLK), F32)
        for a in range(GATHER_WB):
            m1 = p1[a:a + 1, :] == row_id
            m2 = p2[a:a + 1, :] == row_id
            pieces.append(jnp.where(m1 | m2, 1.0, 0.0).astype(BF16))
            g = g + jnp.where(m1, w1[a:a + 1, :], 0.0) + jnp.where(m2, w2[a:a + 1, :], 0.0)
        onehot = jnp.concatenate(pieces, axis=1)
        return onehot, g.sum(axis=-1, keepdims=True), pl.multiple_of(sb * TOK_BLK, TOK_BLK)

    gathered = lambda onehot, tok0: _dot(onehot, h_ref[pl.ds(tok0, GATHER_WB * TOK_BLK), :]).astype(BF16)

    firsts = [window(r, 0) for r in sub_ids]
    for rows, (onehot, weight, tok0) in zip(out_rows, firsts):
        hs_ref[rows, :] = gathered(onehot, tok0)
        gs_ref[rows, :] = weight

    for r, rows in zip(sub_ids, out_rows):
        def more(k, carry, r=r, rows=rows):
            onehot, weight, tok0 = window(r, k)
            hs_ref[rows, :] += gathered(onehot, tok0)
            gs_ref[rows, :] += weight
            return carry

        lax.fori_loop(1, nwin_ref[r], more, 0)


def _moe_gather(plan, h, rows):
    t = h.shape[0]
    n_rows = plan["g_code"].shape[0] * MOE_TG
    n_blk = t // TOK_BLK
    grid_spec = pltpu.PrefetchScalarGridSpec(
        num_scalar_prefetch=3,
        grid=(n_rows // GATHER_ROWS,),
        in_specs=[
            pl.BlockSpec((4, n_blk, TOK_BLK), lambda r, *_: (0, 0, 0), pipeline_mode=pl.Buffered(1)),
            pl.BlockSpec((t, D_MODEL), lambda r, *_: (0, 0), pipeline_mode=pl.Buffered(1)),
        ],
        out_specs=[
            pl.BlockSpec((GATHER_ROWS, D_MODEL), lambda r, *_: (r, 0)),
            pl.BlockSpec((GATHER_ROWS, 1), lambda r, *_: (r, 0)),
        ],
    )
    return pl.pallas_call(
        _moe_gather_kernel,
        grid_spec=grid_spec,
        out_shape=[
            jax.ShapeDtypeStruct((n_rows, D_MODEL), BF16),
            jax.ShapeDtypeStruct((n_rows, 1), F32),
        ],
        compiler_params=_params(("parallel",), VMEM_LIMIT_RESIDENT),
        name="moe_gather",
    )(plan["g_code"], plan["g_slo"], plan["g_nwin"], rows, h)


def _moe_expert_kernel(te_ref, tv_ref, hs_hbm, gs_ref, wg_ref, wu_ref, wd_ref, y_ref, acc_ref, hbuf, hsem,
                       *, nj, pieces):
    r = pl.program_id(0)
    j = pl.program_id(1)
    n_tiles = pl.num_programs(0)
    valid = tv_ref[r] > 0

    def tile_copy(tile):
        rows = pl.ds(pl.multiple_of(tile * MOE_TM, MOE_TM), MOE_TM)
        return pltpu.make_async_copy(hs_hbm.at[rows, :], hbuf.at[tile % STREAM_BUFS], hsem.at[tile % STREAM_BUFS])

    @pl.when(j == 0)
    def _():
        @pl.when(r == 0)
        def _():
            for ahead in range(STREAM_BUFS - 1):
                @pl.when(ahead < n_tiles)
                def _():
                    tile_copy(ahead).start()

        @pl.when(r + STREAM_BUFS - 1 < n_tiles)
        def _():
            tile_copy(r + STREAM_BUFS - 1).start()

        tile_copy(r).wait()

    part = lambda: _swiglu_pieces(hbuf[r % STREAM_BUFS], wg_ref, wu_ref, wd_ref, pieces)

    if nj == 1:
        @pl.when(valid)
        def _():
            y_ref[...] = (part() * gs_ref[...]).astype(BF16)
    else:
        @pl.when(valid & (j == 0))
        def _():
            acc_ref[...] = part()

        if nj > 2:
            @pl.when(valid & (j > 0) & (j < nj - 1))
            def _():
                acc_ref[...] += part()

        @pl.when(valid & (j == nj - 1))
        def _():
            y_ref[...] = ((acc_ref[...] + part()) * gs_ref[...]).astype(BF16)

    @pl.when(jnp.logical_not(valid) & (j == nj - 1))
    def _():
        y_ref[...] = jnp.zeros_like(y_ref)


def _moe_experts(plan, hs, gs, w_gate, w_up, w_down, pieces):
    tf = sum(pieces)
    n_rows, d = hs.shape
    ff = w_gate.shape[2]
    nj = ff // tf
    jj = lambda r, j, tv: jnp.where(tv[r] > 0, j, nj - 1)
    grid_spec = pltpu.PrefetchScalarGridSpec(
        num_scalar_prefetch=2,
        grid=(n_rows // MOE_TM, nj),
        in_specs=[
            pl.BlockSpec(memory_space=pl.ANY),
            pl.BlockSpec((MOE_TM, 1), lambda r, j, te, tv: (r, 0)),
            pl.BlockSpec((None, d, tf), lambda r, j, te, tv: (te[r], 0, jj(r, j, tv))),
            pl.BlockSpec((None, d, tf), lambda r, j, te, tv: (te[r], 0, jj(r, j, tv))),
            pl.BlockSpec((None, tf, d), lambda r, j, te, tv: (te[r], jj(r, j, tv), 0)),
        ],
        out_specs=pl.BlockSpec((MOE_TM, d), lambda r, j, te, tv: (r, 0)),
        scratch_shapes=[pltpu.VMEM((MOE_TM if nj > 1 else SUBLANES, d), F32),
                        pltpu.VMEM((STREAM_BUFS, MOE_TM, d), BF16), pltpu.SemaphoreType.DMA((STREAM_BUFS,))],
    )
    return pl.pallas_call(
        functools.partial(_moe_expert_kernel, nj=nj, pieces=pieces),
        grid_spec=grid_spec,
        out_shape=jax.ShapeDtypeStruct((n_rows, d), BF16),
        compiler_params=_params(("arbitrary", "arbitrary"), VMEM_LIMIT_RESIDENT),
        name="moe_experts",
    )(plan["tile_expert"], plan["tile_valid"], hs, gs, w_gate, w_up, w_down)


def _moe_combine_kernel(start_ref, off_ref, lo_ref, hi_ref, nwin_ref, x_ref, info_ref, y_hbm, o_ref,
                        ybuf, ybuf_more, sem, sem_more):
    step = pl.program_id(0)
    n_steps = pl.num_programs(0)
    per = COMB_STEP_TILES
    ring = COMB_AHEAD_BUFS * per

    def window_copy(idx, dst, dsem):
        start = pl.multiple_of(start_ref[idx], BF16_ROWS)
        return pltpu.make_async_copy(y_hbm.at[pl.ds(start, COMB_WC), :], dst, dsem)

    def first_group(tile, wait):
        buf = tile % ring
        for s in range(COMB_GROUP):
            @pl.when(s < nwin_ref[tile])
            def _():
                cp = window_copy(tile * COMB_MAXW + s, ybuf.at[buf, pl.ds(s * COMB_WC, COMB_WC), :],
                                 sem.at[buf, s])
                cp.wait() if wait else cp.start()

    def request(grid_step):
        for sub in range(per):
            first_group(grid_step * per + sub, wait=False)

    @pl.when(step == 0)
    def _():
        ybuf[...] = jnp.zeros_like(ybuf)
        ybuf_more[...] = jnp.zeros_like(ybuf_more)
        for ahead in range(COMB_AHEAD_BUFS - 1):
            @pl.when(ahead < n_steps)
            def _():
                request(ahead)

    @pl.when(step + COMB_AHEAD_BUFS - 1 < n_steps)
    def _():
        request(step + COMB_AHEAD_BUFS - 1)

    tiles = [step * per + sub for sub in range(per)]
    rows = [slice(sub * COMB_TS, (sub + 1) * COMB_TS) for sub in range(per)]
    for tile in tiles:
        first_group(tile, wait=True)

    lane = lax.broadcasted_iota(I32, (COMB_TS, COMB_WC), 1)
    lane_row = lax.broadcasted_iota(I32, (1, COMB_WC), 1)
    codes = [[jnp.broadcast_to(info_ref[r, n:n + 1].astype(I32), (COMB_TS, COMB_WC)) for n in range(TOP_K)]
             for r in rows]

    def onehot_group(sub, g):
        pieces = []
        for s in range(COMB_GROUP):
            idx = tiles[sub] * COMB_MAXW + g * COMB_GROUP + s
            off = off_ref[idx]
            in_span = (lane_row >= lo_ref[idx] - off) & (lane_row < hi_ref[idx] - off)
            target = lane + off
            hit = ((codes[sub][0] == target) | (codes[sub][1] == target)) & in_span
            pieces.append(jnp.where(hit, 1.0, 0.0).astype(BF16))
        return jnp.concatenate(pieces, axis=1)

    onehots = [onehot_group(sub, 0) for sub in range(per)]
    for sub in range(per):
        o_ref[rows[sub], :] = x_ref[rows[sub], :] + _dot(onehots[sub], ybuf[tiles[sub] % ring])

    for sub in range(per):
        n = nwin_ref[tiles[sub]]

        def more(g, carry, sub=sub, n=n):
            for wait in (False, True):
                for s in range(COMB_GROUP):
                    @pl.when(g * COMB_GROUP + s < n)
                    def _():
                        cp = window_copy(tiles[sub] * COMB_MAXW + g * COMB_GROUP + s,
                                         ybuf_more.at[pl.ds(s * COMB_WC, COMB_WC), :], sem_more.at[s])
                        cp.wait() if wait else cp.start()
            o_ref[rows[sub], :] += _dot(onehot_group(sub, g), ybuf_more[...])
            return carry

        lax.fori_loop(1, (n + COMB_GROUP - 1) // COMB_GROUP, more, 0)


def _moe_combine(plan, x, info, y):
    t, d = x.shape
    width = COMB_GROUP * COMB_WC
    step_rows = COMB_STEP_TILES * COMB_TS
    grid_spec = pltpu.PrefetchScalarGridSpec(
        num_scalar_prefetch=5,
        grid=(t // step_rows,),
        in_specs=[
            pl.BlockSpec((step_rows, d), lambda i, *_: (i, 0)),
            pl.BlockSpec((step_rows, 4), lambda i, *_: (i, 0)),
            pl.BlockSpec(memory_space=pl.ANY),
        ],
        out_specs=pl.BlockSpec((step_rows, d), lambda i, *_: (i, 0)),
        scratch_shapes=[
            pltpu.VMEM((COMB_AHEAD_BUFS * COMB_STEP_TILES, width, d), BF16),
            pltpu.VMEM((width, d), BF16),
            pltpu.SemaphoreType.DMA((COMB_AHEAD_BUFS * COMB_STEP_TILES, COMB_GROUP)),
            pltpu.SemaphoreType.DMA((COMB_GROUP,)),
        ],
    )
    return pl.pallas_call(
        _moe_combine_kernel,
        grid_spec=grid_spec,
        out_shape=jax.ShapeDtypeStruct((t, d), F32),
        compiler_params=_params(("arbitrary",), VMEM_LIMIT),
        name="moe_combine",
    )(plan["c_start"], plan["c_off"], plan["c_lo"], plan["c_hi"], plan["c_nwin"], x, info, y)


def kernel(x, norm_mix, norm_ffn, gla_w_in, gla_w_gate2, gla_b_gate, gla_out_gain, gla_w_out,
           moba_w_qkv, moba_q_gain, moba_k_gain, moba_w_out, ffn_w_gate, ffn_w_up, ffn_w_down,
           moe_w_router, moe_w_gate, moe_w_up, moe_w_down):
    batch, seq, d = x.shape
    t = batch * seq
    xt = x.reshape(t, d)

    (q, k, v, g, la), (ffn_wg, ffn_wu, ffn_wd, gla_wo, moba_wqkv, moba_wo) = _gla_inproj(
        xt, norm_mix[0:1], gla_w_in, gla_w_gate2[0], gla_b_gate[0:1],
        ride_along=(ffn_w_gate, ffn_w_up, ffn_w_down, gla_w_out, moba_w_qkv, moba_w_out), tm=PROJ_TM)
    o, _ = _gla(q, k, v, g, la, gla_out_gain[0:1], (), batch, seq, blk=GLA_BLK,
                rows_per_step=GLA_ROWS_PER_STEP)
    xt = _ffn_residual(xt, o, gla_wo[0], norm_ffn[0:1], ffn_wg[0], ffn_wu[0], ffn_wd[0],
                       tm=PROJ_TM, pieces=MOE_PIECES + MOE_PIECES)

    q, k, v = _moba_qkv(xt, norm_mix[1:2], moba_wqkv[0], moba_q_gain[0:1], moba_k_gain[0:1], tm=PROJ_TM)
    o, (moe_wg, moe_wu, moe_wd) = _moba(q, k, v, (moe_w_gate[0], moe_w_up[0], moe_w_down[0]), batch, seq,
                                        heads_per_step=MOBA_HEADS_PER_STEP)
    xt, h, info, rows, cnt = _router(xt, o, moba_wo[0], norm_ffn[1:2], moe_w_router[0], tr=ROUTER_TR)
    plan = _moe_plan(cnt, t)
    hs, gs = _moe_gather(plan, h, rows)
    y = _moe_experts(plan, hs, gs, moe_wg, moe_wu, moe_wd, pieces=MOE_PIECES + MOE_PIECES)
    xt = _moe_combine(plan, xt, info, y)
    return xt.reshape(batch, seq, d)
```

```python
import functools

import jax
import jax.numpy as jnp
from jax import lax
from jax.experimental import pallas as pl
from jax.experimental.pallas import tpu as pltpu

F32 = jnp.float32
BF16 = jnp.bfloat16
I32 = jnp.int32

EPS = 1e-6
NEG_INF = -1e30

D_MODEL = 1024
GLA_HEADS = 4
GLA_DK = 512
GLA_DV = 1024
GLA_HK = 128
GLA_HV = 256
GLA_RANK = 16
GLA_NORMALIZER = 16.0
GLA_CHUNK = 64

MOBA_HEADS = 8
MOBA_HD = 128
MOBA_BLOCK = 256
MOBA_TOPK = 3
MOBA_Q_SCALE = MOBA_HD ** -0.5 * 1.4426950408889634

N_EXPERTS = 8
TOP_K = 2

LANES = 128
SUBLANES = 8
BF16_ROWS = 16
ROUTER_ROWS = BF16_ROWS
VMEM_LIMIT = 48 * 1024 * 1024
VMEM_LIMIT_RESIDENT = 56 * 1024 * 1024

PROJ_TM = 512
ROUTER_TR = 1024
GLA_BLK = 256
GLA_ROWS_PER_STEP = 4
MOBA_HEADS_PER_STEP = 2

MOE_PIECES = (512, 512, 768)
MOE_TM = 256
MOE_TG = 128
GATHER_ROWS = 512
TOK_BLK = LANES
GATHER_WB = 6
COMB_TS = 256
COMB_STEP_TILES = 2
COMB_WC = 128
COMB_GROUP = 8
COMB_AHEAD_BUFS = 3
COMB_MAXW = 16


def _params(semantics, vmem_bytes=None):
    return pltpu.CompilerParams(dimension_semantics=semantics, vmem_limit_bytes=vmem_bytes)


def _rms(x, gain):
    y = x * lax.rsqrt(jnp.mean(x * x, axis=-1, keepdims=True) + EPS)
    return y * gain


def _dot(a, b):
    return jnp.dot(a, b, preferred_element_type=F32)


def _dot_nt(a, b):
    return lax.dot_general(a, b, (((1,), (1,)), ((), ())), preferred_element_type=F32)


def _dot_tn(a, b):
    return lax.dot_general(a, b, (((0,), (0,)), ((), ())), preferred_element_type=F32)


def _split(a):
    hi = a.astype(BF16)
    lo = (a - hi.astype(F32)).astype(BF16)
    return hi, lo


def _log_sigmoid(z):
    return jnp.minimum(z, 0.0) - jnp.log(1.0 + jnp.exp(-jnp.abs(z)))


def _silu(z):
    return z / (1.0 + jnp.exp(-z))


def _swiglu_pieces(h, wg_ref, wu_ref, wd_ref, widths):
    bounds = [sum(widths[:n]) for n in range(len(widths) + 1)]
    assert bounds[-1] == wg_ref.shape[1], (widths, wg_ref.shape)
    gate_up = lambda c: (_dot(h, wg_ref[:, bounds[c]:bounds[c + 1]]), _dot(h, wu_ref[:, bounds[c]:bounds[c + 1]]))
    out = None
    upcoming = gate_up(0)
    for c in range(len(widths)):
        g, u = upcoming
        if c + 1 < len(widths):
            upcoming = gate_up(c + 1)
        d = _dot((_silu(g) * u).astype(BF16), wd_ref[bounds[c]:bounds[c + 1], :])
        out = d if out is None else out + d
    return out


def _ride_along_specs(arrays, n_steps, step):
    specs, shapes = [], []
    for w in arrays:
        n_e, rows, cols = w.shape
        per_e = n_steps // n_e
        if rows % (per_e * BF16_ROWS) == 0:
            blk = (None, rows // per_e, cols)
            imap = lambda *ids, per_e=per_e: (step(*ids) // per_e, step(*ids) % per_e, 0)
        else:
            assert cols % (per_e * LANES) == 0, (w.shape, n_steps)
            blk = (None, rows, cols // per_e)
            imap = lambda *ids, per_e=per_e: (step(*ids) // per_e, 0, step(*ids) % per_e)
        specs.append(pl.BlockSpec(blk, imap))
        shapes.append(jax.ShapeDtypeStruct(w.shape, BF16))
    return specs, shapes


def _with_ride_along(body, n_in, n_out, n_cast):
    def kernel(*refs):
        ins, refs = refs[:n_in], refs[n_in:]
        cast_in, refs = refs[:n_cast], refs[n_cast:]
        outs, refs = refs[:n_out], refs[n_out:]
        cast_out, scratch = refs[:n_cast], refs[n_cast:]
        for src, dst in zip(cast_in, cast_out):
            dst[...] = src[...].astype(BF16)
        body(*ins, *outs, *scratch)
    return kernel


def _gla_inproj_kernel(x_next_ref, x_first_ref, gain_ref, w_ref, wg2_ref, b_ref,
                       q_ref, k_ref, v_ref, g_ref, la_ref, wb_ref, h_ref):
    i = pl.program_id(0)

    @pl.when(i == 0)
    def _():
        wb_ref[...] = w_ref[...].astype(BF16)
        h_ref[0] = _rms(x_first_ref[...], gain_ref[...]).astype(BF16)

    n_main = 2 * GLA_DK + 2 * GLA_DV

    def step(cur, nxt):
        h = h_ref[cur]
        h_ref[nxt] = _rms(x_next_ref[...], gain_ref[...]).astype(BF16)
        a_lr = _dot(h, wb_ref[:, n_main:])
        a_hi, a_lo = _split(a_lr)
        w_hi, w_lo = _split(wg2_ref[...])
        z = _dot(jnp.concatenate([a_hi, a_lo, a_hi], axis=1), jnp.concatenate([w_hi, w_hi, w_lo], axis=0))
        la_ref[...] = _log_sigmoid(z + b_ref[...]) / GLA_NORMALIZER
        q_ref[...] = _dot(h, wb_ref[:, 0:GLA_DK]).astype(BF16)
        k_ref[...] = _dot(h, wb_ref[:, GLA_DK:2 * GLA_DK]).astype(BF16)
        v_ref[...] = _dot(h, wb_ref[:, 2 * GLA_DK:2 * GLA_DK + GLA_DV]).astype(BF16)
        g_ref[...] = _dot(h, wb_ref[:, 2 * GLA_DK + GLA_DV:n_main]).astype(BF16)

    @pl.when(i % 2 == 0)
    def _():
        step(0, 1)

    @pl.when(i % 2 == 1)
    def _():
        step(1, 0)


def _gla_inproj(x, gain, w_in, w_gate2, b_gate, ride_along, tm):
    t = x.shape[0]
    n_steps = t // tm
    row = lambda i: (i, 0)
    fixed = lambda i: (0, 0)
    cast_specs, cast_shapes = _ride_along_specs(ride_along, n_steps, lambda i: i)
    outs = pl.pallas_call(
        _with_ride_along(_gla_inproj_kernel, 6, 5, len(ride_along)),
        grid=(n_steps,),
        in_specs=[
            pl.BlockSpec((tm, D_MODEL), lambda i: (jnp.minimum(i + 1, n_steps - 1), 0)),
            pl.BlockSpec((tm, D_MODEL), fixed),
            pl.BlockSpec((1, D_MODEL), fixed),
            pl.BlockSpec((None,) + w_in.shape[1:], lambda i: (0, 0, 0), pipeline_mode=pl.Buffered(1)),
            pl.BlockSpec((GLA_RANK, GLA_DK), fixed),
            pl.BlockSpec((1, GLA_DK), fixed),
        ] + cast_specs,
        out_specs=[
            pl.BlockSpec((tm, GLA_DK), row),
            pl.BlockSpec((tm, GLA_DK), row),
            pl.BlockSpec((tm, GLA_DV), row),
            pl.BlockSpec((tm, GLA_DV), row),
            pl.BlockSpec((tm, GLA_DK), row),
        ] + cast_specs,
        out_shape=[
            jax.ShapeDtypeStruct((t, GLA_DK), BF16),
            jax.ShapeDtypeStruct((t, GLA_DK), BF16),
            jax.ShapeDtypeStruct((t, GLA_DV), BF16),
            jax.ShapeDtypeStruct((t, GLA_DV), BF16),
            jax.ShapeDtypeStruct((t, GLA_DK), F32),
        ] + cast_shapes,
        scratch_shapes=[pltpu.VMEM(w_in.shape[1:], BF16), pltpu.VMEM((2, tm, D_MODEL), BF16)],
        compiler_params=_params(("arbitrary",), VMEM_LIMIT),
        name="gla_inproj",
    )(x, x, gain, w_in, w_gate2, b_gate, *ride_along)
    return outs[:5], outs[5:]


def _gla_kernel(q_ref, k_ref, v_ref, g_ref, la_ref, og_ref, o_ref, st_ref, *, n_chunks):
    @pl.when(pl.program_id(1) == 0)
    def _():
        st_ref[...] = jnp.zeros_like(st_ref)

    n_rows = q_ref.shape[0]
    blk = n_chunks * GLA_CHUNK
    r = lax.broadcasted_iota(I32, (blk, blk), 0)
    c = lax.broadcasted_iota(I32, (blk, blk), 1)
    causal = (r >= c) & (r // GLA_CHUNK == c // GLA_CHUNK)
    tril = causal.astype(BF16)
    og = og_ref[...]
    chunk_rows = [slice(ci * GLA_CHUNK, (ci + 1) * GLA_CHUNK) for ci in range(n_chunks)]
    hk = [slice(h * GLA_HK, (h + 1) * GLA_HK) for h in range(GLA_HEADS)]
    hv = [slice(h * GLA_HV, (h + 1) * GLA_HV) for h in range(GLA_HEADS)]

    q_dec, k_inv, k_end, chunk_decay = [], [], [], []
    for b in range(n_rows):
        la = la_ref[b]
        la_hi, la_mid = _split(la)
        la_lo = (la - la_hi.astype(F32) - la_mid.astype(F32)).astype(BF16)
        cum = _dot(tril, la_hi) + _dot(tril, la_mid) + _dot(tril, la_lo)
        ends = [cum[rows.stop - 1:rows.stop, :] for rows in chunk_rows]
        last = jnp.concatenate([jnp.broadcast_to(e, (GLA_CHUNK, GLA_DK)) for e in ends], axis=0)
        q = q_ref[b].astype(F32) * (GLA_HK ** -0.5)
        k = k_ref[b].astype(F32)
        q_dec.append((q * jnp.exp(cum)).astype(BF16))
        k_inv.append((k * jnp.exp(-cum)).astype(BF16))
        k_end.append((k * jnp.exp(last - cum)).astype(BF16))
        chunk_decay.append([jnp.exp(e) for e in ends])

    streams = [(b, h) for b in range(n_rows) for h in range(GLA_HEADS)]
    scores = [jnp.where(causal, _dot_nt(q_dec[b][:, hk[h]], k_inv[b][:, hk[h]]), 0.0).astype(BF16)
              for b, h in streams]
    state_in = [[_dot_tn(v_ref[b, rows, hv[h]], k_end[b][rows, hk[h]]) for rows in chunk_rows]
                for b, h in streams]
    intra = [_dot(scores[n], v_ref[b, :, hv[h]]) for n, (b, h) in enumerate(streams)]
    for n, (b, h) in enumerate(streams):
        st = st_ref[b, h]
        inter = []
        for ci, rows in enumerate(chunk_rows):
            inter.append(_dot_nt(q_dec[b][rows, hk[h]], st.astype(BF16)))
            st = st * chunk_decay[b][ci][:, hk[h]] + state_in[n][ci]
        st_ref[b, h] = st
        o = intra[n] + jnp.concatenate(inter, axis=0)
        o = o * lax.rsqrt(jnp.mean(o * o, axis=-1, keepdims=True) + EPS) * og
        o = o * _silu(g_ref[b, :, hv[h]].astype(F32))
        o_ref[b, :, hv[h]] = o.astype(BF16)


def _gla(q, k, v, g, la, out_gain, ride_along, batch, seq, blk, rows_per_step):
    nb = seq // blk
    nbt = batch // rows_per_step
    per_row = lambda a: a.reshape(batch, seq, a.shape[-1])
    spec = lambda width: pl.BlockSpec((rows_per_step, blk, width), lambda b, i: (b, i, 0))
    cast_specs, cast_shapes = _ride_along_specs(ride_along, nbt * nb, lambda b, i: b * nb + i)
    outs = pl.pallas_call(
        _with_ride_along(functools.partial(_gla_kernel, n_chunks=blk // GLA_CHUNK), 6, 1, len(ride_along)),
        grid=(nbt, nb),
        in_specs=[spec(GLA_DK), spec(GLA_DK), spec(GLA_DV), spec(GLA_DV), spec(GLA_DK),
                  pl.BlockSpec((1, GLA_HV), lambda b, i: (0, 0))] + cast_specs,
        out_specs=[spec(GLA_DV)] + cast_specs,
        out_shape=[jax.ShapeDtypeStruct((batch, seq, GLA_DV), BF16)] + cast_shapes,
        scratch_shapes=[pltpu.VMEM((rows_per_step, GLA_HEADS, GLA_HV, GLA_HK), F32)],
        compiler_params=_params(("arbitrary", "arbitrary"), VMEM_LIMIT),
        name="gla_core",
    )(per_row(q), per_row(k), per_row(v), per_row(g), per_row(la), out_gain, *ride_along)
    return outs[0].reshape(batch * seq, GLA_DV), outs[1:]


def _ffn_kernel(x_ref, o_ref, wo_ref, gain_ref, wg_ref, wu_ref, wd_ref, y_ref, *, pieces):
    x1 = x_ref[...] + _dot(o_ref[...], wo_ref[...])
    h = _rms(x1, gain_ref[...]).astype(BF16)
    y_ref[...] = x1 + _swiglu_pieces(h, wg_ref, wu_ref, wd_ref, pieces)


def _ffn_residual(x, o, w_out, gain, w_gate, w_up, w_down, tm, pieces):
    t, d = x.shape
    row = lambda i: (i, 0)
    resident = lambda a: pl.BlockSpec(a.shape, lambda i: (0, 0), pipeline_mode=pl.Buffered(1))
    return pl.pallas_call(
        functools.partial(_ffn_kernel, pieces=pieces),
        grid=(t // tm,),
        in_specs=[pl.BlockSpec((tm, d), row), pl.BlockSpec((tm, d), row), resident(w_out), resident(gain),
                  resident(w_gate), resident(w_up), resident(w_down)],
        out_specs=pl.BlockSpec((tm, d), row),
        out_shape=jax.ShapeDtypeStruct((t, d), F32),
        compiler_params=_params(("parallel",), VMEM_LIMIT_RESIDENT),
        name="ffn_swiglu",
    )(x, o, w_out, gain, w_gate, w_up, w_down)


def _moba_qkv_kernel(x_ref, gain_ref, w_ref, qg_ref, kg_ref, q_ref, k_ref, v_ref):
    h = _rms(x_ref[...], gain_ref[...]).astype(BF16)
    d = MOBA_HEADS * MOBA_HD
    qg = qg_ref[...]
    kg = kg_ref[...]
    q = _dot(h, w_ref[:, 0:d])
    k = _dot(h, w_ref[:, d:2 * d])
    for hd in range(MOBA_HEADS):
        cols = slice(hd * MOBA_HD, (hd + 1) * MOBA_HD)
        q_ref[:, cols] = (_rms(q[:, cols], qg) * MOBA_Q_SCALE).astype(BF16)
        k_ref[:, cols] = _rms(k[:, cols], kg).astype(BF16)
    v_ref[...] = _dot(h, w_ref[:, 2 * d:]).astype(BF16)


def _moba_qkv(x, gain, w, q_gain, k_gain, tm):
    t = x.shape[0]
    d = MOBA_HEADS * MOBA_HD
    row = lambda i: (i, 0)
    fixed = lambda i: (0, 0)
    return pl.pallas_call(
        _moba_qkv_kernel,
        grid=(t // tm,),
        in_specs=[
            pl.BlockSpec((tm, D_MODEL), row),
            pl.BlockSpec((1, D_MODEL), fixed),
            pl.BlockSpec((D_MODEL, 3 * d), fixed),
            pl.BlockSpec((1, MOBA_HD), fixed),
            pl.BlockSpec((1, MOBA_HD), fixed),
        ],
        out_specs=[pl.BlockSpec((tm, d), row)] * 3,
        out_shape=[jax.ShapeDtypeStruct((t, d), BF16)] * 3,
        compiler_params=_params(("parallel",), VMEM_LIMIT),
        name="moba_qkv",
    )(x, gain, w, q_gain, k_gain)


def _reduce_rows(x, op, stop=SUBLANES):
    while x.shape[0] > stop:
        half = x.shape[0] // 2
        x = op(x[:half], x[half:])
    return x


def _moba_kernel(q_ref, k_ref, v_ref, o_ref, *, n_blocks):
    seq = n_blocks * MOBA_BLOCK
    heads = range(q_ref.shape[1] // MOBA_HD)
    cols = [slice(hd * MOBA_HD, (hd + 1) * MOBA_HD) for hd in heads]
    n_pad = BF16_ROWS
    ones_rows = jnp.where(lax.broadcasted_iota(I32, (BF16_ROWS, seq), 0) == 0, 1.0, 0.0)
    v_t = [jnp.concatenate([v_ref[:, cols[hd]].astype(F32).T, ones_rows], axis=0).astype(BF16) for hd in heads]
    blk_of_key = lax.broadcasted_iota(I32, (n_pad, seq), 1) // MOBA_BLOCK
    blk_row = lax.broadcasted_iota(I32, (n_pad, seq), 0)
    indicator = jnp.where(blk_of_key == blk_row, 1.0 / MOBA_BLOCK, 0.0).astype(BF16)
    k_mean = [_split(_dot(indicator, k_ref[:, cols[hd]])) for hd in heads]

    blk_id = lax.broadcasted_iota(I32, (n_pad, MOBA_BLOCK), 0)
    key_pos = lax.broadcasted_iota(I32, (MOBA_BLOCK, MOBA_BLOCK), 0)
    query_pos = lax.broadcasted_iota(I32, (MOBA_BLOCK, MOBA_BLOCK), 1)
    causal = key_pos <= query_pos

    def masked_scores(hd, i):
        rows = slice(i * MOBA_BLOCK, (i + 1) * MOBA_BLOCK)
        q = q_ref[rows, cols[hd]]
        selected = None
        if i > MOBA_TOPK:
            km_hi, km_lo = k_mean[hd]
            gate = _dot_nt(km_hi, q) + _dot_nt(km_lo, q)
            gate = jnp.where(blk_id < i, gate, NEG_INF)
            rank = jnp.zeros_like(gate)
            for jp in range(i):
                row = gate[jp:jp + 1, :]
                beats = (row > gate) | ((row == gate) & (jp < blk_id))
                rank = rank + beats.astype(F32)
            selected = jnp.where((rank < MOBA_TOPK) & (blk_id < i), 1.0, 0.0)
        pieces = []
        top = None
        for j in range(i + 1):
            sj = _dot_nt(k_ref[j * MOBA_BLOCK:(j + 1) * MOBA_BLOCK, cols[hd]], q).astype(BF16)
            if j == i:
                sj = jnp.where(causal, sj, NEG_INF)
            elif selected is not None:
                sj = jnp.where(selected[j:j + 1, :] > 0.5, sj, NEG_INF)
            pieces.append(sj)
            folded = _reduce_rows(sj, jnp.maximum, stop=BF16_ROWS)
            top = folded if top is None else jnp.maximum(top, folded)
        return pieces, top

    def weighted_values(hd, i, p):
        o_t = _dot(v_t[hd][:, 0:(i + 1) * MOBA_BLOCK], p)
        rows = slice(i * MOBA_BLOCK, (i + 1) * MOBA_BLOCK)
        o_ref[rows, cols[hd]] = (o_t[:MOBA_HD, :] / o_t[MOBA_HD:MOBA_HD + 1, :]).T.astype(BF16)

    def softmax_numerators(scored):
        pieces, top = scored
        m = top.max(axis=0, keepdims=True)
        probs = [jnp.exp2(sj - m) for sj in pieces]
        return probs[0] if len(probs) == 1 else jnp.concatenate(probs, axis=0)

    upcoming = [masked_scores(hd, 0) for hd in heads]
    pending = None
    for i in range(n_blocks):
        pieces = upcoming
        if i + 1 < n_blocks:
            upcoming = [masked_scores(hd, i + 1) for hd in heads]
        if pending is not None:
            for hd in heads:
                weighted_values(hd, i - 1, pending[hd])
        pending = [softmax_numerators(pieces[hd]) for hd in heads]
    for hd in heads:
        weighted_values(hd, n_blocks - 1, pending[hd])


def _moba(q, k, v, ride_along, batch, seq, heads_per_step):
    t, d = q.shape
    n_groups = MOBA_HEADS // heads_per_step
    spec = pl.BlockSpec((seq, heads_per_step * MOBA_HD), lambda b, h: (b, h))
    cast_specs, cast_shapes = _ride_along_specs(ride_along, batch * n_groups, lambda b, h: b * n_groups + h)
    outs = pl.pallas_call(
        _with_ride_along(functools.partial(_moba_kernel, n_blocks=seq // MOBA_BLOCK), 3, 1, len(ride_along)),
        grid=(batch, n_groups),
        in_specs=[spec, spec, spec] + cast_specs,
        out_specs=[spec] + cast_specs,
        out_shape=[jax.ShapeDtypeStruct((t, d), BF16)] + cast_shapes,
        compiler_params=_params(("arbitrary", "arbitrary"), VMEM_LIMIT_RESIDENT),
        name="moba_attn",
    )(q, k, v, *ride_along)
    return outs[0], outs[1:]


def _router_kernel(x_ref, o_ref, wo_ref, gain_ref, w_ref, x2_ref, h_ref, info_ref, rows_ref, cnt_ref,
                   carry_ref, tri_ref,
                   *, region_rows):
    tr = x_ref.shape[0]

    @pl.when(pl.program_id(0) == 0)
    def _():
        carry_ref[...] = jnp.zeros_like(carry_ref)
        r = lax.broadcasted_iota(I32, (TOK_BLK, TOK_BLK), 0)
        c = lax.broadcasted_iota(I32, (TOK_BLK, TOK_BLK), 1)
        tri_ref[...] = (r < c).astype(BF16)

    x2 = x_ref[...] + _dot(o_ref[...], wo_ref[...])
    x2_ref[...] = x2
    h = _rms(x2, gain_ref[...])
    h_hi, h_lo = _split(h)
    h_ref[...] = h_hi
    n_e = ROUTER_ROWS
    wt = w_ref[...]
    passes = _dot_nt(wt, h_hi)
    logits = passes[:n_e] + passes[n_e:] + _dot_nt(wt[:n_e], h_lo)
    e = lax.broadcasted_iota(I32, logits.shape, 0)
    logits = jnp.where(e < N_EXPERTS, logits, -jnp.inf)
    m1 = logits.max(axis=0, keepdims=True)
    i1 = jnp.where(logits == m1, e, n_e).min(axis=0, keepdims=True)
    first = e == i1
    rest = jnp.where(first, -jnp.inf, logits)
    m2 = rest.max(axis=0, keepdims=True)
    i2 = jnp.where(rest == m2, e, n_e).min(axis=0, keepdims=True)
    second = e == i2
    e2 = jnp.exp(m2 - m1)
    denom = 1.0 + e2
    sel = jnp.where(first | second, 1.0, 0.0).astype(BF16)
    nb = tr // TOK_BLK
    blk_row = lax.broadcasted_iota(I32, (nb, tr), 0)
    blk_tok = lax.broadcasted_iota(I32, (nb, tr), 1) // TOK_BLK
    cnt = _dot_nt((blk_row == blk_tok).astype(BF16), sel)
    cnt_ref[...] = jnp.concatenate([cnt, jnp.zeros((nb, LANES - n_e), F32)], axis=1)
    offset = carry_ref[...]
    before = []
    for b in range(nb):
        sel_b = sel[:, b * TOK_BLK:(b + 1) * TOK_BLK]
        before.append(_dot(sel_b, tri_ref[...]) + offset)
        offset = offset + jnp.broadcast_to(sel_b.astype(F32).sum(axis=1, keepdims=True), offset.shape)
    carry_ref[...] = offset
    before = jnp.concatenate(before, axis=1)
    rank1 = jnp.where(first, before, 0.0).sum(axis=0, keepdims=True)
    rank2 = jnp.where(second, before, 0.0).sum(axis=0, keepdims=True)
    fields = (i1.astype(F32) * region_rows + rank1, i2.astype(F32) * region_rows + rank2,
              1.0 / denom, e2 / denom)
    for n, f in enumerate(fields):
        for b in range(nb):
            rows_ref[n, b:b + 1, :] = f[:, b * TOK_BLK:(b + 1) * TOK_BLK]
    info_t = jnp.concatenate(list(fields) + [jnp.zeros((LANES - len(fields), tr), F32)], axis=0)
    info_ref[...] = info_t.T[:, :len(fields)]


def _router(x, o, w_out, gain, w_router, tr):
    t = x.shape[0]
    nb = tr // TOK_BLK
    pad = ((0, ROUTER_ROWS - N_EXPERTS), (0, 0))
    w_hi = w_router.T.astype(BF16)
    w_lo = (w_router.T - w_hi.astype(F32)).astype(BF16)
    w_both = jnp.concatenate([jnp.pad(w_hi, pad), jnp.pad(w_lo, pad)], axis=0)
    row = lambda i: (i, 0)
    fixed = lambda i: (0, 0)
    return pl.pallas_call(
        functools.partial(_router_kernel, region_rows=float(t)),
        grid=(t // tr,),
        in_specs=[
            pl.BlockSpec((tr, D_MODEL), row),
            pl.BlockSpec((tr, D_MODEL), row),
            pl.BlockSpec((D_MODEL, D_MODEL), fixed),
            pl.BlockSpec((1, D_MODEL), fixed),
            pl.BlockSpec((2 * ROUTER_ROWS, D_MODEL), fixed),
        ],
        out_specs=[
            pl.BlockSpec((tr, D_MODEL), row),
            pl.BlockSpec((tr, D_MODEL), row),
            pl.BlockSpec((tr, 4), row),
            pl.BlockSpec((4, nb, TOK_BLK), lambda i: (0, i, 0)),
            pl.BlockSpec((nb, LANES), row),
        ],
        out_shape=[
            jax.ShapeDtypeStruct((t, D_MODEL), F32),
            jax.ShapeDtypeStruct((t, D_MODEL), BF16),
            jax.ShapeDtypeStruct((t, 4), F32),
            jax.ShapeDtypeStruct((4, t // TOK_BLK, TOK_BLK), F32),
            jax.ShapeDtypeStruct((t // TOK_BLK, LANES), F32),
        ],
        scratch_shapes=[pltpu.VMEM((ROUTER_ROWS, LANES), F32), pltpu.VMEM((TOK_BLK, TOK_BLK), BF16)],
        compiler_params=_params(("arbitrary",), VMEM_LIMIT_RESIDENT),
        name="moe_router",
    )(x, o, w_out, gain, w_both)


def _moe_plan(cnt, t):
    e_ids = jnp.arange(N_EXPERTS, dtype=I32)
    by_expert = lambda table, idx: jnp.sum(jnp.where(idx[..., None] == e_ids, table, 0), axis=-1)
    cnt_blk = cnt[:, :N_EXPERTS].astype(I32)
    cum_blk = jnp.cumsum(cnt_blk, axis=0)
    counts = cum_blk[-1]
    tiles_e = (counts + MOE_TM - 1) // MOE_TM
    tile_end = jnp.cumsum(tiles_e)
    tile_off = tile_end - tiles_e
    nt = TOP_K * t // MOE_TM + N_EXPERTS
    r = jnp.arange(nt, dtype=I32)
    valid = r < tile_end[-1]
    te = jnp.minimum(jnp.sum((r[:, None] >= tile_end[None, :]).astype(I32), axis=1), N_EXPERTS - 1)
    local = r - by_expert(tile_off, te)
    last_e = jnp.max(jnp.where(tiles_e > 0, e_ids, 0))
    tile_expert = jnp.where(valid, te, last_e)
    shift = e_ids * t - tile_off * MOE_TM

    per = MOE_TM // MOE_TG
    rg = jnp.arange(nt * per, dtype=I32)
    eg = jnp.repeat(te, per)
    k0 = jnp.repeat(local, per) * MOE_TM + (rg % per) * MOE_TG
    k1 = jnp.minimum(k0 + MOE_TG, by_expert(counts, eg))
    has = jnp.repeat(valid, per) & (k1 > k0)
    cum_e = by_expert(cum_blk[None, :, :], eg[:, None])
    s_lo = jnp.sum((cum_e <= k0[:, None]).astype(I32), axis=1)
    s_hi = jnp.sum((cum_e <= (k1 - 1)[:, None]).astype(I32), axis=1)
    g_nwin = jnp.where(has, (s_hi - s_lo) // GATHER_WB + 1, 0)
    g_slo = jnp.where(has, s_lo, 0)
    g_code = eg * t + k0

    nts = t // COMB_TS
    cnt_ts = cnt_blk.reshape(nts, COMB_TS // TOK_BLK, N_EXPERTS).sum(axis=1)
    seg_lo = (tile_off * MOE_TM)[None, :] + jnp.cumsum(cnt_ts, axis=0) - cnt_ts
    seg_hi = seg_lo + cnt_ts
    ws0 = (seg_lo // BF16_ROWS) * BF16_ROWS
    nw = jnp.where(cnt_ts > 0, (seg_hi - ws0 + COMB_WC - 1) // COMB_WC, 0)
    nw_end = jnp.cumsum(nw, axis=1)
    k = jnp.arange(COMB_MAXW, dtype=I32)
    ek = jnp.minimum(jnp.sum((k[None, :, None] >= nw_end[:, None, :]).astype(I32), axis=2), N_EXPERTS - 1)
    pick = lambda a: by_expert(a[:, None, :], ek)
    live = k[None, :] < nw_end[:, -1:]
    nominal = pick(ws0) + (k[None, :] - pick(nw_end - nw)) * COMB_WC
    shift_k = by_expert(shift, ek)
    c_start = jnp.minimum(nominal, by_expert(tile_end * MOE_TM, ek) - COMB_WC)
    c_off = shift_k + c_start
    c_lo = jnp.maximum(pick(seg_lo), nominal) + shift_k
    c_hi = jnp.minimum(pick(seg_hi), nominal + COMB_WC) + shift_k
    flat = lambda a: jnp.where(live, a, 0).reshape(-1)
    return dict(
        tile_expert=tile_expert, tile_valid=valid.astype(I32),
        g_code=g_code, g_slo=g_slo, g_nwin=g_nwin,
        c_start=flat(c_start), c_off=flat(c_off), c_lo=flat(c_lo), c_hi=flat(c_hi), c_nwin=nw_end[:, -1])


def _moe_gather_kernel(code_ref, slo_ref, nwin_ref, rows_ref, h_ref, hs_ref, gs_ref):
    n_blk = rows_ref.shape[1]
    n_sub = hs_ref.shape[0] // MOE_TG
    sub_ids = [pl.program_id(0) * n_sub + sub for sub in range(n_sub)]
    out_rows = [slice(sub * MOE_TG, (sub + 1) * MOE_TG) for sub in range(n_sub)]

    def window(r, k):
        row_id = (code_ref[r] + lax.broadcasted_iota(I32, (MOE_TG, TOK_BLK), 0)).astype(F32)
        nominal = slo_ref[r] + k * GATHER_WB
        sb = jnp.minimum(nominal, n_blk - GATHER_WB)
        fresh = sb + lax.broadcasted_iota(I32, (GATHER_WB, TOK_BLK), 0) >= nominal
        p1 = jnp.where(fresh, rows_ref[0, pl.ds(sb, GATHER_WB), :], -1.0)
        p2 = jnp.where(fresh, rows_ref[1, pl.ds(sb, GATHER_WB), :], -1.0)
        w1 = rows_ref[2, pl.ds(sb, GATHER_WB), :]
        w2 = rows_ref[3, pl.ds(sb, GATHER_WB), :]
        pieces = []
        g = jnp.zeros((MOE_TG, TOK_BLK), F32)
        for a in range(GATHER_WB):
            m1 = p1[a:a + 1, :] == row_id
            m2 = p2[a:a + 1, :] == row_id
            pieces.append(jnp.where(m1 | m2, 1.0, 0.0).astype(BF16))
            g = g + jnp.where(m1, w1[a:a + 1, :], 0.0) + jnp.where(m2, w2[a:a + 1, :], 0.0)
        onehot = jnp.concatenate(pieces, axis=1)
        return onehot, g.sum(axis=-1, keepdims=True), pl.multiple_of(sb * TOK_BLK, TOK_BLK)

    gathered = lambda onehot, tok0: _dot(onehot, h_ref[pl.ds(tok0, GATHER_WB * TOK_BLK), :]).astype(BF16)

    firsts = [window(r, 0) for r in sub_ids]
    for rows, (onehot, weight, tok0) in zip(out_rows, firsts):
        hs_ref[rows, :] = gathered(onehot, tok0)
        gs_ref[rows, :] = weight

    for r, rows in zip(sub_ids, out_rows):
        def more(k, carry, r=r, rows=rows):
            onehot, weight, tok0 = window(r, k)
            hs_ref[rows, :] += gathered(onehot, tok0)
            gs_ref[rows, :] += weight
            return carry

        lax.fori_loop(1, nwin_ref[r], more, 0)


def _moe_gather(plan, h, rows):
    t = h.shape[0]
    n_rows = plan["g_code"].shape[0] * MOE_TG
    n_blk = t // TOK_BLK
    grid_spec = pltpu.PrefetchScalarGridSpec(
        num_scalar_prefetch=3,
        grid=(n_rows // GATHER_ROWS,),
        in_specs=[
            pl.BlockSpec((4, n_blk, TOK_BLK), lambda r, *_: (0, 0, 0), pipeline_mode=pl.Buffered(1)),
            pl.BlockSpec((t, D_MODEL), lambda r, *_: (0, 0), pipeline_mode=pl.Buffered(1)),
        ],
        out_specs=[
            pl.BlockSpec((GATHER_ROWS, D_MODEL), lambda r, *_: (r, 0)),
            pl.BlockSpec((GATHER_ROWS, 1), lambda r, *_: (r, 0)),
        ],
    )
    return pl.pallas_call(
        _moe_gather_kernel,
        grid_spec=grid_spec,
        out_shape=[
            jax.ShapeDtypeStruct((n_rows, D_MODEL), BF16),
            jax.ShapeDtypeStruct((n_rows, 1), F32),
        ],
        compiler_params=_params(("parallel",), VMEM_LIMIT_RESIDENT),
        name="moe_gather",
    )(plan["g_code"], plan["g_slo"], plan["g_nwin"], rows, h)


def _moe_expert_kernel(te_ref, tv_ref, hs_ref, gs_ref, wg_ref, wu_ref, wd_ref, y_ref, acc_ref, *, nj, pieces):
    r = pl.program_id(0)
    j = pl.program_id(1)
    valid = tv_ref[r] > 0
    part = lambda: _swiglu_pieces(hs_ref[...], wg_ref, wu_ref, wd_ref, pieces)

    if nj == 1:
        @pl.when(valid)
        def _():
            y_ref[...] = (part() * gs_ref[...]).astype(BF16)
    else:
        @pl.when(valid & (j == 0))
        def _():
            acc_ref[...] = part()

        if nj > 2:
            @pl.when(valid & (j > 0) & (j < nj - 1))
            def _():
                acc_ref[...] += part()

        @pl.when(valid & (j == nj - 1))
        def _():
            y_ref[...] = ((acc_ref[...] + part()) * gs_ref[...]).astype(BF16)

    @pl.when(jnp.logical_not(valid) & (j == nj - 1))
    def _():
        y_ref[...] = jnp.zeros_like(y_ref)


def _moe_experts(plan, hs, gs, w_gate, w_up, w_down, pieces):
    tf = sum(pieces)
    n_rows, d = hs.shape
    ff = w_gate.shape[2]
    nj = ff // tf
    jj = lambda r, j, tv: jnp.where(tv[r] > 0, j, nj - 1)
    grid_spec = pltpu.PrefetchScalarGridSpec(
        num_scalar_prefetch=2,
        grid=(n_rows // MOE_TM, nj),
        in_specs=[
            pl.BlockSpec((MOE_TM, d), lambda r, j, te, tv: (r, 0)),
            pl.BlockSpec((MOE_TM, 1), lambda r, j, te, tv: (r, 0)),
            pl.BlockSpec((None, d, tf), lambda r, j, te, tv: (te[r], 0, jj(r, j, tv))),
            pl.BlockSpec((None, d, tf), lambda r, j, te, tv: (te[r], 0, jj(r, j, tv))),
            pl.BlockSpec((None, tf, d), lambda r, j, te, tv: (te[r], jj(r, j, tv), 0)),
        ],
        out_specs=pl.BlockSpec((MOE_TM, d), lambda r, j, te, tv: (r, 0)),
        scratch_shapes=[pltpu.VMEM((MOE_TM if nj > 1 else SUBLANES, d), F32)],
    )
    return pl.pallas_call(
        functools.partial(_moe_expert_kernel, nj=nj, pieces=pieces),
        grid_spec=grid_spec,
        out_shape=jax.ShapeDtypeStruct((n_rows, d), BF16),
        compiler_params=_params(("arbitrary", "arbitrary"), VMEM_LIMIT_RESIDENT),
        name="moe_experts",
    )(plan["tile_expert"], plan["tile_valid"], hs, gs, w_gate, w_up, w_down)


def _moe_combine_kernel(start_ref, off_ref, lo_ref, hi_ref, nwin_ref, x_ref, info_ref, y_hbm, o_ref,
                        ybuf, ybuf_more, sem, sem_more):
    step = pl.program_id(0)
    n_steps = pl.num_programs(0)
    per = COMB_STEP_TILES
    ring = COMB_AHEAD_BUFS * per

    def window_copy(idx, dst, dsem):
        start = pl.multiple_of(start_ref[idx], BF16_ROWS)
        return pltpu.make_async_copy(y_hbm.at[pl.ds(start, COMB_WC), :], dst, dsem)

    def first_group(tile, wait):
        buf = tile % ring
        for s in range(COMB_GROUP):
            @pl.when(s < nwin_ref[tile])
            def _():
                cp = window_copy(tile * COMB_MAXW + s, ybuf.at[buf, pl.ds(s * COMB_WC, COMB_WC), :],
                                 sem.at[buf, s])
                cp.wait() if wait else cp.start()

    def request(grid_step):
        for sub in range(per):
            first_group(grid_step * per + sub, wait=False)

    @pl.when(step == 0)
    def _():
        ybuf[...] = jnp.zeros_like(ybuf)
        ybuf_more[...] = jnp.zeros_like(ybuf_more)
        for ahead in range(COMB_AHEAD_BUFS - 1):
            @pl.when(ahead < n_steps)
            def _():
                request(ahead)

    @pl.when(step + COMB_AHEAD_BUFS - 1 < n_steps)
    def _():
        request(step + COMB_AHEAD_BUFS - 1)

    tiles = [step * per + sub for sub in range(per)]
    rows = [slice(sub * COMB_TS, (sub + 1) * COMB_TS) for sub in range(per)]
    for tile in tiles:
        first_group(tile, wait=True)

    lane = lax.broadcasted_iota(I32, (COMB_TS, COMB_WC), 1)
    lane_row = lax.broadcasted_iota(I32, (1, COMB_WC), 1)
    codes = [[jnp.broadcast_to(info_ref[r, n:n + 1].astype(I32), (COMB_TS, COMB_WC)) for n in range(TOP_K)]
             for r in rows]

    def onehot_group(sub, g):
        pieces = []
        for s in range(COMB_GROUP):
            idx = tiles[sub] * COMB_MAXW + g * COMB_GROUP + s
            off = off_ref[idx]
            in_span = (lane_row >= lo_ref[idx] - off) & (lane_row < hi_ref[idx] - off)
            target = lane + off
            hit = ((codes[sub][0] == target) | (codes[sub][1] == target)) & in_span
            pieces.append(jnp.where(hit, 1.0, 0.0).astype(BF16))
        return jnp.concatenate(pieces, axis=1)

    onehots = [onehot_group(sub, 0) for sub in range(per)]
    for sub in range(per):
        o_ref[rows[sub], :] = x_ref[rows[sub], :] + _dot(onehots[sub], ybuf[tiles[sub] % ring])

    for sub in range(per):
        n = nwin_ref[tiles[sub]]

        def more(g, carry, sub=sub, n=n):
            for wait in (False, True):
                for s in range(COMB_GROUP):
                    @pl.when(g * COMB_GROUP + s < n)
                    def _():
                        cp = window_copy(tiles[sub] * COMB_MAXW + g * COMB_GROUP + s,
                                         ybuf_more.at[pl.ds(s * COMB_WC, COMB_WC), :], sem_more.at[s])
                        cp.wait() if wait else cp.start()
            o_ref[rows[sub], :] += _dot(onehot_group(sub, g), ybuf_more[...])
            return carry

        lax.fori_loop(1, (n + COMB_GROUP - 1) // COMB_GROUP, more, 0)


def _moe_combine(plan, x, info, y):
    t, d = x.shape
    width = COMB_GROUP * COMB_WC
    step_rows = COMB_STEP_TILES * COMB_TS
    grid_spec = pltpu.PrefetchScalarGridSpec(
        num_scalar_prefetch=5,
        grid=(t // step_rows,),
        in_specs=[
            pl.BlockSpec((step_rows, d), lambda i, *_: (i, 0)),
            pl.BlockSpec((step_rows, 4), lambda i, *_: (i, 0)),
            pl.BlockSpec(memory_space=pl.ANY),
        ],
        out_specs=pl.BlockSpec((step_rows, d), lambda i, *_: (i, 0)),
        scratch_shapes=[
            pltpu.VMEM((COMB_AHEAD_BUFS * COMB_STEP_TILES, width, d), BF16),
            pltpu.VMEM((width, d), BF16),
            pltpu.SemaphoreType.DMA((COMB_AHEAD_BUFS * COMB_STEP_TILES, COMB_GROUP)),
            pltpu.SemaphoreType.DMA((COMB_GROUP,)),
        ],
    )
    return pl.pallas_call(
        _moe_combine_kernel,
        grid_spec=grid_spec,
        out_shape=jax.ShapeDtypeStruct((t, d), F32),
        compiler_params=_params(("arbitrary",), VMEM_LIMIT),
        name="moe_combine",
    )(plan["c_start"], plan["c_off"], plan["c_lo"], plan["c_hi"], plan["c_nwin"], x, info, y)


def kernel(x, norm_mix, norm_ffn, gla_w_in, gla_w_gate2, gla_b_gate, gla_out_gain, gla_w_out,
           moba_w_qkv, moba_q_gain, moba_k_gain, moba_w_out, ffn_w_gate, ffn_w_up, ffn_w_down,
           moe_w_router, moe_w_gate, moe_w_up, moe_w_down):
    batch, seq, d = x.shape
    t = batch * seq
    xt = x.reshape(t, d)

    (q, k, v, g, la), (ffn_wg, ffn_wu, ffn_wd, gla_wo, moba_wqkv, moba_wo) = _gla_inproj(
        xt, norm_mix[0:1], gla_w_in, gla_w_gate2[0], gla_b_gate[0:1],
        ride_along=(ffn_w_gate, ffn_w_up, ffn_w_down, gla_w_out, moba_w_qkv, moba_w_out), tm=PROJ_TM)
    o, _ = _gla(q, k, v, g, la, gla_out_gain[0:1], (), batch, seq, blk=GLA_BLK,
                rows_per_step=GLA_ROWS_PER_STEP)
    xt = _ffn_residual(xt, o, gla_wo[0], norm_ffn[0:1], ffn_wg[0], ffn_wu[0], ffn_wd[0],
                       tm=PROJ_TM, pieces=MOE_PIECES + MOE_PIECES)

    q, k, v = _moba_qkv(xt, norm_mix[1:2], moba_wqkv[0], moba_q_gain[0:1], moba_k_gain[0:1], tm=PROJ_TM)
    o, (moe_wg, moe_wu, moe_wd) = _moba(q, k, v, (moe_w_gate[0], moe_w_up[0], moe_w_down[0]), batch, seq,
                                        heads_per_step=MOBA_HEADS_PER_STEP)
    xt, h, info, rows, cnt = _router(xt, o, moba_wo[0], norm_ffn[1:2], moe_w_router[0], tr=ROUTER_TR)
    plan = _moe_plan(cnt, t)
    hs, gs = _moe_gather(plan, h, rows)
    y = _moe_experts(plan, hs, gs, moe_wg, moe_wu, moe_wd, pieces=MOE_PIECES + MOE_PIECES)
    xt = _moe_combine(plan, xt, info, y)
    return xt.reshape(batch, seq, d)
```
